```python
import math
import jax, jax.numpy as jnp
from jax import lax
import numpy as np

D_MODEL = 1024
BATCH = 8
SEQ = 2048
DEPTH = 1
DEC_BATCH = 128
DEC_SEQ = 8
PAST_LEN = 16384
PAGE_SIZE = 128

HEAD_DIM = 64
D_A = D_MODEL // 2
H_A = D_A // HEAD_DIM
D_B = D_MODEL // 2
H_B = D_B // HEAD_DIM
LORA_W = 64
LORA_A = 64
CONV_W = 4
MLSTM_CHUNK = 64
NORM_EPS = 1e-6
RWKV_GN_EPS = 64e-5
MLSTM_GN_EPS = 1e-6

SHIFT_W = 3 * D_A + LORA_W + LORA_A
CONV_COLS = 2 * D_B
IN_SIZES = (SHIFT_W, D_A, CONV_COLS, D_B, D_B, H_B, H_B, D_B, D_MODEL, D_MODEL)
N_IN = sum(IN_SIZES)
SHIFT_SIZES = (D_A, D_A, D_A, LORA_W, LORA_A)

kernel_name = "rwkv7_mlstm_gated_parallel_step"


def split_cols(p, sizes):
    idx = np.cumsum(sizes)[:-1].tolist()
    return jnp.split(p, idx, axis=-1)


def rmsnorm(x, g):
    xf = x.astype(jnp.float32)
    y = xf * lax.rsqrt(jnp.mean(xf * xf, axis=-1, keepdims=True) + NORM_EPS)
    return (y * g.astype(jnp.float32)).astype(x.dtype)


def head_layernorm(y, w, b, eps):
    mu = jnp.mean(y, axis=-1, keepdims=True)
    var = jnp.mean(jnp.square(y - mu), axis=-1, keepdims=True)
    H, d = y.shape[-2:]
    return (y - mu) * lax.rsqrt(var + eps) * w.reshape(H, d) + b.reshape(H, d)


def head_rmsnorm(y, w, eps):
    H, d = y.shape[-2:]
    return y * lax.rsqrt(jnp.mean(y * y, axis=-1, keepdims=True) + eps) * w.reshape(H, d)


def rwkv7_mix(r, k, v, wl, al, S0, w_decay2, w0, w_iclr2, a0, k_k, k_a, r_k, ln_w, ln_b):
    f32 = jnp.float32
    B, T, _ = r.shape
    r, k, v, wl, al = (t.astype(f32) for t in (r, k, v, wl, al))
    w = w0.astype(f32) + jnp.tanh(wl) @ w_decay2.astype(f32)
    decay = jnp.exp(-jnp.exp(-jax.nn.softplus(-w) - 0.5))
    a = jax.nn.sigmoid(a0.astype(f32) + al @ w_iclr2.astype(f32))
    heads = lambda t: t.reshape(B, T, H_A, HEAD_DIM)
    kk = heads(k * k_k.astype(f32))
    kk = kk / jnp.maximum(jnp.sqrt(jnp.sum(kk * kk, axis=-1, keepdims=True)), 1e-12)
    k = k * (1.0 + (a - 1.0) * k_a.astype(f32))
    r_h, d_h, k_h, v_h, a_h = heads(r), heads(decay), heads(k), heads(v), heads(a)

    def step(S, inp):
        rt, dt, kt, vt, kkt, at = inp
        sk = jnp.einsum('bhij,bhj->bhi', S, kkt)
        S = (S * dt[:, :, None, :]
             - sk[..., :, None] * (kkt * at)[..., None, :]
             + vt[..., :, None] * kt[..., None, :])
        return S, jnp.einsum('bhij,bhj->bhi', S, rt)

    xs = tuple(jnp.moveaxis(t, 1, 0) for t in (r_h, d_h, k_h, v_h, kk, a_h))
    S_T, y = lax.scan(step, S0.astype(f32), xs)
    y = jnp.moveaxis(y, 0, 1)
    y = head_layernorm(y, ln_w.astype(f32), ln_b.astype(f32), RWKV_GN_EPS)
    bonus = jnp.sum(r_h * k_h * r_k.astype(f32).reshape(H_A, HEAD_DIM), axis=-1, keepdims=True)
    y = y + bonus * v_h
    return y.reshape(B, T, D_A), S_T


def mlstm_chunkwise(q, k, v, i_pre, logf, C0, n0, m0):
    f32 = jnp.float32
    B, T, H, d = q.shape
    L = math.gcd(T, MLSTM_CHUNK)
    NC = T // L
    q, v = q.astype(f32), v.astype(f32)
    k = k.astype(f32) * (1.0 / math.sqrt(d))
    ch4 = lambda t: t.reshape(B, NC, L, H, d).transpose(1, 0, 3, 2, 4)
    ch3 = lambda t: t.astype(f32).reshape(B, NC, L, H).transpose(1, 0, 3, 2)
    causal = jnp.tril(jnp.ones((L, L), dtype=bool))

    def step(carry, inp):
        C, n, m = carry
        qc, kc, vc, ic, fc = inp
        b = jnp.cumsum(fc, axis=-1)
        D = b[..., :, None] - b[..., None, :] + ic[..., None, :]
        D = jnp.where(causal, D, -jnp.inf)
        inter = b + m[..., None]
        m_t = jnp.maximum(jnp.max(D, axis=-1), inter)
        w_ts = jnp.exp(D - m_t[..., None])
        w_in = jnp.exp(inter - m_t)
        s = jnp.einsum('bhtd,bhsd->bhts', qc, kc) * w_ts
        num = (jnp.einsum('bhts,bhsd->bhtd', s, vc)
               + w_in[..., None] * jnp.einsum('bhij,bhtj->bhti', C, qc))
        den = jnp.sum(s, axis=-1) + w_in * jnp.einsum('bhj,bhtj->bht', n, qc)
        h = num / jnp.maximum(jnp.abs(den), jnp.exp(-m_t))[..., None]
        bL = b[..., -1]
        g = bL[..., None] - b + ic
        m_new = jnp.maximum(bL + m, jnp.max(g, axis=-1))
        ws = jnp.exp(g - m_new[..., None])
        dec = jnp.exp(bL + m - m_new)
        C_new = dec[..., None, None] * C + jnp.einsum('bhs,bhsi,bhsj->bhij', ws, vc, kc)
        n_new = dec[..., None] * n + jnp.einsum('bhs,bhsj->bhj', ws, kc)
        return (C_new, n_new, m_new), h

    carry0 = (C0.astype(f32), n0.astype(f32), m0.astype(f32))
    (C_T, n_T, m_T), h = lax.scan(step, carry0, (ch4(q), ch4(k), ch4(v), ch3(i_pre), ch3(logf)))
    h = h.transpose(1, 0, 3, 2, 4).reshape(B, T, H, d)
    return h, C_T, n_T, m_T


def mixer_layer(x, c, shift0, S0, conv0, C0, n0, m0,
                g_norm, w_ada, b_ada, w_in, mu_shift, w_decay2, w0, w_iclr2, a0, k_k, k_a, r_k,
                ln_w, ln_b, conv_w, conv_b, b_i, b_f, gn_w, w_up_a, w_up_b, w_out):
    B, T, _ = x.shape
    f32 = jnp.float32
    ada_shift, ada_scale, ada_gate = jnp.split(jax.nn.silu(c) @ w_ada + b_ada, 3, axis=-1)
    h = rmsnorm(x, g_norm) * (1.0 + ada_scale[:, None]) + ada_shift[:, None]
    p = h @ w_in
    p_shift, z_a, qk_pre, v_b, o_b, i_b, f_b, z_b, gl_a, gl_b = split_cols(p, IN_SIZES)

    prev = jnp.concatenate([shift0[:, None].astype(p.dtype), p_shift[:, :-1]], axis=1)
    ps = p_shift + mu_shift * (prev - p_shift)
    r, k, v, wl, al = split_cols(ps, SHIFT_SIZES)
    y_a, S_T = rwkv7_mix(r, k, v, wl, al, S0, w_decay2, w0, w_iclr2, a0, k_k, k_a, r_k, ln_w, ln_b)
    out_a = (y_a * jax.nn.silu(z_a.astype(f32))).astype(x.dtype)

    buf = jnp.concatenate([conv0.astype(qk_pre.dtype), qk_pre], axis=1)
    conv = conv_b + sum(buf[:, j:j + T] * conv_w[j] for j in range(CONV_W))
    qk = jax.nn.silu(conv)
    q_b, k_b = jnp.split(qk, 2, axis=-1)
    hds = lambda t: t.reshape(B, T, H_B, HEAD_DIM)
    logf = jax.nn.log_sigmoid((f_b + b_f).astype(f32))
    i_pre = (i_b + b_i).astype(f32)
    hb, C_T, n_T, m_T = mlstm_chunkwise(hds(q_b), hds(k_b), hds(v_b), i_pre, logf, C0, n0, m0)
    hb = head_rmsnorm(hb, gn_w.astype(f32), MLSTM_GN_EPS).reshape(B, T, D_B)
    out_b = (jax.nn.sigmoid(o_b.astype(f32)) * hb * jax.nn.silu(z_b.astype(f32))).astype(x.dtype)

    merged = jax.nn.sigmoid(gl_a) * (out_a @ w_up_a) + jax.nn.sigmoid(gl_b) * (out_b @ w_up_b)
    x = x + ada_gate[:, None] * (merged @ w_out)
    conv_T = buf[:, buf.shape[1] - (CONV_W - 1):]
    return x, (p_shift[:, -1], S_T, conv_T, C_T, n_T, m_T)


def trunk(x, c, states, params, g_final):
    new = [[] for _ in range(len(states))]
    for l in range(DEPTH):
        st = tuple(s[l] for s in states)
        pl = tuple(w[l] for w in params)
        x, st_new = mixer_layer(x, c, *st, *pl)
        for lst, s in zip(new, st_new):
            lst.append(s.astype(x.dtype))
    return rmsnorm(x, g_final), tuple(jnp.stack(lst) for lst in new)


def setup_inputs(seed: int = 0) -> dict:
    key = jax.random.key(seed)
    ks = iter(jax.random.split(key, 48))
    f32 = jnp.float32
    nrm = lambda shape, s: jax.random.normal(next(ks), shape, f32) * s
    L = DEPTH
    return {
        "x_prompt": nrm((BATCH, SEQ, D_MODEL), 1.0),
        "x_sample": nrm((DEC_BATCH, DEC_SEQ, D_MODEL), 1.0),
        "c_prompt": nrm((BATCH, D_MODEL), 1.0),
        "c_sample": nrm((DEC_BATCH, D_MODEL), 1.0),
        "state_rwkv_shift": nrm((L, DEC_BATCH, SHIFT_W), 1.0),
        "state_rwkv_S": nrm((L, DEC_BATCH, H_A, HEAD_DIM, HEAD_DIM), 0.1),
        "state_mlstm_conv": nrm((L, DEC_BATCH, CONV_W - 1, CONV_COLS), 1.0),
        "state_mlstm_C": nrm((L, DEC_BATCH, H_B, HEAD_DIM, HEAD_DIM), 0.1),
        "state_mlstm_n": nrm((L, DEC_BATCH, H_B, HEAD_DIM), 0.1),
        "state_mlstm_m": nrm((L, DEC_BATCH, H_B), 1.0),
        "g_norm": 1.0 + nrm((L, D_MODEL), 0.01),
        "w_ada": nrm((L, D_MODEL, 3 * D_MODEL), 0.5 * D_MODEL ** -0.5),
        "b_ada": nrm((L, 3 * D_MODEL), 0.02),
        "w_in": nrm((L, D_MODEL, N_IN), D_MODEL ** -0.5),
        "mu_shift": jax.random.uniform(next(ks), (L, SHIFT_W), f32),
        "w_decay2": nrm((L, LORA_W, D_A), 0.1),
        "w0": jnp.linspace(-2.5, 1.5, D_A, dtype=f32)[None] + nrm((L, D_A), 0.1),
        "w_iclr2": nrm((L, LORA_A, D_A), 0.1),
        "a0": nrm((L, D_A), 0.1),
        "k_k": 0.85 + nrm((L, D_A), 0.02),
        "k_a": 1.0 + nrm((L, D_A), 0.02),
        "r_k": nrm((L, D_A), 0.1),
        "ln_w": 1.0 + nrm((L, D_A), 0.02),
        "ln_b": nrm((L, D_A), 0.02),
        "conv_w": nrm((L, CONV_W, CONV_COLS), CONV_W ** -0.5),
        "conv_b": nrm((L, CONV_COLS), 0.02),
        "b_i": nrm((L, H_B), 0.1),
        "b_f": jnp.linspace(3.0, 6.0, H_B, dtype=f32)[None] + nrm((L, H_B), 0.1),
        "gn_w": 1.0 + nrm((L, D_B), 0.02),
        "w_up_a": nrm((L, D_A, D_MODEL), D_A ** -0.5),
        "w_up_b": nrm((L, D_B, D_MODEL), D_B ** -0.5),
        "w_out": nrm((L, D_MODEL, D_MODEL), D_MODEL ** -0.5),
        "g_final": 1.0 + nrm((D_MODEL,), 0.01),
    }


def reference(x_prompt, x_sample, c_prompt, c_sample,
              state_rwkv_shift, state_rwkv_S, state_mlstm_conv, state_mlstm_C, state_mlstm_n, state_mlstm_m,
              g_norm, w_ada, b_ada, w_in, mu_shift, w_decay2, w0, w_iclr2, a0, k_k, k_a, r_k,
              ln_w, ln_b, conv_w, conv_b, b_i, b_f, gn_w, w_up_a, w_up_b, w_out, g_final):
    params = (g_norm, w_ada, b_ada, w_in, mu_shift, w_decay2, w0, w_iclr2, a0, k_k, k_a, r_k,
              ln_w, ln_b, conv_w, conv_b, b_i, b_f, gn_w, w_up_a, w_up_b, w_out)
    Bp = x_prompt.shape[0]
    dt = x_prompt.dtype
    prompt_states = (
        jnp.zeros((DEPTH, Bp, SHIFT_W), dt),
        jnp.zeros((DEPTH, Bp, H_A, HEAD_DIM, HEAD_DIM), dt),
        jnp.zeros((DEPTH, Bp, CONV_W - 1, CONV_COLS), dt),
        jnp.zeros((DEPTH, Bp, H_B, HEAD_DIM, HEAD_DIM), dt),
        jnp.zeros((DEPTH, Bp, H_B, HEAD_DIM), dt),
        jnp.zeros((DEPTH, Bp, H_B), dt),
    )
    sample_states = (state_rwkv_shift, state_rwkv_S, state_mlstm_conv,
                     state_mlstm_C, state_mlstm_n, state_mlstm_m)
    y_prompt, (p_shift, p_S, p_conv, p_C, p_n, p_m) = trunk(x_prompt, c_prompt, prompt_states, params, g_final)
    y_sample, (s_shift, s_S, s_conv, s_C, s_n, s_m) = trunk(x_sample, c_sample, sample_states, params, g_final)
    return (y_prompt, y_sample, p_shift, p_S, p_conv, p_C, p_n, p_m, s_shift, s_S, s_conv, s_C, s_n, s_m)
```

```python
import functools
import math

import jax
import jax.numpy as jnp
from jax import lax
from jax.experimental import pallas as pl
from jax.experimental.pallas import tpu as pltpu

F32 = jnp.float32
BF16 = jnp.bfloat16

HEAD_DIM = 64
NORM_EPS = 1e-6
RWKV_GN_EPS = 64e-5
MLSTM_GN_EPS = 1e-6
MLSTM_CHUNK = 64
SUBLANES = 8
LANES = 128
TRI_BASE = 16
VMEM_LIMIT = 56 * 1024 * 1024

NT_DIMS = (((1,), (1,)), ((), ()))
TN_DIMS = (((0,), (0,)), ((), ()))


def _mm(a, b):
    return jnp.dot(a.astype(BF16), b.astype(BF16), preferred_element_type=F32)


def _mm_nt(a, b):
    return lax.dot_general(a.astype(BF16), b.astype(BF16), NT_DIMS, preferred_element_type=F32)


def _mm_tn(a, b):
    return lax.dot_general(a.astype(BF16), b.astype(BF16), TN_DIMS, preferred_element_type=F32)


def _sigmoid(x):
    return 1.0 / (1.0 + jnp.exp(-x))


def _silu(x):
    return x * _sigmoid(x)


def _softplus(x):
    return jnp.maximum(x, 0.0) + jnp.log(1.0 + jnp.exp(-jnp.abs(x)))


def _cumsum_time(x, tri_bf16):
    hi = x.astype(BF16)
    r1 = x - hi.astype(F32)
    mid = r1.astype(BF16)
    lo = (r1 - mid.astype(F32)).astype(BF16)
    dot = functools.partial(jnp.dot, preferred_element_type=F32)
    return dot(tri_bf16, hi) + dot(tri_bf16, mid) + dot(tri_bf16, lo)


def _shr(x, n):
    return lax.shift_right_logical(x, jnp.full(x.shape, n, x.dtype))


def _tri_inverse(a, row, col):
    L = a.shape[0]
    bs = min(TRI_BASE, L)
    sh = int(math.log2(bs))
    same = _shr(row, sh) == _shr(col, sh)
    eye = (row == col).astype(F32)
    n = jnp.where(same, -a, 0.0)
    t = eye + n
    p = n
    k = 1
    while 2 * k < bs:
        p = _mm(p, p)
        t = t + _mm(t, p)
        k *= 2
    size = bs
    while size < L:
        sh = int(math.log2(size))
        off = jnp.where((_shr(row, sh + 1) == _shr(col, sh + 1)) & (_shr(row, sh) != _shr(col, sh)), a, 0.0)
        t = t - _mm(_mm(t, off), t)
        size *= 2
    return t


def _ada_kernel(c_ref, w_ref, b_ref, o_ref):
    o_ref[...] = _mm(_silu(c_ref[...]), w_ref[...]) + b_ref[...]


def _ada(c, w_ada, b_ada):
    n, d = c.shape
    n3 = w_ada.shape[1]
    tn = d
    return pl.pallas_call(
        _ada_kernel,
        grid=(n3 // tn,),
        in_specs=[pl.BlockSpec((n, d), lambda j: (0, 0)),
                  pl.BlockSpec((d, tn), lambda j: (0, j)),
                  pl.BlockSpec((1, tn), lambda j: (0, j))],
        out_specs=pl.BlockSpec((n, tn), lambda j: (0, j)),
        out_shape=jax.ShapeDtypeStruct((n, n3), F32),
        name="ada",
    )(c, w_ada, b_ada.reshape(1, n3))


def _inproj_kernel(x_ref, sc_ref, sh_ref, g_ref, *refs, n_seg):
    w_refs, o_refs = refs[:n_seg], refs[n_seg:]
    x = x_ref[...]
    nb, tt, d = x.shape
    y = x * lax.rsqrt(jnp.mean(x * x, axis=-1, keepdims=True) + NORM_EPS) * g_ref[...]
    h = y * (1.0 + sc_ref[...]) + sh_ref[...]
    h = h.reshape(nb * tt, d).astype(BF16)
    for w_ref, o_ref in zip(w_refs, o_refs):
        o_ref[...] = jnp.dot(h, w_ref[...], preferred_element_type=F32).reshape(o_ref.shape)


def _inproj(x, scale, shift, g_norm, w_segs, nb, tt):
    b, t, d = x.shape
    n_seg = len(w_segs)
    const = lambda i, j: (0, 0)
    in_specs = [pl.BlockSpec((nb, tt, d), lambda i, j: (i, j, 0)),
                pl.BlockSpec((nb, 1, d), lambda i, j: (i, 0, 0)),
                pl.BlockSpec((nb, 1, d), lambda i, j: (i, 0, 0)),
                pl.BlockSpec((1, d), const)]
    in_specs += [pl.BlockSpec(w.shape, const, pipeline_mode=pl.Buffered(1)) for w in w_segs]
    out_specs = [pl.BlockSpec((nb, tt, w.shape[1]), lambda i, j: (i, j, 0)) for w in w_segs]
    out_shape = [jax.ShapeDtypeStruct((b, t, w.shape[1]), F32) for w in w_segs]
    return pl.pallas_call(
        functools.partial(_inproj_kernel, n_seg=n_seg),
        grid=(b // nb, t // tt),
        in_specs=in_specs, out_specs=out_specs, out_shape=out_shape,
        compiler_params=pltpu.CompilerParams(vmem_limit_bytes=VMEM_LIMIT),
        name="inproj",
    )(x, scale, shift, g_norm.reshape(1, d), *w_segs)


def _rwkv_kernel(ps_ref, za_ref, shift0_ref, s0_ref, mu_ref, wd2_ref, w0_ref, wi2_ref, a0_ref,
                 kk_ref, ka_ref, rk_ref, lnw_ref, lnb_ref,
                 oa_ref, st_ref, prev_scr, s_scr, *, n_heads, lora_w):
    j = pl.program_id(1)
    L = ps_ref.shape[1]
    da = n_heads * HEAD_DIM

    @pl.when(j == 0)
    def _():
        prev_scr[...] = shift0_ref[0]
        s_scr[...] = s0_ref[0]

    p = ps_ref[0]
    row_w = lax.broadcasted_iota(jnp.int32, p.shape, 0)
    prev = jnp.where(row_w == 0, prev_scr[...], pltpu.roll(p, 1, axis=0))
    prev_scr[...] = p[L - 1:L, :]
    ps = p + mu_ref[...] * (prev - p)

    r = ps[:, 0:da]
    k = ps[:, da:2 * da]
    v = ps[:, 2 * da:3 * da]
    wl = ps[:, 3 * da:3 * da + lora_w]
    al = ps[:, 3 * da + lora_w:]

    w = w0_ref[...] + _mm(jnp.tanh(wl), wd2_ref[...])
    lw = -jnp.exp(-_softplus(-w) - 0.5)
    a = _sigmoid(a0_ref[...] + _mm(al, wi2_ref[...]))
    kkr = k * kk_ref[...]
    k2 = k * (1.0 + (a - 1.0) * ka_ref[...])
    rkk = r * k2 * rk_ref[...]

    row = lax.broadcasted_iota(jnp.int32, (L, L), 0)
    col = lax.broadcasted_iota(jnp.int32, (L, L), 1)
    incl = row >= col
    strict = row > col
    tri = incl.astype(BF16)

    c = _cumsum_time(lw, tri)
    c_last = c[L - 1:L, :]
    e_c = jnp.exp(c)
    e_ce = jnp.exp(c - lw)
    e_nc = jnp.exp(-c)
    e_cl = jnp.exp(c_last - c)
    p_last = jnp.exp(c_last)

    outs = []
    for h in range(n_heads):
        sl = slice(h * HEAD_DIM, (h + 1) * HEAD_DIM)
        kkr_h = kkr[:, sl]
        kk_h = kkr_h / jnp.maximum(jnp.sqrt(jnp.sum(kkr_h * kkr_h, axis=-1, keepdims=True)), 1e-12)
        b_h = kk_h * a[:, sl]
        r_h, k_h, v_h = r[:, sl], k2[:, sl], v[:, sl]
        kq = kk_h * e_ce[:, sl]
        rq = r_h * e_c[:, sl]
        bk = b_h * e_nc[:, sl]
        kk = k_h * e_nc[:, sl]
        bk2 = b_h * e_cl[:, sl]
        kk2 = k_h * e_cl[:, sl]

        a_ab = jnp.where(strict, _mm_nt(kq, bk), 0.0)
        a_ak = jnp.where(strict, _mm_nt(kq, kk), 0.0)
        m_rb = jnp.where(incl, _mm_nt(rq, bk), 0.0)
        m_rk = jnp.where(incl, _mm_nt(rq, kk), 0.0)
        t_inv = _tri_inverse(a_ab, row, col)

        s0 = s_scr[h]
        u = -_mm(t_inv, _mm_nt(kq, s0) + _mm(a_ak, v_h))
        y = _mm_nt(rq, s0) + _mm(m_rb, u) + _mm(m_rk, v_h)
        s_scr[h] = s0 * p_last[:, sl] + _mm_tn(u, bk2) + _mm_tn(v_h, kk2)

        mu_y = jnp.mean(y, axis=-1, keepdims=True)
        yc = y - mu_y
        var = jnp.mean(yc * yc, axis=-1, keepdims=True)
        yn = yc * lax.rsqrt(var + RWKV_GN_EPS) * lnw_ref[:, sl] + lnb_ref[:, sl]
        bonus = jnp.sum(rkk[:, sl], axis=-1, keepdims=True)
        outs.append(yn + bonus * v_h)

    y_all = jnp.concatenate(outs, axis=-1)
    oa_ref[0] = (y_all * _silu(za_ref[0])).astype(oa_ref.dtype)

    @pl.when(j == pl.num_programs(1) - 1)
    def _():
        st_ref[0] = s_scr[...]


def _rwkv(p_shift, z_a, shift0, s0, rows, wd2, wi2, L):
    b, t, sw = p_shift.shape
    n_heads = s0.shape[1]
    da = n_heads * HEAD_DIM
    const2 = lambda i, j: (0, 0)
    row_specs = [pl.BlockSpec(x.shape, const2) for x in rows]
    mu, w0, a0, k_k, k_a, r_k, ln_w, ln_b = rows
    in_specs = [pl.BlockSpec((1, L, sw), lambda i, j: (i, j, 0)),
                pl.BlockSpec((1, L, da), lambda i, j: (i, j, 0)),
                pl.BlockSpec((1, 1, sw), lambda i, j: (i, 0, 0)),
                pl.BlockSpec((1, n_heads, HEAD_DIM, HEAD_DIM), lambda i, j: (i, 0, 0, 0)),
                row_specs[0], pl.BlockSpec(wd2.shape, const2), row_specs[1],
                pl.BlockSpec(wi2.shape, const2)] + row_specs[2:]
    out_specs = [pl.BlockSpec((1, L, da), lambda i, j: (i, j, 0)),
                 pl.BlockSpec((1, n_heads, HEAD_DIM, HEAD_DIM), lambda i, j: (i, 0, 0, 0))]
    out_shape = [jax.ShapeDtypeStruct((b, t, da), BF16),
                 jax.ShapeDtypeStruct(s0.shape, F32)]
    return pl.pallas_call(
        functools.partial(_rwkv_kernel, n_heads=n_heads, lora_w=wd2.shape[0]),
        grid=(b, t // L),
        in_specs=in_specs, out_specs=out_specs, out_shape=out_shape,
        scratch_shapes=[pltpu.VMEM((1, sw), F32),
                        pltpu.VMEM((n_heads, HEAD_DIM, HEAD_DIM), F32)],
        compiler_params=pltpu.CompilerParams(vmem_limit_bytes=VMEM_LIMIT),
        name="rwkv7",
    )(p_shift, z_a, shift0, s0, mu, wd2, w0, wi2, a0, k_k, k_a, r_k, ln_w, ln_b)


def _mlstm_kernel(qk_ref, v_ref, g_ref, o_ref, zb_ref, conv0_ref, c0_ref, n0_ref, m0_ref,
                  cw_ref, cb_ref, gb_ref, gnw_ref,
                  ob_ref, ct_ref, nt_ref, mt_ref, xbuf, cn_scr, m_scr, *, n_heads, conv_w):
    j = pl.program_id(1)
    L = qk_ref.shape[1]
    db = n_heads * HEAD_DIM
    pad = SUBLANES

    @pl.when(j == 0)
    def _():
        xbuf[0:pad, :] = conv0_ref[0]
        cn_scr[:, 0:HEAD_DIM, :] = c0_ref[0]
        cn_scr[:, HEAD_DIM:, :] = n0_ref[0]
        m_scr[...] = m0_ref[0]

    xbuf[pad:pad + L, :] = qk_ref[0]
    conv = cb_ref[...]
    for tap in range(conv_w):
        conv = conv + xbuf[pl.ds(pad - (conv_w - 1) + tap, L), :] * cw_ref[tap:tap + 1, :]
    xbuf[0:pad, :] = xbuf[L:L + pad, :]
    qk = _silu(conv)
    q = qk[:, 0:db]
    k = qk[:, db:] * (1.0 / math.sqrt(HEAD_DIM))
    v = v_ref[0]

    row = lax.broadcasted_iota(jnp.int32, (L, L), 0)
    col = lax.broadcasted_iota(jnp.int32, (L, L), 1)
    incl = row >= col
    eye = row == col
    tri = incl.astype(BF16)

    g = g_ref[0] + gb_ref[...]
    bcum = _cumsum_time(-_softplus(-g), tri)
    ones = jnp.ones((L, SUBLANES), F32)

    outs = []
    for h in range(n_heads):
        sl = slice(h * HEAD_DIM, (h + 1) * HEAD_DIM)
        i_col = g[:, h:h + 1]
        b_col = bcum[:, n_heads + h:n_heads + h + 1]
        m_prev = m_scr[h:h + 1, 0:1]
        x_col = i_col - b_col
        x_row = jnp.sum(jnp.where(eye, x_col, 0.0), axis=0, keepdims=True)
        d = jnp.where(incl, b_col + x_row, -jnp.inf)
        inter = b_col + m_prev
        m_t = jnp.maximum(jnp.max(d, axis=-1, keepdims=True), inter)
        w_ts = jnp.exp(d - m_t)
        w_in = jnp.exp(inter - m_t)

        q_h, k_h = q[:, sl], k[:, sl]
        v1 = jnp.concatenate([v[:, sl], ones], axis=-1)
        s = _mm_nt(q_h, k_h) * w_ts
        cn = cn_scr[h]
        numden = _mm(s, v1) + w_in * _mm_nt(q_h, cn)
        den = numden[:, HEAD_DIM:HEAD_DIM + 1]
        hh = numden[:, 0:HEAD_DIM] / jnp.maximum(jnp.abs(den), jnp.exp(-m_t))

        b_last = b_col[L - 1:L, :]
        g_col = b_last - b_col + i_col
        m_new = jnp.maximum(b_last + m_prev, jnp.max(g_col, axis=0, keepdims=True))
        ws = jnp.exp(g_col - m_new)
        dec = jnp.exp(b_last + m_prev - m_new)
        cn_scr[h] = dec * cn + _mm_tn(v1 * ws, k_h)
        m_scr[h:h + 1, :] = jnp.broadcast_to(m_new, (1, LANES))

        hn = hh * lax.rsqrt(jnp.mean(hh * hh, axis=-1, keepdims=True) + MLSTM_GN_EPS) * gnw_ref[:, sl]
        outs.append(hn)

    hb = jnp.concatenate(outs, axis=-1)
    ob_ref[0] = (_sigmoid(o_ref[0]) * hb * _silu(zb_ref[0])).astype(ob_ref.dtype)

    @pl.when(j == pl.num_programs(1) - 1)
    def _():
        ct_ref[0] = cn_scr[:, 0:HEAD_DIM, :]
        nt_ref[0] = cn_scr[:, HEAD_DIM:, :]
        mt_ref[0] = m_scr[...]


def _mlstm(qk_pre, v_b, gates, o_b, z_b, conv0, c0, n0, m0, conv_w, conv_b, gate_b, gn_w, L):
    b, t, cc = qk_pre.shape
    n_heads = c0.shape[1]
    db = n_heads * HEAD_DIM
    cw = conv_w.shape[0]
    tok = lambda i, j: (i, j, 0)
    per_b3 = lambda i, j: (i, 0, 0)
    per_b4 = lambda i, j: (i, 0, 0, 0)
    const2 = lambda i, j: (0, 0)
    in_specs = [pl.BlockSpec((1, L, cc), tok),
                pl.BlockSpec((1, L, db), tok),
                pl.BlockSpec((1, L, LANES), tok),
                pl.BlockSpec((1, L, db), tok),
                pl.BlockSpec((1, L, db), tok),
                pl.BlockSpec((1, SUBLANES, cc), per_b3),
                pl.BlockSpec((1, n_heads, HEAD_DIM, HEAD_DIM), per_b4),
                pl.BlockSpec((1, n_heads, SUBLANES, HEAD_DIM), per_b4),
                pl.BlockSpec((1, n_heads, LANES), per_b3),
                pl.BlockSpec(conv_w.shape, const2),
                pl.BlockSpec(conv_b.shape, const2),
                pl.BlockSpec(gate_b.shape, const2),
                pl.BlockSpec(gn_w.shape, const2)]
    out_specs = [pl.BlockSpec((1, L, db), tok),
                 pl.BlockSpec((1, n_heads, HEAD_DIM, HEAD_DIM), per_b4),
                 pl.BlockSpec((1, n_heads, SUBLANES, HEAD_DIM), per_b4),
                 pl.BlockSpec((1, n_heads, LANES), per_b3)]
    out_shape = [jax.ShapeDtypeStruct((b, t, db), BF16),
                 jax.ShapeDtypeStruct(c0.shape, F32),
                 jax.ShapeDtypeStruct(n0.shape, F32),
                 jax.ShapeDtypeStruct(m0.shape, F32)]
    return pl.pallas_call(
        functools.partial(_mlstm_kernel, n_heads=n_heads, conv_w=cw),
        grid=(b, t // L),
        in_specs=in_specs, out_specs=out_specs, out_shape=out_shape,
        scratch_shapes=[pltpu.VMEM((L + SUBLANES, cc), F32),
                        pltpu.VMEM((n_heads, HEAD_DIM + SUBLANES, HEAD_DIM), F32),
                        pltpu.VMEM((n_heads, LANES), F32)],
        compiler_params=pltpu.CompilerParams(vmem_limit_bytes=VMEM_LIMIT),
        name="mlstm",
    )(qk_pre, v_b, gates, o_b, z_b, conv0, c0, n0, m0, conv_w, conv_b, gate_b, gn_w)


def _out_kernel(oa_ref, ob_ref, gla_ref, glb_ref, x_ref, gate_ref, wua_ref, wub_ref, wo_ref, gf_ref,
                y_ref, *, final_norm):
    nb, tt, d = x_ref.shape
    m = nb * tt
    oa = oa_ref[...].reshape(m, oa_ref.shape[-1])
    ob = ob_ref[...].reshape(m, ob_ref.shape[-1])
    ua = jnp.dot(oa, wua_ref[...], preferred_element_type=F32)
    ub = jnp.dot(ob, wub_ref[...], preferred_element_type=F32)
    merged = _sigmoid(gla_ref[...].reshape(m, d)) * ua + _sigmoid(glb_ref[...].reshape(m, d)) * ub
    mo = jnp.dot(merged.astype(BF16), wo_ref[...], preferred_element_type=F32)
    xn = x_ref[...] + gate_ref[...] * mo.reshape(nb, tt, d)
    if final_norm:
        xn = xn * lax.rsqrt(jnp.mean(xn * xn, axis=-1, keepdims=True) + NORM_EPS) * gf_ref[...]
    y_ref[...] = xn


def _out(out_a, out_b, gl_a, gl_b, x, gate, w_up_a, w_up_b, w_out, g_final, nb, tt, final_norm):
    b, t, d = x.shape
    tok = lambda i, j: (i, j, 0)
    const2 = lambda i, j: (0, 0)
    wspec = lambda w: pl.BlockSpec(w.shape, const2, pipeline_mode=pl.Buffered(1))
    in_specs = [pl.BlockSpec((nb, tt, out_a.shape[-1]), tok),
                pl.BlockSpec((nb, tt, out_b.shape[-1]), tok),
                pl.BlockSpec((nb, tt, d), tok),
                pl.BlockSpec((nb, tt, d), tok),
                pl.BlockSpec((nb, tt, d), tok),
                pl.BlockSpec((nb, 1, d), lambda i, j: (i, 0, 0)),
                wspec(w_up_a), wspec(w_up_b), wspec(w_out),
                pl.BlockSpec((1, d), const2)]
    return pl.pallas_call(
        functools.partial(_out_kernel, final_norm=final_norm),
        grid=(b // nb, t // tt),
        in_specs=in_specs,
        out_specs=pl.BlockSpec((nb, tt, d), tok),
        out_shape=jax.ShapeDtypeStruct((b, t, d), F32),
        compiler_params=pltpu.CompilerParams(vmem_limit_bytes=VMEM_LIMIT),
        name="merge_out",
    )(out_a, out_b, gl_a, gl_b, x, gate, w_up_a, w_up_b, w_out, g_final.reshape(1, d))


def _token_tiling(b, t, target=256):
    if t >= target:
        tt = target
        while t % tt:
            tt //= 2
        return 1, tt
    nb = max(1, min(b, target // t))
    while b % nb:
        nb -= 1
    return nb, t


def _layer(x, mod, states, lp, g_final, final_norm):
    b, t, d = x.shape
    shift0, s0, conv0, c0, n0, m0 = states
    n_ha, n_hb = s0.shape[1], c0.shape[1]
    da, db = n_ha * HEAD_DIM, n_hb * HEAD_DIM
    sw = shift0.shape[-1]
    cc = conv0.shape[-1]
    cw = lp["conv_w"].shape[0]

    ada_shift, ada_scale, ada_gate = (mod[:, None, i * d:(i + 1) * d] for i in range(3))
    nb, tt = _token_tiling(b, t)
    p_shift, z_a, qk_pre, v_b, o_b, z_b, gl_a, gl_b, gates = _inproj(
        x, ada_scale, ada_shift, lp["g_norm"], lp["w_segs"], nb, tt)

    L = math.gcd(t, MLSTM_CHUNK)
    row = lambda a: a.reshape(1, -1)
    rows = tuple(row(lp[n]) for n in ("mu_shift", "w0", "a0", "k_k", "k_a", "r_k", "ln_w", "ln_b"))
    out_a, s_t = _rwkv(p_shift, z_a, shift0[:, None, :], s0, rows, lp["w_decay2"], lp["w_iclr2"], L)

    conv0p = jnp.pad(conv0, ((0, 0), (SUBLANES - (cw - 1), 0), (0, 0)))
    n0p = jnp.broadcast_to(n0[:, :, None, :], (b, n_hb, SUBLANES, HEAD_DIM))
    m0p = jnp.broadcast_to(m0[:, :, None], (b, n_hb, LANES))
    gate_b = jnp.zeros((1, LANES), F32).at[0, 0:n_hb].set(lp["b_i"]).at[0, n_hb:2 * n_hb].set(lp["b_f"])
    out_b, c_t, n_t, m_t = _mlstm(qk_pre, v_b, gates, o_b, z_b, conv0p, c0, n0p, m0p,
                                  lp["conv_w"], row(lp["conv_b"]), gate_b, row(lp["gn_w"]), L)

    y = _out(out_a, out_b, gl_a, gl_b, x, ada_gate, lp["w_up_a"], lp["w_up_b"], lp["w_out"],
             g_final, nb, tt, final_norm)
    conv_t = jnp.concatenate([conv0, qk_pre], axis=1)[:, t:] if t < cw - 1 else qk_pre[:, t - (cw - 1):]
    new_states = (p_shift[:, t - 1], s_t, conv_t, c_t, n_t[:, :, 0, :], m_t[:, :, 0])
    return y, new_states


def _trunk(x, mods, states, layers, g_final):
    depth = len(layers)
    new = [[] for _ in states]
    for l in range(depth):
        st = tuple(s[l] for s in states)
        x, st_new = _layer(x, mods[l], st, layers[l], g_final, final_norm=(l == depth - 1))
        for lst, s in zip(new, st_new):
            lst.append(s.astype(x.dtype))
    return x, tuple(jnp.stack(lst) for lst in new)


def kernel(x_prompt, x_sample, c_prompt, c_sample, state_rwkv_shift, state_rwkv_S, state_mlstm_conv, state_mlstm_C, state_mlstm_n, state_mlstm_m, g_norm, w_ada, b_ada, w_in, mu_shift, w_decay2, w0, w_iclr2, a0, k_k, k_a, r_k, ln_w, ln_b, conv_w, conv_b, b_i, b_f, gn_w, w_up_a, w_up_b, w_out, g_final):
    depth = g_norm.shape[0]
    bp, bs = x_prompt.shape[0], x_sample.shape[0]
    d = x_prompt.shape[-1]
    da, db = w_up_a.shape[1], w_up_b.shape[1]
    n_ha, n_hb = da // HEAD_DIM, db // HEAD_DIM
    sw = mu_shift.shape[-1]
    cc = conv_w.shape[-1]
    cw = conv_w.shape[1]
    dt = x_prompt.dtype

    sizes = (sw, da, cc, db, db, n_hb, n_hb, db, d, d)
    offs = [0]
    for s in sizes:
        offs.append(offs[-1] + s)
    seg = lambda w, i: w[:, offs[i]:offs[i + 1]]

    c_all = jnp.concatenate([c_prompt, c_sample], axis=0)
    n_c = c_all.shape[0]
    c_pad = jnp.pad(c_all, ((0, (-n_c) % SUBLANES), (0, 0)))

    layers, mods_p, mods_s = [], [], []
    for l in range(depth):
        w = w_in[l]
        wg = jnp.concatenate([seg(w, 5), seg(w, 6)], axis=1)
        wg = jnp.pad(wg, ((0, 0), (0, LANES - wg.shape[1])))
        w_segs = tuple(x.astype(BF16) for x in
                       (seg(w, 0), seg(w, 1), seg(w, 2), seg(w, 3), seg(w, 4), seg(w, 7), seg(w, 8), seg(w, 9), wg))
        layers.append(dict(
            g_norm=g_norm[l], w_segs=w_segs, mu_shift=mu_shift[l], w_decay2=w_decay2[l].astype(BF16),
            w0=w0[l], w_iclr2=w_iclr2[l].astype(BF16), a0=a0[l], k_k=k_k[l], k_a=k_a[l], r_k=r_k[l],
            ln_w=ln_w[l], ln_b=ln_b[l], conv_w=conv_w[l], conv_b=conv_b[l], b_i=b_i[l], b_f=b_f[l],
            gn_w=gn_w[l], w_up_a=w_up_a[l].astype(BF16), w_up_b=w_up_b[l].astype(BF16),
            w_out=w_out[l].astype(BF16)))
        mod = _ada(c_pad, w_ada[l], b_ada[l])
        mods_p.append(mod[:bp])
        mods_s.append(mod[bp:bp + bs])

    prompt_states = (
        jnp.zeros((depth, bp, sw), dt),
        jnp.zeros((depth, bp, n_ha, HEAD_DIM, HEAD_DIM), dt),
        jnp.zeros((depth, bp, cw - 1, cc), dt),
        jnp.zeros((depth, bp, n_hb, HEAD_DIM, HEAD_DIM), dt),
        jnp.zeros((depth, bp, n_hb, HEAD_DIM), dt),
        jnp.zeros((depth, bp, n_hb), dt),
    )
    sample_states = (state_rwkv_shift, state_rwkv_S, state_mlstm_conv,
                     state_mlstm_C, state_mlstm_n, state_mlstm_m)
    y_p, st_p = _trunk(x_prompt, mods_p, prompt_states, layers, g_final)
    y_s, st_s = _trunk(x_sample, mods_s, sample_states, layers, g_final)
    return (y_p, y_s) + st_p + st_s
```

```python
import functools
import math

import jax
import jax.numpy as jnp
from jax import lax
from jax.experimental import pallas as pl
from jax.experimental.pallas import tpu as pltpu

F32 = jnp.float32
BF16 = jnp.bfloat16

HEAD_DIM = 64
NORM_EPS = 1e-6
RWKV_GN_EPS = 64e-5
MLSTM_GN_EPS = 1e-6
ROWS = 64
SUBLANES = 8
LANES = 128
TRI_BASE = 16
VMEM_LIMIT = 56 * 1024 * 1024

NT_DIMS = (((1,), (1,)), ((), ()))
TN_DIMS = (((0,), (0,)), ((), ()))


def _mm(a, b):
    return jnp.dot(a.astype(BF16), b.astype(BF16), preferred_element_type=F32)


def _mm_nt(a, b):
    return lax.dot_general(a.astype(BF16), b.astype(BF16), NT_DIMS, preferred_element_type=F32)


def _mm_tn(a, b):
    return lax.dot_general(a.astype(BF16), b.astype(BF16), TN_DIMS, preferred_element_type=F32)


def _sigmoid(x):
    return 1.0 / (1.0 + jnp.exp(-x))


def _silu(x):
    return x * _sigmoid(x)


def _softplus(x):
    return jnp.maximum(x, 0.0) + jnp.log(1.0 + jnp.exp(-jnp.abs(x)))


def _cumsum_time(x, tri_bf16):
    hi = x.astype(BF16)
    r1 = x - hi.astype(F32)
    mid = r1.astype(BF16)
    lo = (r1 - mid.astype(F32)).astype(BF16)
    dot = functools.partial(jnp.dot, preferred_element_type=F32)
    return dot(tri_bf16, hi) + dot(tri_bf16, mid) + dot(tri_bf16, lo)


def _shr(x, n):
    return lax.shift_right_logical(x, jnp.full(x.shape, n, x.dtype))


def _rows_of_seq(x, n_seq, L):
    if n_seq == 1:
        return x[0]
    return jnp.broadcast_to(x, (n_seq, L, x.shape[-1])).reshape(n_seq * L, x.shape[-1])


def _last_rows(x, n_seq):
    m, n = x.shape
    L = m // n_seq
    return x.reshape(n_seq, L, n)[:, L - 1:L, :]


def _tri_inverse_all(mats, row, col, L):
    bs = min(TRI_BASE, L)
    sh = int(math.log2(bs))
    same = _shr(row, sh) == _shr(col, sh)
    eye = (row == col).astype(F32)
    ns = [jnp.where(same, -a, 0.0) for a in mats]
    ts = [eye + n for n in ns]
    ps = ns
    k = 1
    while 2 * k < bs:
        ps = [_mm(p, p) for p in ps]
        ts = [t + _mm(t, p) for t, p in zip(ts, ps)]
        k *= 2
    size = bs
    while size < L:
        sh = int(math.log2(size))
        lower_left = (_shr(row, sh + 1) == _shr(col, sh + 1)) & (_shr(row, sh) != _shr(col, sh))
        offs = [jnp.where(lower_left, a, 0.0) for a in mats]
        tmp = [_mm(t, o) for t, o in zip(ts, offs)]
        ts = [t - _mm(x, t) for t, x in zip(ts, tmp)]
        size *= 2
    return ts


def _ada_kernel(c_ref, w_ref, b_ref, o_ref):
    o_ref[...] = _mm(_silu(c_ref[...]), w_ref[...]) + b_ref[...]


def _ada(c, w_ada, b_ada):
    n, d = c.shape
    n3 = w_ada.shape[1]
    tn = d
    return pl.pallas_call(
        _ada_kernel,
        grid=(n3 // tn,),
        in_specs=[pl.BlockSpec((n, d), lambda j: (0, 0)),
                  pl.BlockSpec((d, tn), lambda j: (0, j)),
                  pl.BlockSpec((1, tn), lambda j: (0, j))],
        out_specs=pl.BlockSpec((n, tn), lambda j: (0, j)),
        out_shape=jax.ShapeDtypeStruct((n, n3), F32),
        name="ada",
    )(c, w_ada, b_ada.reshape(1, n3))


def _inproj_kernel(x_ref, sc_ref, sh_ref, g_ref, *refs, n_seg):
    w_refs, o_refs = refs[:n_seg], refs[n_seg:]
    x = x_ref[...]
    nb, tt, d = x.shape
    y = x * lax.rsqrt(jnp.mean(x * x, axis=-1, keepdims=True) + NORM_EPS) * g_ref[...]
    h = y * (1.0 + sc_ref[...]) + sh_ref[...]
    h = h.reshape(nb * tt, d).astype(BF16)
    for w_ref, o_ref in zip(w_refs, o_refs):
        o_ref[...] = jnp.dot(h, w_ref[...], preferred_element_type=F32).reshape(o_ref.shape)


def _inproj(x, scale, shift, g_norm, w_segs, nb, tt):
    b, t, d = x.shape
    n_seg = len(w_segs)
    const = lambda i, j: (0, 0)
    in_specs = [pl.BlockSpec((nb, tt, d), lambda i, j: (i, j, 0)),
                pl.BlockSpec((nb, 1, d), lambda i, j: (i, 0, 0)),
                pl.BlockSpec((nb, 1, d), lambda i, j: (i, 0, 0)),
                pl.BlockSpec((1, d), const)]
    in_specs += [pl.BlockSpec(w.shape, const, pipeline_mode=pl.Buffered(1)) for w in w_segs]
    out_specs = [pl.BlockSpec((nb, tt, w.shape[1]), lambda i, j: (i, j, 0)) for w in w_segs]
    out_shape = [jax.ShapeDtypeStruct((b, t, w.shape[1]), F32) for w in w_segs]
    return pl.pallas_call(
        functools.partial(_inproj_kernel, n_seg=n_seg),
        grid=(b // nb, t // tt),
        in_specs=in_specs, out_specs=out_specs, out_shape=out_shape,
        compiler_params=pltpu.CompilerParams(vmem_limit_bytes=VMEM_LIMIT),
        name="inproj",
    )(x, scale, shift, g_norm.reshape(1, d), *w_segs)


def _rwkv_kernel(ps_ref, za_ref, shift0_ref, s0_ref, mu_ref, wd2_ref, w0_ref, wi2_ref, a0_ref,
                 kk_ref, ka_ref, rk_ref, lnw_ref, lnb_ref,
                 oa_ref, st_ref, prev_scr, s_scr, *, n_seq, n_heads, lora_w):
    j = pl.program_id(1)
    M = ps_ref.shape[0]
    L = M // n_seq
    da = n_heads * HEAD_DIM
    heads = range(n_heads)
    seqs = range(n_seq)

    @pl.when(j == 0)
    def _():
        prev_scr[...] = shift0_ref[...]
        s_scr[...] = s0_ref[...]

    p = ps_ref[...]
    row_w = lax.broadcasted_iota(jnp.int32, p.shape, 0)
    first = (row_w & (L - 1)) == 0
    prev = jnp.where(first, _rows_of_seq(prev_scr[...], n_seq, L), pltpu.roll(p, 1, axis=0))
    prev_scr[...] = _last_rows(p, n_seq)
    ps = p + mu_ref[...] * (prev - p)

    r = ps[:, 0:da]
    k = ps[:, da:2 * da]
    v = ps[:, 2 * da:3 * da]
    wl = ps[:, 3 * da:3 * da + lora_w]
    al = ps[:, 3 * da + lora_w:]

    w = w0_ref[...] + _mm(jnp.tanh(wl), wd2_ref[...])
    lw = -jnp.exp(-_softplus(-w) - 0.5)
    a = _sigmoid(a0_ref[...] + _mm(al, wi2_ref[...]))
    kkr = k * kk_ref[...]
    k2 = k * (1.0 + (a - 1.0) * ka_ref[...])
    rkk = r * k2 * rk_ref[...]

    row = lax.broadcasted_iota(jnp.int32, (M, M), 0)
    col = lax.broadcasted_iota(jnp.int32, (M, M), 1)
    same_seq = _shr(row, int(math.log2(L))) == _shr(col, int(math.log2(L)))
    incl = (row >= col) & same_seq
    strict = (row > col) & same_seq
    row2 = lax.broadcasted_iota(jnp.int32, (M, 2 * M), 0)
    col2 = lax.broadcasted_iota(jnp.int32, (M, 2 * M), 1) & (M - 1)
    same_seq2 = _shr(row2, int(math.log2(L))) == _shr(col2, int(math.log2(L)))
    incl2 = (row2 >= col2) & same_seq2
    strict2 = (row2 > col2) & same_seq2

    c = _cumsum_time(lw, incl.astype(BF16))
    c_last = _last_rows(c, n_seq)
    c_last_rows = _rows_of_seq(c_last, n_seq, L)
    e_c = jnp.exp(c)
    e_ce = jnp.exp(c - lw)
    e_nc = jnp.exp(-c)
    e_cl = jnp.exp(c_last_rows - c)
    p_last = jnp.exp(c_last)

    sls = [slice(h * HEAD_DIM, (h + 1) * HEAD_DIM) for h in heads]
    kq, rq, qr, bkk, bkk2 = [], [], [], [], []
    for sl in sls:
        kkr_h = kkr[:, sl]
        kk_h = kkr_h / jnp.maximum(jnp.sqrt(jnp.sum(kkr_h * kkr_h, axis=-1, keepdims=True)), 1e-12)
        b_h = kk_h * a[:, sl]
        kq.append(kk_h * e_ce[:, sl])
        rq.append(r[:, sl] * e_c[:, sl])
        qr.append(jnp.concatenate([kq[-1], rq[-1]], axis=0).astype(BF16))
        bkk.append(jnp.concatenate([b_h * e_nc[:, sl], k2[:, sl] * e_nc[:, sl]], axis=0).astype(BF16))
        bkk2.append([b_h * e_cl[:, sl], k2[:, sl] * e_cl[:, sl]])
    vs = [v[:, sl] for sl in sls]

    gs = [_mm_nt(qr[h], bkk[h]) for h in heads]
    ga = [jnp.where(strict2, g[:M, :], 0.0) for g in gs]
    gm = [jnp.where(incl2, g[M:, :], 0.0) for g in gs]
    t_inv = _tri_inverse_all([x[:, :M] for x in ga], row, col, L)

    s0 = [[s_scr[b, h] for h in heads] for b in seqs]
    if n_seq == 1:
        ws = [_mm_nt(qr[h], s0[0][h]) for h in heads]
        w1 = [x[:M] for x in ws]
        wr = [x[M:] for x in ws]
    else:
        w1, wr = [], []
        for h in heads:
            parts = [_mm_nt(jnp.concatenate([kq[h][b * L:(b + 1) * L], rq[h][b * L:(b + 1) * L]], axis=0),
                            s0[b][h]) for b in seqs]
            w1.append(jnp.concatenate([x[:L] for x in parts], axis=0))
            wr.append(jnp.concatenate([x[L:] for x in parts], axis=0))
    zeros = jnp.zeros((M, HEAD_DIM), F32)
    rhs = [w1[h] + _mm(ga[h], jnp.concatenate([zeros, vs[h]], axis=0)) for h in heads]
    us = [-_mm(t_inv[h], rhs[h]) for h in heads]
    uv = [jnp.concatenate([us[h], vs[h]], axis=0) for h in heads]
    ys = [wr[h] + _mm(gm[h], uv[h]) for h in heads]
    for h in heads:
        for b in seqs:
            rows = slice(b * L, (b + 1) * L)
            uv_b = jnp.concatenate([us[h][rows], vs[h][rows]], axis=0)
            bkk2_b = jnp.concatenate([bkk2[h][0][rows], bkk2[h][1][rows]], axis=0)
            s_scr[b, h] = s0[b][h] * p_last[b][:, sls[h]] + _mm_tn(uv_b, bkk2_b)

    outs = []
    for h in heads:
        sl = sls[h]
        y = ys[h]
        mu_y = jnp.mean(y, axis=-1, keepdims=True)
        yc = y - mu_y
        var = jnp.mean(yc * yc, axis=-1, keepdims=True)
        yn = yc * lax.rsqrt(var + RWKV_GN_EPS) * lnw_ref[:, sl] + lnb_ref[:, sl]
        bonus = jnp.sum(rkk[:, sl], axis=-1, keepdims=True)
        outs.append(yn + bonus * vs[h])

    y_all = jnp.concatenate(outs, axis=-1)
    oa_ref[...] = (y_all * _silu(za_ref[...])).astype(oa_ref.dtype)

    @pl.when(j == pl.num_programs(1) - 1)
    def _():
        st_ref[...] = s_scr[...]


def _rwkv(p_shift, z_a, shift0, s0, rows, wd2, wi2, n_seq, L):
    b = s0.shape[0]
    sw = p_shift.shape[-1]
    n_heads = s0.shape[1]
    da = n_heads * HEAD_DIM
    m = n_seq * L
    nj = p_shift.shape[0] // (b * L)
    tok = lambda i, j: (i * nj + j, 0)
    const2 = lambda i, j: (0, 0)
    row_specs = [pl.BlockSpec(x.shape, const2) for x in rows]
    mu, w0, a0, k_k, k_a, r_k, ln_w, ln_b = rows
    in_specs = [pl.BlockSpec((m, sw), tok),
                pl.BlockSpec((m, da), tok),
                pl.BlockSpec((n_seq, 1, sw), lambda i, j: (i, 0, 0)),
                pl.BlockSpec((n_seq, n_heads, HEAD_DIM, HEAD_DIM), lambda i, j: (i, 0, 0, 0)),
                row_specs[0], pl.BlockSpec(wd2.shape, const2), row_specs[1],
                pl.BlockSpec(wi2.shape, const2)] + row_specs[2:]
    out_specs = [pl.BlockSpec((m, da), tok),
                 pl.BlockSpec((n_seq, n_heads, HEAD_DIM, HEAD_DIM), lambda i, j: (i, 0, 0, 0))]
    out_shape = [jax.ShapeDtypeStruct((p_shift.shape[0], da), BF16),
                 jax.ShapeDtypeStruct(s0.shape, F32)]
    return pl.pallas_call(
        functools.partial(_rwkv_kernel, n_seq=n_seq, n_heads=n_heads, lora_w=wd2.shape[0]),
        grid=(b // n_seq, nj),
        in_specs=in_specs, out_specs=out_specs, out_shape=out_shape,
        scratch_shapes=[pltpu.VMEM((n_seq, 1, sw), F32),
                        pltpu.VMEM((n_seq, n_heads, HEAD_DIM, HEAD_DIM), F32)],
        compiler_params=pltpu.CompilerParams(vmem_limit_bytes=VMEM_LIMIT),
        name="rwkv7",
    )(p_shift, z_a, shift0, s0, mu, wd2, w0, wi2, a0, k_k, k_a, r_k, ln_w, ln_b)


def _mlstm_kernel(qk_ref, v_ref, g_ref, o_ref, zb_ref, conv0_ref, c0_ref, n0_ref, m0_ref,
                  cw_ref, cb_ref, gb_ref, gnw_ref,
                  ob_ref, ct_ref, nt_ref, mt_ref, xbuf, cn_scr, m_scr, *, n_seq, n_heads, conv_w):
    j = pl.program_id(1)
    M = qk_ref.shape[0]
    L = M // n_seq
    cc = qk_ref.shape[1]
    db = n_heads * HEAD_DIM
    pad = SUBLANES
    heads = range(n_heads)
    seqs = range(n_seq)

    @pl.when(j == 0)
    def _():
        xbuf[:, 0:pad, :] = conv0_ref[...]
        cn_scr[:, :, 0:HEAD_DIM, :] = c0_ref[...]
        cn_scr[:, :, HEAD_DIM:, :] = n0_ref[...]
        m_scr[...] = m0_ref[...]

    xbuf[:, pad:pad + L, :] = qk_ref[...].reshape(n_seq, L, cc)
    conv = cb_ref[...]
    for tap in range(conv_w):
        conv = conv + xbuf[:, pl.ds(pad - (conv_w - 1) + tap, L), :].reshape(M, cc) * cw_ref[tap:tap + 1, :]
    xbuf[:, 0:pad, :] = xbuf[:, L:L + pad, :]
    qk = _silu(conv)
    q = qk[:, 0:db]
    k = qk[:, db:] * (1.0 / math.sqrt(HEAD_DIM))
    v = v_ref[...]

    row = lax.broadcasted_iota(jnp.int32, (M, M), 0)
    col = lax.broadcasted_iota(jnp.int32, (M, M), 1)
    same_seq = _shr(row, int(math.log2(L))) == _shr(col, int(math.log2(L)))
    incl = (row >= col) & same_seq
    eye = row == col

    g = g_ref[...] + gb_ref[...]
    bcum = pltpu.roll(_cumsum_time(-_softplus(-g), incl.astype(BF16)), LANES - n_heads, axis=1)
    m_prev = m_scr[...]
    b_last = _last_rows(bcum, n_seq)
    x_all = g - bcum
    inter_all = bcum + _rows_of_seq(m_prev, n_seq, L)
    g_all = _rows_of_seq(b_last, n_seq, L) - bcum + g
    m_new = jnp.maximum(b_last + m_prev, jnp.max(g_all.reshape(n_seq, L, LANES), axis=1, keepdims=True))
    ws_all = jnp.exp(g_all - _rows_of_seq(m_new, n_seq, L))
    dec_all = jnp.exp(b_last + m_prev - m_new)
    m_scr[...] = m_new

    ones = jnp.ones((M, SUBLANES), F32)
    sls = [slice(h * HEAD_DIM, (h + 1) * HEAD_DIM) for h in heads]
    qs = [q[:, sl] for sl in sls]
    ks = [k[:, sl] for sl in sls]
    v1 = [jnp.concatenate([v[:, sl], ones], axis=-1) for sl in sls]
    cn = [[cn_scr[b, h] for h in heads] for b in seqs]

    qk_t = [_mm_nt(qs[h], ks[h]) for h in heads]
    if n_seq == 1:
        qc = [_mm_nt(qs[h], cn[0][h]) for h in heads]
    else:
        qc = [jnp.concatenate([_mm_nt(qs[h][b * L:(b + 1) * L], cn[b][h]) for b in seqs], axis=0) for h in heads]

    outs = []
    for h in heads:
        b_col = bcum[:, h:h + 1]
        x_row = jnp.sum(jnp.where(eye, x_all[:, h:h + 1], 0.0), axis=0, keepdims=True)
        d = jnp.where(incl, b_col + x_row, -jnp.inf)
        inter = inter_all[:, h:h + 1]
        m_t = jnp.maximum(jnp.max(d, axis=-1, keepdims=True), inter)
        s = qk_t[h] * jnp.exp(d - m_t)
        numden = _mm(s, v1[h]) + jnp.exp(inter - m_t) * qc[h]
        den = numden[:, HEAD_DIM:HEAD_DIM + 1]
        hh = numden[:, 0:HEAD_DIM] / jnp.maximum(jnp.abs(den), jnp.exp(-m_t))
        outs.append(hh * lax.rsqrt(jnp.mean(hh * hh, axis=-1, keepdims=True) + MLSTM_GN_EPS) * gnw_ref[:, sls[h]])

    for h in heads:
        v1w = v1[h] * ws_all[:, h:h + 1]
        for b in seqs:
            rows = slice(b * L, (b + 1) * L)
            cn_scr[b, h] = dec_all[b][:, h:h + 1] * cn[b][h] + _mm_tn(v1w[rows], ks[h][rows])

    hb = jnp.concatenate(outs, axis=-1)
    ob_ref[...] = (_sigmoid(o_ref[...]) * hb * _silu(zb_ref[...])).astype(ob_ref.dtype)

    @pl.when(j == pl.num_programs(1) - 1)
    def _():
        ct_ref[...] = cn_scr[:, :, 0:HEAD_DIM, :]
        nt_ref[...] = cn_scr[:, :, HEAD_DIM:, :]
        mt_ref[...] = m_scr[...]


def _mlstm(qk_pre, v_b, gates, o_b, z_b, conv0, c0, n0, m0, conv_w, conv_b, gate_b, gn_w, n_seq, L):
    b = c0.shape[0]
    cc = qk_pre.shape[-1]
    n_heads = c0.shape[1]
    db = n_heads * HEAD_DIM
    cw = conv_w.shape[0]
    m = n_seq * L
    nj = qk_pre.shape[0] // (b * L)
    tok = lambda i, j: (i * nj + j, 0)
    per_b3 = lambda i, j: (i, 0, 0)
    per_b4 = lambda i, j: (i, 0, 0, 0)
    const2 = lambda i, j: (0, 0)
    in_specs = [pl.BlockSpec((m, cc), tok),
                pl.BlockSpec((m, db), tok),
                pl.BlockSpec((m, LANES), tok),
                pl.BlockSpec((m, db), tok),
                pl.BlockSpec((m, db), tok),
                pl.BlockSpec((n_seq, SUBLANES, cc), per_b3),
                pl.BlockSpec((n_seq, n_heads, HEAD_DIM, HEAD_DIM), per_b4),
                pl.BlockSpec((n_seq, n_heads, SUBLANES, HEAD_DIM), per_b4),
                pl.BlockSpec((n_seq, 1, LANES), per_b3),
                pl.BlockSpec(conv_w.shape, const2),
                pl.BlockSpec(conv_b.shape, const2),
                pl.BlockSpec(gate_b.shape, const2),
                pl.BlockSpec(gn_w.shape, const2)]
    out_specs = [pl.BlockSpec((m, db), tok),
                 pl.BlockSpec((n_seq, n_heads, HEAD_DIM, HEAD_DIM), per_b4),
                 pl.BlockSpec((n_seq, n_heads, SUBLANES, HEAD_DIM), per_b4),
                 pl.BlockSpec((n_seq, 1, LANES), per_b3)]
    out_shape = [jax.ShapeDtypeStruct((qk_pre.shape[0], db), BF16),
                 jax.ShapeDtypeStruct(c0.shape, F32),
                 jax.ShapeDtypeStruct(n0.shape, F32),
                 jax.ShapeDtypeStruct(m0.shape, F32)]
    return pl.pallas_call(
        functools.partial(_mlstm_kernel, n_seq=n_seq, n_heads=n_heads, conv_w=cw),
        grid=(b // n_seq, nj),
        in_specs=in_specs, out_specs=out_specs, out_shape=out_shape,
        scratch_shapes=[pltpu.VMEM((n_seq, L + SUBLANES, cc), F32),
                        pltpu.VMEM((n_seq, n_heads, HEAD_DIM + SUBLANES, HEAD_DIM), F32),
                        pltpu.VMEM((n_seq, 1, LANES), F32)],
        compiler_params=pltpu.CompilerParams(vmem_limit_bytes=VMEM_LIMIT),
        name="mlstm",
    )(qk_pre, v_b, gates, o_b, z_b, conv0, c0, n0, m0, conv_w, conv_b, gate_b, gn_w)


def _out_kernel(oa_ref, ob_ref, gla_ref, glb_ref, x_ref, gate_ref, wua_ref, wub_ref, wo_ref, gf_ref,
                y_ref, *, final_norm):
    nb, tt, d = x_ref.shape
    m = nb * tt
    ua = jnp.dot(oa_ref[...], wua_ref[...], preferred_element_type=F32)
    ub = jnp.dot(ob_ref[...], wub_ref[...], preferred_element_type=F32)
    merged = _sigmoid(gla_ref[...].reshape(m, d)) * ua + _sigmoid(glb_ref[...].reshape(m, d)) * ub
    mo = jnp.dot(merged.astype(BF16), wo_ref[...], preferred_element_type=F32)
    xn = x_ref[...] + gate_ref[...] * mo.reshape(nb, tt, d)
    if final_norm:
        xn = xn * lax.rsqrt(jnp.mean(xn * xn, axis=-1, keepdims=True) + NORM_EPS) * gf_ref[...]
    y_ref[...] = xn


def _out(out_a, out_b, gl_a, gl_b, x, gate, w_up_a, w_up_b, w_out, g_final, nb, tt, final_norm):
    b, t, d = x.shape
    nj = t // tt
    tok = lambda i, j: (i, j, 0)
    tok2 = lambda i, j: (i * nj + j, 0)
    const2 = lambda i, j: (0, 0)
    wspec = lambda w: pl.BlockSpec(w.shape, const2, pipeline_mode=pl.Buffered(1))
    in_specs = [pl.BlockSpec((nb * tt, out_a.shape[-1]), tok2),
                pl.BlockSpec((nb * tt, out_b.shape[-1]), tok2),
                pl.BlockSpec((nb, tt, d), tok),
                pl.BlockSpec((nb, tt, d), tok),
                pl.BlockSpec((nb, tt, d), tok),
                pl.BlockSpec((nb, 1, d), lambda i, j: (i, 0, 0)),
                wspec(w_up_a), wspec(w_up_b), wspec(w_out),
                pl.BlockSpec((1, d), const2)]
    return pl.pallas_call(
        functools.partial(_out_kernel, final_norm=final_norm),
        grid=(b // nb, nj),
        in_specs=in_specs,
        out_specs=pl.BlockSpec((nb, tt, d), tok),
        out_shape=jax.ShapeDtypeStruct((b, t, d), F32),
        compiler_params=pltpu.CompilerParams(vmem_limit_bytes=VMEM_LIMIT),
        name="merge_out",
    )(out_a, out_b, gl_a, gl_b, x, gate, w_up_a, w_up_b, w_out, g_final.reshape(1, d))


def _token_tiling(b, t, target=256):
    if t >= target:
        tt = target
        while t % tt:
            tt //= 2
        return 1, tt
    nb = max(1, min(b, target // t))
    while b % nb:
        nb -= 1
    return nb, t


def _layer(x, mod, states, lp, g_final, final_norm):
    b, t, d = x.shape
    shift0, s0, conv0, c0, n0, m0 = states
    n_hb = c0.shape[1]
    cw = lp["conv_w"].shape[0]
    flat = lambda a: a.reshape(b * t, a.shape[-1])

    ada_shift, ada_scale, ada_gate = (mod[:, None, i * d:(i + 1) * d] for i in range(3))
    nb, tt = _token_tiling(b, t)
    p_shift, z_a, qk_pre, v_b, o_b, z_b, gl_a, gl_b, gates = _inproj(
        x, ada_scale, ada_shift, lp["g_norm"], lp["w_segs"], nb, tt)

    L = math.gcd(t, ROWS)
    n_seq = math.gcd(b, ROWS // L) if L == t else 1
    row = lambda a: a.reshape(1, -1)
    rows = tuple(row(lp[n]) for n in ("mu_shift", "w0", "a0", "k_k", "k_a", "r_k", "ln_w", "ln_b"))
    out_a, s_t = _rwkv(flat(p_shift), flat(z_a), shift0[:, None, :], s0, rows,
                       lp["w_decay2"], lp["w_iclr2"], n_seq, L)

    conv0p = jnp.pad(conv0, ((0, 0), (SUBLANES - (cw - 1), 0), (0, 0)))
    n0p = jnp.broadcast_to(n0[:, :, None, :], (b, n_hb, SUBLANES, HEAD_DIM))
    m0p = jnp.pad(m0, ((0, 0), (0, LANES - n_hb)))[:, None, :]
    gate_b = jnp.zeros((1, LANES), F32).at[0, 0:n_hb].set(lp["b_i"]).at[0, n_hb:2 * n_hb].set(lp["b_f"])
    out_b, c_t, n_t, m_t = _mlstm(flat(qk_pre), flat(v_b), flat(gates), flat(o_b), flat(z_b), conv0p, c0, n0p, m0p,
                                  lp["conv_w"], row(lp["conv_b"]), gate_b, row(lp["gn_w"]), n_seq, L)

    y = _out(out_a, out_b, gl_a, gl_b, x, ada_gate, lp["w_up_a"], lp["w_up_b"], lp["w_out"],
             g_final, nb, tt, final_norm)
    conv_t = qk_pre[:, t - (cw - 1):] if t >= cw - 1 else jnp.concatenate([conv0, qk_pre], axis=1)[:, t:]
    new_states = (p_shift[:, t - 1], s_t, conv_t, c_t, n_t[:, :, 0, :], m_t[:, 0, :n_hb])
    return y, new_states


def _trunk(x, mods, states, layers, g_final):
    depth = len(layers)
    new = [[] for _ in states]
    for l in range(depth):
        st = tuple(s[l] for s in states)
        x, st_new = _layer(x, mods[l], st, layers[l], g_final, final_norm=(l == depth - 1))
        for lst, s in zip(new, st_new):
            lst.append(s.astype(x.dtype))
    return x, tuple(jnp.stack(lst) for lst in new)


def kernel(x_prompt, x_sample, c_prompt, c_sample, state_rwkv_shift, state_rwkv_S, state_mlstm_conv, state_mlstm_C, state_mlstm_n, state_mlstm_m, g_norm, w_ada, b_ada, w_in, mu_shift, w_decay2, w0, w_iclr2, a0, k_k, k_a, r_k, ln_w, ln_b, conv_w, conv_b, b_i, b_f, gn_w, w_up_a, w_up_b, w_out, g_final):
    depth = g_norm.shape[0]
    bp, bs = x_prompt.shape[0], x_sample.shape[0]
    d = x_prompt.shape[-1]
    da, db = w_up_a.shape[1], w_up_b.shape[1]
    n_ha, n_hb = da // HEAD_DIM, db // HEAD_DIM
    sw = mu_shift.shape[-1]
    cc = conv_w.shape[-1]
    cw = conv_w.shape[1]
    dt = x_prompt.dtype

    sizes = (sw, da, cc, db, db, n_hb, n_hb, db, d, d)
    offs = [0]
    for s in sizes:
        offs.append(offs[-1] + s)
    seg = lambda w, i: w[:, offs[i]:offs[i + 1]]

    c_all = jnp.concatenate([c_prompt, c_sample], axis=0)
    n_c = c_all.shape[0]
    c_pad = jnp.pad(c_all, ((0, (-n_c) % SUBLANES), (0, 0)))

    layers, mods_p, mods_s = [], [], []
    for l in range(depth):
        w = w_in[l]
        wg = jnp.concatenate([seg(w, 5), seg(w, 6)], axis=1)
        wg = jnp.pad(wg, ((0, 0), (0, LANES - wg.shape[1])))
        w_segs = tuple(x.astype(BF16) for x in
                       (seg(w, 0), seg(w, 1), seg(w, 2), seg(w, 3), seg(w, 4), seg(w, 7), seg(w, 8), seg(w, 9), wg))
        layers.append(dict(
            g_norm=g_norm[l], w_segs=w_segs, mu_shift=mu_shift[l], w_decay2=w_decay2[l].astype(BF16),
            w0=w0[l], w_iclr2=w_iclr2[l].astype(BF16), a0=a0[l], k_k=k_k[l], k_a=k_a[l], r_k=r_k[l],
            ln_w=ln_w[l], ln_b=ln_b[l], conv_w=conv_w[l], conv_b=conv_b[l], b_i=b_i[l], b_f=b_f[l],
            gn_w=gn_w[l], w_up_a=w_up_a[l].astype(BF16), w_up_b=w_up_b[l].astype(BF16),
            w_out=w_out[l].astype(BF16)))
        mod = _ada(c_pad, w_ada[l], b_ada[l])
        mods_p.append(mod[:bp])
        mods_s.append(mod[bp:bp + bs])

    prompt_states = (
        jnp.zeros((depth, bp, sw), dt),
        jnp.zeros((depth, bp, n_ha, HEAD_DIM, HEAD_DIM), dt),
        jnp.zeros((depth, bp, cw - 1, cc), dt),
        jnp.zeros((depth, bp, n_hb, HEAD_DIM, HEAD_DIM), dt),
        jnp.zeros((depth, bp, n_hb, HEAD_DIM), dt),
        jnp.zeros((depth, bp, n_hb), dt),
    )
    sample_states = (state_rwkv_shift, state_rwkv_S, state_mlstm_conv,
                     state_mlstm_C, state_mlstm_n, state_mlstm_m)
    y_p, st_p = _trunk(x_prompt, mods_p, prompt_states, layers, g_final)
    y_s, st_s = _trunk(x_sample, mods_s, sample_states, layers, g_final)
    return (y_p, y_s) + st_p + st_s
```

```python
import functools
import math

import jax
import jax.numpy as jnp
from jax import lax
from jax.experimental import pallas as pl
from jax.experimental.pallas import tpu as pltpu

F32 = jnp.float32
BF16 = jnp.bfloat16

HEAD_DIM = 64
NORM_EPS = 1e-6
RWKV_GN_EPS = 64e-5
MLSTM_GN_EPS = 1e-6
ROWS = 64
MLSTM_ROWS = 128
SUBLANES = 8
LANES = 128
TRI_BASE = 16
VMEM_LIMIT = 56 * 1024 * 1024

NT_DIMS = (((1,), (1,)), ((), ()))
TN_DIMS = (((0,), (0,)), ((), ()))


def _mm(a, b):
    return jnp.dot(a.astype(BF16), b.astype(BF16), preferred_element_type=F32)


def _mm_nt(a, b):
    return lax.dot_general(a.astype(BF16), b.astype(BF16), NT_DIMS, preferred_element_type=F32)


def _mm_tn(a, b):
    return lax.dot_general(a.astype(BF16), b.astype(BF16), TN_DIMS, preferred_element_type=F32)


def _sigmoid(x):
    return 1.0 / (1.0 + jnp.exp(-x))


def _silu(x):
    return x * _sigmoid(x)


def _softplus(x):
    return jnp.maximum(x, 0.0) + jnp.log(1.0 + jnp.exp(-jnp.abs(x)))


def _cumsum_time(x, tri_bf16):
    hi = x.astype(BF16)
    r1 = x - hi.astype(F32)
    mid = r1.astype(BF16)
    lo = (r1 - mid.astype(F32)).astype(BF16)
    dot = functools.partial(jnp.dot, preferred_element_type=F32)
    return dot(tri_bf16, hi) + dot(tri_bf16, mid) + dot(tri_bf16, lo)


def _shr(x, n):
    return lax.shift_right_logical(x, jnp.full(x.shape, n, x.dtype))


def _rows_of_seq(x, n_seq, L):
    if n_seq == 1:
        return x[0]
    return jnp.broadcast_to(x, (n_seq, L, x.shape[-1])).reshape(n_seq * L, x.shape[-1])


def _last_rows(x, n_seq):
    m, n = x.shape
    L = m // n_seq
    return x.reshape(n_seq, L, n)[:, L - 1:L, :]


def _cummax_time(x, L, pos):
    y = x
    sh = 1
    while sh < L:
        y = jnp.where(pos >= sh, jnp.maximum(y, pltpu.roll(y, sh, axis=0)), y)
        sh *= 2
    return y


def _tri_inverse_all(mats, row, col, L):
    bs = min(TRI_BASE, L)
    sh = int(math.log2(bs))
    same = _shr(row, sh) == _shr(col, sh)
    eye = (row == col).astype(F32)
    ns = [jnp.where(same, -a, 0.0) for a in mats]
    ts = [eye + n for n in ns]
    ps = ns
    k = 1
    while 2 * k < bs:
        ps = [_mm(p, p) for p in ps]
        ts = [t + _mm(t, p) for t, p in zip(ts, ps)]
        k *= 2
    size = bs
    while size < L:
        sh = int(math.log2(size))
        lower_left = (_shr(row, sh + 1) == _shr(col, sh + 1)) & (_shr(row, sh) != _shr(col, sh))
        offs = [jnp.where(lower_left, a, 0.0) for a in mats]
        tmp = [_mm(t, o) for t, o in zip(ts, offs)]
        ts = [t - _mm(x, t) for t, x in zip(ts, tmp)]
        size *= 2
    return ts


def _ada_kernel(c_ref, w_ref, b_ref, o_ref):
    o_ref[...] = _mm(_silu(c_ref[...]), w_ref[...]) + b_ref[...]


def _ada(c, w_ada, b_ada):
    n, d = c.shape
    n3 = w_ada.shape[1]
    tn = d
    return pl.pallas_call(
        _ada_kernel,
        grid=(n3 // tn,),
        in_specs=[pl.BlockSpec((n, d), lambda j: (0, 0)),
                  pl.BlockSpec((d, tn), lambda j: (0, j)),
                  pl.BlockSpec((1, tn), lambda j: (0, j))],
        out_specs=pl.BlockSpec((n, tn), lambda j: (0, j)),
        out_shape=jax.ShapeDtypeStruct((n, n3), F32),
        name="ada",
    )(c, w_ada, b_ada.reshape(1, n3))


def _inproj_kernel(x_ref, sc_ref, sh_ref, g_ref, *refs, n_seg):
    w_refs, o_refs = refs[:n_seg], refs[n_seg:]
    x = x_ref[...]
    nb, tt, d = x.shape
    y = x * lax.rsqrt(jnp.mean(x * x, axis=-1, keepdims=True) + NORM_EPS) * g_ref[...]
    h = y * (1.0 + sc_ref[...]) + sh_ref[...]
    h = h.reshape(nb * tt, d).astype(BF16)
    for w_ref, o_ref in zip(w_refs, o_refs):
        o_ref[...] = jnp.dot(h, w_ref[...], preferred_element_type=F32).reshape(o_ref.shape)


def _inproj(x, scale, shift, g_norm, w_segs, nb, tt):
    b, t, d = x.shape
    n_seg = len(w_segs)
    const = lambda i, j: (0, 0)
    in_specs = [pl.BlockSpec((nb, tt, d), lambda i, j: (i, j, 0)),
                pl.BlockSpec((nb, 1, d), lambda i, j: (i, 0, 0)),
                pl.BlockSpec((nb, 1, d), lambda i, j: (i, 0, 0)),
                pl.BlockSpec((1, d), const)]
    in_specs += [pl.BlockSpec(w.shape, const, pipeline_mode=pl.Buffered(1)) for w in w_segs]
    out_specs = [pl.BlockSpec((nb, tt, w.shape[1]), lambda i, j: (i, j, 0)) for w in w_segs]
    out_shape = [jax.ShapeDtypeStruct((b, t, w.shape[1]), F32) for w in w_segs]
    return pl.pallas_call(
        functools.partial(_inproj_kernel, n_seg=n_seg),
        grid=(b // nb, t // tt),
        in_specs=in_specs, out_specs=out_specs, out_shape=out_shape,
        compiler_params=pltpu.CompilerParams(vmem_limit_bytes=VMEM_LIMIT),
        name="inproj",
    )(x, scale, shift, g_norm.reshape(1, d), *w_segs)


def _rwkv_kernel(ps_ref, za_ref, shift0_ref, s0_ref, mu_ref, wd2_ref, w0_ref, wi2_ref, a0_ref,
                 kk_ref, ka_ref, rk_ref, lnw_ref, lnb_ref,
                 oa_ref, st_ref, prev_scr, s_scr, *, n_seq, n_heads, lora_w):
    j = pl.program_id(1)
    M = ps_ref.shape[0]
    L = M // n_seq
    da = n_heads * HEAD_DIM
    heads = range(n_heads)
    seqs = range(n_seq)

    @pl.when(j == 0)
    def _():
        prev_scr[...] = shift0_ref[...]
        s_scr[...] = s0_ref[...]

    p = ps_ref[...]
    row_w = lax.broadcasted_iota(jnp.int32, p.shape, 0)
    first = (row_w & (L - 1)) == 0
    prev = jnp.where(first, _rows_of_seq(prev_scr[...], n_seq, L), pltpu.roll(p, 1, axis=0))
    prev_scr[...] = _last_rows(p, n_seq)
    ps = p + mu_ref[...] * (prev - p)

    r = ps[:, 0:da]
    k = ps[:, da:2 * da]
    v = ps[:, 2 * da:3 * da]
    wl = ps[:, 3 * da:3 * da + lora_w]
    al = ps[:, 3 * da + lora_w:]

    w = w0_ref[...] + _mm(jnp.tanh(wl), wd2_ref[...])
    lw = -jnp.exp(-_softplus(-w) - 0.5)
    a = _sigmoid(a0_ref[...] + _mm(al, wi2_ref[...]))
    kkr = k * kk_ref[...]
    k2 = k * (1.0 + (a - 1.0) * ka_ref[...])
    rkk = r * k2 * rk_ref[...]

    row = lax.broadcasted_iota(jnp.int32, (M, M), 0)
    col = lax.broadcasted_iota(jnp.int32, (M, M), 1)
    same_seq = _shr(row, int(math.log2(L))) == _shr(col, int(math.log2(L)))
    incl = (row >= col) & same_seq
    strict = (row > col) & same_seq
    row2 = lax.broadcasted_iota(jnp.int32, (M, 2 * M), 0)
    col2 = lax.broadcasted_iota(jnp.int32, (M, 2 * M), 1) & (M - 1)
    same_seq2 = _shr(row2, int(math.log2(L))) == _shr(col2, int(math.log2(L)))
    incl2 = (row2 >= col2) & same_seq2
    strict2 = (row2 > col2) & same_seq2

    c = _cumsum_time(lw, incl.astype(BF16))
    c_last = _last_rows(c, n_seq)
    c_last_rows = _rows_of_seq(c_last, n_seq, L)
    e_c = jnp.exp(c)
    e_ce = jnp.exp(c - lw)
    e_nc = jnp.exp(-c)
    e_cl = jnp.exp(c_last_rows - c)
    p_last = jnp.exp(c_last)

    sls = [slice(h * HEAD_DIM, (h + 1) * HEAD_DIM) for h in heads]
    kq, rq, qr, bkk, bkk2 = [], [], [], [], []
    for sl in sls:
        kkr_h = kkr[:, sl]
        kk_h = kkr_h / jnp.maximum(jnp.sqrt(jnp.sum(kkr_h * kkr_h, axis=-1, keepdims=True)), 1e-12)
        b_h = kk_h * a[:, sl]
        kq.append(kk_h * e_ce[:, sl])
        rq.append(r[:, sl] * e_c[:, sl])
        qr.append(jnp.concatenate([kq[-1], rq[-1]], axis=0).astype(BF16))
        bkk.append(jnp.concatenate([b_h * e_nc[:, sl], k2[:, sl] * e_nc[:, sl]], axis=0).astype(BF16))
        bkk2.append([b_h * e_cl[:, sl], k2[:, sl] * e_cl[:, sl]])
    vs = [v[:, sl] for sl in sls]

    gs = [_mm_nt(qr[h], bkk[h]) for h in heads]
    ga = [jnp.where(strict2, g[:M, :], 0.0) for g in gs]
    gm = [jnp.where(incl2, g[M:, :], 0.0) for g in gs]
    t_inv = _tri_inverse_all([x[:, :M] for x in ga], row, col, L)

    s0 = [[s_scr[b, h] for h in heads] for b in seqs]
    if n_seq == 1:
        ws = [_mm_nt(qr[h], s0[0][h]) for h in heads]
        w1 = [x[:M] for x in ws]
        wr = [x[M:] for x in ws]
    else:
        w1, wr = [], []
        for h in heads:
            parts = [_mm_nt(jnp.concatenate([kq[h][b * L:(b + 1) * L], rq[h][b * L:(b + 1) * L]], axis=0),
                            s0[b][h]) for b in seqs]
            w1.append(jnp.concatenate([x[:L] for x in parts], axis=0))
            wr.append(jnp.concatenate([x[L:] for x in parts], axis=0))
    zeros = jnp.zeros((M, HEAD_DIM), F32)
    rhs = [w1[h] + _mm(ga[h], jnp.concatenate([zeros, vs[h]], axis=0)) for h in heads]
    us = [-_mm(t_inv[h], rhs[h]) for h in heads]
    uv = [jnp.concatenate([us[h], vs[h]], axis=0) for h in heads]
    ys = [wr[h] + _mm(gm[h], uv[h]) for h in heads]
    for h in heads:
        for b in seqs:
            rows = slice(b * L, (b + 1) * L)
            uv_b = jnp.concatenate([us[h][rows], vs[h][rows]], axis=0)
            bkk2_b = jnp.concatenate([bkk2[h][0][rows], bkk2[h][1][rows]], axis=0)
            s_scr[b, h] = s0[b][h] * p_last[b][:, sls[h]] + _mm_tn(uv_b, bkk2_b)

    outs = []
    for h in heads:
        sl = sls[h]
        y = ys[h]
        mu_y = jnp.mean(y, axis=-1, keepdims=True)
        yc = y - mu_y
        var = jnp.mean(yc * yc, axis=-1, keepdims=True)
        yn = yc * lax.rsqrt(var + RWKV_GN_EPS) * lnw_ref[:, sl] + lnb_ref[:, sl]
        bonus = jnp.sum(rkk[:, sl], axis=-1, keepdims=True)
        outs.append(yn + bonus * vs[h])

    y_all = jnp.concatenate(outs, axis=-1)
    oa_ref[...] = (y_all * _silu(za_ref[...])).astype(oa_ref.dtype)

    @pl.when(j == pl.num_programs(1) - 1)
    def _():
        st_ref[...] = s_scr[...]


def _rwkv(p_shift, z_a, shift0, s0, rows, wd2, wi2, n_seq, L):
    b = s0.shape[0]
    sw = p_shift.shape[-1]
    n_heads = s0.shape[1]
    da = n_heads * HEAD_DIM
    m = n_seq * L
    nj = p_shift.shape[0] // (b * L)
    tok = lambda i, j: (i * nj + j, 0)
    const2 = lambda i, j: (0, 0)
    row_specs = [pl.BlockSpec(x.shape, const2) for x in rows]
    mu, w0, a0, k_k, k_a, r_k, ln_w, ln_b = rows
    in_specs = [pl.BlockSpec((m, sw), tok),
                pl.BlockSpec((m, da), tok),
                pl.BlockSpec((n_seq, 1, sw), lambda i, j: (i, 0, 0)),
                pl.BlockSpec((n_seq, n_heads, HEAD_DIM, HEAD_DIM), lambda i, j: (i, 0, 0, 0)),
                row_specs[0], pl.BlockSpec(wd2.shape, const2), row_specs[1],
                pl.BlockSpec(wi2.shape, const2)] + row_specs[2:]
    out_specs = [pl.BlockSpec((m, da), tok),
                 pl.BlockSpec((n_seq, n_heads, HEAD_DIM, HEAD_DIM), lambda i, j: (i, 0, 0, 0))]
    out_shape = [jax.ShapeDtypeStruct((p_shift.shape[0], da), BF16),
                 jax.ShapeDtypeStruct(s0.shape, F32)]
    return pl.pallas_call(
        functools.partial(_rwkv_kernel, n_seq=n_seq, n_heads=n_heads, lora_w=wd2.shape[0]),
        grid=(b // n_seq, nj),
        in_specs=in_specs, out_specs=out_specs, out_shape=out_shape,
        scratch_shapes=[pltpu.VMEM((n_seq, 1, sw), F32),
                        pltpu.VMEM((n_seq, n_heads, HEAD_DIM, HEAD_DIM), F32)],
        compiler_params=pltpu.CompilerParams(vmem_limit_bytes=VMEM_LIMIT),
        name="rwkv7",
    )(p_shift, z_a, shift0, s0, mu, wd2, w0, wi2, a0, k_k, k_a, r_k, ln_w, ln_b)


def _mlstm_kernel(qk_ref, v_ref, g_ref, o_ref, zb_ref, conv0_ref, c0_ref, n0_ref, m0_ref,
                  cw_ref, cb_ref, gb_ref, gnw_ref,
                  ob_ref, ct_ref, nt_ref, mt_ref, xbuf, cn_scr, m_scr, *, n_seq, n_heads, conv_w):
    j = pl.program_id(1)
    M = qk_ref.shape[0]
    L = M // n_seq
    cc = qk_ref.shape[1]
    db = n_heads * HEAD_DIM
    pad = SUBLANES
    heads = range(n_heads)
    seqs = range(n_seq)

    @pl.when(j == 0)
    def _():
        xbuf[:, 0:pad, :] = conv0_ref[...]
        cn_scr[:, :, 0:HEAD_DIM, :] = c0_ref[...]
        cn_scr[:, :, HEAD_DIM:, :] = n0_ref[...]
        m_scr[...] = m0_ref[...]

    xbuf[:, pad:pad + L, :] = qk_ref[...].reshape(n_seq, L, cc)
    conv = cb_ref[...]
    for tap in range(conv_w):
        conv = conv + xbuf[:, pl.ds(pad - (conv_w - 1) + tap, L), :].reshape(M, cc) * cw_ref[tap:tap + 1, :]
    xbuf[:, 0:pad, :] = xbuf[:, L:L + pad, :]
    qk = _silu(conv)
    q = qk[:, 0:db]
    k = qk[:, db:] * (1.0 / math.sqrt(HEAD_DIM))
    v = v_ref[...]

    row = lax.broadcasted_iota(jnp.int32, (M, M), 0)
    col = lax.broadcasted_iota(jnp.int32, (M, M), 1)
    same_seq = _shr(row, int(math.log2(L))) == _shr(col, int(math.log2(L)))
    incl = (row >= col) & same_seq

    g = g_ref[...] + gb_ref[...]
    bcum = pltpu.roll(_cumsum_time(-_softplus(-g), incl.astype(BF16)), LANES - n_heads, axis=1)
    m_prev = m_scr[...]
    m_prev_rows = _rows_of_seq(m_prev, n_seq, L)
    x_all = g - bcum
    pos = lax.broadcasted_iota(jnp.int32, (M, LANES), 0) & (L - 1)
    m_all = bcum + jnp.maximum(_cummax_time(x_all, L, pos), m_prev_rows)
    bm_all = bcum - m_all
    w_in_all = jnp.exp(bcum + m_prev_rows - m_all)
    e_negm_all = jnp.exp(-m_all)
    m_new = _last_rows(m_all, n_seq)
    b_last = _last_rows(bcum, n_seq)
    ws_all = jnp.exp(_rows_of_seq(b_last - m_new, n_seq, L) + x_all)
    dec_all = jnp.exp(b_last + m_prev - m_new)
    m_scr[...] = m_new
    x_t = jnp.transpose(x_all)

    ones = jnp.ones((M, SUBLANES), F32)
    sls = [slice(h * HEAD_DIM, (h + 1) * HEAD_DIM) for h in heads]
    qs = [q[:, sl] for sl in sls]
    ks = [k[:, sl] for sl in sls]
    v1 = [jnp.concatenate([v[:, sl], ones], axis=-1) for sl in sls]
    cn = [[cn_scr[b, h] for h in heads] for b in seqs]

    qk_t = [_mm_nt(qs[h], ks[h]) for h in heads]
    if n_seq == 1:
        qc = [_mm_nt(qs[h], cn[0][h]) for h in heads]
    else:
        qc = [jnp.concatenate([_mm_nt(qs[h][b * L:(b + 1) * L], cn[b][h]) for b in seqs], axis=0) for h in heads]
    w_ts = [jnp.where(incl, jnp.exp(bm_all[:, h:h + 1] + x_t[h:h + 1, :]), 0.0) for h in heads]
    s = [qk_t[h] * w_ts[h] for h in heads]
    numden = [_mm(s[h], v1[h]) + w_in_all[:, h:h + 1] * qc[h] for h in heads]
    den = [jnp.maximum(jnp.abs(numden[h][:, HEAD_DIM:HEAD_DIM + 1]), e_negm_all[:, h:h + 1]) for h in heads]
    hh = [numden[h][:, 0:HEAD_DIM] / den[h] for h in heads]
    outs = [hh[h] * lax.rsqrt(jnp.mean(hh[h] * hh[h], axis=-1, keepdims=True) + MLSTM_GN_EPS) * gnw_ref[:, sls[h]]
            for h in heads]

    v1w = [v1[h] * ws_all[:, h:h + 1] for h in heads]
    for h in heads:
        for b in seqs:
            rows = slice(b * L, (b + 1) * L)
            cn_scr[b, h] = dec_all[b][:, h:h + 1] * cn[b][h] + _mm_tn(v1w[h][rows], ks[h][rows])

    hb = jnp.concatenate(outs, axis=-1)
    ob_ref[...] = (_sigmoid(o_ref[...]) * hb * _silu(zb_ref[...])).astype(ob_ref.dtype)

    @pl.when(j == pl.num_programs(1) - 1)
    def _():
        ct_ref[...] = cn_scr[:, :, 0:HEAD_DIM, :]
        nt_ref[...] = cn_scr[:, :, HEAD_DIM:, :]
        mt_ref[...] = m_scr[...]


def _mlstm(qk_pre, v_b, gates, o_b, z_b, conv0, c0, n0, m0, conv_w, conv_b, gate_b, gn_w, n_seq, L):
    b = c0.shape[0]
    cc = qk_pre.shape[-1]
    n_heads = c0.shape[1]
    db = n_heads * HEAD_DIM
    cw = conv_w.shape[0]
    m = n_seq * L
    nj = qk_pre.shape[0] // (b * L)
    tok = lambda i, j: (i * nj + j, 0)
    per_b3 = lambda i, j: (i, 0, 0)
    per_b4 = lambda i, j: (i, 0, 0, 0)
    const2 = lambda i, j: (0, 0)
    in_specs = [pl.BlockSpec((m, cc), tok),
                pl.BlockSpec((m, db), tok),
                pl.BlockSpec((m, LANES), tok),
                pl.BlockSpec((m, db), tok),
                pl.BlockSpec((m, db), tok),
                pl.BlockSpec((n_seq, SUBLANES, cc), per_b3),
                pl.BlockSpec((n_seq, n_heads, HEAD_DIM, HEAD_DIM), per_b4),
                pl.BlockSpec((n_seq, n_heads, SUBLANES, HEAD_DIM), per_b4),
                pl.BlockSpec((n_seq, 1, LANES), per_b3),
                pl.BlockSpec(conv_w.shape, const2),
                pl.BlockSpec(conv_b.shape, const2),
                pl.BlockSpec(gate_b.shape, const2),
                pl.BlockSpec(gn_w.shape, const2)]
    out_specs = [pl.BlockSpec((m, db), tok),
                 pl.BlockSpec((n_seq, n_heads, HEAD_DIM, HEAD_DIM), per_b4),
                 pl.BlockSpec((n_seq, n_heads, SUBLANES, HEAD_DIM), per_b4),
                 pl.BlockSpec((n_seq, 1, LANES), per_b3)]
    out_shape = [jax.ShapeDtypeStruct((qk_pre.shape[0], db), BF16),
                 jax.ShapeDtypeStruct(c0.shape, F32),
                 jax.ShapeDtypeStruct(n0.shape, F32),
                 jax.ShapeDtypeStruct(m0.shape, F32)]
    return pl.pallas_call(
        functools.partial(_mlstm_kernel, n_seq=n_seq, n_heads=n_heads, conv_w=cw),
        grid=(b // n_seq, nj),
        in_specs=in_specs, out_specs=out_specs, out_shape=out_shape,
        scratch_shapes=[pltpu.VMEM((n_seq, L + SUBLANES, cc), F32),
                        pltpu.VMEM((n_seq, n_heads, HEAD_DIM + SUBLANES, HEAD_DIM), F32),
                        pltpu.VMEM((n_seq, 1, LANES), F32)],
        compiler_params=pltpu.CompilerParams(vmem_limit_bytes=VMEM_LIMIT),
        name="mlstm",
    )(qk_pre, v_b, gates, o_b, z_b, conv0, c0, n0, m0, conv_w, conv_b, gate_b, gn_w)


def _out_kernel(oa_ref, ob_ref, gla_ref, glb_ref, x_ref, gate_ref, wua_ref, wub_ref, wo_ref, gf_ref,
                y_ref, *, final_norm):
    nb, tt, d = x_ref.shape
    m = nb * tt
    ua = jnp.dot(oa_ref[...], wua_ref[...], preferred_element_type=F32)
    ub = jnp.dot(ob_ref[...], wub_ref[...], preferred_element_type=F32)
    merged = _sigmoid(gla_ref[...].reshape(m, d)) * ua + _sigmoid(glb_ref[...].reshape(m, d)) * ub
    mo = jnp.dot(merged.astype(BF16), wo_ref[...], preferred_element_type=F32)
    xn = x_ref[...] + gate_ref[...] * mo.reshape(nb, tt, d)
    if final_norm:
        xn = xn * lax.rsqrt(jnp.mean(xn * xn, axis=-1, keepdims=True) + NORM_EPS) * gf_ref[...]
    y_ref[...] = xn


def _out(out_a, out_b, gl_a, gl_b, x, gate, w_up_a, w_up_b, w_out, g_final, nb, tt, final_norm):
    b, t, d = x.shape
    nj = t // tt
    tok = lambda i, j: (i, j, 0)
    tok2 = lambda i, j: (i * nj + j, 0)
    const2 = lambda i, j: (0, 0)
    wspec = lambda w: pl.BlockSpec(w.shape, const2, pipeline_mode=pl.Buffered(1))
    in_specs = [pl.BlockSpec((nb * tt, out_a.shape[-1]), tok2),
                pl.BlockSpec((nb * tt, out_b.shape[-1]), tok2),
                pl.BlockSpec((nb, tt, d), tok),
                pl.BlockSpec((nb, tt, d), tok),
                pl.BlockSpec((nb, tt, d), tok),
                pl.BlockSpec((nb, 1, d), lambda i, j: (i, 0, 0)),
                wspec(w_up_a), wspec(w_up_b), wspec(w_out),
                pl.BlockSpec((1, d), const2)]
    return pl.pallas_call(
        functools.partial(_out_kernel, final_norm=final_norm),
        grid=(b // nb, nj),
        in_specs=in_specs,
        out_specs=pl.BlockSpec((nb, tt, d), tok),
        out_shape=jax.ShapeDtypeStruct((b, t, d), F32),
        compiler_params=pltpu.CompilerParams(vmem_limit_bytes=VMEM_LIMIT),
        name="merge_out",
    )(out_a, out_b, gl_a, gl_b, x, gate, w_up_a, w_up_b, w_out, g_final.reshape(1, d))


def _token_tiling(b, t, target=256):
    if t >= target:
        tt = target
        while t % tt:
            tt //= 2
        return 1, tt
    nb = max(1, min(b, target // t))
    while b % nb:
        nb -= 1
    return nb, t


def _layer(x, mod, states, lp, g_final, final_norm):
    b, t, d = x.shape
    shift0, s0, conv0, c0, n0, m0 = states
    n_hb = c0.shape[1]
    cw = lp["conv_w"].shape[0]
    flat = lambda a: a.reshape(b * t, a.shape[-1])

    ada_shift, ada_scale, ada_gate = (mod[:, None, i * d:(i + 1) * d] for i in range(3))
    nb, tt = _token_tiling(b, t)
    p_shift, z_a, qk_pre, v_b, o_b, z_b, gl_a, gl_b, gates = _inproj(
        x, ada_scale, ada_shift, lp["g_norm"], lp["w_segs"], nb, tt)

    L = math.gcd(t, ROWS)
    n_seq = math.gcd(b, ROWS // L) if L == t else 1
    row = lambda a: a.reshape(1, -1)
    rows = tuple(row(lp[n]) for n in ("mu_shift", "w0", "a0", "k_k", "k_a", "r_k", "ln_w", "ln_b"))
    out_a, s_t = _rwkv(flat(p_shift), flat(z_a), shift0[:, None, :], s0, rows,
                       lp["w_decay2"], lp["w_iclr2"], n_seq, L)

    conv0p = jnp.pad(conv0, ((0, 0), (SUBLANES - (cw - 1), 0), (0, 0)))
    n0p = jnp.broadcast_to(n0[:, :, None, :], (b, n_hb, SUBLANES, HEAD_DIM))
    m0p = jnp.pad(m0, ((0, 0), (0, LANES - n_hb)))[:, None, :]
    gate_b = jnp.zeros((1, LANES), F32).at[0, 0:n_hb].set(lp["b_i"]).at[0, n_hb:2 * n_hb].set(lp["b_f"])
    L_b = math.gcd(t, MLSTM_ROWS)
    n_seq_b = math.gcd(b, MLSTM_ROWS // L_b) if L_b == t else 1
    out_b, c_t, n_t, m_t = _mlstm(flat(qk_pre), flat(v_b), flat(gates), flat(o_b), flat(z_b), conv0p, c0, n0p, m0p,
                                  lp["conv_w"], row(lp["conv_b"]), gate_b, row(lp["gn_w"]), n_seq_b, L_b)

    y = _out(out_a, out_b, gl_a, gl_b, x, ada_gate, lp["w_up_a"], lp["w_up_b"], lp["w_out"],
             g_final, nb, tt, final_norm)
    conv_t = qk_pre[:, t - (cw - 1):] if t >= cw - 1 else jnp.concatenate([conv0, qk_pre], axis=1)[:, t:]
    new_states = (p_shift[:, t - 1], s_t, conv_t, c_t, n_t[:, :, 0, :], m_t[:, 0, :n_hb])
    return y, new_states


def _trunk(x, mods, states, layers, g_final):
    depth = len(layers)
    new = [[] for _ in states]
    for l in range(depth):
        st = tuple(s[l] for s in states)
        x, st_new = _layer(x, mods[l], st, layers[l], g_final, final_norm=(l == depth - 1))
        for lst, s in zip(new, st_new):
            lst.append(s.astype(x.dtype))
    return x, tuple(jnp.stack(lst) for lst in new)


def kernel(x_prompt, x_sample, c_prompt, c_sample, state_rwkv_shift, state_rwkv_S, state_mlstm_conv, state_mlstm_C, state_mlstm_n, state_mlstm_m, g_norm, w_ada, b_ada, w_in, mu_shift, w_decay2, w0, w_iclr2, a0, k_k, k_a, r_k, ln_w, ln_b, conv_w, conv_b, b_i, b_f, gn_w, w_up_a, w_up_b, w_out, g_final):
    depth = g_norm.shape[0]
    bp, bs = x_prompt.shape[0], x_sample.shape[0]
    d = x_prompt.shape[-1]
    da, db = w_up_a.shape[1], w_up_b.shape[1]
    n_ha, n_hb = da // HEAD_DIM, db // HEAD_DIM
    sw = mu_shift.shape[-1]
    cc = conv_w.shape[-1]
    cw = conv_w.shape[1]
    dt = x_prompt.dtype

    sizes = (sw, da, cc, db, db, n_hb, n_hb, db, d, d)
    offs = [0]
    for s in sizes:
        offs.append(offs[-1] + s)
    seg = lambda w, i: w[:, offs[i]:offs[i + 1]]

    c_all = jnp.concatenate([c_prompt, c_sample], axis=0)
    n_c = c_all.shape[0]
    c_pad = jnp.pad(c_all, ((0, (-n_c) % SUBLANES), (0, 0)))

    layers, mods_p, mods_s = [], [], []
    for l in range(depth):
        w = w_in[l]
        wg = jnp.concatenate([seg(w, 5), seg(w, 6)], axis=1)
        wg = jnp.pad(wg, ((0, 0), (0, LANES - wg.shape[1])))
        w_segs = tuple(x.astype(BF16) for x in
                       (seg(w, 0), seg(w, 1), seg(w, 2), seg(w, 3), seg(w, 4), seg(w, 7), seg(w, 8), seg(w, 9), wg))
        layers.append(dict(
            g_norm=g_norm[l], w_segs=w_segs, mu_shift=mu_shift[l], w_decay2=w_decay2[l].astype(BF16),
            w0=w0[l], w_iclr2=w_iclr2[l].astype(BF16), a0=a0[l], k_k=k_k[l], k_a=k_a[l], r_k=r_k[l],
            ln_w=ln_w[l], ln_b=ln_b[l], conv_w=conv_w[l], conv_b=conv_b[l], b_i=b_i[l], b_f=b_f[l],
            gn_w=gn_w[l], w_up_a=w_up_a[l].astype(BF16), w_up_b=w_up_b[l].astype(BF16),
            w_out=w_out[l].astype(BF16)))
        mod = _ada(c_pad, w_ada[l], b_ada[l])
        mods_p.append(mod[:bp])
        mods_s.append(mod[bp:bp + bs])

    prompt_states = (
        jnp.zeros((depth, bp, sw), dt),
        jnp.zeros((depth, bp, n_ha, HEAD_DIM, HEAD_DIM), dt),
        jnp.zeros((depth, bp, cw - 1, cc), dt),
        jnp.zeros((depth, bp, n_hb, HEAD_DIM, HEAD_DIM), dt),
        jnp.zeros((depth, bp, n_hb, HEAD_DIM), dt),
        jnp.zeros((depth, bp, n_hb), dt),
    )
    sample_states = (state_rwkv_shift, state_rwkv_S, state_mlstm_conv,
                     state_mlstm_C, state_mlstm_n, state_mlstm_m)
    y_p, st_p = _trunk(x_prompt, mods_p, prompt_states, layers, g_final)
    y_s, st_s = _trunk(x_sample, mods_s, sample_states, layers, g_final)
    return (y_p, y_s) + st_p + st_s
```

```python
import functools
import math

import jax
import jax.numpy as jnp
from jax import lax
from jax.experimental import pallas as pl
from jax.experimental.pallas import tpu as pltpu

F32 = jnp.float32
BF16 = jnp.bfloat16

HEAD_DIM = 64
NORM_EPS = 1e-6
RWKV_GN_EPS = 64e-5
MLSTM_GN_EPS = 1e-6
ROWS = 64
RWKV_STEP_ROWS = 128
MLSTM_ROWS = 128
SUBLANES = 8
LANES = 128
TRI_BASE = 16
VMEM_LIMIT = 56 * 1024 * 1024

NT_DIMS = (((1,), (1,)), ((), ()))
TN_DIMS = (((0,), (0,)), ((), ()))


def _mm(a, b):
    return jnp.dot(a.astype(BF16), b.astype(BF16), preferred_element_type=F32)


def _mm_nt(a, b):
    return lax.dot_general(a.astype(BF16), b.astype(BF16), NT_DIMS, preferred_element_type=F32)


def _mm_tn(a, b):
    return lax.dot_general(a.astype(BF16), b.astype(BF16), TN_DIMS, preferred_element_type=F32)


def _sigmoid(x):
    return 1.0 / (1.0 + jnp.exp(-x))


def _silu(x):
    return x * _sigmoid(x)


def _softplus(x):
    return jnp.maximum(x, 0.0) + jnp.log(1.0 + jnp.exp(-jnp.abs(x)))


def _cumsum_time(x, tri_bf16):
    hi = x.astype(BF16)
    r1 = x - hi.astype(F32)
    mid = r1.astype(BF16)
    lo = (r1 - mid.astype(F32)).astype(BF16)
    dot = functools.partial(jnp.dot, preferred_element_type=F32)
    return dot(tri_bf16, hi) + dot(tri_bf16, mid) + dot(tri_bf16, lo)


def _shr(x, n):
    return lax.shift_right_logical(x, jnp.full(x.shape, n, x.dtype))


def _rows_of_seq(x, n_seq, L):
    if n_seq == 1:
        return x[0]
    return jnp.broadcast_to(x, (n_seq, L, x.shape[-1])).reshape(n_seq * L, x.shape[-1])


def _last_rows(x, n_seq):
    m, n = x.shape
    L = m // n_seq
    return x.reshape(n_seq, L, n)[:, L - 1:L, :]


def _cummax_time(x, L, pos):
    y = x
    sh = 1
    while sh < L:
        y = jnp.where(pos >= sh, jnp.maximum(y, pltpu.roll(y, sh, axis=0)), y)
        sh *= 2
    return y


def _tri_inverse_all(mats, row, col, L):
    bs = min(TRI_BASE, L)
    sh = int(math.log2(bs))
    same = _shr(row, sh) == _shr(col, sh)
    eye = (row == col).astype(F32)
    ns = [jnp.where(same, -a, 0.0) for a in mats]
    ts = [eye + n for n in ns]
    ps = ns
    k = 1
    while 2 * k < bs:
        ps = [_mm(p, p) for p in ps]
        ts = [t + _mm(t, p) for t, p in zip(ts, ps)]
        k *= 2
    size = bs
    while size < L:
        sh = int(math.log2(size))
        lower_left = (_shr(row, sh + 1) == _shr(col, sh + 1)) & (_shr(row, sh) != _shr(col, sh))
        offs = [jnp.where(lower_left, a, 0.0) for a in mats]
        tmp = [_mm(t, o) for t, o in zip(ts, offs)]
        ts = [t - _mm(x, t) for t, x in zip(ts, tmp)]
        size *= 2
    return ts


def _ada_kernel(c_ref, w_ref, b_ref, o_ref):
    o_ref[...] = _mm(_silu(c_ref[...]), w_ref[...]) + b_ref[...]


def _ada(c, w_ada, b_ada):
    n, d = c.shape
    n3 = w_ada.shape[1]
    tn = d
    return pl.pallas_call(
        _ada_kernel,
        grid=(n3 // tn,),
        in_specs=[pl.BlockSpec((n, d), lambda j: (0, 0)),
                  pl.BlockSpec((d, tn), lambda j: (0, j)),
                  pl.BlockSpec((1, tn), lambda j: (0, j))],
        out_specs=pl.BlockSpec((n, tn), lambda j: (0, j)),
        out_shape=jax.ShapeDtypeStruct((n, n3), F32),
        name="ada",
    )(c, w_ada, b_ada.reshape(1, n3))


def _inproj_kernel(x_ref, sc_ref, sh_ref, g_ref, *refs, n_seg):
    w_refs, o_refs = refs[:n_seg], refs[n_seg:]
    x = x_ref[...]
    nb, tt, d = x.shape
    y = x * lax.rsqrt(jnp.mean(x * x, axis=-1, keepdims=True) + NORM_EPS) * g_ref[...]
    h = y * (1.0 + sc_ref[...]) + sh_ref[...]
    h = h.reshape(nb * tt, d).astype(BF16)
    for w_ref, o_ref in zip(w_refs, o_refs):
        o_ref[...] = jnp.dot(h, w_ref[...], preferred_element_type=F32).reshape(o_ref.shape)


def _inproj(x, scale, shift, g_norm, w_segs, nb, tt):
    b, t, d = x.shape
    n_seg = len(w_segs)
    const = lambda i, j: (0, 0)
    in_specs = [pl.BlockSpec((nb, tt, d), lambda i, j: (i, j, 0)),
                pl.BlockSpec((nb, 1, d), lambda i, j: (i, 0, 0)),
                pl.BlockSpec((nb, 1, d), lambda i, j: (i, 0, 0)),
                pl.BlockSpec((1, d), const)]
    in_specs += [pl.BlockSpec(w.shape, const, pipeline_mode=pl.Buffered(1)) for w in w_segs]
    out_specs = [pl.BlockSpec((nb, tt, w.shape[1]), lambda i, j: (i, j, 0)) for w in w_segs]
    out_shape = [jax.ShapeDtypeStruct((b, t, w.shape[1]), F32) for w in w_segs]
    return pl.pallas_call(
        functools.partial(_inproj_kernel, n_seg=n_seg),
        grid=(b // nb, t // tt),
        in_specs=in_specs, out_specs=out_specs, out_shape=out_shape,
        compiler_params=pltpu.CompilerParams(vmem_limit_bytes=VMEM_LIMIT),
        name="inproj",
    )(x, scale, shift, g_norm.reshape(1, d), *w_segs)


def _rwkv_kernel(ps_ref, za_ref, shift0_ref, s0_ref, mu_ref, wd2_ref, w0_ref, wi2_ref, a0_ref,
                 kk_ref, ka_ref, rk_ref, lnw_ref, lnb_ref,
                 oa_ref, st_ref, prev_scr, s_scr, *, n_seq, chunk, n_heads, lora_w):
    j = pl.program_id(1)
    M = ps_ref.shape[0]
    L = chunk
    t_step = M // n_seq
    RU = min(M, ROWS)
    n_units = M // RU
    n_groups = M // L
    da = n_heads * HEAD_DIM
    heads = range(n_heads)
    units = range(n_units)
    pairs = [(u, h) for u in units for h in heads]

    @pl.when(j == 0)
    def _():
        prev_scr[...] = shift0_ref[...]
        s_scr[...] = s0_ref[...]

    p = ps_ref[...]
    row_w = lax.broadcasted_iota(jnp.int32, p.shape, 0)
    first = (row_w & (t_step - 1)) == 0
    prev = jnp.where(first, _rows_of_seq(prev_scr[...], n_seq, t_step), pltpu.roll(p, 1, axis=0))
    prev_scr[...] = _last_rows(p, n_seq)
    ps = p + mu_ref[...] * (prev - p)

    r = ps[:, 0:da]
    k = ps[:, da:2 * da]
    v = ps[:, 2 * da:3 * da]
    wl = ps[:, 3 * da:3 * da + lora_w]
    al = ps[:, 3 * da + lora_w:]

    w = w0_ref[...] + _mm(jnp.tanh(wl), wd2_ref[...])
    lw = -jnp.exp(-_softplus(-w) - 0.5)
    a = _sigmoid(a0_ref[...] + _mm(al, wi2_ref[...]))
    kkr = k * kk_ref[...]
    k2 = k * (1.0 + (a - 1.0) * ka_ref[...])
    rkk = r * k2 * rk_ref[...]

    lg = int(math.log2(L))
    row_m = lax.broadcasted_iota(jnp.int32, (M, M), 0)
    col_m = lax.broadcasted_iota(jnp.int32, (M, M), 1)
    tri_m = ((row_m >= col_m) & (_shr(row_m, lg) == _shr(col_m, lg))).astype(BF16)
    row = lax.broadcasted_iota(jnp.int32, (RU, RU), 0)
    col = lax.broadcasted_iota(jnp.int32, (RU, RU), 1)
    row2 = lax.broadcasted_iota(jnp.int32, (RU, 2 * RU), 0)
    col2 = lax.broadcasted_iota(jnp.int32, (RU, 2 * RU), 1) & (RU - 1)
    same2 = _shr(row2, lg) == _shr(col2, lg)
    incl2 = (row2 >= col2) & same2
    strict2 = (row2 > col2) & same2

    c = _cumsum_time(lw, tri_m)
    c_last = _last_rows(c, n_groups)
    c_last_rows = _rows_of_seq(c_last, n_groups, L)
    e_c = jnp.exp(c)
    e_ce = jnp.exp(c - lw)
    e_nc = jnp.exp(-c)
    e_cl = jnp.exp(c_last_rows - c)
    p_last = jnp.exp(c_last)

    sls = [slice(h * HEAD_DIM, (h + 1) * HEAD_DIM) for h in heads]
    rus = [slice(u * RU, (u + 1) * RU) for u in units]
    kq, rq, qr, bkk, bk2, kk2, vs = {}, {}, {}, {}, {}, {}, {}
    for h in heads:
        sl = sls[h]
        kkr_h = kkr[:, sl]
        kk_h = kkr_h / jnp.maximum(jnp.sqrt(jnp.sum(kkr_h * kkr_h, axis=-1, keepdims=True)), 1e-12)
        b_h = kk_h * a[:, sl]
        kq_h = kk_h * e_ce[:, sl]
        rq_h = r[:, sl] * e_c[:, sl]
        bk_h = b_h * e_nc[:, sl]
        kk_h2 = k2[:, sl] * e_nc[:, sl]
        bk2_h = b_h * e_cl[:, sl]
        kk2_h = k2[:, sl] * e_cl[:, sl]
        for u in units:
            ru = rus[u]
            kq[u, h], rq[u, h] = kq_h[ru], rq_h[ru]
            qr[u, h] = jnp.concatenate([kq_h[ru], rq_h[ru]], axis=0).astype(BF16)
            bkk[u, h] = jnp.concatenate([bk_h[ru], kk_h2[ru]], axis=0).astype(BF16)
            bk2[u, h], kk2[u, h] = bk2_h[ru], kk2_h[ru]
            vs[u, h] = v[ru, sl]

    gs = {q: _mm_nt(qr[q], bkk[q]) for q in pairs}
    ga = {q: jnp.where(strict2, gs[q][:RU, :], 0.0) for q in pairs}
    gm = {q: jnp.where(incl2, gs[q][RU:, :], 0.0) for q in pairs}
    t_inv = dict(zip(pairs, _tri_inverse_all([ga[q][:, :RU] for q in pairs], row, col, L)))
    zeros = jnp.zeros((RU, HEAD_DIM), F32)
    akv = {q: _mm(ga[q], jnp.concatenate([zeros, vs[q]], axis=0)) for q in pairs}

    state = {}
    ys = {}
    for u in units:
        if L == RU:
            b = (u * RU) // t_step
            s0 = [state.get((b, h), None) for h in heads]
            s0 = [s_scr[b, h] if s0[h] is None else s0[h] for h in heads]
            ws = [_mm_nt(qr[u, h], s0[h]) for h in heads]
            w1 = [x[:RU] for x in ws]
            wr = [x[RU:] for x in ws]
        else:
            n_in = RU // L
            s0s = [[s_scr[u * n_in + i, h] for h in heads] for i in range(n_in)]
            w1, wr = [], []
            for h in heads:
                parts = [_mm_nt(jnp.concatenate([kq[u, h][i * L:(i + 1) * L], rq[u, h][i * L:(i + 1) * L]], axis=0),
                                s0s[i][h]) for i in range(n_in)]
                w1.append(jnp.concatenate([x[:L] for x in parts], axis=0))
                wr.append(jnp.concatenate([x[L:] for x in parts], axis=0))
        us = [-_mm(t_inv[u, h], w1[h] + akv[u, h]) for h in heads]
        uv = [jnp.concatenate([us[h], vs[u, h]], axis=0) for h in heads]
        for h in heads:
            ys[u, h] = wr[h] + _mm(gm[u, h], uv[h])
        if L == RU:
            for h in heads:
                bkk2 = jnp.concatenate([bk2[u, h], kk2[u, h]], axis=0)
                state[b, h] = s0[h] * p_last[u][:, sls[h]] + _mm_tn(uv[h], bkk2)
        else:
            for h in heads:
                for i in range(n_in):
                    rows = slice(i * L, (i + 1) * L)
                    uv_i = jnp.concatenate([us[h][rows], vs[u, h][rows]], axis=0)
                    bkk2_i = jnp.concatenate([bk2[u, h][rows], kk2[u, h][rows]], axis=0)
                    state[u * n_in + i, h] = (s0s[i][h] * p_last[u * n_in + i][:, sls[h]]
                                              + _mm_tn(uv_i, bkk2_i))
    for (b, h), s_new in state.items():
        s_scr[b, h] = s_new

    rows_out = []
    for u in units:
        outs = []
        for h in heads:
            sl = sls[h]
            y = ys[u, h]
            mu_y = jnp.mean(y, axis=-1, keepdims=True)
            yc = y - mu_y
            var = jnp.mean(yc * yc, axis=-1, keepdims=True)
            yn = yc * lax.rsqrt(var + RWKV_GN_EPS) * lnw_ref[:, sl] + lnb_ref[:, sl]
            bonus = jnp.sum(rkk[rus[u], sl], axis=-1, keepdims=True)
            outs.append(yn + bonus * vs[u, h])
        rows_out.append(jnp.concatenate(outs, axis=-1))
    y_all = rows_out[0] if n_units == 1 else jnp.concatenate(rows_out, axis=0)
    oa_ref[...] = (y_all * _silu(za_ref[...])).astype(oa_ref.dtype)

    @pl.when(j == pl.num_programs(1) - 1)
    def _():
        st_ref[...] = s_scr[...]


def _rwkv(p_shift, z_a, shift0, s0, rows, wd2, wi2, n_seq, t_step, chunk):
    b = s0.shape[0]
    sw = p_shift.shape[-1]
    n_heads = s0.shape[1]
    da = n_heads * HEAD_DIM
    m = n_seq * t_step
    nj = p_shift.shape[0] // (b * t_step)
    tok = lambda i, j: (i * nj + j, 0)
    const2 = lambda i, j: (0, 0)
    row_specs = [pl.BlockSpec(x.shape, const2) for x in rows]
    mu, w0, a0, k_k, k_a, r_k, ln_w, ln_b = rows
    in_specs = [pl.BlockSpec((m, sw), tok),
                pl.BlockSpec((m, da), tok),
                pl.BlockSpec((n_seq, 1, sw), lambda i, j: (i, 0, 0)),
                pl.BlockSpec((n_seq, n_heads, HEAD_DIM, HEAD_DIM), lambda i, j: (i, 0, 0, 0)),
                row_specs[0], pl.BlockSpec(wd2.shape, const2), row_specs[1],
                pl.BlockSpec(wi2.shape, const2)] + row_specs[2:]
    out_specs = [pl.BlockSpec((m, da), tok),
                 pl.BlockSpec((n_seq, n_heads, HEAD_DIM, HEAD_DIM), lambda i, j: (i, 0, 0, 0))]
    out_shape = [jax.ShapeDtypeStruct((p_shift.shape[0], da), BF16),
                 jax.ShapeDtypeStruct(s0.shape, F32)]
    return pl.pallas_call(
        functools.partial(_rwkv_kernel, n_seq=n_seq, chunk=chunk, n_heads=n_heads, lora_w=wd2.shape[0]),
        grid=(b // n_seq, nj),
        in_specs=in_specs, out_specs=out_specs, out_shape=out_shape,
        scratch_shapes=[pltpu.VMEM((n_seq, 1, sw), F32),
                        pltpu.VMEM((n_seq, n_heads, HEAD_DIM, HEAD_DIM), F32)],
        compiler_params=pltpu.CompilerParams(vmem_limit_bytes=VMEM_LIMIT),
        name="rwkv7",
    )(p_shift, z_a, shift0, s0, mu, wd2, w0, wi2, a0, k_k, k_a, r_k, ln_w, ln_b)


def _mlstm_kernel(qk_ref, v_ref, g_ref, o_ref, zb_ref, conv0_ref, c0_ref, n0_ref, m0_ref,
                  cw_ref, cb_ref, gb_ref, gnw_ref,
                  ob_ref, ct_ref, nt_ref, mt_ref, xbuf, cn_scr, m_scr, *, n_seq, n_heads, conv_w):
    j = pl.program_id(1)
    M = qk_ref.shape[0]
    L = M // n_seq
    cc = qk_ref.shape[1]
    db = n_heads * HEAD_DIM
    pad = SUBLANES
    heads = range(n_heads)
    seqs = range(n_seq)

    @pl.when(j == 0)
    def _():
        xbuf[:, 0:pad, :] = conv0_ref[...]
        cn_scr[:, :, 0:HEAD_DIM, :] = c0_ref[...]
        cn_scr[:, :, HEAD_DIM:, :] = n0_ref[...]
        m_scr[...] = m0_ref[...]

    xbuf[:, pad:pad + L, :] = qk_ref[...].reshape(n_seq, L, cc)
    conv = cb_ref[...]
    for tap in range(conv_w):
        conv = conv + xbuf[:, pl.ds(pad - (conv_w - 1) + tap, L), :].reshape(M, cc) * cw_ref[tap:tap + 1, :]
    xbuf[:, 0:pad, :] = xbuf[:, L:L + pad, :]
    qk = _silu(conv)
    q = qk[:, 0:db]
    k = qk[:, db:] * (1.0 / math.sqrt(HEAD_DIM))
    v = v_ref[...]

    row = lax.broadcasted_iota(jnp.int32, (M, M), 0)
    col = lax.broadcasted_iota(jnp.int32, (M, M), 1)
    same_seq = _shr(row, int(math.log2(L))) == _shr(col, int(math.log2(L)))
    incl = (row >= col) & same_seq

    g = g_ref[...] + gb_ref[...]
    bcum = pltpu.roll(_cumsum_time(-_softplus(-g), incl.astype(BF16)), LANES - n_heads, axis=1)
    m_prev = m_scr[...]
    m_prev_rows = _rows_of_seq(m_prev, n_seq, L)
    x_all = g - bcum
    pos = lax.broadcasted_iota(jnp.int32, (M, LANES), 0) & (L - 1)
    m_all = bcum + jnp.maximum(_cummax_time(x_all, L, pos), m_prev_rows)
    bm_all = bcum - m_all
    w_in_all = jnp.exp(bcum + m_prev_rows - m_all)
    e_negm_all = jnp.exp(-m_all)
    m_new = _last_rows(m_all, n_seq)
    b_last = _last_rows(bcum, n_seq)
    ws_all = jnp.exp(_rows_of_seq(b_last - m_new, n_seq, L) + x_all)
    dec_all = jnp.exp(b_last + m_prev - m_new)
    m_scr[...] = m_new
    x_t = jnp.transpose(x_all)

    ones = jnp.ones((M, SUBLANES), F32)
    sls = [slice(h * HEAD_DIM, (h + 1) * HEAD_DIM) for h in heads]
    qs = [q[:, sl] for sl in sls]
    ks = [k[:, sl] for sl in sls]
    v1 = [jnp.concatenate([v[:, sl], ones], axis=-1) for sl in sls]
    cn = [[cn_scr[b, h] for h in heads] for b in seqs]

    qk_t = [_mm_nt(qs[h], ks[h]) for h in heads]
    if n_seq == 1:
        qc = [_mm_nt(qs[h], cn[0][h]) for h in heads]
    else:
        qc = [jnp.concatenate([_mm_nt(qs[h][b * L:(b + 1) * L], cn[b][h]) for b in seqs], axis=0) for h in heads]
    w_ts = [jnp.where(incl, jnp.exp(bm_all[:, h:h + 1] + x_t[h:h + 1, :]), 0.0) for h in heads]
    s = [qk_t[h] * w_ts[h] for h in heads]
    numden = [_mm(s[h], v1[h]) + w_in_all[:, h:h + 1] * qc[h] for h in heads]
    den = [jnp.maximum(jnp.abs(numden[h][:, HEAD_DIM:HEAD_DIM + 1]), e_negm_all[:, h:h + 1]) for h in heads]
    hh = [numden[h][:, 0:HEAD_DIM] / den[h] for h in heads]
    outs = [hh[h] * lax.rsqrt(jnp.mean(hh[h] * hh[h], axis=-1, keepdims=True) + MLSTM_GN_EPS) * gnw_ref[:, sls[h]]
            for h in heads]

    v1w = [v1[h] * ws_all[:, h:h + 1] for h in heads]
    for h in heads:
        for b in seqs:
            rows = slice(b * L, (b + 1) * L)
            cn_scr[b, h] = dec_all[b][:, h:h + 1] * cn[b][h] + _mm_tn(v1w[h][rows], ks[h][rows])

    hb = jnp.concatenate(outs, axis=-1)
    ob_ref[...] = (_sigmoid(o_ref[...]) * hb * _silu(zb_ref[...])).astype(ob_ref.dtype)

    @pl.when(j == pl.num_programs(1) - 1)
    def _():
        ct_ref[...] = cn_scr[:, :, 0:HEAD_DIM, :]
        nt_ref[...] = cn_scr[:, :, HEAD_DIM:, :]
        mt_ref[...] = m_scr[...]


def _mlstm(qk_pre, v_b, gates, o_b, z_b, conv0, c0, n0, m0, conv_w, conv_b, gate_b, gn_w, n_seq, L):
    b = c0.shape[0]
    cc = qk_pre.shape[-1]
    n_heads = c0.shape[1]
    db = n_heads * HEAD_DIM
    cw = conv_w.shape[0]
    m = n_seq * L
    nj = qk_pre.shape[0] // (b * L)
    tok = lambda i, j: (i * nj + j, 0)
    per_b3 = lambda i, j: (i, 0, 0)
    per_b4 = lambda i, j: (i, 0, 0, 0)
    const2 = lambda i, j: (0, 0)
    in_specs = [pl.BlockSpec((m, cc), tok),
                pl.BlockSpec((m, db), tok),
                pl.BlockSpec((m, LANES), tok),
                pl.BlockSpec((m, db), tok),
                pl.BlockSpec((m, db), tok),
                pl.BlockSpec((n_seq, SUBLANES, cc), per_b3),
                pl.BlockSpec((n_seq, n_heads, HEAD_DIM, HEAD_DIM), per_b4),
                pl.BlockSpec((n_seq, n_heads, SUBLANES, HEAD_DIM), per_b4),
                pl.BlockSpec((n_seq, 1, LANES), per_b3),
                pl.BlockSpec(conv_w.shape, const2),
                pl.BlockSpec(conv_b.shape, const2),
                pl.BlockSpec(gate_b.shape, const2),
                pl.BlockSpec(gn_w.shape, const2)]
    out_specs = [pl.BlockSpec((m, db), tok),
                 pl.BlockSpec((n_seq, n_heads, HEAD_DIM, HEAD_DIM), per_b4),
                 pl.BlockSpec((n_seq, n_heads, SUBLANES, HEAD_DIM), per_b4),
                 pl.BlockSpec((n_seq, 1, LANES), per_b3)]
    out_shape = [jax.ShapeDtypeStruct((qk_pre.shape[0], db), BF16),
                 jax.ShapeDtypeStruct(c0.shape, F32),
                 jax.ShapeDtypeStruct(n0.shape, F32),
                 jax.ShapeDtypeStruct(m0.shape, F32)]
    return pl.pallas_call(
        functools.partial(_mlstm_kernel, n_seq=n_seq, n_heads=n_heads, conv_w=cw),
        grid=(b // n_seq, nj),
        in_specs=in_specs, out_specs=out_specs, out_shape=out_shape,
        scratch_shapes=[pltpu.VMEM((n_seq, L + SUBLANES, cc), F32),
                        pltpu.VMEM((n_seq, n_heads, HEAD_DIM + SUBLANES, HEAD_DIM), F32),
                        pltpu.VMEM((n_seq, 1, LANES), F32)],
        compiler_params=pltpu.CompilerParams(vmem_limit_bytes=VMEM_LIMIT),
        name="mlstm",
    )(qk_pre, v_b, gates, o_b, z_b, conv0, c0, n0, m0, conv_w, conv_b, gate_b, gn_w)


def _out_kernel(oa_ref, ob_ref, gla_ref, glb_ref, x_ref, gate_ref, wua_ref, wub_ref, wo_ref, gf_ref,
                y_ref, *, final_norm):
    nb, tt, d = x_ref.shape
    m = nb * tt
    ua = jnp.dot(oa_ref[...], wua_ref[...], preferred_element_type=F32)
    ub = jnp.dot(ob_ref[...], wub_ref[...], preferred_element_type=F32)
    merged = _sigmoid(gla_ref[...].reshape(m, d)) * ua + _sigmoid(glb_ref[...].reshape(m, d)) * ub
    mo = jnp.dot(merged.astype(BF16), wo_ref[...], preferred_element_type=F32)
    xn = x_ref[...] + gate_ref[...] * mo.reshape(nb, tt, d)
    if final_norm:
        xn = xn * lax.rsqrt(jnp.mean(xn * xn, axis=-1, keepdims=True) + NORM_EPS) * gf_ref[...]
    y_ref[...] = xn


def _out(out_a, out_b, gl_a, gl_b, x, gate, w_up_a, w_up_b, w_out, g_final, nb, tt, final_norm):
    b, t, d = x.shape
    nj = t // tt
    tok = lambda i, j: (i, j, 0)
    tok2 = lambda i, j: (i * nj + j, 0)
    const2 = lambda i, j: (0, 0)
    wspec = lambda w: pl.BlockSpec(w.shape, const2, pipeline_mode=pl.Buffered(1))
    in_specs = [pl.BlockSpec((nb * tt, out_a.shape[-1]), tok2),
                pl.BlockSpec((nb * tt, out_b.shape[-1]), tok2),
                pl.BlockSpec((nb, tt, d), tok),
                pl.BlockSpec((nb, tt, d), tok),
                pl.BlockSpec((nb, tt, d), tok),
                pl.BlockSpec((nb, 1, d), lambda i, j: (i, 0, 0)),
                wspec(w_up_a), wspec(w_up_b), wspec(w_out),
                pl.BlockSpec((1, d), const2)]
    return pl.pallas_call(
        functools.partial(_out_kernel, final_norm=final_norm),
        grid=(b // nb, nj),
        in_specs=in_specs,
        out_specs=pl.BlockSpec((nb, tt, d), tok),
        out_shape=jax.ShapeDtypeStruct((b, t, d), F32),
        compiler_params=pltpu.CompilerParams(vmem_limit_bytes=VMEM_LIMIT),
        name="merge_out",
    )(out_a, out_b, gl_a, gl_b, x, gate, w_up_a, w_up_b, w_out, g_final.reshape(1, d))


def _token_tiling(b, t, target=256):
    if t >= target:
        tt = target
        while t % tt:
            tt //= 2
        return 1, tt
    nb = max(1, min(b, target // t))
    while b % nb:
        nb -= 1
    return nb, t


def _layer(x, mod, states, lp, g_final, final_norm):
    b, t, d = x.shape
    shift0, s0, conv0, c0, n0, m0 = states
    n_hb = c0.shape[1]
    cw = lp["conv_w"].shape[0]
    flat = lambda a: a.reshape(b * t, a.shape[-1])

    ada_shift, ada_scale, ada_gate = (mod[:, None, i * d:(i + 1) * d] for i in range(3))
    nb, tt = _token_tiling(b, t)
    p_shift, z_a, qk_pre, v_b, o_b, z_b, gl_a, gl_b, gates = _inproj(
        x, ada_scale, ada_shift, lp["g_norm"], lp["w_segs"], nb, tt)

    L = math.gcd(t, ROWS)
    t_step = math.gcd(t, RWKV_STEP_ROWS)
    n_seq = math.gcd(b, RWKV_STEP_ROWS // t_step) if t_step == t else 1
    row = lambda a: a.reshape(1, -1)
    rows = tuple(row(lp[n]) for n in ("mu_shift", "w0", "a0", "k_k", "k_a", "r_k", "ln_w", "ln_b"))
    out_a, s_t = _rwkv(flat(p_shift), flat(z_a), shift0[:, None, :], s0, rows,
                       lp["w_decay2"], lp["w_iclr2"], n_seq, t_step, L)

    conv0p = jnp.pad(conv0, ((0, 0), (SUBLANES - (cw - 1), 0), (0, 0)))
    n0p = jnp.broadcast_to(n0[:, :, None, :], (b, n_hb, SUBLANES, HEAD_DIM))
    m0p = jnp.pad(m0, ((0, 0), (0, LANES - n_hb)))[:, None, :]
    gate_b = jnp.zeros((1, LANES), F32).at[0, 0:n_hb].set(lp["b_i"]).at[0, n_hb:2 * n_hb].set(lp["b_f"])
    L_b = math.gcd(t, MLSTM_ROWS)
    n_seq_b = math.gcd(b, MLSTM_ROWS // L_b) if L_b == t else 1
    out_b, c_t, n_t, m_t = _mlstm(flat(qk_pre), flat(v_b), flat(gates), flat(o_b), flat(z_b), conv0p, c0, n0p, m0p,
                                  lp["conv_w"], row(lp["conv_b"]), gate_b, row(lp["gn_w"]), n_seq_b, L_b)

    y = _out(out_a, out_b, gl_a, gl_b, x, ada_gate, lp["w_up_a"], lp["w_up_b"], lp["w_out"],
             g_final, nb, tt, final_norm)
    conv_t = qk_pre[:, t - (cw - 1):] if t >= cw - 1 else jnp.concatenate([conv0, qk_pre], axis=1)[:, t:]
    new_states = (p_shift[:, t - 1], s_t, conv_t, c_t, n_t[:, :, 0, :], m_t[:, 0, :n_hb])
    return y, new_states


def _trunk(x, mods, states, layers, g_final):
    depth = len(layers)
    new = [[] for _ in states]
    for l in range(depth):
        st = tuple(s[l] for s in states)
        x, st_new = _layer(x, mods[l], st, layers[l], g_final, final_norm=(l == depth - 1))
        for lst, s in zip(new, st_new):
            lst.append(s.astype(x.dtype))
    return x, tuple(jnp.stack(lst) for lst in new)


def kernel(x_prompt, x_sample, c_prompt, c_sample, state_rwkv_shift, state_rwkv_S, state_mlstm_conv, state_mlstm_C, state_mlstm_n, state_mlstm_m, g_norm, w_ada, b_ada, w_in, mu_shift, w_decay2, w0, w_iclr2, a0, k_k, k_a, r_k, ln_w, ln_b, conv_w, conv_b, b_i, b_f, gn_w, w_up_a, w_up_b, w_out, g_final):
    depth = g_norm.shape[0]
    bp, bs = x_prompt.shape[0], x_sample.shape[0]
    d = x_prompt.shape[-1]
    da, db = w_up_a.shape[1], w_up_b.shape[1]
    n_ha, n_hb = da // HEAD_DIM, db // HEAD_DIM
    sw = mu_shift.shape[-1]
    cc = conv_w.shape[-1]
    cw = conv_w.shape[1]
    dt = x_prompt.dtype

    sizes = (sw, da, cc, db, db, n_hb, n_hb, db, d, d)
    offs = [0]
    for s in sizes:
        offs.append(offs[-1] + s)
    seg = lambda w, i: w[:, offs[i]:offs[i + 1]]

    c_all = jnp.concatenate([c_prompt, c_sample], axis=0)
    n_c = c_all.shape[0]
    c_pad = jnp.pad(c_all, ((0, (-n_c) % SUBLANES), (0, 0)))

    layers, mods_p, mods_s = [], [], []
    for l in range(depth):
        w = w_in[l]
        wg = jnp.concatenate([seg(w, 5), seg(w, 6)], axis=1)
        wg = jnp.pad(wg, ((0, 0), (0, LANES - wg.shape[1])))
        w_segs = tuple(x.astype(BF16) for x in
                       (seg(w, 0), seg(w, 1), seg(w, 2), seg(w, 3), seg(w, 4), seg(w, 7), seg(w, 8), seg(w, 9), wg))
        layers.append(dict(
            g_norm=g_norm[l], w_segs=w_segs, mu_shift=mu_shift[l], w_decay2=w_decay2[l].astype(BF16),
            w0=w0[l], w_iclr2=w_iclr2[l].astype(BF16), a0=a0[l], k_k=k_k[l], k_a=k_a[l], r_k=r_k[l],
            ln_w=ln_w[l], ln_b=ln_b[l], conv_w=conv_w[l], conv_b=conv_b[l], b_i=b_i[l], b_f=b_f[l],
            gn_w=gn_w[l], w_up_a=w_up_a[l].astype(BF16), w_up_b=w_up_b[l].astype(BF16),
            w_out=w_out[l].astype(BF16)))
        mod = _ada(c_pad, w_ada[l], b_ada[l])
        mods_p.append(mod[:bp])
        mods_s.append(mod[bp:bp + bs])

    prompt_states = (
        jnp.zeros((depth, bp, sw), dt),
        jnp.zeros((depth, bp, n_ha, HEAD_DIM, HEAD_DIM), dt),
        jnp.zeros((depth, bp, cw - 1, cc), dt),
        jnp.zeros((depth, bp, n_hb, HEAD_DIM, HEAD_DIM), dt),
        jnp.zeros((depth, bp, n_hb, HEAD_DIM), dt),
        jnp.zeros((depth, bp, n_hb), dt),
    )
    sample_states = (state_rwkv_shift, state_rwkv_S, state_mlstm_conv,
                     state_mlstm_C, state_mlstm_n, state_mlstm_m)
    y_p, st_p = _trunk(x_prompt, mods_p, prompt_states, layers, g_final)
    y_s, st_s = _trunk(x_sample, mods_s, sample_states, layers, g_final)
    return (y_p, y_s) + st_p + st_s
```

```python
import functools
import math

import jax
import jax.numpy as jnp
from jax import lax
from jax.experimental import pallas as pl
from jax.experimental.pallas import tpu as pltpu

F32 = jnp.float32
BF16 = jnp.bfloat16

HEAD_DIM = 64
NORM_EPS = 1e-6
RWKV_GN_EPS = 64e-5
MLSTM_GN_EPS = 1e-6
ROWS = 64
RWKV_STEP_ROWS = 128
MLSTM_ROWS = 128
PROJ_ROWS = 256
SUBLANES = 8
LANES = 128
TRI_BASE = 16
VMEM_LIMIT = 56 * 1024 * 1024

NT_DIMS = (((1,), (1,)), ((), ()))
TN_DIMS = (((0,), (0,)), ((), ()))


def _mm(a, b):
    return jnp.dot(a.astype(BF16), b.astype(BF16), preferred_element_type=F32)


def _mm_nt(a, b):
    return lax.dot_general(a.astype(BF16), b.astype(BF16), NT_DIMS, preferred_element_type=F32)


def _mm_tn(a, b):
    return lax.dot_general(a.astype(BF16), b.astype(BF16), TN_DIMS, preferred_element_type=F32)


def _sigmoid(x):
    return 1.0 / (1.0 + jnp.exp(-x))


def _silu(x):
    return x * _sigmoid(x)


def _softplus(x):
    return jnp.maximum(x, 0.0) + jnp.log(1.0 + jnp.exp(-jnp.abs(x)))


def _cumsum_time(x, tri_bf16):
    hi = x.astype(BF16)
    r1 = x - hi.astype(F32)
    mid = r1.astype(BF16)
    lo = (r1 - mid.astype(F32)).astype(BF16)
    dot = functools.partial(jnp.dot, preferred_element_type=F32)
    return dot(tri_bf16, hi) + dot(tri_bf16, mid) + dot(tri_bf16, lo)


def _shr(x, n):
    return lax.shift_right_logical(x, jnp.full(x.shape, n, x.dtype))


def _rows_of_seq(x, n_seq, L):
    if n_seq == 1:
        return x[0]
    return jnp.broadcast_to(x, (n_seq, L, x.shape[-1])).reshape(n_seq * L, x.shape[-1])


def _last_rows(x, n_seq):
    m, n = x.shape
    L = m // n_seq
    return x.reshape(n_seq, L, n)[:, L - 1:L, :]


def _head_sums(x):
    m, n = x.shape
    parts = [jnp.broadcast_to(jnp.sum(x[:, o:o + HEAD_DIM], axis=-1, keepdims=True), (m, HEAD_DIM))
             for o in range(0, n, HEAD_DIM)]
    return jnp.concatenate(parts, axis=-1)


def _cummax_time(x, L, pos):
    y = x
    sh = 1
    while sh < L:
        y = jnp.where(pos >= sh, jnp.maximum(y, pltpu.roll(y, sh, axis=0)), y)
        sh *= 2
    return y


def _tri_inverse_all(mats, row, col, L):
    bs = min(TRI_BASE, L)
    sh = int(math.log2(bs))
    same = _shr(row, sh) == _shr(col, sh)
    eye = (row == col).astype(F32)
    ns = [jnp.where(same, -a, 0.0) for a in mats]
    ts = [eye + n for n in ns]
    ps = ns
    k = 1
    while 2 * k < bs:
        ps = [_mm(p, p) for p in ps]
        ts = [t + _mm(t, p) for t, p in zip(ts, ps)]
        k *= 2
    size = bs
    while size < L:
        sh = int(math.log2(size))
        lower_left = (_shr(row, sh + 1) == _shr(col, sh + 1)) & (_shr(row, sh) != _shr(col, sh))
        offs = [jnp.where(lower_left, a, 0.0) for a in mats]
        tmp = [_mm(t, o) for t, o in zip(ts, offs)]
        ts = [t - _mm(x, t) for t, x in zip(ts, tmp)]
        size *= 2
    return ts


def _ada_kernel(c_ref, w_ref, b_ref, o_ref):
    o_ref[...] = _mm(_silu(c_ref[...]), w_ref[...]) + b_ref[...]


def _ada(c, w_ada, b_ada):
    n, d = c.shape
    n3 = w_ada.shape[1]
    tn = d
    return pl.pallas_call(
        _ada_kernel,
        grid=(n3 // tn,),
        in_specs=[pl.BlockSpec((n, d), lambda j: (0, 0)),
                  pl.BlockSpec((d, tn), lambda j: (0, j)),
                  pl.BlockSpec((1, tn), lambda j: (0, j))],
        out_specs=pl.BlockSpec((n, tn), lambda j: (0, j)),
        out_shape=jax.ShapeDtypeStruct((n, n3), F32),
        name="ada",
    )(c, w_ada, b_ada.reshape(1, n3))


def _inproj_kernel(x_ref, sc_ref, sh_ref, g_ref, shift0_ref, conv0_ref,
                   w_shift, w_za, w_qk, w_vb, w_ob, w_zb, w_gla, w_glb, w_gt,
                   mu_ref, w0_ref, a0_ref, kk_ref, ka_ref, rk_ref, wd2_ref, wi2_ref, cw_ref, cb_ref,
                   kq_o, rq_o, bk_o, kkk_o, bk2_o, kk2_o, va_o, bv_o, sz_o, plast_o, shst_o,
                   q_o, k_o, vb_o, gt_o, gob_o, sa_o, sb_o, cst_o,
                   prev_scr, xbuf, *, chunk, lora_w, conv_w):
    j = pl.program_id(1)
    nb, tt, d = x_ref.shape
    M = nb * tt
    L = chunk
    n_chunks = M // L
    da = va_o.shape[-1]
    db = vb_o.shape[-1]
    cc = xbuf.shape[-1]
    pad = SUBLANES

    @pl.when(j == 0)
    def _():
        prev_scr[...] = shift0_ref[...]
        xbuf[:, 0:pad, :] = conv0_ref[...]

    x = x_ref[...]
    y = x * lax.rsqrt(jnp.mean(x * x, axis=-1, keepdims=True) + NORM_EPS) * g_ref[...]
    h = (y * (1.0 + sc_ref[...]) + sh_ref[...]).reshape(M, d).astype(BF16)
    proj = lambda w_ref: jnp.dot(h, w_ref[...], preferred_element_type=F32)

    p = proj(w_shift)
    row_w = lax.broadcasted_iota(jnp.int32, p.shape, 0)
    first = (row_w & (tt - 1)) == 0
    prev = jnp.where(first, _rows_of_seq(prev_scr[...], nb, tt), pltpu.roll(p, 1, axis=0))
    last = _last_rows(p, nb)
    prev_scr[...] = last
    shst_o[...] = last
    ps = p + mu_ref[...] * (prev - p)
    r = ps[:, 0:da]
    k = ps[:, da:2 * da]
    v = ps[:, 2 * da:3 * da]
    wl = ps[:, 3 * da:3 * da + lora_w]
    al = ps[:, 3 * da + lora_w:]

    w = w0_ref[...] + _mm(jnp.tanh(wl), wd2_ref[...])
    lw = -jnp.exp(-_softplus(-w) - 0.5)
    a = _sigmoid(a0_ref[...] + _mm(al, wi2_ref[...]))
    kkr = k * kk_ref[...]
    kk = kkr / jnp.maximum(jnp.sqrt(_head_sums(kkr * kkr)), 1e-12)
    k2 = k * (1.0 + (a - 1.0) * ka_ref[...])
    b = kk * a
    va_o[...] = v
    bv_o[...] = _head_sums(r * k2 * rk_ref[...]) * v
    sz_o[...] = _silu(proj(w_za))

    lg = int(math.log2(L))
    row_m = lax.broadcasted_iota(jnp.int32, (M, M), 0)
    col_m = lax.broadcasted_iota(jnp.int32, (M, M), 1)
    tri = ((row_m >= col_m) & (_shr(row_m, lg) == _shr(col_m, lg))).astype(BF16)
    c = _cumsum_time(lw, tri)
    c_last = _last_rows(c, n_chunks)
    plast_o[...] = jnp.exp(c_last)
    e_nc = jnp.exp(-c)
    e_cl = jnp.exp(_rows_of_seq(c_last, n_chunks, L) - c)
    kq_o[...] = (kk * jnp.exp(c - lw)).astype(kq_o.dtype)
    rq_o[...] = (r * jnp.exp(c)).astype(rq_o.dtype)
    bk_o[...] = (b * e_nc).astype(bk_o.dtype)
    kkk_o[...] = (k2 * e_nc).astype(kkk_o.dtype)
    bk2_o[...] = (b * e_cl).astype(bk2_o.dtype)
    kk2_o[...] = (k2 * e_cl).astype(kk2_o.dtype)

    xbuf[:, pad:pad + tt, :] = proj(w_qk).reshape(nb, tt, cc)
    conv = cb_ref[...]
    for tap in range(conv_w):
        conv = conv + xbuf[:, pl.ds(pad - (conv_w - 1) + tap, tt), :].reshape(M, cc) * cw_ref[tap:tap + 1, :]
    tail = xbuf[:, tt:tt + pad, :]
    xbuf[:, 0:pad, :] = tail
    cst_o[...] = tail
    qk = _silu(conv)
    q_o[...] = qk[:, 0:db].astype(q_o.dtype)
    k_o[...] = (qk[:, db:] * (1.0 / math.sqrt(HEAD_DIM))).astype(k_o.dtype)
    vb_o[...] = proj(w_vb)
    gt_o[...] = proj(w_gt)
    gob_o[...] = _sigmoid(proj(w_ob)) * _silu(proj(w_zb))

    sa_o[...] = _sigmoid(proj(w_gla)).astype(sa_o.dtype)
    sb_o[...] = _sigmoid(proj(w_glb)).astype(sb_o.dtype)


def _inproj(x, scale, shift, g_norm, shift0, conv0p, lp, nb, tt, chunk, act_dtype):
    b, t, d = x.shape
    n_tok = b * t
    m = nb * tt
    nj = t // tt
    sw = shift0.shape[-1]
    cc = conv0p.shape[-1]
    da, db = lp["w_up_a"].shape[0], lp["w_up_b"].shape[0]
    cw = lp["conv_w"].shape[0]
    row = lambda a: a.reshape(1, -1)
    rows = [row(lp[n]) for n in ("mu_shift", "w0", "a0", "k_k", "k_a", "r_k")]
    smalls = rows + [lp["w_decay2"], lp["w_iclr2"], lp["conv_w"], row(lp["conv_b"])]
    const = lambda i, j: (0, 0)
    tok = lambda i, j: (i * nj + j, 0)
    per_b = lambda i, j: (i, 0, 0)
    in_specs = [pl.BlockSpec((nb, tt, d), lambda i, j: (i, j, 0)),
                pl.BlockSpec((nb, 1, d), per_b),
                pl.BlockSpec((nb, 1, d), per_b),
                pl.BlockSpec((1, d), const),
                pl.BlockSpec((nb, 1, sw), per_b),
                pl.BlockSpec((nb, SUBLANES, cc), per_b)]
    in_specs += [pl.BlockSpec(w.shape, const, pipeline_mode=pl.Buffered(1)) for w in lp["w_segs"]]
    in_specs += [pl.BlockSpec(a.shape, const) for a in smalls]
    tok_out = lambda n, dt: (pl.BlockSpec((m, n), tok), jax.ShapeDtypeStruct((n_tok, n), dt))
    outs = [tok_out(da, act_dtype)] * 6 + [tok_out(da, F32)] * 3
    outs += [(pl.BlockSpec((m // chunk, 1, da), lambda i, j: (i * nj + j, 0, 0)),
              jax.ShapeDtypeStruct((n_tok // chunk, 1, da), F32)),
             (pl.BlockSpec((nb, 1, sw), per_b), jax.ShapeDtypeStruct((b, 1, sw), F32))]
    outs += [tok_out(db, act_dtype)] * 2 + [tok_out(db, F32), tok_out(LANES, F32), tok_out(db, F32)]
    outs += [tok_out(d, BF16)] * 2
    outs += [(pl.BlockSpec((nb, SUBLANES, cc), per_b), jax.ShapeDtypeStruct((b, SUBLANES, cc), F32))]
    return pl.pallas_call(
        functools.partial(_inproj_kernel, chunk=chunk, lora_w=lp["w_decay2"].shape[0], conv_w=cw),
        grid=(b // nb, nj),
        in_specs=in_specs,
        out_specs=[o[0] for o in outs], out_shape=[o[1] for o in outs],
        scratch_shapes=[pltpu.VMEM((nb, 1, sw), F32),
                        pltpu.VMEM((nb, tt + SUBLANES, cc), F32)],
        compiler_params=pltpu.CompilerParams(vmem_limit_bytes=VMEM_LIMIT),
        name="inproj",
    )(x, scale, shift, g_norm.reshape(1, d), shift0, conv0p, *lp["w_segs"], *smalls)


def _rwkv_kernel(kq_ref, rq_ref, bk_ref, kkk_ref, bk2_ref, kk2_ref, v_ref, bv_ref, sz_ref, plast_ref,
                 s0_ref, lnw_ref, lnb_ref, oa_ref, st_ref, s_scr, *, n_seq, chunk, n_heads):
    j = pl.program_id(1)
    M = kq_ref.shape[0]
    L = chunk
    t_step = M // n_seq
    RU = min(M, ROWS)
    n_units = M // RU
    heads = range(n_heads)
    units = range(n_units)
    pairs = [(u, h) for u in units for h in heads]

    @pl.when(j == 0)
    def _():
        s_scr[...] = s0_ref[...]

    lg = int(math.log2(L))
    row = lax.broadcasted_iota(jnp.int32, (RU, RU), 0)
    col = lax.broadcasted_iota(jnp.int32, (RU, RU), 1)
    row2 = lax.broadcasted_iota(jnp.int32, (RU, 2 * RU), 0)
    col2 = lax.broadcasted_iota(jnp.int32, (RU, 2 * RU), 1) & (RU - 1)
    same2 = _shr(row2, lg) == _shr(col2, lg)
    incl2 = (row2 >= col2) & same2
    strict2 = (row2 > col2) & same2

    p_last = plast_ref[...]
    sls = [slice(h * HEAD_DIM, (h + 1) * HEAD_DIM) for h in heads]
    rus = [slice(u * RU, (u + 1) * RU) for u in units]
    blk = lambda ref, q: ref[rus[q[0]], sls[q[1]]]
    kq = {q: blk(kq_ref, q) for q in pairs}
    rq = {q: blk(rq_ref, q) for q in pairs}
    bk2 = {q: blk(bk2_ref, q) for q in pairs}
    kk2 = {q: blk(kk2_ref, q) for q in pairs}
    vs = {q: blk(v_ref, q) for q in pairs}
    qr = {q: jnp.concatenate([kq[q], rq[q]], axis=0).astype(BF16) for q in pairs}
    bkk = {q: jnp.concatenate([blk(bk_ref, q), blk(kkk_ref, q)], axis=0).astype(BF16) for q in pairs}

    gs = {q: _mm_nt(qr[q], bkk[q]) for q in pairs}
    ga = {q: jnp.where(strict2, gs[q][:RU, :], 0.0) for q in pairs}
    gm = {q: jnp.where(incl2, gs[q][RU:, :], 0.0) for q in pairs}
    t_inv = dict(zip(pairs, _tri_inverse_all([ga[q][:, :RU] for q in pairs], row, col, L)))
    zeros = jnp.zeros((RU, HEAD_DIM), F32)
    akv = {q: _mm(ga[q], jnp.concatenate([zeros, vs[q]], axis=0)) for q in pairs}

    state = {}
    ys = {}
    for u in units:
        if L == RU:
            b = (u * RU) // t_step
            s0 = [state[b, h] if (b, h) in state else s_scr[b, h] for h in heads]
            ws = [_mm_nt(qr[u, h], s0[h]) for h in heads]
            w1 = [x[:RU] for x in ws]
            wr = [x[RU:] for x in ws]
        else:
            n_in = RU // L
            s0s = [[s_scr[u * n_in + i, h] for h in heads] for i in range(n_in)]
            w1, wr = [], []
            for h in heads:
                parts = [_mm_nt(jnp.concatenate([kq[u, h][i * L:(i + 1) * L], rq[u, h][i * L:(i + 1) * L]], axis=0),
                                s0s[i][h]) for i in range(n_in)]
                w1.append(jnp.concatenate([x[:L] for x in parts], axis=0))
                wr.append(jnp.concatenate([x[L:] for x in parts], axis=0))
        us = [-_mm(t_inv[u, h], w1[h] + akv[u, h]) for h in heads]
        uv = [jnp.concatenate([us[h], vs[u, h]], axis=0) for h in heads]
        for h in heads:
            ys[u, h] = wr[h] + _mm(gm[u, h], uv[h])
        if L == RU:
            for h in heads:
                bkk2 = jnp.concatenate([bk2[u, h], kk2[u, h]], axis=0)
                state[b, h] = s0[h] * p_last[u][:, sls[h]] + _mm_tn(uv[h], bkk2)
        else:
            for h in heads:
                for i in range(n_in):
                    rows = slice(i * L, (i + 1) * L)
                    uv_i = jnp.concatenate([us[h][rows], vs[u, h][rows]], axis=0)
                    bkk2_i = jnp.concatenate([bk2[u, h][rows], kk2[u, h][rows]], axis=0)
                    state[u * n_in + i, h] = (s0s[i][h] * p_last[u * n_in + i][:, sls[h]]
                                              + _mm_tn(uv_i, bkk2_i))
    for (b, h), s_new in state.items():
        s_scr[b, h] = s_new

    rows_out = []
    for u in units:
        outs = []
        for h in heads:
            y = ys[u, h]
            yc = y - jnp.mean(y, axis=-1, keepdims=True)
            var = jnp.mean(yc * yc, axis=-1, keepdims=True)
            outs.append(yc * lax.rsqrt(var + RWKV_GN_EPS))
        rows_out.append(jnp.concatenate(outs, axis=-1))
    yn = rows_out[0] if n_units == 1 else jnp.concatenate(rows_out, axis=0)
    oa_ref[...] = ((yn * lnw_ref[...] + lnb_ref[...] + bv_ref[...]) * sz_ref[...]).astype(oa_ref.dtype)

    @pl.when(j == pl.num_programs(1) - 1)
    def _():
        st_ref[...] = s_scr[...]


def _rwkv(acts, v, bv, sz, p_last, s0, ln_w, ln_b, n_seq, t_step, chunk):
    b = s0.shape[0]
    n_heads = s0.shape[1]
    n_tok, da = v.shape
    m = n_seq * t_step
    nj = n_tok // (b * t_step)
    tok = lambda i, j: (i * nj + j, 0)
    const2 = lambda i, j: (0, 0)
    st_spec = pl.BlockSpec((n_seq, n_heads, HEAD_DIM, HEAD_DIM), lambda i, j: (i, 0, 0, 0))
    in_specs = [pl.BlockSpec((m, da), tok)] * 9
    in_specs += [pl.BlockSpec((m // chunk, 1, da), lambda i, j: (i * nj + j, 0, 0)), st_spec,
                 pl.BlockSpec(ln_w.shape, const2), pl.BlockSpec(ln_b.shape, const2)]
    return pl.pallas_call(
        functools.partial(_rwkv_kernel, n_seq=n_seq, chunk=chunk, n_heads=n_heads),
        grid=(b // n_seq, nj),
        in_specs=in_specs,
        out_specs=[pl.BlockSpec((m, da), tok), st_spec],
        out_shape=[jax.ShapeDtypeStruct((n_tok, da), BF16), jax.ShapeDtypeStruct(s0.shape, F32)],
        scratch_shapes=[pltpu.VMEM((n_seq, n_heads, HEAD_DIM, HEAD_DIM), F32)],
        compiler_params=pltpu.CompilerParams(vmem_limit_bytes=VMEM_LIMIT),
        name="rwkv7",
    )(*acts, v, bv, sz, p_last, s0, ln_w, ln_b)


def _mlstm_kernel(q_ref, k_ref, v_ref, g_ref, gob_ref, c0_ref, n0_ref, m0_ref, gb_ref, gnw_ref,
                  ob_ref, ct_ref, nt_ref, mt_ref, cn_scr, m_scr, *, n_seq, n_heads):
    j = pl.program_id(1)
    M = q_ref.shape[0]
    L = M // n_seq
    heads = range(n_heads)
    seqs = range(n_seq)

    @pl.when(j == 0)
    def _():
        cn_scr[:, :, 0:HEAD_DIM, :] = c0_ref[...]
        cn_scr[:, :, HEAD_DIM:, :] = n0_ref[...]
        m_scr[...] = m0_ref[...]

    row = lax.broadcasted_iota(jnp.int32, (M, M), 0)
    col = lax.broadcasted_iota(jnp.int32, (M, M), 1)
    same_seq = _shr(row, int(math.log2(L))) == _shr(col, int(math.log2(L)))
    incl = (row >= col) & same_seq

    g = g_ref[...] + gb_ref[...]
    bcum = pltpu.roll(_cumsum_time(-_softplus(-g), incl.astype(BF16)), LANES - n_heads, axis=1)
    m_prev = m_scr[...]
    m_prev_rows = _rows_of_seq(m_prev, n_seq, L)
    x_all = g - bcum
    pos = lax.broadcasted_iota(jnp.int32, (M, LANES), 0) & (L - 1)
    m_all = bcum + jnp.maximum(_cummax_time(x_all, L, pos), m_prev_rows)
    bm_all = bcum - m_all
    w_in_all = jnp.exp(bcum + m_prev_rows - m_all)
    e_negm_all = jnp.exp(-m_all)
    m_new = _last_rows(m_all, n_seq)
    b_last = _last_rows(bcum, n_seq)
    ws_all = jnp.exp(_rows_of_seq(b_last - m_new, n_seq, L) + x_all)
    dec_all = jnp.exp(b_last + m_prev - m_new)
    m_scr[...] = m_new
    x_t = jnp.transpose(x_all)

    ones = jnp.ones((M, SUBLANES), F32)
    sls = [slice(h * HEAD_DIM, (h + 1) * HEAD_DIM) for h in heads]
    qs = [q_ref[:, sl] for sl in sls]
    ks = [k_ref[:, sl] for sl in sls]
    v1 = [jnp.concatenate([v_ref[:, sl], ones], axis=-1) for sl in sls]
    cn = [[cn_scr[b, h] for h in heads] for b in seqs]

    qk_t = [_mm_nt(qs[h], ks[h]) for h in heads]
    if n_seq == 1:
        qc = [_mm_nt(qs[h], cn[0][h]) for h in heads]
    else:
        qc = [jnp.concatenate([_mm_nt(qs[h][b * L:(b + 1) * L], cn[b][h]) for b in seqs], axis=0) for h in heads]
    w_ts = [jnp.where(incl, jnp.exp(bm_all[:, h:h + 1] + x_t[h:h + 1, :]), 0.0) for h in heads]
    s = [qk_t[h] * w_ts[h] for h in heads]
    numden = [_mm(s[h], v1[h]) + w_in_all[:, h:h + 1] * qc[h] for h in heads]
    den = [jnp.maximum(jnp.abs(numden[h][:, HEAD_DIM:HEAD_DIM + 1]), e_negm_all[:, h:h + 1]) for h in heads]
    hh = [numden[h][:, 0:HEAD_DIM] / den[h] for h in heads]
    outs = [hh[h] * lax.rsqrt(jnp.mean(hh[h] * hh[h], axis=-1, keepdims=True) + MLSTM_GN_EPS) for h in heads]

    v1w = [v1[h] * ws_all[:, h:h + 1] for h in heads]
    for h in heads:
        for b in seqs:
            rows = slice(b * L, (b + 1) * L)
            cn_scr[b, h] = dec_all[b][:, h:h + 1] * cn[b][h] + _mm_tn(v1w[h][rows], ks[h][rows])

    hb = jnp.concatenate(outs, axis=-1)
    ob_ref[...] = (hb * gnw_ref[...] * gob_ref[...]).astype(ob_ref.dtype)

    @pl.when(j == pl.num_programs(1) - 1)
    def _():
        ct_ref[...] = cn_scr[:, :, 0:HEAD_DIM, :]
        nt_ref[...] = cn_scr[:, :, HEAD_DIM:, :]
        mt_ref[...] = m_scr[...]


def _mlstm(q, k, v_b, gates, gob, c0, n0, m0, gate_b, gn_w, n_seq, L):
    b = c0.shape[0]
    n_heads = c0.shape[1]
    n_tok, db = v_b.shape
    m = n_seq * L
    nj = n_tok // (b * L)
    tok = lambda i, j: (i * nj + j, 0)
    per_b3 = lambda i, j: (i, 0, 0)
    per_b4 = lambda i, j: (i, 0, 0, 0)
    const2 = lambda i, j: (0, 0)
    c_spec = pl.BlockSpec((n_seq, n_heads, HEAD_DIM, HEAD_DIM), per_b4)
    n_spec = pl.BlockSpec((n_seq, n_heads, SUBLANES, HEAD_DIM), per_b4)
    m_spec = pl.BlockSpec((n_seq, 1, LANES), per_b3)
    in_specs = [pl.BlockSpec((m, db), tok), pl.BlockSpec((m, db), tok), pl.BlockSpec((m, db), tok),
                pl.BlockSpec((m, LANES), tok), pl.BlockSpec((m, db), tok),
                c_spec, n_spec, m_spec,
                pl.BlockSpec(gate_b.shape, const2), pl.BlockSpec(gn_w.shape, const2)]
    return pl.pallas_call(
        functools.partial(_mlstm_kernel, n_seq=n_seq, n_heads=n_heads),
        grid=(b // n_seq, nj),
        in_specs=in_specs,
        out_specs=[pl.BlockSpec((m, db), tok), c_spec, n_spec, m_spec],
        out_shape=[jax.ShapeDtypeStruct((n_tok, db), BF16), jax.ShapeDtypeStruct(c0.shape, F32),
                   jax.ShapeDtypeStruct(n0.shape, F32), jax.ShapeDtypeStruct(m0.shape, F32)],
        scratch_shapes=[pltpu.VMEM((n_seq, n_heads, HEAD_DIM + SUBLANES, HEAD_DIM), F32),
                        pltpu.VMEM((n_seq, 1, LANES), F32)],
        compiler_params=pltpu.CompilerParams(vmem_limit_bytes=VMEM_LIMIT),
        name="mlstm",
    )(q, k, v_b, gates, gob, c0, n0, m0, gate_b, gn_w)


def _out_kernel(oa_ref, ob_ref, sa_ref, sb_ref, x_ref, gate_ref, wua_ref, wub_ref, wo_ref, gf_ref,
                y_ref, *, nb, final_norm):
    m = x_ref.shape[0]
    ua = jnp.dot(oa_ref[...], wua_ref[...], preferred_element_type=F32)
    ub = jnp.dot(ob_ref[...], wub_ref[...], preferred_element_type=F32)
    merged = sa_ref[...] * ua + sb_ref[...] * ub
    mo = jnp.dot(merged.astype(BF16), wo_ref[...], preferred_element_type=F32)
    xn = x_ref[...] + _rows_of_seq(gate_ref[...], nb, m // nb) * mo
    if final_norm:
        xn = xn * lax.rsqrt(jnp.mean(xn * xn, axis=-1, keepdims=True) + NORM_EPS) * gf_ref[...]
    y_ref[...] = xn


def _out(out_a, out_b, sa, sb, x, gate, w_up_a, w_up_b, w_out, g_final, b, nb, tt, final_norm):
    n_tok, d = x.shape
    nj = n_tok // (b * tt)
    m = nb * tt
    tok = lambda i, j: (i * nj + j, 0)
    const2 = lambda i, j: (0, 0)
    wspec = lambda w: pl.BlockSpec(w.shape, const2, pipeline_mode=pl.Buffered(1))
    in_specs = [pl.BlockSpec((m, out_a.shape[-1]), tok),
                pl.BlockSpec((m, out_b.shape[-1]), tok),
                pl.BlockSpec((m, d), tok),
                pl.BlockSpec((m, d), tok),
                pl.BlockSpec((m, d), tok),
                pl.BlockSpec((nb, 1, d), lambda i, j: (i, 0, 0)),
                wspec(w_up_a), wspec(w_up_b), wspec(w_out),
                pl.BlockSpec((1, d), const2)]
    return pl.pallas_call(
        functools.partial(_out_kernel, nb=nb, final_norm=final_norm),
        grid=(b // nb, nj),
        in_specs=in_specs,
        out_specs=pl.BlockSpec((m, d), tok),
        out_shape=jax.ShapeDtypeStruct((n_tok, d), F32),
        compiler_params=pltpu.CompilerParams(vmem_limit_bytes=VMEM_LIMIT),
        name="merge_out",
    )(out_a, out_b, sa, sb, x, gate, w_up_a, w_up_b, w_out, g_final.reshape(1, d))


def _token_tiling(b, t, target):
    if t >= target:
        return 1, math.gcd(t, target)
    return math.gcd(b, target // t), t


def _layer(x, mod, states, lp, g_final, final_norm):
    b, t, d = x.shape
    shift0, s0, conv0, c0, n0, m0 = states
    n_hb = c0.shape[1]
    cw = lp["conv_w"].shape[0]
    row = lambda a: a.reshape(1, -1)

    ada_shift, ada_scale, ada_gate = (mod[:, None, i * d:(i + 1) * d] for i in range(3))
    nb, tt = _token_tiling(b, t, PROJ_ROWS)
    L = math.gcd(t, ROWS)
    act_dtype = BF16 if L % 16 == 0 else F32
    conv0p = jnp.pad(conv0, ((0, 0), (SUBLANES - (cw - 1), 0), (0, 0)))
    (kq, rq, bk, kkk, bk2, kk2, v_a, bv, sz, p_last, shift_t,
     q, k, v_b, gates, gob, sa, sb, conv_tail) = _inproj(
        x, ada_scale, ada_shift, lp["g_norm"], shift0[:, None, :], conv0p, lp, nb, tt, L, act_dtype)

    n_seq, t_step = _token_tiling(b, t, RWKV_STEP_ROWS)
    out_a, s_t = _rwkv((kq, rq, bk, kkk, bk2, kk2), v_a, bv, sz, p_last, s0,
                       row(lp["ln_w"]), row(lp["ln_b"]), n_seq, t_step, L)

    n_seq_b, L_b = _token_tiling(b, t, MLSTM_ROWS)
    n0p = jnp.broadcast_to(n0[:, :, None, :], (b, n_hb, SUBLANES, HEAD_DIM))
    m0p = jnp.pad(m0, ((0, 0), (0, LANES - n_hb)))[:, None, :]
    gate_b = jnp.zeros((1, LANES), F32).at[0, 0:n_hb].set(lp["b_i"]).at[0, n_hb:2 * n_hb].set(lp["b_f"])
    out_b, c_t, n_t, m_t = _mlstm(q, k, v_b, gates, gob, c0, n0p, m0p, gate_b, row(lp["gn_w"]), n_seq_b, L_b)

    y = _out(out_a, out_b, sa, sb, x.reshape(b * t, d), ada_gate, lp["w_up_a"], lp["w_up_b"], lp["w_out"],
             g_final, b, nb, tt, final_norm).reshape(b, t, d)
    new_states = (shift_t[:, 0], s_t, conv_tail[:, SUBLANES - (cw - 1):], c_t, n_t[:, :, 0, :], m_t[:, 0, :n_hb])
    return y, new_states


def _trunk(x, mods, states, layers, g_final):
    depth = len(layers)
    new = [[] for _ in states]
    for l in range(depth):
        st = tuple(s[l] for s in states)
        x, st_new = _layer(x, mods[l], st, layers[l], g_final, final_norm=(l == depth - 1))
        for lst, s in zip(new, st_new):
            lst.append(s.astype(x.dtype))
    return x, tuple(jnp.stack(lst) for lst in new)


def kernel(x_prompt, x_sample, c_prompt, c_sample, state_rwkv_shift, state_rwkv_S, state_mlstm_conv, state_mlstm_C, state_mlstm_n, state_mlstm_m, g_norm, w_ada, b_ada, w_in, mu_shift, w_decay2, w0, w_iclr2, a0, k_k, k_a, r_k, ln_w, ln_b, conv_w, conv_b, b_i, b_f, gn_w, w_up_a, w_up_b, w_out, g_final):
    depth = g_norm.shape[0]
    bp, bs = x_prompt.shape[0], x_sample.shape[0]
    d = x_prompt.shape[-1]
    da, db = w_up_a.shape[1], w_up_b.shape[1]
    n_ha, n_hb = da // HEAD_DIM, db // HEAD_DIM
    sw = mu_shift.shape[-1]
    cc = conv_w.shape[-1]
    cw = conv_w.shape[1]
    dt = x_prompt.dtype

    sizes = (sw, da, cc, db, db, n_hb, n_hb, db, d, d)
    offs = [0]
    for s in sizes:
        offs.append(offs[-1] + s)
    seg = lambda w, i: w[:, offs[i]:offs[i + 1]]

    c_all = jnp.concatenate([c_prompt, c_sample], axis=0)
    n_c = c_all.shape[0]
    c_pad = jnp.pad(c_all, ((0, (-n_c) % SUBLANES), (0, 0)))

    layers, mods_p, mods_s = [], [], []
    for l in range(depth):
        w = w_in[l]
        wg = jnp.concatenate([seg(w, 5), seg(w, 6)], axis=1)
        wg = jnp.pad(wg, ((0, 0), (0, LANES - wg.shape[1])))
        w_segs = tuple(x.astype(BF16) for x in
                       (seg(w, 0), seg(w, 1), seg(w, 2), seg(w, 3), seg(w, 4), seg(w, 7), seg(w, 8), seg(w, 9), wg))
        layers.append(dict(
            g_norm=g_norm[l], w_segs=w_segs, mu_shift=mu_shift[l], w_decay2=w_decay2[l].astype(BF16),
            w0=w0[l], w_iclr2=w_iclr2[l].astype(BF16), a0=a0[l], k_k=k_k[l], k_a=k_a[l], r_k=r_k[l],
            ln_w=ln_w[l], ln_b=ln_b[l], conv_w=conv_w[l], conv_b=conv_b[l], b_i=b_i[l], b_f=b_f[l],
            gn_w=gn_w[l], w_up_a=w_up_a[l].astype(BF16), w_up_b=w_up_b[l].astype(BF16),
            w_out=w_out[l].astype(BF16)))
        mod = _ada(c_pad, w_ada[l], b_ada[l])
        mods_p.append(mod[:bp])
        mods_s.append(mod[bp:bp + bs])

    prompt_states = (
        jnp.zeros((depth, bp, sw), dt),
        jnp.zeros((depth, bp, n_ha, HEAD_DIM, HEAD_DIM), dt),
        jnp.zeros((depth, bp, cw - 1, cc), dt),
        jnp.zeros((depth, bp, n_hb, HEAD_DIM, HEAD_DIM), dt),
        jnp.zeros((depth, bp, n_hb, HEAD_DIM), dt),
        jnp.zeros((depth, bp, n_hb), dt),
    )
    sample_states = (state_rwkv_shift, state_rwkv_S, state_mlstm_conv,
                     state_mlstm_C, state_mlstm_n, state_mlstm_m)
    y_p, st_p = _trunk(x_prompt, mods_p, prompt_states, layers, g_final)
    y_s, st_s = _trunk(x_sample, mods_s, sample_states, layers, g_final)
    return (y_p, y_s) + st_p + st_s
```

```python
import functools
import math

import jax
import jax.numpy as jnp
from jax import lax
from jax.experimental import pallas as pl
from jax.experimental.pallas import tpu as pltpu

F32 = jnp.float32
BF16 = jnp.bfloat16

HEAD_DIM = 64
NORM_EPS = 1e-6
RWKV_GN_EPS = 64e-5
MLSTM_GN_EPS = 1e-6
ROWS = 64
MIX_ROWS = 128
PROJ_ROWS = 256
SUBLANES = 8
LANES = 128
TRI_BASE = 16
VMEM_LIMIT = 60 * 1024 * 1024

NT_DIMS = (((1,), (1,)), ((), ()))
TN_DIMS = (((0,), (0,)), ((), ()))


def _mm(a, b):
    return jnp.dot(a.astype(BF16), b.astype(BF16), preferred_element_type=F32)


def _mm_nt(a, b):
    return lax.dot_general(a.astype(BF16), b.astype(BF16), NT_DIMS, preferred_element_type=F32)


def _mm_tn(a, b):
    return lax.dot_general(a.astype(BF16), b.astype(BF16), TN_DIMS, preferred_element_type=F32)


def _sigmoid(x):
    return 1.0 / (1.0 + jnp.exp(-x))


def _silu(x):
    return x * _sigmoid(x)


def _softplus(x):
    return jnp.maximum(x, 0.0) + jnp.log(1.0 + jnp.exp(-jnp.abs(x)))


def _cumsum_time(x, tri_bf16):
    hi = x.astype(BF16)
    r1 = x - hi.astype(F32)
    mid = r1.astype(BF16)
    lo = (r1 - mid.astype(F32)).astype(BF16)
    dot = functools.partial(jnp.dot, preferred_element_type=F32)
    return dot(tri_bf16, hi) + dot(tri_bf16, mid) + dot(tri_bf16, lo)


def _shr(x, n):
    return lax.shift_right_logical(x, jnp.full(x.shape, n, x.dtype))


def _rows_of_seq(x, n_seq, L):
    if n_seq == 1:
        return x[0]
    return jnp.broadcast_to(x, (n_seq, L, x.shape[-1])).reshape(n_seq * L, x.shape[-1])


def _last_rows(x, n_seq):
    m, n = x.shape
    L = m // n_seq
    return x.reshape(n_seq, L, n)[:, L - 1:L, :]


def _head_sums(x):
    m, n = x.shape
    parts = [jnp.broadcast_to(jnp.sum(x[:, o:o + HEAD_DIM], axis=-1, keepdims=True), (m, HEAD_DIM))
             for o in range(0, n, HEAD_DIM)]
    return jnp.concatenate(parts, axis=-1)


def _head_sums_mxu(x, ones_ref):
    hi = x.astype(BF16)
    lo = (x - hi.astype(F32)).astype(BF16)
    s = jnp.dot(jnp.concatenate([hi, lo], axis=0), ones_ref[...], preferred_element_type=F32)
    return s[:x.shape[0]] + s[x.shape[0]:]


def _cummax_time(x, L, pos):
    y = x
    sh = 1
    while sh < L:
        y = jnp.where(pos >= sh, jnp.maximum(y, pltpu.roll(y, sh, axis=0)), y)
        sh *= 2
    return y


def _tri_inverse_all(mats, row, col, L):
    bs = min(TRI_BASE, L)
    sh = int(math.log2(bs))
    same = _shr(row, sh) == _shr(col, sh)
    eye = (row == col).astype(F32)
    ns = [jnp.where(same, -a, 0.0) for a in mats]
    ts = [eye + n for n in ns]
    ps = ns
    k = 1
    while 2 * k < bs:
        ps = [_mm(p, p) for p in ps]
        yield
        ts = [t + _mm(t, p) for t, p in zip(ts, ps)]
        yield
        k *= 2
    size = bs
    while size < L:
        sh = int(math.log2(size))
        lower_left = (_shr(row, sh + 1) == _shr(col, sh + 1)) & (_shr(row, sh) != _shr(col, sh))
        offs = [jnp.where(lower_left, a, 0.0) for a in mats]
        tmp = [_mm(t, o) for t, o in zip(ts, offs)]
        yield
        ts = [t - _mm(x, t) for t, x in zip(ts, tmp)]
        yield
        size *= 2
    return ts


def _interleave(plan):
    live = [[g, n, s, 0] for g, n, s in plan]
    while live:
        item = min(live, key=lambda it: it[2] + (1.0 - it[2]) * it[3] / it[1])
        try:
            next(item[0])
            item[3] += 1
        except StopIteration:
            live.remove(item)


def _ada_kernel(c_ref, w_ref, b_ref, o_ref):
    o_ref[...] = _mm(_silu(c_ref[...]), w_ref[...]) + b_ref[...]


def _ada(c, w_ada, b_ada):
    n, d = c.shape
    n3 = w_ada.shape[1]
    tn = d
    return pl.pallas_call(
        _ada_kernel,
        grid=(n3 // tn,),
        in_specs=[pl.BlockSpec((n, d), lambda j: (0, 0)),
                  pl.BlockSpec((d, tn), lambda j: (0, j)),
                  pl.BlockSpec((1, tn), lambda j: (0, j))],
        out_specs=pl.BlockSpec((n, tn), lambda j: (0, j)),
        out_shape=jax.ShapeDtypeStruct((n, n3), F32),
        name="ada",
    )(c, w_ada, b_ada.reshape(1, n3))


def _inproj_kernel(x_ref, sc_ref, sh_ref, g_ref, shift0_ref, conv0_ref,
                   w_shift, w_za, w_qk, w_vb, w_ob, w_zb, w_gla, w_glb, w_gt,
                   mu_ref, w0_ref, a0_ref, kk_ref, ka_ref, rk_ref, wd2_ref, wi2_ref, cw_ref, cb_ref,
                   kq_o, rq_o, bk_o, kkk_o, bk2_o, kk2_o, va_o, bv_o, sz_o, plast_o, shst_o,
                   q_o, k_o, vb_o, gt_o, gob_o, sa_o, sb_o, cst_o,
                   prev_scr, xbuf, *, chunk, lora_w, conv_w):
    j = pl.program_id(1)
    nb, tt, d = x_ref.shape
    M = nb * tt
    L = chunk
    n_chunks = M // L
    da = va_o.shape[-1]
    db = vb_o.shape[-1]
    cc = xbuf.shape[-1]
    pad = SUBLANES

    @pl.when(j == 0)
    def _():
        prev_scr[...] = shift0_ref[...]
        xbuf[:, 0:pad, :] = conv0_ref[...]

    x = x_ref[...]
    y = x * lax.rsqrt(jnp.mean(x * x, axis=-1, keepdims=True) + NORM_EPS) * g_ref[...]
    h = (y * (1.0 + sc_ref[...]) + sh_ref[...]).reshape(M, d).astype(BF16)
    proj = lambda w_ref: jnp.dot(h, w_ref[...], preferred_element_type=F32)

    p = proj(w_shift)
    row_w = lax.broadcasted_iota(jnp.int32, p.shape, 0)
    first = (row_w & (tt - 1)) == 0
    prev = jnp.where(first, _rows_of_seq(prev_scr[...], nb, tt), pltpu.roll(p, 1, axis=0))
    last = _last_rows(p, nb)
    prev_scr[...] = last
    shst_o[...] = last
    ps = p + mu_ref[...] * (prev - p)
    r = ps[:, 0:da]
    k = ps[:, da:2 * da]
    v = ps[:, 2 * da:3 * da]
    wl = ps[:, 3 * da:3 * da + lora_w]
    al = ps[:, 3 * da + lora_w:]

    w = w0_ref[...] + _mm(jnp.tanh(wl), wd2_ref[...])
    lw = -jnp.exp(-_softplus(-w) - 0.5)
    a = _sigmoid(a0_ref[...] + _mm(al, wi2_ref[...]))
    kkr = k * kk_ref[...]
    kk = kkr / jnp.maximum(jnp.sqrt(_head_sums(kkr * kkr)), 1e-12)
    k2 = k * (1.0 + (a - 1.0) * ka_ref[...])
    b = kk * a
    va_o[...] = v
    bv_o[...] = _head_sums(r * k2 * rk_ref[...]) * v
    sz_o[...] = _silu(proj(w_za))

    lg = int(math.log2(L))
    row_m = lax.broadcasted_iota(jnp.int32, (M, M), 0)
    col_m = lax.broadcasted_iota(jnp.int32, (M, M), 1)
    tri = ((row_m >= col_m) & (_shr(row_m, lg) == _shr(col_m, lg))).astype(BF16)
    c = _cumsum_time(lw, tri)
    c_last = _last_rows(c, n_chunks)
    plast_o[...] = jnp.exp(c_last)
    e_nc = jnp.exp(-c)
    e_cl = jnp.exp(_rows_of_seq(c_last, n_chunks, L) - c)
    kq_o[...] = (kk * jnp.exp(c - lw)).astype(kq_o.dtype)
    rq_o[...] = (r * jnp.exp(c)).astype(rq_o.dtype)
    bk_o[...] = (b * e_nc).astype(bk_o.dtype)
    kkk_o[...] = (k2 * e_nc).astype(kkk_o.dtype)
    bk2_o[...] = (b * e_cl).astype(bk2_o.dtype)
    kk2_o[...] = (k2 * e_cl).astype(kk2_o.dtype)

    xbuf[:, pad:pad + tt, :] = proj(w_qk).reshape(nb, tt, cc)
    conv = cb_ref[...]
    for tap in range(conv_w):
        conv = conv + xbuf[:, pl.ds(pad - (conv_w - 1) + tap, tt), :].reshape(M, cc) * cw_ref[tap:tap + 1, :]
    tail = xbuf[:, tt:tt + pad, :]
    xbuf[:, 0:pad, :] = tail
    cst_o[...] = tail
    qk = _silu(conv)
    q_o[...] = qk[:, 0:db].astype(q_o.dtype)
    k_o[...] = (qk[:, db:] * (1.0 / math.sqrt(HEAD_DIM))).astype(k_o.dtype)
    vb_o[...] = proj(w_vb)
    gt_o[...] = proj(w_gt)
    gob_o[...] = _sigmoid(proj(w_ob)) * _silu(proj(w_zb))

    sa_o[...] = _sigmoid(proj(w_gla)).astype(sa_o.dtype)
    sb_o[...] = _sigmoid(proj(w_glb)).astype(sb_o.dtype)


def _inproj(x, scale, shift, g_norm, shift0, conv0p, lp, nb, tt, chunk, act_dtype):
    b, t, d = x.shape
    n_tok = b * t
    m = nb * tt
    nj = t // tt
    sw = shift0.shape[-1]
    cc = conv0p.shape[-1]
    da, db = lp["w_up_a"].shape[0], lp["w_up_b"].shape[0]
    cw = lp["conv_w"].shape[0]
    row = lambda a: a.reshape(1, -1)
    rows = [row(lp[n]) for n in ("mu_shift", "w0", "a0", "k_k", "k_a", "r_k")]
    smalls = rows + [lp["w_decay2"], lp["w_iclr2"], lp["conv_w"], row(lp["conv_b"])]
    const = lambda i, j: (0, 0)
    tok = lambda i, j: (i * nj + j, 0)
    per_b = lambda i, j: (i, 0, 0)
    in_specs = [pl.BlockSpec((nb, tt, d), lambda i, j: (i, j, 0)),
                pl.BlockSpec((nb, 1, d), per_b),
                pl.BlockSpec((nb, 1, d), per_b),
                pl.BlockSpec((1, d), const),
                pl.BlockSpec((nb, 1, sw), per_b),
                pl.BlockSpec((nb, SUBLANES, cc), per_b)]
    in_specs += [pl.BlockSpec(w.shape, const, pipeline_mode=pl.Buffered(1)) for w in lp["w_segs"]]
    in_specs += [pl.BlockSpec(a.shape, const) for a in smalls]
    tok_out = lambda n, dt: (pl.BlockSpec((m, n), tok), jax.ShapeDtypeStruct((n_tok, n), dt))
    outs = [tok_out(da, act_dtype)] * 6 + [tok_out(da, F32)] * 3
    outs += [(pl.BlockSpec((m // chunk, 1, da), lambda i, j: (i * nj + j, 0, 0)),
              jax.ShapeDtypeStruct((n_tok // chunk, 1, da), F32)),
             (pl.BlockSpec((nb, 1, sw), per_b), jax.ShapeDtypeStruct((b, 1, sw), F32))]
    outs += [tok_out(db, act_dtype)] * 2 + [tok_out(db, F32), tok_out(LANES, F32), tok_out(db, F32)]
    outs += [tok_out(d, BF16)] * 2
    outs += [(pl.BlockSpec((nb, SUBLANES, cc), per_b), jax.ShapeDtypeStruct((b, SUBLANES, cc), F32))]
    return pl.pallas_call(
        functools.partial(_inproj_kernel, chunk=chunk, lora_w=lp["w_decay2"].shape[0], conv_w=cw),
        grid=(b // nb, nj),
        in_specs=in_specs,
        out_specs=[o[0] for o in outs], out_shape=[o[1] for o in outs],
        scratch_shapes=[pltpu.VMEM((nb, 1, sw), F32),
                        pltpu.VMEM((nb, tt + SUBLANES, cc), F32)],
        compiler_params=pltpu.CompilerParams(vmem_limit_bytes=VMEM_LIMIT),
        name="inproj",
    )(x, scale, shift, g_norm.reshape(1, d), shift0, conv0p, *lp["w_segs"], *smalls)


def _rwkv_body(kq_ref, rq_ref, bk_ref, kkk_ref, bk2_ref, kk2_ref, v_ref, bv_ref, sz_ref, plast_ref,
               s0_ref, lnw_ref, lnb_ref, ones_ref, oa_ref, st_ref, s_scr, *, n_seq, chunk, n_heads):
    j = pl.program_id(1)
    M = kq_ref.shape[0]
    L = chunk
    t_step = M // n_seq
    RU = min(M, ROWS)
    n_units = M // RU
    heads = range(n_heads)
    units = range(n_units)
    pairs = [(u, h) for u in units for h in heads]

    @pl.when(j == 0)
    def _():
        s_scr[...] = s0_ref[...]

    lg = int(math.log2(L))
    row = lax.broadcasted_iota(jnp.int32, (RU, RU), 0)
    col = lax.broadcasted_iota(jnp.int32, (RU, RU), 1)
    row2 = lax.broadcasted_iota(jnp.int32, (RU, 2 * RU), 0)
    col2 = lax.broadcasted_iota(jnp.int32, (RU, 2 * RU), 1) & (RU - 1)
    same2 = _shr(row2, lg) == _shr(col2, lg)
    incl2 = (row2 >= col2) & same2
    strict2 = (row2 > col2) & same2

    p_last = plast_ref[...]
    sls = [slice(h * HEAD_DIM, (h + 1) * HEAD_DIM) for h in heads]
    rus = [slice(u * RU, (u + 1) * RU) for u in units]
    blk = lambda ref, q: ref[rus[q[0]], sls[q[1]]]
    kq = {q: blk(kq_ref, q) for q in pairs}
    rq = {q: blk(rq_ref, q) for q in pairs}
    bk2 = {q: blk(bk2_ref, q) for q in pairs}
    kk2 = {q: blk(kk2_ref, q) for q in pairs}
    vs = {q: blk(v_ref, q) for q in pairs}
    qr = {q: jnp.concatenate([kq[q], rq[q]], axis=0).astype(BF16) for q in pairs}
    bkk = {q: jnp.concatenate([blk(bk_ref, q), blk(kkk_ref, q)], axis=0).astype(BF16) for q in pairs}
    yield

    gs = {q: _mm_nt(qr[q], bkk[q]) for q in pairs}
    yield
    ga = {q: jnp.where(strict2, gs[q][:RU, :], 0.0) for q in pairs}
    gm = {q: jnp.where(incl2, gs[q][RU:, :], 0.0) for q in pairs}
    yield
    t_inv = dict(zip(pairs, (yield from _tri_inverse_all([ga[q][:, :RU] for q in pairs], row, col, L))))
    zeros = jnp.zeros((RU, HEAD_DIM), F32)
    akv = {q: _mm(ga[q], jnp.concatenate([zeros, vs[q]], axis=0)) for q in pairs}
    yield

    state = {}
    ys = {}
    for u in units:
        if L == RU:
            b = (u * RU) // t_step
            s0 = [state[b, h] if (b, h) in state else s_scr[b, h] for h in heads]
            ws = [_mm_nt(qr[u, h], s0[h]) for h in heads]
            w1 = [x[:RU] for x in ws]
            wr = [x[RU:] for x in ws]
        else:
            n_in = RU // L
            s0s = [[s_scr[u * n_in + i, h] for h in heads] for i in range(n_in)]
            w1, wr = [], []
            for h in heads:
                parts = [_mm_nt(jnp.concatenate([kq[u, h][i * L:(i + 1) * L], rq[u, h][i * L:(i + 1) * L]], axis=0),
                                s0s[i][h]) for i in range(n_in)]
                w1.append(jnp.concatenate([x[:L] for x in parts], axis=0))
                wr.append(jnp.concatenate([x[L:] for x in parts], axis=0))
        yield
        us = [-_mm(t_inv[u, h], w1[h] + akv[u, h]) for h in heads]
        yield
        uv = [jnp.concatenate([us[h], vs[u, h]], axis=0) for h in heads]
        for h in heads:
            ys[u, h] = wr[h] + _mm(gm[u, h], uv[h])
        yield
        if L == RU:
            for h in heads:
                bkk2 = jnp.concatenate([bk2[u, h], kk2[u, h]], axis=0)
                state[b, h] = s0[h] * p_last[u][:, sls[h]] + _mm_tn(uv[h], bkk2)
        else:
            for h in heads:
                for i in range(n_in):
                    rows = slice(i * L, (i + 1) * L)
                    uv_i = jnp.concatenate([us[h][rows], vs[u, h][rows]], axis=0)
                    bkk2_i = jnp.concatenate([bk2[u, h][rows], kk2[u, h][rows]], axis=0)
                    state[u * n_in + i, h] = (s0s[i][h] * p_last[u * n_in + i][:, sls[h]]
                                              + _mm_tn(uv_i, bkk2_i))
        yield
    for (b, h), s_new in state.items():
        s_scr[b, h] = s_new

    rows_out = [jnp.concatenate([ys[u, h] for h in heads], axis=-1) for u in units]
    y = rows_out[0] if n_units == 1 else jnp.concatenate(rows_out, axis=0)
    yc = y - _head_sums_mxu(y, ones_ref) * (1.0 / HEAD_DIM)
    yield
    var = _head_sums_mxu(yc * yc, ones_ref) * (1.0 / HEAD_DIM)
    yn = yc * lax.rsqrt(var + RWKV_GN_EPS)
    oa_ref[...] = ((yn * lnw_ref[...] + lnb_ref[...] + bv_ref[...]) * sz_ref[...]).astype(oa_ref.dtype)

    @pl.when(j == pl.num_programs(1) - 1)
    def _():
        st_ref[...] = s_scr[...]


def _mlstm_body(q_ref, k_ref, v_ref, g_ref, gob_ref, c0_ref, n0_ref, m0_ref, gb_ref, gnw_ref, ones_ref,
                ob_ref, ct_ref, nt_ref, mt_ref, cn_scr, m_scr, *, n_seq, n_heads):
    j = pl.program_id(1)
    M = q_ref.shape[0]
    L = M // n_seq
    heads = range(n_heads)
    seqs = range(n_seq)

    @pl.when(j == 0)
    def _():
        cn_scr[:, :, 0:HEAD_DIM, :] = c0_ref[...]
        cn_scr[:, :, HEAD_DIM:, :] = n0_ref[...]
        m_scr[...] = m0_ref[...]

    row = lax.broadcasted_iota(jnp.int32, (M, M), 0)
    col = lax.broadcasted_iota(jnp.int32, (M, M), 1)
    same_seq = _shr(row, int(math.log2(L))) == _shr(col, int(math.log2(L)))
    incl = (row >= col) & same_seq

    g = g_ref[...] + gb_ref[...]
    bcum = pltpu.roll(_cumsum_time(-_softplus(-g), incl.astype(BF16)), LANES - n_heads, axis=1)
    m_prev = m_scr[...]
    m_prev_rows = _rows_of_seq(m_prev, n_seq, L)
    x_all = g - bcum
    pos = lax.broadcasted_iota(jnp.int32, (M, LANES), 0) & (L - 1)
    m_all = bcum + jnp.maximum(_cummax_time(x_all, L, pos), m_prev_rows)
    bm_all = bcum - m_all
    w_in_all = jnp.exp(bcum + m_prev_rows - m_all)
    e_negm_all = jnp.exp(-m_all)
    m_new = _last_rows(m_all, n_seq)
    b_last = _last_rows(bcum, n_seq)
    ws_all = jnp.exp(_rows_of_seq(b_last - m_new, n_seq, L) + x_all)
    dec_all = jnp.exp(b_last + m_prev - m_new)
    m_scr[...] = m_new
    x_t = jnp.transpose(x_all)
    yield

    ones = jnp.ones((M, SUBLANES), F32)
    sls = [slice(h * HEAD_DIM, (h + 1) * HEAD_DIM) for h in heads]
    qs = [q_ref[:, sl] for sl in sls]
    ks = [k_ref[:, sl] for sl in sls]
    v1 = [jnp.concatenate([v_ref[:, sl], ones], axis=-1) for sl in sls]
    cn = [[cn_scr[b, h] for h in heads] for b in seqs]

    qk_t = [_mm_nt(qs[h], ks[h]) for h in heads]
    if n_seq == 1:
        qc = [_mm_nt(qs[h], cn[0][h]) for h in heads]
    else:
        qc = [jnp.concatenate([_mm_nt(qs[h][b * L:(b + 1) * L], cn[b][h]) for b in seqs], axis=0) for h in heads]
    yield
    w_ts = [jnp.where(incl, jnp.exp(bm_all[:, h:h + 1] + x_t[h:h + 1, :]), 0.0) for h in heads]
    yield
    s = [qk_t[h] * w_ts[h] for h in heads]
    numden = [_mm(s[h], v1[h]) + w_in_all[:, h:h + 1] * qc[h] for h in heads]
    yield
    den = [jnp.maximum(jnp.abs(numden[h][:, HEAD_DIM:HEAD_DIM + 1]), e_negm_all[:, h:h + 1]) for h in heads]
    hh = [numden[h][:, 0:HEAD_DIM] / den[h] for h in heads]
    hh_all = jnp.concatenate(hh, axis=-1)
    ssq = _head_sums_mxu(hh_all * hh_all, ones_ref) * (1.0 / HEAD_DIM)

    yield
    v1w = [v1[h] * ws_all[:, h:h + 1] for h in heads]
    for h in heads:
        for b in seqs:
            rows = slice(b * L, (b + 1) * L)
            cn_scr[b, h] = dec_all[b][:, h:h + 1] * cn[b][h] + _mm_tn(v1w[h][rows], ks[h][rows])
    yield

    hb = hh_all * lax.rsqrt(ssq + MLSTM_GN_EPS)
    ob_ref[...] = (hb * gnw_ref[...] * gob_ref[...]).astype(ob_ref.dtype)

    @pl.when(j == pl.num_programs(1) - 1)
    def _():
        ct_ref[...] = cn_scr[:, :, 0:HEAD_DIM, :]
        nt_ref[...] = cn_scr[:, :, HEAD_DIM:, :]
        mt_ref[...] = m_scr[...]


N_RWKV_IN, N_RWKV_OUT, N_MLSTM_IN, N_MLSTM_OUT = 14, 2, 11, 4
RWKV_STAGES, MLSTM_STAGES = 24, 8
MLSTM_START = 0.5


def _mixers_kernel(*refs, n_seq, chunk, n_heads_a, n_heads_b):
    it = iter(refs)
    take = lambda n: [next(it) for _ in range(n)]
    r_in, m_in = take(N_RWKV_IN), take(N_MLSTM_IN)
    r_out, m_out = take(N_RWKV_OUT), take(N_MLSTM_OUT)
    s_scr, cn_scr, m_scr = take(3)
    _interleave([
        (_rwkv_body(*r_in, *r_out, s_scr, n_seq=n_seq, chunk=chunk, n_heads=n_heads_a), RWKV_STAGES, 0.0),
        (_mlstm_body(*m_in, *m_out, cn_scr, m_scr, n_seq=n_seq, n_heads=n_heads_b), MLSTM_STAGES, MLSTM_START)])


def _mixers(acts, v_a, bv, sz, p_last, s0, ln_w, ln_b, q, k, v_b, gates, gob, c0, n0, m0, gate_b, gn_w,
            n_seq, t_step, chunk):
    b = s0.shape[0]
    n_ha, n_hb = s0.shape[1], c0.shape[1]
    n_tok, da = v_a.shape
    db = v_b.shape[1]
    m = n_seq * t_step
    nj = n_tok // (b * t_step)
    tok = lambda i, j: (i * nj + j, 0)
    per_b3 = lambda i, j: (i, 0, 0)
    per_b4 = lambda i, j: (i, 0, 0, 0)
    const2 = lambda i, j: (0, 0)
    s_spec = pl.BlockSpec((n_seq, n_ha, HEAD_DIM, HEAD_DIM), per_b4)
    c_spec = pl.BlockSpec((n_seq, n_hb, HEAD_DIM, HEAD_DIM), per_b4)
    n_spec = pl.BlockSpec((n_seq, n_hb, SUBLANES, HEAD_DIM), per_b4)
    m_spec = pl.BlockSpec((n_seq, 1, LANES), per_b3)
    row_a, row_b = pl.BlockSpec((m, da), tok), pl.BlockSpec((m, db), tok)
    in_specs = [row_a] * 9
    ones_a = jnp.kron(jnp.eye(n_ha, dtype=F32), jnp.ones((HEAD_DIM, HEAD_DIM), F32)).astype(BF16)
    ones_b = jnp.kron(jnp.eye(n_hb, dtype=F32), jnp.ones((HEAD_DIM, HEAD_DIM), F32)).astype(BF16)
    in_specs += [pl.BlockSpec((m // chunk, 1, da), lambda i, j: (i * nj + j, 0, 0)), s_spec,
                 pl.BlockSpec(ln_w.shape, const2), pl.BlockSpec(ln_b.shape, const2),
                 pl.BlockSpec(ones_a.shape, const2)]
    in_specs += [row_b, row_b, row_b, pl.BlockSpec((m, LANES), tok), row_b, c_spec, n_spec, m_spec,
                 pl.BlockSpec(gate_b.shape, const2), pl.BlockSpec(gn_w.shape, const2),
                 pl.BlockSpec(ones_b.shape, const2)]
    return pl.pallas_call(
        functools.partial(_mixers_kernel, n_seq=n_seq, chunk=chunk, n_heads_a=n_ha, n_heads_b=n_hb),
        grid=(b // n_seq, nj),
        in_specs=in_specs,
        out_specs=[row_a, s_spec, row_b, c_spec, n_spec, m_spec],
        out_shape=[jax.ShapeDtypeStruct((n_tok, da), BF16), jax.ShapeDtypeStruct(s0.shape, F32),
                   jax.ShapeDtypeStruct((n_tok, db), BF16), jax.ShapeDtypeStruct(c0.shape, F32),
                   jax.ShapeDtypeStruct(n0.shape, F32), jax.ShapeDtypeStruct(m0.shape, F32)],
        scratch_shapes=[pltpu.VMEM((n_seq, n_ha, HEAD_DIM, HEAD_DIM), F32),
                        pltpu.VMEM((n_seq, n_hb, HEAD_DIM + SUBLANES, HEAD_DIM), F32),
                        pltpu.VMEM((n_seq, 1, LANES), F32)],
        compiler_params=pltpu.CompilerParams(vmem_limit_bytes=VMEM_LIMIT),
        name="mixers",
    )(*acts, v_a, bv, sz, p_last, s0, ln_w, ln_b, ones_a, q, k, v_b, gates, gob, c0, n0, m0, gate_b, gn_w, ones_b)


def _out_kernel(oa_ref, ob_ref, sa_ref, sb_ref, x_ref, gate_ref, wua_ref, wub_ref, wo_ref, gf_ref,
                y_ref, *, nb, final_norm):
    m = x_ref.shape[0]
    ua = jnp.dot(oa_ref[...], wua_ref[...], preferred_element_type=F32)
    ub = jnp.dot(ob_ref[...], wub_ref[...], preferred_element_type=F32)
    merged = sa_ref[...] * ua + sb_ref[...] * ub
    mo = jnp.dot(merged.astype(BF16), wo_ref[...], preferred_element_type=F32)
    xn = x_ref[...] + _rows_of_seq(gate_ref[...], nb, m // nb) * mo
    if final_norm:
        xn = xn * lax.rsqrt(jnp.mean(xn * xn, axis=-1, keepdims=True) + NORM_EPS) * gf_ref[...]
    y_ref[...] = xn


def _out(out_a, out_b, sa, sb, x, gate, w_up_a, w_up_b, w_out, g_final, b, nb, tt, final_norm):
    n_tok, d = x.shape
    nj = n_tok // (b * tt)
    m = nb * tt
    tok = lambda i, j: (i * nj + j, 0)
    const2 = lambda i, j: (0, 0)
    wspec = lambda w: pl.BlockSpec(w.shape, const2, pipeline_mode=pl.Buffered(1))
    in_specs = [pl.BlockSpec((m, out_a.shape[-1]), tok),
                pl.BlockSpec((m, out_b.shape[-1]), tok),
                pl.BlockSpec((m, d), tok),
                pl.BlockSpec((m, d), tok),
                pl.BlockSpec((m, d), tok),
                pl.BlockSpec((nb, 1, d), lambda i, j: (i, 0, 0)),
                wspec(w_up_a), wspec(w_up_b), wspec(w_out),
                pl.BlockSpec((1, d), const2)]
    return pl.pallas_call(
        functools.partial(_out_kernel, nb=nb, final_norm=final_norm),
        grid=(b // nb, nj),
        in_specs=in_specs,
        out_specs=pl.BlockSpec((m, d), tok),
        out_shape=jax.ShapeDtypeStruct((n_tok, d), F32),
        compiler_params=pltpu.CompilerParams(vmem_limit_bytes=VMEM_LIMIT),
        name="merge_out",
    )(out_a, out_b, sa, sb, x, gate, w_up_a, w_up_b, w_out, g_final.reshape(1, d))


def _token_tiling(b, t, target):
    if t >= target:
        return 1, math.gcd(t, target)
    return math.gcd(b, target // t), t


def _layer(x, mod, states, lp, g_final, final_norm):
    b, t, d = x.shape
    shift0, s0, conv0, c0, n0, m0 = states
    n_hb = c0.shape[1]
    cw = lp["conv_w"].shape[0]
    row = lambda a: a.reshape(1, -1)

    ada_shift, ada_scale, ada_gate = (mod[:, None, i * d:(i + 1) * d] for i in range(3))
    nb, tt = _token_tiling(b, t, PROJ_ROWS)
    L = math.gcd(t, ROWS)
    act_dtype = BF16 if L % 16 == 0 else F32
    conv0p = jnp.pad(conv0, ((0, 0), (SUBLANES - (cw - 1), 0), (0, 0)))
    (kq, rq, bk, kkk, bk2, kk2, v_a, bv, sz, p_last, shift_t,
     q, k, v_b, gates, gob, sa, sb, conv_tail) = _inproj(
        x, ada_scale, ada_shift, lp["g_norm"], shift0[:, None, :], conv0p, lp, nb, tt, L, act_dtype)

    n_seq, t_step = _token_tiling(b, t, MIX_ROWS)
    n0p = jnp.broadcast_to(n0[:, :, None, :], (b, n_hb, SUBLANES, HEAD_DIM))
    m0p = jnp.pad(m0, ((0, 0), (0, LANES - n_hb)))[:, None, :]
    gate_b = jnp.zeros((1, LANES), F32).at[0, 0:n_hb].set(lp["b_i"]).at[0, n_hb:2 * n_hb].set(lp["b_f"])
    out_a, s_t, out_b, c_t, n_t, m_t = _mixers(
        (kq, rq, bk, kkk, bk2, kk2), v_a, bv, sz, p_last, s0, row(lp["ln_w"]), row(lp["ln_b"]),
        q, k, v_b, gates, gob, c0, n0p, m0p, gate_b, row(lp["gn_w"]), n_seq, t_step, L)

    y = _out(out_a, out_b, sa, sb, x.reshape(b * t, d), ada_gate, lp["w_up_a"], lp["w_up_b"], lp["w_out"],
             g_final, b, nb, tt, final_norm).reshape(b, t, d)
    new_states = (shift_t[:, 0], s_t, conv_tail[:, SUBLANES - (cw - 1):], c_t, n_t[:, :, 0, :], m_t[:, 0, :n_hb])
    return y, new_states


def _trunk(x, mods, states, layers, g_final):
    depth = len(layers)
    new = [[] for _ in states]
    for l in range(depth):
        st = tuple(s[l] for s in states)
        x, st_new = _layer(x, mods[l], st, layers[l], g_final, final_norm=(l == depth - 1))
        for lst, s in zip(new, st_new):
            lst.append(s.astype(x.dtype))
    return x, tuple(jnp.stack(lst) for lst in new)


def kernel(x_prompt, x_sample, c_prompt, c_sample, state_rwkv_shift, state_rwkv_S, state_mlstm_conv, state_mlstm_C, state_mlstm_n, state_mlstm_m, g_norm, w_ada, b_ada, w_in, mu_shift, w_decay2, w0, w_iclr2, a0, k_k, k_a, r_k, ln_w, ln_b, conv_w, conv_b, b_i, b_f, gn_w, w_up_a, w_up_b, w_out, g_final):
    depth = g_norm.shape[0]
    bp, bs = x_prompt.shape[0], x_sample.shape[0]
    d = x_prompt.shape[-1]
    da, db = w_up_a.shape[1], w_up_b.shape[1]
    n_ha, n_hb = da // HEAD_DIM, db // HEAD_DIM
    sw = mu_shift.shape[-1]
    cc = conv_w.shape[-1]
    cw = conv_w.shape[1]
    dt = x_prompt.dtype

    sizes = (sw, da, cc, db, db, n_hb, n_hb, db, d, d)
    offs = [0]
    for s in sizes:
        offs.append(offs[-1] + s)
    seg = lambda w, i: w[:, offs[i]:offs[i + 1]]

    c_all = jnp.concatenate([c_prompt, c_sample], axis=0)
    n_c = c_all.shape[0]
    c_pad = jnp.pad(c_all, ((0, (-n_c) % SUBLANES), (0, 0)))

    layers, mods_p, mods_s = [], [], []
    for l in range(depth):
        w = w_in[l]
        wg = jnp.concatenate([seg(w, 5), seg(w, 6)], axis=1)
        wg = jnp.pad(wg, ((0, 0), (0, LANES - wg.shape[1])))
        w_segs = tuple(x.astype(BF16) for x in
                       (seg(w, 0), seg(w, 1), seg(w, 2), seg(w, 3), seg(w, 4), seg(w, 7), seg(w, 8), seg(w, 9), wg))
        layers.append(dict(
            g_norm=g_norm[l], w_segs=w_segs, mu_shift=mu_shift[l], w_decay2=w_decay2[l].astype(BF16),
            w0=w0[l], w_iclr2=w_iclr2[l].astype(BF16), a0=a0[l], k_k=k_k[l], k_a=k_a[l], r_k=r_k[l],
            ln_w=ln_w[l], ln_b=ln_b[l], conv_w=conv_w[l], conv_b=conv_b[l], b_i=b_i[l], b_f=b_f[l],
            gn_w=gn_w[l], w_up_a=w_up_a[l].astype(BF16), w_up_b=w_up_b[l].astype(BF16),
            w_out=w_out[l].astype(BF16)))
        mod = _ada(c_pad, w_ada[l], b_ada[l])
        mods_p.append(mod[:bp])
        mods_s.append(mod[bp:bp + bs])

    prompt_states = (
        jnp.zeros((depth, bp, sw), dt),
        jnp.zeros((depth, bp, n_ha, HEAD_DIM, HEAD_DIM), dt),
        jnp.zeros((depth, bp, cw - 1, cc), dt),
        jnp.zeros((depth, bp, n_hb, HEAD_DIM, HEAD_DIM), dt),
        jnp.zeros((depth, bp, n_hb, HEAD_DIM), dt),
        jnp.zeros((depth, bp, n_hb), dt),
    )
    sample_states = (state_rwkv_shift, state_rwkv_S, state_mlstm_conv,
                     state_mlstm_C, state_mlstm_n, state_mlstm_m)
    y_p, st_p = _trunk(x_prompt, mods_p, prompt_states, layers, g_final)
    y_s, st_s = _trunk(x_sample, mods_s, sample_states, layers, g_final)
    return (y_p, y_s) + st_p + st_s
```

```python
import functools
import math

import jax
import jax.numpy as jnp
from jax import lax
from jax.experimental import pallas as pl
from jax.experimental.pallas import tpu as pltpu

F32 = jnp.float32
BF16 = jnp.bfloat16

HEAD_DIM = 64
NORM_EPS = 1e-6
RWKV_GN_EPS = 64e-5
MLSTM_GN_EPS = 1e-6
ROWS = 64
MIX_ROWS = 128
PROJ_ROWS = 256
SUBLANES = 8
LANES = 128
TRI_BASE = 16
VMEM_LIMIT = 60 * 1024 * 1024

NT_DIMS = (((1,), (1,)), ((), ()))
TN_DIMS = (((0,), (0,)), ((), ()))


def _mm(a, b):
    return jnp.dot(a.astype(BF16), b.astype(BF16), preferred_element_type=F32)


def _mm_nt(a, b):
    return lax.dot_general(a.astype(BF16), b.astype(BF16), NT_DIMS, preferred_element_type=F32)


def _mm_tn(a, b):
    return lax.dot_general(a.astype(BF16), b.astype(BF16), TN_DIMS, preferred_element_type=F32)


def _sigmoid(x):
    return 1.0 / (1.0 + jnp.exp(-x))


def _silu(x):
    return x * _sigmoid(x)


def _softplus(x):
    return jnp.maximum(x, 0.0) + jnp.log(1.0 + jnp.exp(-jnp.abs(x)))


def _cumsum_time(x, tri_bf16):
    hi = x.astype(BF16)
    r1 = x - hi.astype(F32)
    mid = r1.astype(BF16)
    lo = (r1 - mid.astype(F32)).astype(BF16)
    dot = functools.partial(jnp.dot, preferred_element_type=F32)
    return dot(tri_bf16, hi) + dot(tri_bf16, mid) + dot(tri_bf16, lo)


def _shr(x, n):
    return lax.shift_right_logical(x, jnp.full(x.shape, n, x.dtype))


def _rows_of_seq(x, n_seq, L):
    if n_seq == 1:
        return x[0]
    return jnp.broadcast_to(x, (n_seq, L, x.shape[-1])).reshape(n_seq * L, x.shape[-1])


def _last_rows(x, n_seq):
    m, n = x.shape
    L = m // n_seq
    return x.reshape(n_seq, L, n)[:, L - 1:L, :]


def _head_sums(x):
    m, n = x.shape
    parts = [jnp.broadcast_to(jnp.sum(x[:, o:o + HEAD_DIM], axis=-1, keepdims=True), (m, HEAD_DIM))
             for o in range(0, n, HEAD_DIM)]
    return jnp.concatenate(parts, axis=-1)


def _head_sums_mxu(x, ones_ref):
    hi = x.astype(BF16)
    lo = (x - hi.astype(F32)).astype(BF16)
    s = jnp.dot(jnp.concatenate([hi, lo], axis=0), ones_ref[...], preferred_element_type=F32)
    return s[:x.shape[0]] + s[x.shape[0]:]


def _cummax_time(x, L, pos):
    y = x
    sh = 1
    while sh < L:
        y = jnp.where(pos >= sh, jnp.maximum(y, pltpu.roll(y, sh, axis=0)), y)
        sh *= 2
    return y


def _tri_inverse_all(mats, row, col, L):
    bs = min(TRI_BASE, L)
    sh = int(math.log2(bs))
    same = _shr(row, sh) == _shr(col, sh)
    eye = (row == col).astype(F32)
    ns = [jnp.where(same, -a, 0.0) for a in mats]
    ts = [eye + n for n in ns]
    ps = ns
    k = 1
    while 2 * k < bs:
        ps = [_mm(p, p) for p in ps]
        yield
        ts = [t + _mm(t, p) for t, p in zip(ts, ps)]
        yield
        k *= 2
    size = bs
    while size < L:
        sh = int(math.log2(size))
        lower_left = (_shr(row, sh + 1) == _shr(col, sh + 1)) & (_shr(row, sh) != _shr(col, sh))
        offs = [jnp.where(lower_left, a, 0.0) for a in mats]
        tmp = [_mm(t, o) for t, o in zip(ts, offs)]
        yield
        ts = [t - _mm(x, t) for t, x in zip(ts, tmp)]
        yield
        size *= 2
    return ts


def _interleave(plan):
    live = [[g, n, s, 0] for g, n, s in plan]
    while live:
        item = min(live, key=lambda it: it[2] + (1.0 - it[2]) * it[3] / it[1])
        try:
            next(item[0])
            item[3] += 1
        except StopIteration:
            live.remove(item)


def _ada_kernel(c_ref, w_ref, b_ref, o_ref):
    o_ref[...] = _mm(_silu(c_ref[...]), w_ref[...]) + b_ref[...]


def _ada(c, w_ada, b_ada):
    n, d = c.shape
    n3 = w_ada.shape[1]
    tn = d
    return pl.pallas_call(
        _ada_kernel,
        grid=(n3 // tn,),
        in_specs=[pl.BlockSpec((n, d), lambda j: (0, 0)),
                  pl.BlockSpec((d, tn), lambda j: (0, j)),
                  pl.BlockSpec((1, tn), lambda j: (0, j))],
        out_specs=pl.BlockSpec((n, tn), lambda j: (0, j)),
        out_shape=jax.ShapeDtypeStruct((n, n3), F32),
        name="ada",
    )(c, w_ada, b_ada.reshape(1, n3))


def _inproj_kernel(x_ref, sc_ref, sh_ref, g_ref, shift0_ref, conv0_ref,
                   w_shift, w_za, w_qk, w_vb, w_ob, w_zb, w_gla, w_glb, w_gt,
                   mu_ref, w0_ref, a0_ref, kk_ref, ka_ref, rk_ref, wd2_ref, wi2_ref, cw_ref, cb_ref,
                   kq_o, rq_o, bk_o, kkk_o, bk2_o, kk2_o, va_o, bv_o, sz_o, plast_o, shst_o,
                   q_o, k_o, vb_o, gt_o, gob_o, sa_o, sb_o, cst_o,
                   prev_scr, xbuf, *, chunk, lora_w, conv_w):
    j = pl.program_id(1)
    nb, tt, d = x_ref.shape
    M = nb * tt
    L = chunk
    n_chunks = M // L
    da = va_o.shape[-1]
    db = vb_o.shape[-1]
    cc = xbuf.shape[-1]
    pad = SUBLANES

    @pl.when(j == 0)
    def _():
        prev_scr[...] = shift0_ref[...]
        xbuf[:, 0:pad, :] = conv0_ref[...]

    x = x_ref[...]
    y = x * lax.rsqrt(jnp.mean(x * x, axis=-1, keepdims=True) + NORM_EPS) * g_ref[...]
    h = (y * (1.0 + sc_ref[...]) + sh_ref[...]).reshape(M, d).astype(BF16)
    proj = lambda w_ref: jnp.dot(h, w_ref[...], preferred_element_type=F32)

    p = proj(w_shift)
    row_w = lax.broadcasted_iota(jnp.int32, p.shape, 0)
    first = (row_w & (tt - 1)) == 0
    prev = jnp.where(first, _rows_of_seq(prev_scr[...], nb, tt), pltpu.roll(p, 1, axis=0))
    last = _last_rows(p, nb)
    prev_scr[...] = last
    shst_o[...] = last
    ps = p + mu_ref[...] * (prev - p)
    r = ps[:, 0:da]
    k = ps[:, da:2 * da]
    v = ps[:, 2 * da:3 * da]
    wl = ps[:, 3 * da:3 * da + lora_w]
    al = ps[:, 3 * da + lora_w:]

    qk_pre = proj(w_qk)
    z_a = proj(w_za)
    w = w0_ref[...] + _mm(jnp.tanh(wl), wd2_ref[...])
    lw = -math.exp(-0.5) * _sigmoid(w)
    a = _sigmoid(a0_ref[...] + _mm(al, wi2_ref[...]))
    v_b = proj(w_vb)
    g_t = proj(w_gt)
    o_b = proj(w_ob)
    z_b = proj(w_zb)
    kkr = k * kk_ref[...]
    kk = kkr * lax.rsqrt(jnp.maximum(_head_sums(kkr * kkr), 1e-24))
    k2 = k * (1.0 + (a - 1.0) * ka_ref[...])
    b = kk * a
    va_o[...] = v
    bv_o[...] = _head_sums(r * k2 * rk_ref[...]) * v
    sz_o[...] = _silu(z_a)

    lg = int(math.log2(L))
    row_m = lax.broadcasted_iota(jnp.int32, (M, M), 0)
    col_m = lax.broadcasted_iota(jnp.int32, (M, M), 1)
    tri = ((row_m >= col_m) & (_shr(row_m, lg) == _shr(col_m, lg))).astype(BF16)
    c = _cumsum_time(lw, tri)
    gl_a = proj(w_gla)
    gl_b = proj(w_glb)
    c_last = _last_rows(c, n_chunks)
    plast_o[...] = jnp.exp(c_last)
    e_nc = jnp.exp(-c)
    e_cl = jnp.exp(_rows_of_seq(c_last, n_chunks, L) - c)
    kq_o[...] = (kk * jnp.exp(c - lw)).astype(kq_o.dtype)
    rq_o[...] = (r * jnp.exp(c)).astype(rq_o.dtype)
    bk_o[...] = (b * e_nc).astype(bk_o.dtype)
    kkk_o[...] = (k2 * e_nc).astype(kkk_o.dtype)
    bk2_o[...] = (b * e_cl).astype(bk2_o.dtype)
    kk2_o[...] = (k2 * e_cl).astype(kk2_o.dtype)

    xbuf[:, pad:pad + tt, :] = qk_pre.reshape(nb, tt, cc)
    conv = cb_ref[...]
    for tap in range(conv_w):
        conv = conv + xbuf[:, pl.ds(pad - (conv_w - 1) + tap, tt), :].reshape(M, cc) * cw_ref[tap:tap + 1, :]
    tail = xbuf[:, tt:tt + pad, :]
    xbuf[:, 0:pad, :] = tail
    cst_o[...] = tail
    qk = _silu(conv)
    q_o[...] = qk[:, 0:db].astype(q_o.dtype)
    k_o[...] = (qk[:, db:] * (1.0 / math.sqrt(HEAD_DIM))).astype(k_o.dtype)
    vb_o[...] = v_b
    gt_o[...] = g_t
    gob_o[...] = _sigmoid(o_b) * _silu(z_b)

    sa_o[...] = _sigmoid(gl_a).astype(sa_o.dtype)
    sb_o[...] = _sigmoid(gl_b).astype(sb_o.dtype)


def _inproj(x, scale, shift, g_norm, shift0, conv0p, lp, nb, tt, chunk, act_dtype):
    b, t, d = x.shape
    n_tok = b * t
    m = nb * tt
    nj = t // tt
    sw = shift0.shape[-1]
    cc = conv0p.shape[-1]
    da, db = lp["w_up_a"].shape[0], lp["w_up_b"].shape[0]
    cw = lp["conv_w"].shape[0]
    row = lambda a: a.reshape(1, -1)
    rows = [row(lp[n]) for n in ("mu_shift", "w0", "a0", "k_k", "k_a", "r_k")]
    smalls = rows + [lp["w_decay2"], lp["w_iclr2"], lp["conv_w"], row(lp["conv_b"])]
    const = lambda i, j: (0, 0)
    tok = lambda i, j: (i * nj + j, 0)
    per_b = lambda i, j: (i, 0, 0)
    in_specs = [pl.BlockSpec((nb, tt, d), lambda i, j: (i, j, 0)),
                pl.BlockSpec((nb, 1, d), per_b),
                pl.BlockSpec((nb, 1, d), per_b),
                pl.BlockSpec((1, d), const),
                pl.BlockSpec((nb, 1, sw), per_b),
                pl.BlockSpec((nb, SUBLANES, cc), per_b)]
    in_specs += [pl.BlockSpec(w.shape, const, pipeline_mode=pl.Buffered(1)) for w in lp["w_segs"]]
    in_specs += [pl.BlockSpec(a.shape, const) for a in smalls]
    tok_out = lambda n, dt: (pl.BlockSpec((m, n), tok), jax.ShapeDtypeStruct((n_tok, n), dt))
    outs = [tok_out(da, act_dtype)] * 6 + [tok_out(da, F32)] * 3
    outs += [(pl.BlockSpec((m // chunk, 1, da), lambda i, j: (i * nj + j, 0, 0)),
              jax.ShapeDtypeStruct((n_tok // chunk, 1, da), F32)),
             (pl.BlockSpec((nb, 1, sw), per_b), jax.ShapeDtypeStruct((b, 1, sw), F32))]
    outs += [tok_out(db, act_dtype)] * 2 + [tok_out(db, F32), tok_out(LANES, F32), tok_out(db, F32)]
    outs += [tok_out(d, BF16)] * 2
    outs += [(pl.BlockSpec((nb, SUBLANES, cc), per_b), jax.ShapeDtypeStruct((b, SUBLANES, cc), F32))]
    return pl.pallas_call(
        functools.partial(_inproj_kernel, chunk=chunk, lora_w=lp["w_decay2"].shape[0], conv_w=cw),
        grid=(b // nb, nj),
        in_specs=in_specs,
        out_specs=[o[0] for o in outs], out_shape=[o[1] for o in outs],
        scratch_shapes=[pltpu.VMEM((nb, 1, sw), F32),
                        pltpu.VMEM((nb, tt + SUBLANES, cc), F32)],
        compiler_params=pltpu.CompilerParams(vmem_limit_bytes=VMEM_LIMIT),
        name="inproj",
    )(x, scale, shift, g_norm.reshape(1, d), shift0, conv0p, *lp["w_segs"], *smalls)


def _rwkv_body(kq_ref, rq_ref, bk_ref, kkk_ref, bk2_ref, kk2_ref, v_ref, bv_ref, sz_ref, plast_ref,
               s0_ref, lnw_ref, lnb_ref, ones_ref, oa_ref, st_ref, s_scr, *, n_seq, chunk, n_heads):
    j = pl.program_id(1)
    M = kq_ref.shape[0]
    L = chunk
    t_step = M // n_seq
    RU = min(M, ROWS)
    n_units = M // RU
    heads = range(n_heads)
    units = range(n_units)
    pairs = [(u, h) for u in units for h in heads]

    @pl.when(j == 0)
    def _():
        s_scr[...] = s0_ref[...]

    lg = int(math.log2(L))
    row = lax.broadcasted_iota(jnp.int32, (RU, RU), 0)
    col = lax.broadcasted_iota(jnp.int32, (RU, RU), 1)
    row2 = lax.broadcasted_iota(jnp.int32, (RU, 2 * RU), 0)
    col2 = lax.broadcasted_iota(jnp.int32, (RU, 2 * RU), 1) & (RU - 1)
    same2 = _shr(row2, lg) == _shr(col2, lg)
    incl2 = (row2 >= col2) & same2
    strict2 = (row2 > col2) & same2

    p_last = plast_ref[...]
    sls = [slice(h * HEAD_DIM, (h + 1) * HEAD_DIM) for h in heads]
    rus = [slice(u * RU, (u + 1) * RU) for u in units]
    blk = lambda ref, q: ref[rus[q[0]], sls[q[1]]]
    kq = {q: blk(kq_ref, q) for q in pairs}
    rq = {q: blk(rq_ref, q) for q in pairs}
    bk2 = {q: blk(bk2_ref, q) for q in pairs}
    kk2 = {q: blk(kk2_ref, q) for q in pairs}
    vs = {q: blk(v_ref, q) for q in pairs}
    qr = {q: jnp.concatenate([kq[q], rq[q]], axis=0).astype(BF16) for q in pairs}
    bkk = {q: jnp.concatenate([blk(bk_ref, q), blk(kkk_ref, q)], axis=0).astype(BF16) for q in pairs}
    yield

    gs = {q: _mm_nt(qr[q], bkk[q]) for q in pairs}
    yield
    ga = {q: jnp.where(strict2, gs[q][:RU, :], 0.0) for q in pairs}
    gm = {q: jnp.where(incl2, gs[q][RU:, :], 0.0) for q in pairs}
    yield
    t_inv = dict(zip(pairs, (yield from _tri_inverse_all([ga[q][:, :RU] for q in pairs], row, col, L))))
    zeros = jnp.zeros((RU, HEAD_DIM), F32)
    akv = {q: _mm(ga[q], jnp.concatenate([zeros, vs[q]], axis=0)) for q in pairs}
    yield

    state = {}
    ys = {}
    for u in units:
        if L == RU:
            b = (u * RU) // t_step
            s0 = [state[b, h] if (b, h) in state else s_scr[b, h] for h in heads]
            ws = [_mm_nt(qr[u, h], s0[h]) for h in heads]
            w1 = [x[:RU] for x in ws]
            wr = [x[RU:] for x in ws]
        else:
            n_in = RU // L
            s0s = [[s_scr[u * n_in + i, h] for h in heads] for i in range(n_in)]
            w1, wr = [], []
            for h in heads:
                parts = [_mm_nt(jnp.concatenate([kq[u, h][i * L:(i + 1) * L], rq[u, h][i * L:(i + 1) * L]], axis=0),
                                s0s[i][h]) for i in range(n_in)]
                w1.append(jnp.concatenate([x[:L] for x in parts], axis=0))
                wr.append(jnp.concatenate([x[L:] for x in parts], axis=0))
        yield
        us = [-_mm(t_inv[u, h], w1[h] + akv[u, h]) for h in heads]
        yield
        uv = [jnp.concatenate([us[h], vs[u, h]], axis=0) for h in heads]
        for h in heads:
            ys[u, h] = wr[h] + _mm(gm[u, h], uv[h])
        yield
        if L == RU:
            for h in heads:
                bkk2 = jnp.concatenate([bk2[u, h], kk2[u, h]], axis=0)
                state[b, h] = s0[h] * p_last[u][:, sls[h]] + _mm_tn(uv[h], bkk2)
        else:
            for h in heads:
                for i in range(n_in):
                    rows = slice(i * L, (i + 1) * L)
                    uv_i = jnp.concatenate([us[h][rows], vs[u, h][rows]], axis=0)
                    bkk2_i = jnp.concatenate([bk2[u, h][rows], kk2[u, h][rows]], axis=0)
                    state[u * n_in + i, h] = (s0s[i][h] * p_last[u * n_in + i][:, sls[h]]
                                              + _mm_tn(uv_i, bkk2_i))
        yield
    for (b, h), s_new in state.items():
        s_scr[b, h] = s_new

    rows_out = [jnp.concatenate([ys[u, h] for h in heads], axis=-1) for u in units]
    y = rows_out[0] if n_units == 1 else jnp.concatenate(rows_out, axis=0)
    yc = y - _head_sums_mxu(y, ones_ref) * (1.0 / HEAD_DIM)
    yield
    var = _head_sums_mxu(yc * yc, ones_ref) * (1.0 / HEAD_DIM)
    yn = yc * lax.rsqrt(var + RWKV_GN_EPS)
    oa_ref[...] = ((yn * lnw_ref[...] + lnb_ref[...] + bv_ref[...]) * sz_ref[...]).astype(oa_ref.dtype)

    @pl.when(j == pl.num_programs(1) - 1)
    def _():
        st_ref[...] = s_scr[...]


def _mlstm_body(q_ref, k_ref, v_ref, g_ref, gob_ref, c0_ref, n0_ref, m0_ref, gb_ref, gnw_ref, ones_ref,
                ob_ref, ct_ref, nt_ref, mt_ref, cn_scr, m_scr, *, n_seq, n_heads):
    j = pl.program_id(1)
    M = q_ref.shape[0]
    L = M // n_seq
    heads = range(n_heads)
    seqs = range(n_seq)

    @pl.when(j == 0)
    def _():
        cn_scr[:, :, 0:HEAD_DIM, :] = c0_ref[...]
        cn_scr[:, :, HEAD_DIM:, :] = n0_ref[...]
        m_scr[...] = m0_ref[...]

    row = lax.broadcasted_iota(jnp.int32, (M, M), 0)
    col = lax.broadcasted_iota(jnp.int32, (M, M), 1)
    same_seq = _shr(row, int(math.log2(L))) == _shr(col, int(math.log2(L)))
    incl = (row >= col) & same_seq

    g = g_ref[...] + gb_ref[...]
    bcum = pltpu.roll(_cumsum_time(-_softplus(-g), incl.astype(BF16)), LANES - n_heads, axis=1)
    m_prev = m_scr[...]
    m_prev_rows = _rows_of_seq(m_prev, n_seq, L)
    x_all = g - bcum
    pos = lax.broadcasted_iota(jnp.int32, (M, LANES), 0) & (L - 1)
    m_all = bcum + jnp.maximum(_cummax_time(x_all, L, pos), m_prev_rows)
    bm_all = bcum - m_all
    w_in_all = jnp.exp(bcum + m_prev_rows - m_all)
    e_negm_all = jnp.exp(-m_all)
    m_new = _last_rows(m_all, n_seq)
    b_last = _last_rows(bcum, n_seq)
    ws_all = jnp.exp(_rows_of_seq(b_last - m_new, n_seq, L) + x_all)
    dec_all = jnp.exp(b_last + m_prev - m_new)
    m_scr[...] = m_new
    x_t = jnp.transpose(x_all)
    yield

    ones = jnp.ones((M, SUBLANES), F32)
    sls = [slice(h * HEAD_DIM, (h + 1) * HEAD_DIM) for h in heads]
    qs = [q_ref[:, sl] for sl in sls]
    ks = [k_ref[:, sl] for sl in sls]
    v1 = [jnp.concatenate([v_ref[:, sl], ones], axis=-1) for sl in sls]
    cn = [[cn_scr[b, h] for h in heads] for b in seqs]

    qk_t = [_mm_nt(qs[h], ks[h]) for h in heads]
    if n_seq == 1:
        qc = [_mm_nt(qs[h], cn[0][h]) for h in heads]
    else:
        qc = [jnp.concatenate([_mm_nt(qs[h][b * L:(b + 1) * L], cn[b][h]) for b in seqs], axis=0) for h in heads]
    yield
    w_ts = [jnp.where(incl, jnp.exp(bm_all[:, h:h + 1] + x_t[h:h + 1, :]), 0.0) for h in heads]
    yield
    s = [qk_t[h] * w_ts[h] for h in heads]
    numden = [_mm(s[h], v1[h]) + w_in_all[:, h:h + 1] * qc[h] for h in heads]
    yield
    den = [jnp.maximum(jnp.abs(numden[h][:, HEAD_DIM:HEAD_DIM + 1]), e_negm_all[:, h:h + 1]) for h in heads]
    hh = [numden[h][:, 0:HEAD_DIM] / den[h] for h in heads]
    hh_all = jnp.concatenate(hh, axis=-1)
    ssq = _head_sums_mxu(hh_all * hh_all, ones_ref) * (1.0 / HEAD_DIM)

    yield
    v1w = [v1[h] * ws_all[:, h:h + 1] for h in heads]
    for h in heads:
        for b in seqs:
            rows = slice(b * L, (b + 1) * L)
            cn_scr[b, h] = dec_all[b][:, h:h + 1] * cn[b][h] + _mm_tn(v1w[h][rows], ks[h][rows])
    yield

    hb = hh_all * lax.rsqrt(ssq + MLSTM_GN_EPS)
    ob_ref[...] = (hb * gnw_ref[...] * gob_ref[...]).astype(ob_ref.dtype)

    @pl.when(j == pl.num_programs(1) - 1)
    def _():
        ct_ref[...] = cn_scr[:, :, 0:HEAD_DIM, :]
        nt_ref[...] = cn_scr[:, :, HEAD_DIM:, :]
        mt_ref[...] = m_scr[...]


N_RWKV_IN, N_RWKV_OUT, N_MLSTM_IN, N_MLSTM_OUT = 14, 2, 11, 4
RWKV_STAGES, MLSTM_STAGES = 24, 8
MLSTM_START = 0.5


def _mixers_kernel(*refs, n_seq, chunk, n_heads_a, n_heads_b):
    it = iter(refs)
    take = lambda n: [next(it) for _ in range(n)]
    r_in, m_in = take(N_RWKV_IN), take(N_MLSTM_IN)
    r_out, m_out = take(N_RWKV_OUT), take(N_MLSTM_OUT)
    s_scr, cn_scr, m_scr = take(3)
    _interleave([
        (_rwkv_body(*r_in, *r_out, s_scr, n_seq=n_seq, chunk=chunk, n_heads=n_heads_a), RWKV_STAGES, 0.0),
        (_mlstm_body(*m_in, *m_out, cn_scr, m_scr, n_seq=n_seq, n_heads=n_heads_b), MLSTM_STAGES, MLSTM_START)])


def _mixers(acts, v_a, bv, sz, p_last, s0, ln_w, ln_b, q, k, v_b, gates, gob, c0, n0, m0, gate_b, gn_w,
            n_seq, t_step, chunk):
    b = s0.shape[0]
    n_ha, n_hb = s0.shape[1], c0.shape[1]
    n_tok, da = v_a.shape
    db = v_b.shape[1]
    m = n_seq * t_step
    nj = n_tok // (b * t_step)
    tok = lambda i, j: (i * nj + j, 0)
    per_b3 = lambda i, j: (i, 0, 0)
    per_b4 = lambda i, j: (i, 0, 0, 0)
    const2 = lambda i, j: (0, 0)
    s_spec = pl.BlockSpec((n_seq, n_ha, HEAD_DIM, HEAD_DIM), per_b4)
    c_spec = pl.BlockSpec((n_seq, n_hb, HEAD_DIM, HEAD_DIM), per_b4)
    n_spec = pl.BlockSpec((n_seq, n_hb, SUBLANES, HEAD_DIM), per_b4)
    m_spec = pl.BlockSpec((n_seq, 1, LANES), per_b3)
    row_a, row_b = pl.BlockSpec((m, da), tok), pl.BlockSpec((m, db), tok)
    in_specs = [row_a] * 9
    ones_a = jnp.kron(jnp.eye(n_ha, dtype=F32), jnp.ones((HEAD_DIM, HEAD_DIM), F32)).astype(BF16)
    ones_b = jnp.kron(jnp.eye(n_hb, dtype=F32), jnp.ones((HEAD_DIM, HEAD_DIM), F32)).astype(BF16)
    in_specs += [pl.BlockSpec((m // chunk, 1, da), lambda i, j: (i * nj + j, 0, 0)), s_spec,
                 pl.BlockSpec(ln_w.shape, const2), pl.BlockSpec(ln_b.shape, const2),
                 pl.BlockSpec(ones_a.shape, const2)]
    in_specs += [row_b, row_b, row_b, pl.BlockSpec((m, LANES), tok), row_b, c_spec, n_spec, m_spec,
                 pl.BlockSpec(gate_b.shape, const2), pl.BlockSpec(gn_w.shape, const2),
                 pl.BlockSpec(ones_b.shape, const2)]
    return pl.pallas_call(
        functools.partial(_mixers_kernel, n_seq=n_seq, chunk=chunk, n_heads_a=n_ha, n_heads_b=n_hb),
        grid=(b // n_seq, nj),
        in_specs=in_specs,
        out_specs=[row_a, s_spec, row_b, c_spec, n_spec, m_spec],
        out_shape=[jax.ShapeDtypeStruct((n_tok, da), BF16), jax.ShapeDtypeStruct(s0.shape, F32),
                   jax.ShapeDtypeStruct((n_tok, db), BF16), jax.ShapeDtypeStruct(c0.shape, F32),
                   jax.ShapeDtypeStruct(n0.shape, F32), jax.ShapeDtypeStruct(m0.shape, F32)],
        scratch_shapes=[pltpu.VMEM((n_seq, n_ha, HEAD_DIM, HEAD_DIM), F32),
                        pltpu.VMEM((n_seq, n_hb, HEAD_DIM + SUBLANES, HEAD_DIM), F32),
                        pltpu.VMEM((n_seq, 1, LANES), F32)],
        compiler_params=pltpu.CompilerParams(vmem_limit_bytes=VMEM_LIMIT),
        name="mixers",
    )(*acts, v_a, bv, sz, p_last, s0, ln_w, ln_b, ones_a, q, k, v_b, gates, gob, c0, n0, m0, gate_b, gn_w, ones_b)


def _out_kernel(oa_ref, ob_ref, sa_ref, sb_ref, x_ref, gate_ref, wua_ref, wub_ref, wo_ref, gf_ref,
                y_ref, *, nb, final_norm):
    m = x_ref.shape[0]
    ua = jnp.dot(oa_ref[...], wua_ref[...], preferred_element_type=F32)
    ub = jnp.dot(ob_ref[...], wub_ref[...], preferred_element_type=F32)
    merged = sa_ref[...] * ua + sb_ref[...] * ub
    mo = jnp.dot(merged.astype(BF16), wo_ref[...], preferred_element_type=F32)
    xn = x_ref[...] + _rows_of_seq(gate_ref[...], nb, m // nb) * mo
    if final_norm:
        xn = xn * lax.rsqrt(jnp.mean(xn * xn, axis=-1, keepdims=True) + NORM_EPS) * gf_ref[...]
    y_ref[...] = xn


def _out(out_a, out_b, sa, sb, x, gate, w_up_a, w_up_b, w_out, g_final, b, nb, tt, final_norm):
    n_tok, d = x.shape
    nj = n_tok // (b * tt)
    m = nb * tt
    tok = lambda i, j: (i * nj + j, 0)
    const2 = lambda i, j: (0, 0)
    wspec = lambda w: pl.BlockSpec(w.shape, const2, pipeline_mode=pl.Buffered(1))
    in_specs = [pl.BlockSpec((m, out_a.shape[-1]), tok),
                pl.BlockSpec((m, out_b.shape[-1]), tok),
                pl.BlockSpec((m, d), tok),
                pl.BlockSpec((m, d), tok),
                pl.BlockSpec((m, d), tok),
                pl.BlockSpec((nb, 1, d), lambda i, j: (i, 0, 0)),
                wspec(w_up_a), wspec(w_up_b), wspec(w_out),
                pl.BlockSpec((1, d), const2)]
    return pl.pallas_call(
        functools.partial(_out_kernel, nb=nb, final_norm=final_norm),
        grid=(b // nb, nj),
        in_specs=in_specs,
        out_specs=pl.BlockSpec((m, d), tok),
        out_shape=jax.ShapeDtypeStruct((n_tok, d), F32),
        compiler_params=pltpu.CompilerParams(vmem_limit_bytes=VMEM_LIMIT),
        name="merge_out",
    )(out_a, out_b, sa, sb, x, gate, w_up_a, w_up_b, w_out, g_final.reshape(1, d))


def _token_tiling(b, t, target):
    if t >= target:
        return 1, math.gcd(t, target)
    return math.gcd(b, target // t), t


def _layer(x, mod, states, lp, g_final, final_norm):
    b, t, d = x.shape
    shift0, s0, conv0, c0, n0, m0 = states
    n_hb = c0.shape[1]
    cw = lp["conv_w"].shape[0]
    row = lambda a: a.reshape(1, -1)

    ada_shift, ada_scale, ada_gate = (mod[:, None, i * d:(i + 1) * d] for i in range(3))
    nb, tt = _token_tiling(b, t, PROJ_ROWS)
    L = math.gcd(t, ROWS)
    act_dtype = BF16 if L % 16 == 0 else F32
    conv0p = jnp.pad(conv0, ((0, 0), (SUBLANES - (cw - 1), 0), (0, 0)))
    (kq, rq, bk, kkk, bk2, kk2, v_a, bv, sz, p_last, shift_t,
     q, k, v_b, gates, gob, sa, sb, conv_tail) = _inproj(
        x, ada_scale, ada_shift, lp["g_norm"], shift0[:, None, :], conv0p, lp, nb, tt, L, act_dtype)

    n_seq, t_step = _token_tiling(b, t, MIX_ROWS)
    n0p = jnp.broadcast_to(n0[:, :, None, :], (b, n_hb, SUBLANES, HEAD_DIM))
    m0p = jnp.pad(m0, ((0, 0), (0, LANES - n_hb)))[:, None, :]
    gate_b = jnp.zeros((1, LANES), F32).at[0, 0:n_hb].set(lp["b_i"]).at[0, n_hb:2 * n_hb].set(lp["b_f"])
    out_a, s_t, out_b, c_t, n_t, m_t = _mixers(
        (kq, rq, bk, kkk, bk2, kk2), v_a, bv, sz, p_last, s0, row(lp["ln_w"]), row(lp["ln_b"]),
        q, k, v_b, gates, gob, c0, n0p, m0p, gate_b, row(lp["gn_w"]), n_seq, t_step, L)

    y = _out(out_a, out_b, sa, sb, x.reshape(b * t, d), ada_gate, lp["w_up_a"], lp["w_up_b"], lp["w_out"],
             g_final, b, nb, tt, final_norm).reshape(b, t, d)
    new_states = (shift_t[:, 0], s_t, conv_tail[:, SUBLANES - (cw - 1):], c_t, n_t[:, :, 0, :], m_t[:, 0, :n_hb])
    return y, new_states


def _trunk(x, mods, states, layers, g_final):
    depth = len(layers)
    new = [[] for _ in states]
    for l in range(depth):
        st = tuple(s[l] for s in states)
        x, st_new = _layer(x, mods[l], st, layers[l], g_final, final_norm=(l == depth - 1))
        for lst, s in zip(new, st_new):
            lst.append(s.astype(x.dtype))
    return x, tuple(jnp.stack(lst) for lst in new)


def kernel(x_prompt, x_sample, c_prompt, c_sample, state_rwkv_shift, state_rwkv_S, state_mlstm_conv, state_mlstm_C, state_mlstm_n, state_mlstm_m, g_norm, w_ada, b_ada, w_in, mu_shift, w_decay2, w0, w_iclr2, a0, k_k, k_a, r_k, ln_w, ln_b, conv_w, conv_b, b_i, b_f, gn_w, w_up_a, w_up_b, w_out, g_final):
    depth = g_norm.shape[0]
    bp, bs = x_prompt.shape[0], x_sample.shape[0]
    d = x_prompt.shape[-1]
    da, db = w_up_a.shape[1], w_up_b.shape[1]
    n_ha, n_hb = da // HEAD_DIM, db // HEAD_DIM
    sw = mu_shift.shape[-1]
    cc = conv_w.shape[-1]
    cw = conv_w.shape[1]
    dt = x_prompt.dtype

    sizes = (sw, da, cc, db, db, n_hb, n_hb, db, d, d)
    offs = [0]
    for s in sizes:
        offs.append(offs[-1] + s)
    seg = lambda w, i: w[:, offs[i]:offs[i + 1]]

    c_all = jnp.concatenate([c_prompt, c_sample], axis=0)
    n_c = c_all.shape[0]
    c_pad = jnp.pad(c_all, ((0, (-n_c) % SUBLANES), (0, 0)))

    layers, mods_p, mods_s = [], [], []
    for l in range(depth):
        w = w_in[l]
        wg = jnp.concatenate([seg(w, 5), seg(w, 6)], axis=1)
        wg = jnp.pad(wg, ((0, 0), (0, LANES - wg.shape[1])))
        w_segs = tuple(x.astype(BF16) for x in
                       (seg(w, 0), seg(w, 1), seg(w, 2), seg(w, 3), seg(w, 4), seg(w, 7), seg(w, 8), seg(w, 9), wg))
        layers.append(dict(
            g_norm=g_norm[l], w_segs=w_segs, mu_shift=mu_shift[l], w_decay2=w_decay2[l].astype(BF16),
            w0=w0[l], w_iclr2=w_iclr2[l].astype(BF16), a0=a0[l], k_k=k_k[l], k_a=k_a[l], r_k=r_k[l],
            ln_w=ln_w[l], ln_b=ln_b[l], conv_w=conv_w[l], conv_b=conv_b[l], b_i=b_i[l], b_f=b_f[l],
            gn_w=gn_w[l], w_up_a=w_up_a[l].astype(BF16), w_up_b=w_up_b[l].astype(BF16),
            w_out=w_out[l].astype(BF16)))
        mod = _ada(c_pad, w_ada[l], b_ada[l])
        mods_p.append(mod[:bp])
        mods_s.append(mod[bp:bp + bs])

    prompt_states = (
        jnp.zeros((depth, bp, sw), dt),
        jnp.zeros((depth, bp, n_ha, HEAD_DIM, HEAD_DIM), dt),
        jnp.zeros((depth, bp, cw - 1, cc), dt),
        jnp.zeros((depth, bp, n_hb, HEAD_DIM, HEAD_DIM), dt),
        jnp.zeros((depth, bp, n_hb, HEAD_DIM), dt),
        jnp.zeros((depth, bp, n_hb), dt),
    )
    sample_states = (state_rwkv_shift, state_rwkv_S, state_mlstm_conv,
                     state_mlstm_C, state_mlstm_n, state_mlstm_m)
    y_p, st_p = _trunk(x_prompt, mods_p, prompt_states, layers, g_final)
    y_s, st_s = _trunk(x_sample, mods_s, sample_states, layers, g_final)
    return (y_p, y_s) + st_p + st_s
```

```python
import functools
import math

import jax
import jax.numpy as jnp
from jax import lax
from jax.experimental import pallas as pl
from jax.experimental.pallas import tpu as pltpu

F32 = jnp.float32
BF16 = jnp.bfloat16

HEAD_DIM = 64
NORM_EPS = 1e-6
RWKV_GN_EPS = 64e-5
MLSTM_GN_EPS = 1e-6
ROWS = 64
MIX_ROWS = 128
PROJ_ROWS = 256
SUBLANES = 8
LANES = 128
TRI_BASE = 16
VMEM_LIMIT = 60 * 1024 * 1024

NT_DIMS = (((1,), (1,)), ((), ()))
TN_DIMS = (((0,), (0,)), ((), ()))


def _mm(a, b):
    return jnp.dot(a.astype(BF16), b.astype(BF16), preferred_element_type=F32)


def _mm_nt(a, b):
    return lax.dot_general(a.astype(BF16), b.astype(BF16), NT_DIMS, preferred_element_type=F32)


def _mm_tn(a, b):
    return lax.dot_general(a.astype(BF16), b.astype(BF16), TN_DIMS, preferred_element_type=F32)


def _sigmoid(x):
    return 1.0 / (1.0 + jnp.exp(-x))


def _silu(x):
    return x * _sigmoid(x)


def _softplus(x):
    return jnp.maximum(x, 0.0) + jnp.log(1.0 + jnp.exp(-jnp.abs(x)))


def _cumsum_time(x, tri_bf16):
    hi = x.astype(BF16)
    r1 = x - hi.astype(F32)
    mid = r1.astype(BF16)
    lo = (r1 - mid.astype(F32)).astype(BF16)
    dot = functools.partial(jnp.dot, preferred_element_type=F32)
    return dot(tri_bf16, hi) + dot(tri_bf16, mid) + dot(tri_bf16, lo)


def _shr(x, n):
    return lax.shift_right_logical(x, jnp.full(x.shape, n, x.dtype))


def _rows_of_seq(x, n_seq, L):
    if n_seq == 1:
        return x[0]
    return jnp.broadcast_to(x, (n_seq, L, x.shape[-1])).reshape(n_seq * L, x.shape[-1])


def _last_rows(x, n_seq):
    m, n = x.shape
    L = m // n_seq
    return x.reshape(n_seq, L, n)[:, L - 1:L, :]


def _head_sums(x):
    m, n = x.shape
    parts = [jnp.broadcast_to(jnp.sum(x[:, o:o + HEAD_DIM], axis=-1, keepdims=True), (m, HEAD_DIM))
             for o in range(0, n, HEAD_DIM)]
    return jnp.concatenate(parts, axis=-1)


def _head_sums_mxu(x, ones_ref):
    hi = x.astype(BF16)
    lo = (x - hi.astype(F32)).astype(BF16)
    s = jnp.dot(jnp.concatenate([hi, lo], axis=0), ones_ref[...], preferred_element_type=F32)
    return s[:x.shape[0]] + s[x.shape[0]:]


def _cummax_time(x, L, pos):
    y = x
    sh = 1
    while sh < L:
        y = jnp.where(pos >= sh, jnp.maximum(y, pltpu.roll(y, sh, axis=0)), y)
        sh *= 2
    return y


def _tri_inverse_all(mats, row, col, L, mm=_mm):
    bs = min(TRI_BASE, L)
    sh = int(math.log2(bs))
    same = _shr(row, sh) == _shr(col, sh)
    eye = (row == col).astype(F32)
    ns = [jnp.where(same, -a, 0.0) for a in mats]
    ts = [eye + n for n in ns]
    ps = ns
    k = 1
    while 2 * k < bs:
        ps = [mm(p, p) for p in ps]
        yield
        ts = [t + mm(t, p) for t, p in zip(ts, ps)]
        yield
        k *= 2
    size = bs
    while size < L:
        sh = int(math.log2(size))
        lower_left = (_shr(row, sh + 1) == _shr(col, sh + 1)) & (_shr(row, sh) != _shr(col, sh))
        offs = [jnp.where(lower_left, a, 0.0) for a in mats]
        tmp = [mm(t, o) for t, o in zip(ts, offs)]
        yield
        ts = [t - mm(x, t) for t, x in zip(ts, tmp)]
        yield
        size *= 2
    return ts


def _interleave(plan):
    live = [[g, n, s, 0] for g, n, s in plan]
    while live:
        item = min(live, key=lambda it: it[2] + (1.0 - it[2]) * it[3] / it[1])
        try:
            next(item[0])
            item[3] += 1
        except StopIteration:
            live.remove(item)


def _ada_kernel(c_ref, w_ref, b_ref, o_ref):
    o_ref[...] = _mm(_silu(c_ref[...]), w_ref[...]) + b_ref[...]


def _ada(c, w_ada, b_ada):
    n, d = c.shape
    n3 = w_ada.shape[1]
    tn = d
    return pl.pallas_call(
        _ada_kernel,
        grid=(n3 // tn,),
        in_specs=[pl.BlockSpec((n, d), lambda j: (0, 0)),
                  pl.BlockSpec((d, tn), lambda j: (0, j)),
                  pl.BlockSpec((1, tn), lambda j: (0, j))],
        out_specs=pl.BlockSpec((n, tn), lambda j: (0, j)),
        out_shape=jax.ShapeDtypeStruct((n, n3), F32),
        name="ada",
    )(c, w_ada, b_ada.reshape(1, n3))


def _inproj_kernel(x_ref, sc_ref, sh_ref, g_ref, shift0_ref, conv0_ref,
                   w_shift, w_za, w_qk, w_vb, w_ob, w_zb, w_gla, w_glb, w_gt,
                   mu_ref, w0_ref, a0_ref, kk_ref, ka_ref, rk_ref, wd2_ref, wi2_ref, cw_ref, cb_ref,
                   kq_o, rq_o, bk_o, kkk_o, bk2_o, kk2_o, va_o, bv_o, sz_o, plast_o, shst_o,
                   q_o, k_o, vb_o, gt_o, gob_o, sa_o, sb_o, cst_o,
                   prev_scr, xbuf, *, chunk, lora_w, conv_w):
    j = pl.program_id(1)
    nb, tt, d = x_ref.shape
    M = nb * tt
    L = chunk
    n_chunks = M // L
    da = va_o.shape[-1]
    db = vb_o.shape[-1]
    cc = xbuf.shape[-1]
    pad = SUBLANES

    @pl.when(j == 0)
    def _():
        prev_scr[...] = shift0_ref[...]
        xbuf[:, 0:pad, :] = conv0_ref[...]

    x = x_ref[...]
    y = x * lax.rsqrt(jnp.mean(x * x, axis=-1, keepdims=True) + NORM_EPS) * g_ref[...]
    h = (y * (1.0 + sc_ref[...]) + sh_ref[...]).reshape(M, d).astype(BF16)
    proj = lambda w_ref: jnp.dot(h, w_ref[...], preferred_element_type=F32)

    p = proj(w_shift)
    row_w = lax.broadcasted_iota(jnp.int32, p.shape, 0)
    first = (row_w & (tt - 1)) == 0
    prev = jnp.where(first, _rows_of_seq(prev_scr[...], nb, tt), pltpu.roll(p, 1, axis=0))
    last = _last_rows(p, nb)
    prev_scr[...] = last
    shst_o[...] = last
    ps = p + mu_ref[...] * (prev - p)
    r = ps[:, 0:da]
    k = ps[:, da:2 * da]
    v = ps[:, 2 * da:3 * da]
    wl = ps[:, 3 * da:3 * da + lora_w]
    al = ps[:, 3 * da + lora_w:]

    qk_pre = proj(w_qk)
    z_a = proj(w_za)
    w = w0_ref[...] + _mm(jnp.tanh(wl), wd2_ref[...])
    lw = -math.exp(-0.5) * _sigmoid(w)
    a = _sigmoid(a0_ref[...] + _mm(al, wi2_ref[...]))
    v_b = proj(w_vb)
    g_t = proj(w_gt)
    o_b = proj(w_ob)
    z_b = proj(w_zb)
    kkr = k * kk_ref[...]
    kk = kkr * lax.rsqrt(jnp.maximum(_head_sums(kkr * kkr), 1e-24))
    k2 = k * (1.0 + (a - 1.0) * ka_ref[...])
    b = kk * a
    va_o[...] = v
    bv_o[...] = _head_sums(r * k2 * rk_ref[...]) * v
    sz_o[...] = _silu(z_a)

    lg = int(math.log2(L))
    row_m = lax.broadcasted_iota(jnp.int32, (M, M), 0)
    col_m = lax.broadcasted_iota(jnp.int32, (M, M), 1)
    tri = ((row_m >= col_m) & (_shr(row_m, lg) == _shr(col_m, lg))).astype(BF16)
    c = _cumsum_time(lw, tri)
    gl_a = proj(w_gla)
    gl_b = proj(w_glb)
    c_last = _last_rows(c, n_chunks)
    plast_o[...] = jnp.exp(c_last)
    e_nc = jnp.exp(-c)
    e_cl = jnp.exp(_rows_of_seq(c_last, n_chunks, L) - c)
    kq_o[...] = (kk * jnp.exp(c - lw)).astype(kq_o.dtype)
    rq_o[...] = (r * jnp.exp(c)).astype(rq_o.dtype)
    bk_o[...] = (b * e_nc).astype(bk_o.dtype)
    kkk_o[...] = (k2 * e_nc).astype(kkk_o.dtype)
    bk2_o[...] = (b * e_cl).astype(bk2_o.dtype)
    kk2_o[...] = (k2 * e_cl).astype(kk2_o.dtype)

    xbuf[:, pad:pad + tt, :] = qk_pre.reshape(nb, tt, cc)
    conv = cb_ref[...]
    for tap in range(conv_w):
        conv = conv + xbuf[:, pl.ds(pad - (conv_w - 1) + tap, tt), :].reshape(M, cc) * cw_ref[tap:tap + 1, :]
    tail = xbuf[:, tt:tt + pad, :]
    xbuf[:, 0:pad, :] = tail
    cst_o[...] = tail
    qk = _silu(conv)
    q_o[...] = qk[:, 0:db].astype(q_o.dtype)
    k_o[...] = (qk[:, db:] * (1.0 / math.sqrt(HEAD_DIM))).astype(k_o.dtype)
    vb_o[...] = v_b
    gt_o[...] = g_t
    gob_o[...] = _sigmoid(o_b) * _silu(z_b)

    sa_o[...] = _sigmoid(gl_a).astype(sa_o.dtype)
    sb_o[...] = _sigmoid(gl_b).astype(sb_o.dtype)


def _inproj(x, scale, shift, g_norm, shift0, conv0p, lp, nb, tt, chunk, act_dtype):
    b, t, d = x.shape
    n_tok = b * t
    m = nb * tt
    nj = t // tt
    sw = shift0.shape[-1]
    cc = conv0p.shape[-1]
    da, db = lp["w_up_a"].shape[0], lp["w_up_b"].shape[0]
    cw = lp["conv_w"].shape[0]
    row = lambda a: a.reshape(1, -1)
    rows = [row(lp[n]) for n in ("mu_shift", "w0", "a0", "k_k", "k_a", "r_k")]
    smalls = rows + [lp["w_decay2"], lp["w_iclr2"], lp["conv_w"], row(lp["conv_b"])]
    const = lambda i, j: (0, 0)
    tok = lambda i, j: (i * nj + j, 0)
    per_b = lambda i, j: (i, 0, 0)
    in_specs = [pl.BlockSpec((nb, tt, d), lambda i, j: (i, j, 0)),
                pl.BlockSpec((nb, 1, d), per_b),
                pl.BlockSpec((nb, 1, d), per_b),
                pl.BlockSpec((1, d), const),
                pl.BlockSpec((nb, 1, sw), per_b),
                pl.BlockSpec((nb, SUBLANES, cc), per_b)]
    in_specs += [pl.BlockSpec(w.shape, const, pipeline_mode=pl.Buffered(1)) for w in lp["w_segs"]]
    in_specs += [pl.BlockSpec(a.shape, const) for a in smalls]
    tok_out = lambda n, dt: (pl.BlockSpec((m, n), tok), jax.ShapeDtypeStruct((n_tok, n), dt))
    outs = [tok_out(da, act_dtype)] * 6 + [tok_out(da, F32)] * 3
    outs += [(pl.BlockSpec((m // chunk, 1, da), lambda i, j: (i * nj + j, 0, 0)),
              jax.ShapeDtypeStruct((n_tok // chunk, 1, da), F32)),
             (pl.BlockSpec((nb, 1, sw), per_b), jax.ShapeDtypeStruct((b, 1, sw), F32))]
    outs += [tok_out(db, act_dtype)] * 2 + [tok_out(db, F32), tok_out(LANES, F32), tok_out(db, F32)]
    outs += [tok_out(d, BF16)] * 2
    outs += [(pl.BlockSpec((nb, SUBLANES, cc), per_b), jax.ShapeDtypeStruct((b, SUBLANES, cc), F32))]
    return pl.pallas_call(
        functools.partial(_inproj_kernel, chunk=chunk, lora_w=lp["w_decay2"].shape[0], conv_w=cw),
        grid=(b // nb, nj),
        in_specs=in_specs,
        out_specs=[o[0] for o in outs], out_shape=[o[1] for o in outs],
        scratch_shapes=[pltpu.VMEM((nb, 1, sw), F32),
                        pltpu.VMEM((nb, tt + SUBLANES, cc), F32)],
        compiler_params=pltpu.CompilerParams(vmem_limit_bytes=VMEM_LIMIT),
        name="inproj",
    )(x, scale, shift, g_norm.reshape(1, d), shift0, conv0p, *lp["w_segs"], *smalls)


def _pair_blockdiag(y, left):
    return jnp.concatenate([jnp.where(left, y, 0.0), jnp.where(left, 0.0, y)], axis=0)


def _rwkv_body(kq_ref, rq_ref, bk_ref, kkk_ref, bk2_ref, kk2_ref, v_ref, bv_ref, sz_ref, plast_ref,
               s0_ref, lnw_ref, lnb_ref, ones_ref, oa_ref, st_ref, s_scr, *, n_seq, chunk, n_heads):
    j = pl.program_id(1)
    M = kq_ref.shape[0]
    L = chunk
    t_step = M // n_seq
    RU = min(M, ROWS)
    PW = 2 * HEAD_DIM
    n_units = M // RU
    n_hp = n_heads // 2
    units = range(n_units)
    hps = range(n_hp)
    pairs = [(u, g) for u in units for g in hps]
    zero_blk = jnp.zeros((n_seq, HEAD_DIM, HEAD_DIM), F32)

    @pl.when(j == 0)
    def _():
        for g in hps:
            top = jnp.concatenate([s0_ref[:, 2 * g], zero_blk], axis=-1)
            bot = jnp.concatenate([zero_blk, s0_ref[:, 2 * g + 1]], axis=-1)
            s_scr[:, g] = jnp.concatenate([top, bot], axis=-2)

    lg = int(math.log2(L))
    row = lax.broadcasted_iota(jnp.int32, (RU, 2 * RU), 0)
    lane = lax.broadcasted_iota(jnp.int32, (RU, 2 * RU), 1)
    col = lane & (RU - 1)
    same = _shr(row, lg) == _shr(col, lg)
    incl = (row >= col) & same
    strict = (row > col) & same
    left = lax.broadcasted_iota(jnp.int32, (RU, PW), 1) < HEAD_DIM
    bd_mask = (lax.broadcasted_iota(jnp.int32, (PW, PW), 0) < HEAD_DIM) == (
        lax.broadcasted_iota(jnp.int32, (PW, PW), 1) < HEAD_DIM)
    mmp = lambda x, y: _mm(x, _pair_blockdiag(y, left))

    p_last = plast_ref[...]
    pls = [slice(g * PW, (g + 1) * PW) for g in hps]
    rus = [slice(u * RU, (u + 1) * RU) for u in units]
    blk = lambda ref, q: ref[rus[q[0]], pls[q[1]]]
    kq = {q: blk(kq_ref, q) for q in pairs}
    rq = {q: blk(rq_ref, q) for q in pairs}
    bk2 = {q: blk(bk2_ref, q) for q in pairs}
    kk2 = {q: blk(kk2_ref, q) for q in pairs}
    vs = {q: blk(v_ref, q) for q in pairs}
    qr = {q: jnp.concatenate([kq[q], rq[q]], axis=0).astype(BF16) for q in pairs}
    bkk = {q: jnp.concatenate([_pair_blockdiag(blk(bk_ref, q).astype(F32), left),
                               _pair_blockdiag(blk(kkk_ref, q).astype(F32), left)], axis=0).astype(BF16)
           for q in pairs}
    yield

    gs = {q: _mm_nt(qr[q], bkk[q]) for q in pairs}
    yield
    a_ab = {q: jnp.where(strict, gs[q][:RU, :2 * RU], 0.0) for q in pairs}
    a_ak = {q: jnp.where(strict, gs[q][:RU, 2 * RU:], 0.0) for q in pairs}
    m_rb = {q: jnp.where(incl, gs[q][RU:, :2 * RU], 0.0) for q in pairs}
    m_rk = {q: jnp.where(incl, gs[q][RU:, 2 * RU:], 0.0) for q in pairs}
    yield
    t_inv = dict(zip(pairs, (yield from _tri_inverse_all([a_ab[q] for q in pairs], row, col, L, mmp))))
    akv = {q: mmp(a_ak[q], vs[q]) for q in pairs}
    yield

    state = {}
    ys = {}
    for u in units:
        if L == RU:
            b = (u * RU) // t_step
            s0 = [state[b, g] if (b, g) in state else s_scr[b, g] for g in hps]
            ws = [_mm_nt(qr[u, g], s0[g]) for g in hps]
            w1 = [x[:RU] for x in ws]
            wr = [x[RU:] for x in ws]
        else:
            n_in = RU // L
            s0s = [[s_scr[u * n_in + i, g] for g in hps] for i in range(n_in)]
            w1, wr = [], []
            for g in hps:
                parts = [_mm_nt(jnp.concatenate([kq[u, g][i * L:(i + 1) * L], rq[u, g][i * L:(i + 1) * L]], axis=0),
                                s0s[i][g]) for i in range(n_in)]
                w1.append(jnp.concatenate([x[:L] for x in parts], axis=0))
                wr.append(jnp.concatenate([x[L:] for x in parts], axis=0))
        yield
        us = [-mmp(t_inv[u, g], w1[g] + akv[u, g]) for g in hps]
        yield
        for g in hps:
            rhs = jnp.concatenate([_pair_blockdiag(us[g], left), _pair_blockdiag(vs[u, g].astype(F32), left)], axis=0)
            ys[u, g] = wr[g] + _mm(jnp.concatenate([m_rb[u, g], m_rk[u, g]], axis=1), rhs)
        yield
        if L == RU:
            for g in hps:
                uv = jnp.concatenate([us[g], vs[u, g].astype(F32)], axis=0)
                bkk2 = jnp.concatenate([bk2[u, g], kk2[u, g]], axis=0)
                state[b, g] = s0[g] * p_last[u][:, pls[g]] + jnp.where(bd_mask, _mm_tn(uv, bkk2), 0.0)
        else:
            for g in hps:
                for i in range(n_in):
                    rows = slice(i * L, (i + 1) * L)
                    uv_i = jnp.concatenate([us[g][rows], vs[u, g][rows]], axis=0)
                    bkk2_i = jnp.concatenate([bk2[u, g][rows], kk2[u, g][rows]], axis=0)
                    state[u * n_in + i, g] = (s0s[i][g] * p_last[u * n_in + i][:, pls[g]]
                                              + jnp.where(bd_mask, _mm_tn(uv_i, bkk2_i), 0.0))
        yield
    for (b, g), s_new in state.items():
        s_scr[b, g] = s_new

    rows_out = [jnp.concatenate([ys[u, g] for g in hps], axis=-1) for u in units]
    y = rows_out[0] if n_units == 1 else jnp.concatenate(rows_out, axis=0)
    yc = y - _head_sums_mxu(y, ones_ref) * (1.0 / HEAD_DIM)
    yield
    var = _head_sums_mxu(yc * yc, ones_ref) * (1.0 / HEAD_DIM)
    yn = yc * lax.rsqrt(var + RWKV_GN_EPS)
    oa_ref[...] = ((yn * lnw_ref[...] + lnb_ref[...] + bv_ref[...]) * sz_ref[...]).astype(oa_ref.dtype)

    @pl.when(j == pl.num_programs(1) - 1)
    def _():
        for g in hps:
            st_ref[:, 2 * g] = s_scr[:, g, 0:HEAD_DIM, 0:HEAD_DIM]
            st_ref[:, 2 * g + 1] = s_scr[:, g, HEAD_DIM:, HEAD_DIM:]


def _mlstm_body(q_ref, k_ref, v_ref, g_ref, gob_ref, c0_ref, n0_ref, m0_ref, gb_ref, gnw_ref, ones_ref,
                ob_ref, ct_ref, nt_ref, mt_ref, cn_scr, m_scr, *, n_seq, n_heads):
    j = pl.program_id(1)
    M = q_ref.shape[0]
    L = M // n_seq
    heads = range(n_heads)
    seqs = range(n_seq)

    @pl.when(j == 0)
    def _():
        cn_scr[:, :, 0:HEAD_DIM, :] = c0_ref[...]
        cn_scr[:, :, HEAD_DIM:, :] = n0_ref[...]
        m_scr[...] = m0_ref[...]

    row = lax.broadcasted_iota(jnp.int32, (M, M), 0)
    col = lax.broadcasted_iota(jnp.int32, (M, M), 1)
    same_seq = _shr(row, int(math.log2(L))) == _shr(col, int(math.log2(L)))
    incl = (row >= col) & same_seq

    g = g_ref[...] + gb_ref[...]
    bcum = pltpu.roll(_cumsum_time(-_softplus(-g), incl.astype(BF16)), LANES - n_heads, axis=1)
    m_prev = m_scr[...]
    m_prev_rows = _rows_of_seq(m_prev, n_seq, L)
    x_all = g - bcum
    pos = lax.broadcasted_iota(jnp.int32, (M, LANES), 0) & (L - 1)
    m_all = bcum + jnp.maximum(_cummax_time(x_all, L, pos), m_prev_rows)
    bm_all = bcum - m_all
    w_in_all = jnp.exp(bcum + m_prev_rows - m_all)
    e_negm_all = jnp.exp(-m_all)
    m_new = _last_rows(m_all, n_seq)
    b_last = _last_rows(bcum, n_seq)
    ws_all = jnp.exp(_rows_of_seq(b_last - m_new, n_seq, L) + x_all)
    dec_all = jnp.exp(b_last + m_prev - m_new)
    m_scr[...] = m_new
    x_t = jnp.transpose(x_all)
    yield

    ones = jnp.ones((M, SUBLANES), F32)
    sls = [slice(h * HEAD_DIM, (h + 1) * HEAD_DIM) for h in heads]
    qs = [q_ref[:, sl] for sl in sls]
    ks = [k_ref[:, sl] for sl in sls]
    v1 = [jnp.concatenate([v_ref[:, sl], ones], axis=-1) for sl in sls]
    cn = [[cn_scr[b, h] for h in heads] for b in seqs]

    qk_t = [_mm_nt(qs[h], ks[h]) for h in heads]
    if n_seq == 1:
        qc = [_mm_nt(qs[h], cn[0][h]) for h in heads]
    else:
        qc = [jnp.concatenate([_mm_nt(qs[h][b * L:(b + 1) * L], cn[b][h]) for b in seqs], axis=0) for h in heads]
    yield
    w_ts = [jnp.where(incl, jnp.exp(bm_all[:, h:h + 1] + x_t[h:h + 1, :]), 0.0) for h in heads]
    yield
    s = [qk_t[h] * w_ts[h] for h in heads]
    numden = [_mm(s[h], v1[h]) + w_in_all[:, h:h + 1] * qc[h] for h in heads]
    yield
    den = [jnp.maximum(jnp.abs(numden[h][:, HEAD_DIM:HEAD_DIM + 1]), e_negm_all[:, h:h + 1]) for h in heads]
    hh = [numden[h][:, 0:HEAD_DIM] / den[h] for h in heads]
    hh_all = jnp.concatenate(hh, axis=-1)
    ssq = _head_sums_mxu(hh_all * hh_all, ones_ref) * (1.0 / HEAD_DIM)

    yield
    v1w = [v1[h] * ws_all[:, h:h + 1] for h in heads]
    for h in heads:
        for b in seqs:
            rows = slice(b * L, (b + 1) * L)
            cn_scr[b, h] = dec_all[b][:, h:h + 1] * cn[b][h] + _mm_tn(v1w[h][rows], ks[h][rows])
    yield

    hb = hh_all * lax.rsqrt(ssq + MLSTM_GN_EPS)
    ob_ref[...] = (hb * gnw_ref[...] * gob_ref[...]).astype(ob_ref.dtype)

    @pl.when(j == pl.num_programs(1) - 1)
    def _():
        ct_ref[...] = cn_scr[:, :, 0:HEAD_DIM, :]
        nt_ref[...] = cn_scr[:, :, HEAD_DIM:, :]
        mt_ref[...] = m_scr[...]


N_RWKV_IN, N_RWKV_OUT, N_MLSTM_IN, N_MLSTM_OUT = 14, 2, 11, 4
RWKV_STAGES, MLSTM_STAGES = 24, 8
MLSTM_START = 0.0


def _mixers_kernel(*refs, n_seq, chunk, n_heads_a, n_heads_b):
    it = iter(refs)
    take = lambda n: [next(it) for _ in range(n)]
    r_in, m_in = take(N_RWKV_IN), take(N_MLSTM_IN)
    r_out, m_out = take(N_RWKV_OUT), take(N_MLSTM_OUT)
    s_scr, cn_scr, m_scr = take(3)
    _interleave([
        (_rwkv_body(*r_in, *r_out, s_scr, n_seq=n_seq, chunk=chunk, n_heads=n_heads_a), RWKV_STAGES, 0.0),
        (_mlstm_body(*m_in, *m_out, cn_scr, m_scr, n_seq=n_seq, n_heads=n_heads_b), MLSTM_STAGES, MLSTM_START)])


def _mixers(acts, v_a, bv, sz, p_last, s0, ln_w, ln_b, q, k, v_b, gates, gob, c0, n0, m0, gate_b, gn_w,
            n_seq, t_step, chunk):
    b = s0.shape[0]
    n_ha, n_hb = s0.shape[1], c0.shape[1]
    n_tok, da = v_a.shape
    db = v_b.shape[1]
    m = n_seq * t_step
    nj = n_tok // (b * t_step)
    tok = lambda i, j: (i * nj + j, 0)
    per_b3 = lambda i, j: (i, 0, 0)
    per_b4 = lambda i, j: (i, 0, 0, 0)
    const2 = lambda i, j: (0, 0)
    s_spec = pl.BlockSpec((n_seq, n_ha, HEAD_DIM, HEAD_DIM), per_b4)
    c_spec = pl.BlockSpec((n_seq, n_hb, HEAD_DIM, HEAD_DIM), per_b4)
    n_spec = pl.BlockSpec((n_seq, n_hb, SUBLANES, HEAD_DIM), per_b4)
    m_spec = pl.BlockSpec((n_seq, 1, LANES), per_b3)
    row_a, row_b = pl.BlockSpec((m, da), tok), pl.BlockSpec((m, db), tok)
    in_specs = [row_a] * 9
    ones_a = jnp.kron(jnp.eye(n_ha, dtype=F32), jnp.ones((HEAD_DIM, HEAD_DIM), F32)).astype(BF16)
    ones_b = jnp.kron(jnp.eye(n_hb, dtype=F32), jnp.ones((HEAD_DIM, HEAD_DIM), F32)).astype(BF16)
    in_specs += [pl.BlockSpec((m // chunk, 1, da), lambda i, j: (i * nj + j, 0, 0)), s_spec,
                 pl.BlockSpec(ln_w.shape, const2), pl.BlockSpec(ln_b.shape, const2),
                 pl.BlockSpec(ones_a.shape, const2)]
    in_specs += [row_b, row_b, row_b, pl.BlockSpec((m, LANES), tok), row_b, c_spec, n_spec, m_spec,
                 pl.BlockSpec(gate_b.shape, const2), pl.BlockSpec(gn_w.shape, const2),
                 pl.BlockSpec(ones_b.shape, const2)]
    return pl.pallas_call(
        functools.partial(_mixers_kernel, n_seq=n_seq, chunk=chunk, n_heads_a=n_ha, n_heads_b=n_hb),
        grid=(b // n_seq, nj),
        in_specs=in_specs,
        out_specs=[row_a, s_spec, row_b, c_spec, n_spec, m_spec],
        out_shape=[jax.ShapeDtypeStruct((n_tok, da), BF16), jax.ShapeDtypeStruct(s0.shape, F32),
                   jax.ShapeDtypeStruct((n_tok, db), BF16), jax.ShapeDtypeStruct(c0.shape, F32),
                   jax.ShapeDtypeStruct(n0.shape, F32), jax.ShapeDtypeStruct(m0.shape, F32)],
        scratch_shapes=[pltpu.VMEM((n_seq, n_ha // 2, 2 * HEAD_DIM, 2 * HEAD_DIM), F32),
                        pltpu.VMEM((n_seq, n_hb, HEAD_DIM + SUBLANES, HEAD_DIM), F32),
                        pltpu.VMEM((n_seq, 1, LANES), F32)],
        compiler_params=pltpu.CompilerParams(vmem_limit_bytes=VMEM_LIMIT),
        name="mixers",
    )(*acts, v_a, bv, sz, p_last, s0, ln_w, ln_b, ones_a, q, k, v_b, gates, gob, c0, n0, m0, gate_b, gn_w, ones_b)


def _out_kernel(oa_ref, ob_ref, sa_ref, sb_ref, x_ref, gate_ref, wua_ref, wub_ref, wo_ref, gf_ref,
                y_ref, *, nb, final_norm):
    m = x_ref.shape[0]
    ua = jnp.dot(oa_ref[...], wua_ref[...], preferred_element_type=F32)
    ub = jnp.dot(ob_ref[...], wub_ref[...], preferred_element_type=F32)
    merged = sa_ref[...] * ua + sb_ref[...] * ub
    mo = jnp.dot(merged.astype(BF16), wo_ref[...], preferred_element_type=F32)
    xn = x_ref[...] + _rows_of_seq(gate_ref[...], nb, m // nb) * mo
    if final_norm:
        xn = xn * lax.rsqrt(jnp.mean(xn * xn, axis=-1, keepdims=True) + NORM_EPS) * gf_ref[...]
    y_ref[...] = xn


def _out(out_a, out_b, sa, sb, x, gate, w_up_a, w_up_b, w_out, g_final, b, nb, tt, final_norm):
    n_tok, d = x.shape
    nj = n_tok // (b * tt)
    m = nb * tt
    tok = lambda i, j: (i * nj + j, 0)
    const2 = lambda i, j: (0, 0)
    wspec = lambda w: pl.BlockSpec(w.shape, const2, pipeline_mode=pl.Buffered(1))
    in_specs = [pl.BlockSpec((m, out_a.shape[-1]), tok),
                pl.BlockSpec((m, out_b.shape[-1]), tok),
                pl.BlockSpec((m, d), tok),
                pl.BlockSpec((m, d), tok),
                pl.BlockSpec((m, d), tok),
                pl.BlockSpec((nb, 1, d), lambda i, j: (i, 0, 0)),
                wspec(w_up_a), wspec(w_up_b), wspec(w_out),
                pl.BlockSpec((1, d), const2)]
    return pl.pallas_call(
        functools.partial(_out_kernel, nb=nb, final_norm=final_norm),
        grid=(b // nb, nj),
        in_specs=in_specs,
        out_specs=pl.BlockSpec((m, d), tok),
        out_shape=jax.ShapeDtypeStruct((n_tok, d), F32),
        compiler_params=pltpu.CompilerParams(vmem_limit_bytes=VMEM_LIMIT),
        name="merge_out",
    )(out_a, out_b, sa, sb, x, gate, w_up_a, w_up_b, w_out, g_final.reshape(1, d))


def _token_tiling(b, t, target):
    if t >= target:
        return 1, math.gcd(t, target)
    return math.gcd(b, target // t), t


def _layer(x, mod, states, lp, g_final, final_norm):
    b, t, d = x.shape
    shift0, s0, conv0, c0, n0, m0 = states
    n_hb = c0.shape[1]
    cw = lp["conv_w"].shape[0]
    row = lambda a: a.reshape(1, -1)

    ada_shift, ada_scale, ada_gate = (mod[:, None, i * d:(i + 1) * d] for i in range(3))
    nb, tt = _token_tiling(b, t, PROJ_ROWS)
    L = math.gcd(t, ROWS)
    act_dtype = BF16 if L % 16 == 0 else F32
    conv0p = jnp.pad(conv0, ((0, 0), (SUBLANES - (cw - 1), 0), (0, 0)))
    (kq, rq, bk, kkk, bk2, kk2, v_a, bv, sz, p_last, shift_t,
     q, k, v_b, gates, gob, sa, sb, conv_tail) = _inproj(
        x, ada_scale, ada_shift, lp["g_norm"], shift0[:, None, :], conv0p, lp, nb, tt, L, act_dtype)

    n_seq, t_step = _token_tiling(b, t, MIX_ROWS)
    n0p = jnp.broadcast_to(n0[:, :, None, :], (b, n_hb, SUBLANES, HEAD_DIM))
    m0p = jnp.pad(m0, ((0, 0), (0, LANES - n_hb)))[:, None, :]
    gate_b = jnp.zeros((1, LANES), F32).at[0, 0:n_hb].set(lp["b_i"]).at[0, n_hb:2 * n_hb].set(lp["b_f"])
    out_a, s_t, out_b, c_t, n_t, m_t = _mixers(
        (kq, rq, bk, kkk, bk2, kk2), v_a, bv, sz, p_last, s0, row(lp["ln_w"]), row(lp["ln_b"]),
        q, k, v_b, gates, gob, c0, n0p, m0p, gate_b, row(lp["gn_w"]), n_seq, t_step, L)

    y = _out(out_a, out_b, sa, sb, x.reshape(b * t, d), ada_gate, lp["w_up_a"], lp["w_up_b"], lp["w_out"],
             g_final, b, nb, tt, final_norm).reshape(b, t, d)
    new_states = (shift_t[:, 0], s_t, conv_tail[:, SUBLANES - (cw - 1):], c_t, n_t[:, :, 0, :], m_t[:, 0, :n_hb])
    return y, new_states


def _trunk(x, mods, states, layers, g_final):
    depth = len(layers)
    new = [[] for _ in states]
    for l in range(depth):
        st = tuple(s[l] for s in states)
        x, st_new = _layer(x, mods[l], st, layers[l], g_final, final_norm=(l == depth - 1))
        for lst, s in zip(new, st_new):
            lst.append(s.astype(x.dtype))
    return x, tuple(jnp.stack(lst) for lst in new)


def kernel(x_prompt, x_sample, c_prompt, c_sample, state_rwkv_shift, state_rwkv_S, state_mlstm_conv, state_mlstm_C, state_mlstm_n, state_mlstm_m, g_norm, w_ada, b_ada, w_in, mu_shift, w_decay2, w0, w_iclr2, a0, k_k, k_a, r_k, ln_w, ln_b, conv_w, conv_b, b_i, b_f, gn_w, w_up_a, w_up_b, w_out, g_final):
    depth = g_norm.shape[0]
    bp, bs = x_prompt.shape[0], x_sample.shape[0]
    d = x_prompt.shape[-1]
    da, db = w_up_a.shape[1], w_up_b.shape[1]
    n_ha, n_hb = da // HEAD_DIM, db // HEAD_DIM
    sw = mu_shift.shape[-1]
    cc = conv_w.shape[-1]
    cw = conv_w.shape[1]
    dt = x_prompt.dtype

    sizes = (sw, da, cc, db, db, n_hb, n_hb, db, d, d)
    offs = [0]
    for s in sizes:
        offs.append(offs[-1] + s)
    seg = lambda w, i: w[:, offs[i]:offs[i + 1]]

    c_all = jnp.concatenate([c_prompt, c_sample], axis=0)
    n_c = c_all.shape[0]
    c_pad = jnp.pad(c_all, ((0, (-n_c) % SUBLANES), (0, 0)))

    layers, mods_p, mods_s = [], [], []
    for l in range(depth):
        w = w_in[l]
        wg = jnp.concatenate([seg(w, 5), seg(w, 6)], axis=1)
        wg = jnp.pad(wg, ((0, 0), (0, LANES - wg.shape[1])))
        w_segs = tuple(x.astype(BF16) for x in
                       (seg(w, 0), seg(w, 1), seg(w, 2), seg(w, 3), seg(w, 4), seg(w, 7), seg(w, 8), seg(w, 9), wg))
        layers.append(dict(
            g_norm=g_norm[l], w_segs=w_segs, mu_shift=mu_shift[l], w_decay2=w_decay2[l].astype(BF16),
            w0=w0[l], w_iclr2=w_iclr2[l].astype(BF16), a0=a0[l], k_k=k_k[l], k_a=k_a[l], r_k=r_k[l],
            ln_w=ln_w[l], ln_b=ln_b[l], conv_w=conv_w[l], conv_b=conv_b[l], b_i=b_i[l], b_f=b_f[l],
            gn_w=gn_w[l], w_up_a=w_up_a[l].astype(BF16), w_up_b=w_up_b[l].astype(BF16),
            w_out=w_out[l].astype(BF16)))
        mod = _ada(c_pad, w_ada[l], b_ada[l])
        mods_p.append(mod[:bp])
        mods_s.append(mod[bp:bp + bs])

    prompt_states = (
        jnp.zeros((depth, bp, sw), dt),
        jnp.zeros((depth, bp, n_ha, HEAD_DIM, HEAD_DIM), dt),
        jnp.zeros((depth, bp, cw - 1, cc), dt),
        jnp.zeros((depth, bp, n_hb, HEAD_DIM, HEAD_DIM), dt),
        jnp.zeros((depth, bp, n_hb, HEAD_DIM), dt),
        jnp.zeros((depth, bp, n_hb), dt),
    )
    sample_states = (state_rwkv_shift, state_rwkv_S, state_mlstm_conv,
                     state_mlstm_C, state_mlstm_n, state_mlstm_m)
    y_p, st_p = _trunk(x_prompt, mods_p, prompt_states, layers, g_final)
    y_s, st_s = _trunk(x_sample, mods_s, sample_states, layers, g_final)
    return (y_p, y_s) + st_p + st_s
```

```python
import functools
import math

import jax
import jax.numpy as jnp
from jax import lax
from jax.experimental import pallas as pl
from jax.experimental.pallas import tpu as pltpu

F32 = jnp.float32
BF16 = jnp.bfloat16

HEAD_DIM = 64
NORM_EPS = 1e-6
RWKV_GN_EPS = 64e-5
MLSTM_GN_EPS = 1e-6
ROWS = 64
MLSTM_ROWS = 128
MIX_ROWS = 256
PROJ_ROWS = 256
SUBLANES = 8
LANES = 128
TRI_BASE = 16
VMEM_LIMIT = 60 * 1024 * 1024

NT_DIMS = (((1,), (1,)), ((), ()))
TN_DIMS = (((0,), (0,)), ((), ()))


def _mm(a, b):
    return jnp.dot(a.astype(BF16), b.astype(BF16), preferred_element_type=F32)


def _mm_nt(a, b):
    return lax.dot_general(a.astype(BF16), b.astype(BF16), NT_DIMS, preferred_element_type=F32)


def _mm_tn(a, b):
    return lax.dot_general(a.astype(BF16), b.astype(BF16), TN_DIMS, preferred_element_type=F32)


def _sigmoid(x):
    return 1.0 / (1.0 + jnp.exp(-x))


def _silu(x):
    return x * _sigmoid(x)


def _softplus(x):
    return jnp.maximum(x, 0.0) + jnp.log(1.0 + jnp.exp(-jnp.abs(x)))


def _cumsum_time(x, tri_bf16):
    hi = x.astype(BF16)
    r1 = x - hi.astype(F32)
    mid = r1.astype(BF16)
    lo = (r1 - mid.astype(F32)).astype(BF16)
    dot = functools.partial(jnp.dot, preferred_element_type=F32)
    return dot(tri_bf16, hi) + dot(tri_bf16, mid) + dot(tri_bf16, lo)


def _shr(x, n):
    return lax.shift_right_logical(x, jnp.full(x.shape, n, x.dtype))


def _rows_of_seq(x, n_seq, L):
    if n_seq == 1:
        return x[0]
    return jnp.broadcast_to(x, (n_seq, L, x.shape[-1])).reshape(n_seq * L, x.shape[-1])


def _last_rows(x, n_seq):
    m, n = x.shape
    L = m // n_seq
    return x.reshape(n_seq, L, n)[:, L - 1:L, :]


def _head_sums(x):
    m, n = x.shape
    parts = [jnp.broadcast_to(jnp.sum(x[:, o:o + HEAD_DIM], axis=-1, keepdims=True), (m, HEAD_DIM))
             for o in range(0, n, HEAD_DIM)]
    return jnp.concatenate(parts, axis=-1)


def _head_sums_mxu(x, ones_ref):
    hi = x.astype(BF16)
    lo = (x - hi.astype(F32)).astype(BF16)
    s = jnp.dot(jnp.concatenate([hi, lo], axis=0), ones_ref[...], preferred_element_type=F32)
    return s[:x.shape[0]] + s[x.shape[0]:]


def _cummax_time(x, L, pos):
    y = x
    sh = 1
    while sh < L:
        y = jnp.where(pos >= sh, jnp.maximum(y, pltpu.roll(y, sh, axis=0)), y)
        sh *= 2
    return y


def _tri_inverse_all(mats, row, col, L, mm=_mm):
    bs = min(TRI_BASE, L)
    sh = int(math.log2(bs))
    same = _shr(row, sh) == _shr(col, sh)
    eye = (row == col).astype(F32)
    rows = mats[0].shape[0]
    ns = [jnp.where(same, -a, 0.0) for a in mats]
    ts = [eye + n for n in ns]
    if bs > 2:
        ps = [mm(n, n) for n in ns]
        yield
        k = 2
        while 2 * k < bs:
            both = [mm(jnp.concatenate([t, p], axis=0), p) for t, p in zip(ts, ps)]
            ts = [t + x[:rows] for t, x in zip(ts, both)]
            ps = [x[rows:] for x in both]
            yield
            k *= 2
        ts = [t + mm(t, p) for t, p in zip(ts, ps)]
        yield
    size = bs
    while size < L:
        sh = int(math.log2(size))
        lower_left = (_shr(row, sh + 1) == _shr(col, sh + 1)) & (_shr(row, sh) != _shr(col, sh))
        offs = [jnp.where(lower_left, a, 0.0) for a in mats]
        tmp = [mm(t, o) for t, o in zip(ts, offs)]
        yield
        ts = [t - mm(x, t) for t, x in zip(ts, tmp)]
        yield
        size *= 2
    return ts


def _interleave(plan):
    live = [[g, n, s, 0] for g, n, s in plan]
    while live:
        item = min(live, key=lambda it: it[2] + (1.0 - it[2]) * it[3] / it[1])
        try:
            next(item[0])
            item[3] += 1
            yield
        except StopIteration:
            live.remove(item)


def _chain(*gens):
    for g in gens:
        yield from g


def _run(gen):
    for _ in gen:
        pass


def _when(pred, fn):
    if pred is None:
        return
    if pred is True:
        fn()
    else:
        pl.when(pred)(fn)


def _ada_kernel(c_ref, w_ref, b_ref, o_ref):
    o_ref[...] = _mm(_silu(c_ref[...]), w_ref[...]) + b_ref[...]


def _ada(c, w_ada, b_ada):
    n, d = c.shape
    n3 = w_ada.shape[1]
    tn = d
    return pl.pallas_call(
        _ada_kernel,
        grid=(n3 // tn,),
        in_specs=[pl.BlockSpec((n, d), lambda j: (0, 0)),
                  pl.BlockSpec((d, tn), lambda j: (0, j)),
                  pl.BlockSpec((1, tn), lambda j: (0, j))],
        out_specs=pl.BlockSpec((n, tn), lambda j: (0, j)),
        out_shape=jax.ShapeDtypeStruct((n, n3), F32),
        name="ada",
    )(c, w_ada, b_ada.reshape(1, n3))


N_INPROJ_IN, N_INPROJ_OUT = 25, 19
INPROJ_STAGES = 31
PROJ_COLS = 256


def _inproj_body(x_ref, sc_ref, sh_ref, g_ref, shift0_ref, conv0_ref,
                 w_shift, w_za, w_qk, w_vb, w_ob, w_zb, w_gla, w_glb, w_gt,
                 mu_ref, w0_ref, a0_ref, kk_ref, ka_ref, rk_ref, wd2_ref, wi2_ref, cw_ref, cb_ref,
                 kq_o, rq_o, bk_o, kkk_o, bk2_o, kk2_o, va_o, bv_o, sz_o, plast_o, shst_o,
                 q_o, k_o, vb_o, gt_o, gob_o, sa_o, sb_o, cst_o,
                 prev_scr, xbuf, *, chunk, lora_w, conv_w, first, live):
    nb, tt, d = x_ref.shape
    M = nb * tt
    L = chunk
    n_chunks = M // L
    da = va_o.shape[-1]
    db = vb_o.shape[-1]
    cc = xbuf.shape[-1]
    pad = SUBLANES

    def _init():
        prev_scr[...] = shift0_ref[...]
        xbuf[:, 0:pad, :] = conv0_ref[...]
    _when(first, _init)

    x = x_ref[...]
    y = x * lax.rsqrt(jnp.mean(x * x, axis=-1, keepdims=True) + NORM_EPS) * g_ref[...]
    h = (y * (1.0 + sc_ref[...]) + sh_ref[...]).reshape(M, d).astype(BF16)
    def proj(w_ref):
        n = w_ref.shape[1]
        parts = []
        for o in range(0, n, PROJ_COLS):
            parts.append(jnp.dot(h, w_ref[:, o:min(o + PROJ_COLS, n)], preferred_element_type=F32))
            yield
        return parts[0] if len(parts) == 1 else jnp.concatenate(parts, axis=-1)

    p = yield from proj(w_shift)
    row_w = lax.broadcasted_iota(jnp.int32, p.shape, 0)
    first = (row_w & (tt - 1)) == 0
    prev = jnp.where(first, _rows_of_seq(prev_scr[...], nb, tt), pltpu.roll(p, 1, axis=0))
    last = _last_rows(p, nb)
    prev_scr[...] = last

    def _shift_state():
        shst_o[...] = last
    _when(live, _shift_state)
    ps = p + mu_ref[...] * (prev - p)
    r = ps[:, 0:da]
    k = ps[:, da:2 * da]
    v = ps[:, 2 * da:3 * da]
    wl = ps[:, 3 * da:3 * da + lora_w]
    al = ps[:, 3 * da + lora_w:]

    qk_pre = yield from proj(w_qk)
    z_a = yield from proj(w_za)
    w = w0_ref[...] + _mm(jnp.tanh(wl), wd2_ref[...])
    lw = -math.exp(-0.5) * _sigmoid(w)
    a = _sigmoid(a0_ref[...] + _mm(al, wi2_ref[...]))
    yield
    v_b = yield from proj(w_vb)
    g_t = yield from proj(w_gt)
    o_b = yield from proj(w_ob)
    z_b = yield from proj(w_zb)
    kkr = k * kk_ref[...]
    kk = kkr * lax.rsqrt(jnp.maximum(_head_sums(kkr * kkr), 1e-24))
    k2 = k * (1.0 + (a - 1.0) * ka_ref[...])
    b = kk * a
    va_o[...] = v
    bv_o[...] = _head_sums(r * k2 * rk_ref[...]) * v
    sz_o[...] = _silu(z_a)

    lg = int(math.log2(L))
    row_m = lax.broadcasted_iota(jnp.int32, (M, M), 0)
    col_m = lax.broadcasted_iota(jnp.int32, (M, M), 1)
    tri = ((row_m >= col_m) & (_shr(row_m, lg) == _shr(col_m, lg))).astype(BF16)
    c = _cumsum_time(lw, tri)
    yield
    gl_a = yield from proj(w_gla)
    gl_b = yield from proj(w_glb)
    c_last = _last_rows(c, n_chunks)
    plast_o[...] = jnp.exp(c_last)
    e_nc = jnp.exp(-c)
    e_cl = jnp.exp(_rows_of_seq(c_last, n_chunks, L) - c)
    kq_o[...] = (kk * jnp.exp(c - lw)).astype(kq_o.dtype)
    rq_o[...] = (r * jnp.exp(c)).astype(rq_o.dtype)
    bk_o[...] = (b * e_nc).astype(bk_o.dtype)
    kkk_o[...] = (k2 * e_nc).astype(kkk_o.dtype)
    bk2_o[...] = (b * e_cl).astype(bk2_o.dtype)
    kk2_o[...] = (k2 * e_cl).astype(kk2_o.dtype)

    xbuf[:, pad:pad + tt, :] = qk_pre.reshape(nb, tt, cc)
    conv = cb_ref[...]
    for tap in range(conv_w):
        conv = conv + xbuf[:, pl.ds(pad - (conv_w - 1) + tap, tt), :].reshape(M, cc) * cw_ref[tap:tap + 1, :]
    tail = xbuf[:, tt:tt + pad, :]
    xbuf[:, 0:pad, :] = tail

    def _conv_state():
        cst_o[...] = tail
    _when(live, _conv_state)
    yield
    qk = _silu(conv)
    q_o[...] = qk[:, 0:db].astype(q_o.dtype)
    k_o[...] = (qk[:, db:] * (1.0 / math.sqrt(HEAD_DIM))).astype(k_o.dtype)
    vb_o[...] = v_b
    gt_o[...] = g_t
    gob_o[...] = _sigmoid(o_b) * _silu(z_b)

    sa_o[...] = _sigmoid(gl_a).astype(sa_o.dtype)
    sb_o[...] = _sigmoid(gl_b).astype(sb_o.dtype)


def _inproj_kernel(*refs, chunk, lora_w, conv_w):
    _run(_inproj_body(*refs, chunk=chunk, lora_w=lora_w, conv_w=conv_w,
                      first=pl.program_id(1) == 0, live=True))


def _inproj(x, scale, shift, g_norm, shift0, conv0p, lp, nb, tt, chunk, act_dtype):
    b, t, d = x.shape
    n_tok = b * t
    m = nb * tt
    nj = t // tt
    sw = shift0.shape[-1]
    cc = conv0p.shape[-1]
    da, db = lp["w_up_a"].shape[0], lp["w_up_b"].shape[0]
    cw = lp["conv_w"].shape[0]
    row = lambda a: a.reshape(1, -1)
    rows = [row(lp[n]) for n in ("mu_shift", "w0", "a0", "k_k", "k_a", "r_k")]
    smalls = rows + [lp["w_decay2"], lp["w_iclr2"], lp["conv_w"], row(lp["conv_b"])]
    const = lambda i, j: (0, 0)
    tok = lambda i, j: (i * nj + j, 0)
    per_b = lambda i, j: (i, 0, 0)
    in_specs = [pl.BlockSpec((nb, tt, d), lambda i, j: (i, j, 0)),
                pl.BlockSpec((nb, 1, d), per_b),
                pl.BlockSpec((nb, 1, d), per_b),
                pl.BlockSpec((1, d), const),
                pl.BlockSpec((nb, 1, sw), per_b),
                pl.BlockSpec((nb, SUBLANES, cc), per_b)]
    in_specs += [pl.BlockSpec(w.shape, const, pipeline_mode=pl.Buffered(1)) for w in lp["w_segs"]]
    in_specs += [pl.BlockSpec(a.shape, const) for a in smalls]
    tok_out = lambda n, dt: (pl.BlockSpec((m, n), tok), jax.ShapeDtypeStruct((n_tok, n), dt))
    outs = [tok_out(da, act_dtype)] * 6 + [tok_out(da, F32)] * 3
    outs += [(pl.BlockSpec((m // chunk, 1, da), lambda i, j: (i * nj + j, 0, 0)),
              jax.ShapeDtypeStruct((n_tok // chunk, 1, da), F32)),
             (pl.BlockSpec((nb, 1, sw), per_b), jax.ShapeDtypeStruct((b, 1, sw), F32))]
    outs += [tok_out(db, act_dtype)] * 2 + [tok_out(db, F32), tok_out(LANES, F32), tok_out(db, F32)]
    outs += [tok_out(d, BF16)] * 2
    outs += [(pl.BlockSpec((nb, SUBLANES, cc), per_b), jax.ShapeDtypeStruct((b, SUBLANES, cc), F32))]
    return pl.pallas_call(
        functools.partial(_inproj_kernel, chunk=chunk, lora_w=lp["w_decay2"].shape[0], conv_w=cw),
        grid=(b // nb, nj),
        in_specs=in_specs,
        out_specs=[o[0] for o in outs], out_shape=[o[1] for o in outs],
        scratch_shapes=[pltpu.VMEM((nb, 1, sw), F32),
                        pltpu.VMEM((nb, tt + SUBLANES, cc), F32)],
        compiler_params=pltpu.CompilerParams(vmem_limit_bytes=VMEM_LIMIT),
        name="inproj",
    )(x, scale, shift, g_norm.reshape(1, d), shift0, conv0p, *lp["w_segs"], *smalls)


def _pair_blockdiag(y, left):
    return jnp.concatenate([jnp.where(left, y, 0.0), jnp.where(left, 0.0, y)], axis=0)


def _rwkv_body(kq_ref, rq_ref, bk_ref, kkk_ref, bk2_ref, kk2_ref, v_ref, bv_ref, sz_ref, plast_ref,
               s0_ref, lnw_ref, lnb_ref, ones_ref, oa_ref, st_ref, s_scr, *, n_seq, chunk, n_heads,
               first, last):
    M = kq_ref.shape[0]
    L = chunk
    t_step = M // n_seq
    RU = min(M, ROWS)
    PW = 2 * HEAD_DIM
    n_units = M // RU
    n_hp = n_heads // 2
    units = range(n_units)
    hps = range(n_hp)
    pairs = [(u, g) for u in units for g in hps]
    zero_blk = jnp.zeros((n_seq, HEAD_DIM, HEAD_DIM), F32)

    def _load_state():
        for g in hps:
            top = jnp.concatenate([s0_ref[:, 2 * g], zero_blk], axis=-1)
            bot = jnp.concatenate([zero_blk, s0_ref[:, 2 * g + 1]], axis=-1)
            s_scr[:, g] = jnp.concatenate([top, bot], axis=-2)
    _when(first, _load_state)

    lg = int(math.log2(L))
    row = lax.broadcasted_iota(jnp.int32, (RU, 2 * RU), 0)
    lane = lax.broadcasted_iota(jnp.int32, (RU, 2 * RU), 1)
    col = lane & (RU - 1)
    same = _shr(row, lg) == _shr(col, lg)
    incl = (row >= col) & same
    strict = (row > col) & same
    left = lax.broadcasted_iota(jnp.int32, (RU, PW), 1) < HEAD_DIM
    bd_mask = (lax.broadcasted_iota(jnp.int32, (PW, PW), 0) < HEAD_DIM) == (
        lax.broadcasted_iota(jnp.int32, (PW, PW), 1) < HEAD_DIM)
    mmp = lambda x, y: _mm(x, _pair_blockdiag(y, left))

    p_last = plast_ref[...]
    pls = [slice(g * PW, (g + 1) * PW) for g in hps]
    rus = [slice(u * RU, (u + 1) * RU) for u in units]
    blk = lambda ref, q: ref[rus[q[0]], pls[q[1]]]
    kq = {q: blk(kq_ref, q) for q in pairs}
    rq = {q: blk(rq_ref, q) for q in pairs}
    bk2 = {q: blk(bk2_ref, q) for q in pairs}
    kk2 = {q: blk(kk2_ref, q) for q in pairs}
    vs = {q: blk(v_ref, q) for q in pairs}
    qr = {q: jnp.concatenate([kq[q], rq[q]], axis=0).astype(BF16) for q in pairs}
    bkk = {q: jnp.concatenate([_pair_blockdiag(blk(bk_ref, q).astype(F32), left),
                               _pair_blockdiag(blk(kkk_ref, q).astype(F32), left)], axis=0).astype(BF16)
           for q in pairs}
    yield

    gs = {q: _mm_nt(qr[q], bkk[q]) for q in pairs}
    yield
    a_ab = {q: jnp.where(strict, gs[q][:RU, :2 * RU], 0.0) for q in pairs}
    a_ak = {q: jnp.where(strict, gs[q][:RU, 2 * RU:], 0.0) for q in pairs}
    m_rb = {q: jnp.where(incl, gs[q][RU:, :2 * RU], 0.0) for q in pairs}
    m_rk = {q: jnp.where(incl, gs[q][RU:, 2 * RU:], 0.0) for q in pairs}
    yield
    t_inv = dict(zip(pairs, (yield from _tri_inverse_all([a_ab[q] for q in pairs], row, col, L, mmp))))
    akv = {q: mmp(a_ak[q], vs[q]) for q in pairs}
    yield

    state = {}
    ys = {}
    for u in units:
        if L == RU:
            b = (u * RU) // t_step
            s0 = [state[b, g] if (b, g) in state else s_scr[b, g] for g in hps]
            ws = [_mm_nt(qr[u, g], s0[g]) for g in hps]
            w1 = [x[:RU] for x in ws]
            wr = [x[RU:] for x in ws]
        else:
            n_in = RU // L
            s0s = [[s_scr[u * n_in + i, g] for g in hps] for i in range(n_in)]
            w1, wr = [], []
            for g in hps:
                parts = [_mm_nt(jnp.concatenate([kq[u, g][i * L:(i + 1) * L], rq[u, g][i * L:(i + 1) * L]], axis=0),
                                s0s[i][g]) for i in range(n_in)]
                w1.append(jnp.concatenate([x[:L] for x in parts], axis=0))
                wr.append(jnp.concatenate([x[L:] for x in parts], axis=0))
        yield
        us = [-mmp(t_inv[u, g], w1[g] + akv[u, g]) for g in hps]
        yield
        for g in hps:
            rhs = jnp.concatenate([_pair_blockdiag(us[g], left), _pair_blockdiag(vs[u, g].astype(F32), left)], axis=0)
            ys[u, g] = wr[g] + _mm(jnp.concatenate([m_rb[u, g], m_rk[u, g]], axis=1), rhs)
        yield
        if L == RU:
            for g in hps:
                uv = jnp.concatenate([us[g], vs[u, g].astype(F32)], axis=0)
                bkk2 = jnp.concatenate([bk2[u, g], kk2[u, g]], axis=0)
                state[b, g] = s0[g] * p_last[u][:, pls[g]] + jnp.where(bd_mask, _mm_tn(uv, bkk2), 0.0)
        else:
            for g in hps:
                for i in range(n_in):
                    rows = slice(i * L, (i + 1) * L)
                    uv_i = jnp.concatenate([us[g][rows], vs[u, g][rows]], axis=0)
                    bkk2_i = jnp.concatenate([bk2[u, g][rows], kk2[u, g][rows]], axis=0)
                    state[u * n_in + i, g] = (s0s[i][g] * p_last[u * n_in + i][:, pls[g]]
                                              + jnp.where(bd_mask, _mm_tn(uv_i, bkk2_i), 0.0))
        yield
    for (b, g), s_new in state.items():
        s_scr[b, g] = s_new

    rows_out = [jnp.concatenate([ys[u, g] for g in hps], axis=-1) for u in units]
    y = rows_out[0] if n_units == 1 else jnp.concatenate(rows_out, axis=0)
    yc = y - _head_sums_mxu(y, ones_ref) * (1.0 / HEAD_DIM)
    yield
    var = _head_sums_mxu(yc * yc, ones_ref) * (1.0 / HEAD_DIM)
    yn = yc * lax.rsqrt(var + RWKV_GN_EPS)
    oa_ref[...] = ((yn * lnw_ref[...] + lnb_ref[...] + bv_ref[...]) * sz_ref[...]).astype(oa_ref.dtype)

    def _store_state():
        for g in hps:
            st_ref[:, 2 * g] = s_scr[:, g, 0:HEAD_DIM, 0:HEAD_DIM]
            st_ref[:, 2 * g + 1] = s_scr[:, g, HEAD_DIM:, HEAD_DIM:]
    _when(last, _store_state)


def _mlstm_body(q_ref, k_ref, v_ref, g_ref, gob_ref, c0_ref, n0_ref, m0_ref, gb_ref, gnw_ref, ones_ref,
                ob_ref, ct_ref, nt_ref, mt_ref, cn_scr, m_scr, *, n_seq, n_heads, first, last):
    M = q_ref.shape[0]
    L = M // n_seq
    heads = range(n_heads)
    seqs = range(n_seq)

    def _load_state():
        cn_scr[:, :, 0:HEAD_DIM, :] = c0_ref[...]
        cn_scr[:, :, HEAD_DIM:, :] = n0_ref[...]
        m_scr[...] = m0_ref[...]
    _when(first, _load_state)

    row = lax.broadcasted_iota(jnp.int32, (M, M), 0)
    col = lax.broadcasted_iota(jnp.int32, (M, M), 1)
    same_seq = _shr(row, int(math.log2(L))) == _shr(col, int(math.log2(L)))
    incl = (row >= col) & same_seq

    g = g_ref[...] + gb_ref[...]
    bcum = pltpu.roll(_cumsum_time(-_softplus(-g), incl.astype(BF16)), LANES - n_heads, axis=1)
    m_prev = m_scr[...]
    m_prev_rows = _rows_of_seq(m_prev, n_seq, L)
    x_all = g - bcum
    pos = lax.broadcasted_iota(jnp.int32, (M, LANES), 0) & (L - 1)
    m_all = bcum + jnp.maximum(_cummax_time(x_all, L, pos), m_prev_rows)
    bm_all = bcum - m_all
    w_in_all = jnp.exp(bcum + m_prev_rows - m_all)
    e_negm_all = jnp.exp(-m_all)
    m_new = _last_rows(m_all, n_seq)
    b_last = _last_rows(bcum, n_seq)
    ws_all = jnp.exp(_rows_of_seq(b_last - m_new, n_seq, L) + x_all)
    dec_all = jnp.exp(b_last + m_prev - m_new)
    m_scr[...] = m_new
    x_t = jnp.transpose(x_all)
    yield

    ones = jnp.ones((M, SUBLANES), F32)
    sls = [slice(h * HEAD_DIM, (h + 1) * HEAD_DIM) for h in heads]
    qs = [q_ref[:, sl] for sl in sls]
    ks = [k_ref[:, sl] for sl in sls]
    v1 = [jnp.concatenate([v_ref[:, sl], ones], axis=-1) for sl in sls]
    cn = [[cn_scr[b, h] for h in heads] for b in seqs]

    qk_t = [_mm_nt(qs[h], ks[h]) for h in heads]
    if n_seq == 1:
        qc = [_mm_nt(qs[h], cn[0][h]) for h in heads]
    else:
        qc = [jnp.concatenate([_mm_nt(qs[h][b * L:(b + 1) * L], cn[b][h]) for b in seqs], axis=0) for h in heads]
    yield
    w_ts = [jnp.where(incl, jnp.exp(bm_all[:, h:h + 1] + x_t[h:h + 1, :]), 0.0) for h in heads]
    yield
    s = [qk_t[h] * w_ts[h] for h in heads]
    numden = [_mm(s[h], v1[h]) + w_in_all[:, h:h + 1] * qc[h] for h in heads]
    yield
    den = [jnp.maximum(jnp.abs(numden[h][:, HEAD_DIM:HEAD_DIM + 1]), e_negm_all[:, h:h + 1]) for h in heads]
    hh = [numden[h][:, 0:HEAD_DIM] / den[h] for h in heads]
    hh_all = jnp.concatenate(hh, axis=-1)
    ssq = _head_sums_mxu(hh_all * hh_all, ones_ref) * (1.0 / HEAD_DIM)

    yield
    v1w = [v1[h] * ws_all[:, h:h + 1] for h in heads]
    for h in heads:
        for b in seqs:
            rows = slice(b * L, (b + 1) * L)
            cn_scr[b, h] = dec_all[b][:, h:h + 1] * cn[b][h] + _mm_tn(v1w[h][rows], ks[h][rows])
    yield

    hb = hh_all * lax.rsqrt(ssq + MLSTM_GN_EPS)
    ob_ref[...] = (hb * gnw_ref[...] * gob_ref[...]).astype(ob_ref.dtype)

    def _store_state():
        ct_ref[...] = cn_scr[:, :, 0:HEAD_DIM, :]
        nt_ref[...] = cn_scr[:, :, HEAD_DIM:, :]
        mt_ref[...] = m_scr[...]
    _when(last, _store_state)


N_RWKV_IN, N_RWKV_OUT, N_MLSTM_IN, N_MLSTM_OUT = 14, 2, 11, 4
RWKV_STAGES, MLSTM_STAGES = 24, 8
MLSTM_START = 0.0


def _mixers_kernel(*refs, n_seq, chunk, n_heads_a, n_heads_b):
    it = iter(refs)
    take = lambda n: [next(it) for _ in range(n)]
    r_in, m_in = take(N_RWKV_IN), take(N_MLSTM_IN)
    r_out, m_out = take(N_RWKV_OUT), take(N_MLSTM_OUT)
    s_scr, cn_scr, m_scr = take(3)
    first = pl.program_id(1) == 0
    last = pl.program_id(1) == pl.num_programs(1) - 1
    rwkv = _rwkv_body(*r_in, *r_out, s_scr, n_seq=n_seq, chunk=chunk, n_heads=n_heads_a, first=first, last=last)
    M = m_in[0].shape[0]
    n_sub = M // MLSTM_ROWS if n_seq == 1 and M > MLSTM_ROWS else 1
    mlstm = []
    for h in range(n_sub):
        rows = (lambda r: r.at[pl.ds(h * MLSTM_ROWS, MLSTM_ROWS)]) if n_sub > 1 else (lambda r: r)
        mlstm.append(_mlstm_body(
            *[rows(r) for r in m_in[:5]], *m_in[5:], rows(m_out[0]), *m_out[1:], cn_scr, m_scr,
            n_seq=n_seq, n_heads=n_heads_b,
            first=first if h == 0 else None, last=last if h == n_sub - 1 else None))
    _run(_interleave([(rwkv, RWKV_STAGES * max(1, M // MLSTM_ROWS), 0.0),
                      (_chain(*mlstm), MLSTM_STAGES * n_sub, MLSTM_START)]))


def _mixers(acts, v_a, bv, sz, p_last, s0, ln_w, ln_b, q, k, v_b, gates, gob, c0, n0, m0, gate_b, gn_w,
            n_seq, t_step, chunk):
    b = s0.shape[0]
    n_ha, n_hb = s0.shape[1], c0.shape[1]
    n_tok, da = v_a.shape
    db = v_b.shape[1]
    m = n_seq * t_step
    nj = n_tok // (b * t_step)
    tok = lambda i, j: (i * nj + j, 0)
    per_b3 = lambda i, j: (i, 0, 0)
    per_b4 = lambda i, j: (i, 0, 0, 0)
    const2 = lambda i, j: (0, 0)
    s_spec = pl.BlockSpec((n_seq, n_ha, HEAD_DIM, HEAD_DIM), per_b4)
    c_spec = pl.BlockSpec((n_seq, n_hb, HEAD_DIM, HEAD_DIM), per_b4)
    n_spec = pl.BlockSpec((n_seq, n_hb, SUBLANES, HEAD_DIM), per_b4)
    m_spec = pl.BlockSpec((n_seq, 1, LANES), per_b3)
    row_a, row_b = pl.BlockSpec((m, da), tok), pl.BlockSpec((m, db), tok)
    in_specs = [row_a] * 9
    ones_a = jnp.kron(jnp.eye(n_ha, dtype=F32), jnp.ones((HEAD_DIM, HEAD_DIM), F32)).astype(BF16)
    ones_b = jnp.kron(jnp.eye(n_hb, dtype=F32), jnp.ones((HEAD_DIM, HEAD_DIM), F32)).astype(BF16)
    in_specs += [pl.BlockSpec((m // chunk, 1, da), lambda i, j: (i * nj + j, 0, 0)), s_spec,
                 pl.BlockSpec(ln_w.shape, const2), pl.BlockSpec(ln_b.shape, const2),
                 pl.BlockSpec(ones_a.shape, const2)]
    in_specs += [row_b, row_b, row_b, pl.BlockSpec((m, LANES), tok), row_b, c_spec, n_spec, m_spec,
                 pl.BlockSpec(gate_b.shape, const2), pl.BlockSpec(gn_w.shape, const2),
                 pl.BlockSpec(ones_b.shape, const2)]
    return pl.pallas_call(
        functools.partial(_mixers_kernel, n_seq=n_seq, chunk=chunk, n_heads_a=n_ha, n_heads_b=n_hb),
        grid=(b // n_seq, nj),
        in_specs=in_specs,
        out_specs=[row_a, s_spec, row_b, c_spec, n_spec, m_spec],
        out_shape=[jax.ShapeDtypeStruct((n_tok, da), BF16), jax.ShapeDtypeStruct(s0.shape, F32),
                   jax.ShapeDtypeStruct((n_tok, db), BF16), jax.ShapeDtypeStruct(c0.shape, F32),
                   jax.ShapeDtypeStruct(n0.shape, F32), jax.ShapeDtypeStruct(m0.shape, F32)],
        scratch_shapes=[pltpu.VMEM((n_seq, n_ha // 2, 2 * HEAD_DIM, 2 * HEAD_DIM), F32),
                        pltpu.VMEM((n_seq, n_hb, HEAD_DIM + SUBLANES, HEAD_DIM), F32),
                        pltpu.VMEM((n_seq, 1, LANES), F32)],
        compiler_params=pltpu.CompilerParams(vmem_limit_bytes=VMEM_LIMIT),
        name="mixers",
    )(*acts, v_a, bv, sz, p_last, s0, ln_w, ln_b, ones_a, q, k, v_b, gates, gob, c0, n0, m0, gate_b, gn_w, ones_b)


def _out_kernel(oa_ref, ob_ref, sa_ref, sb_ref, x_ref, gate_ref, wua_ref, wub_ref, wo_ref, gf_ref,
                y_ref, *, nb, final_norm):
    m = x_ref.shape[0]
    ua = jnp.dot(oa_ref[...], wua_ref[...], preferred_element_type=F32)
    ub = jnp.dot(ob_ref[...], wub_ref[...], preferred_element_type=F32)
    merged = sa_ref[...] * ua + sb_ref[...] * ub
    mo = jnp.dot(merged.astype(BF16), wo_ref[...], preferred_element_type=F32)
    xn = x_ref[...] + _rows_of_seq(gate_ref[...], nb, m // nb) * mo
    if final_norm:
        xn = xn * lax.rsqrt(jnp.mean(xn * xn, axis=-1, keepdims=True) + NORM_EPS) * gf_ref[...]
    y_ref[...] = xn


def _out(out_a, out_b, sa, sb, x, gate, w_up_a, w_up_b, w_out, g_final, b, nb, tt, final_norm):
    n_tok, d = x.shape
    nj = n_tok // (b * tt)
    m = nb * tt
    tok = lambda i, j: (i * nj + j, 0)
    const2 = lambda i, j: (0, 0)
    wspec = lambda w: pl.BlockSpec(w.shape, const2, pipeline_mode=pl.Buffered(1))
    in_specs = [pl.BlockSpec((m, out_a.shape[-1]), tok),
                pl.BlockSpec((m, out_b.shape[-1]), tok),
                pl.BlockSpec((m, d), tok),
                pl.BlockSpec((m, d), tok),
                pl.BlockSpec((m, d), tok),
                pl.BlockSpec((nb, 1, d), lambda i, j: (i, 0, 0)),
                wspec(w_up_a), wspec(w_up_b), wspec(w_out),
                pl.BlockSpec((1, d), const2)]
    return pl.pallas_call(
        functools.partial(_out_kernel, nb=nb, final_norm=final_norm),
        grid=(b // nb, nj),
        in_specs=in_specs,
        out_specs=pl.BlockSpec((m, d), tok),
        out_shape=jax.ShapeDtypeStruct((n_tok, d), F32),
        compiler_params=pltpu.CompilerParams(vmem_limit_bytes=VMEM_LIMIT),
        name="merge_out",
    )(out_a, out_b, sa, sb, x, gate, w_up_a, w_up_b, w_out, g_final.reshape(1, d))


def _token_tiling(b, t, target):
    if t >= target:
        return 1, math.gcd(t, target)
    return math.gcd(b, target // t), t


def _layer(x, mod, states, lp, g_final, final_norm):
    b, t, d = x.shape
    shift0, s0, conv0, c0, n0, m0 = states
    n_hb = c0.shape[1]
    cw = lp["conv_w"].shape[0]
    row = lambda a: a.reshape(1, -1)

    ada_shift, ada_scale, ada_gate = (mod[:, None, i * d:(i + 1) * d] for i in range(3))
    nb, tt = _token_tiling(b, t, PROJ_ROWS)
    L = math.gcd(t, ROWS)
    act_dtype = BF16 if L % 16 == 0 else F32
    conv0p = jnp.pad(conv0, ((0, 0), (SUBLANES - (cw - 1), 0), (0, 0)))
    n0p = jnp.broadcast_to(n0[:, :, None, :], (b, n_hb, SUBLANES, HEAD_DIM))
    m0p = jnp.pad(m0, ((0, 0), (0, LANES - n_hb)))[:, None, :]
    gate_b = jnp.zeros((1, LANES), F32).at[0, 0:n_hb].set(lp["b_i"]).at[0, n_hb:2 * n_hb].set(lp["b_f"])
    n_seq, t_step = _token_tiling(b, t, MIX_ROWS if t >= MIX_ROWS else MLSTM_ROWS)
    (kq, rq, bk, kkk, bk2, kk2, v_a, bv, sz, p_last, shift_t,
     q, k, v_b, gates, gob, sa, sb, conv_tail) = _inproj(
        x, ada_scale, ada_shift, lp["g_norm"], shift0[:, None, :], conv0p, lp, nb, tt, L, act_dtype)
    out_a, s_t, out_b, c_t, n_t, m_t = _mixers(
        (kq, rq, bk, kkk, bk2, kk2), v_a, bv, sz, p_last, s0, row(lp["ln_w"]), row(lp["ln_b"]),
        q, k, v_b, gates, gob, c0, n0p, m0p, gate_b, row(lp["gn_w"]), n_seq, t_step, L)

    y = _out(out_a, out_b, sa, sb, x.reshape(b * t, d), ada_gate, lp["w_up_a"], lp["w_up_b"], lp["w_out"],
             g_final, b, nb, tt, final_norm).reshape(b, t, d)
    new_states = (shift_t[:, 0], s_t, conv_tail[:, SUBLANES - (cw - 1):], c_t, n_t[:, :, 0, :], m_t[:, 0, :n_hb])
    return y, new_states


def _trunk(x, mods, states, layers, g_final):
    depth = len(layers)
    new = [[] for _ in states]
    for l in range(depth):
        st = tuple(s[l] for s in states)
        x, st_new = _layer(x, mods[l], st, layers[l], g_final, final_norm=(l == depth - 1))
        for lst, s in zip(new, st_new):
            lst.append(s.astype(x.dtype))
    return x, tuple(jnp.stack(lst) for lst in new)


def kernel(x_prompt, x_sample, c_prompt, c_sample, state_rwkv_shift, state_rwkv_S, state_mlstm_conv, state_mlstm_C, state_mlstm_n, state_mlstm_m, g_norm, w_ada, b_ada, w_in, mu_shift, w_decay2, w0, w_iclr2, a0, k_k, k_a, r_k, ln_w, ln_b, conv_w, conv_b, b_i, b_f, gn_w, w_up_a, w_up_b, w_out, g_final):
    depth = g_norm.shape[0]
    bp, bs = x_prompt.shape[0], x_sample.shape[0]
    d = x_prompt.shape[-1]
    da, db = w_up_a.shape[1], w_up_b.shape[1]
    n_ha, n_hb = da // HEAD_DIM, db // HEAD_DIM
    sw = mu_shift.shape[-1]
    cc = conv_w.shape[-1]
    cw = conv_w.shape[1]
    dt = x_prompt.dtype

    sizes = (sw, da, cc, db, db, n_hb, n_hb, db, d, d)
    offs = [0]
    for s in sizes:
        offs.append(offs[-1] + s)
    seg = lambda w, i: w[:, offs[i]:offs[i + 1]]

    c_all = jnp.concatenate([c_prompt, c_sample], axis=0)
    n_c = c_all.shape[0]
    c_pad = jnp.pad(c_all, ((0, (-n_c) % SUBLANES), (0, 0)))

    layers, mods_p, mods_s = [], [], []
    for l in range(depth):
        w = w_in[l]
        wg = jnp.concatenate([seg(w, 5), seg(w, 6)], axis=1)
        wg = jnp.pad(wg, ((0, 0), (0, LANES - wg.shape[1])))
        w_segs = tuple(x.astype(BF16) for x in
                       (seg(w, 0), seg(w, 1), seg(w, 2), seg(w, 3), seg(w, 4), seg(w, 7), seg(w, 8), seg(w, 9), wg))
        layers.append(dict(
            g_norm=g_norm[l], w_segs=w_segs, mu_shift=mu_shift[l], w_decay2=w_decay2[l].astype(BF16),
            w0=w0[l], w_iclr2=w_iclr2[l].astype(BF16), a0=a0[l], k_k=k_k[l], k_a=k_a[l], r_k=r_k[l],
            ln_w=ln_w[l], ln_b=ln_b[l], conv_w=conv_w[l], conv_b=conv_b[l], b_i=b_i[l], b_f=b_f[l],
            gn_w=gn_w[l], w_up_a=w_up_a[l].astype(BF16), w_up_b=w_up_b[l].astype(BF16),
            w_out=w_out[l].astype(BF16)))
        mod = _ada(c_pad, w_ada[l], b_ada[l])
        mods_p.append(mod[:bp])
        mods_s.append(mod[bp:bp + bs])

    prompt_states = (
        jnp.zeros((depth, bp, sw), dt),
        jnp.zeros((depth, bp, n_ha, HEAD_DIM, HEAD_DIM), dt),
        jnp.zeros((depth, bp, cw - 1, cc), dt),
        jnp.zeros((depth, bp, n_hb, HEAD_DIM, HEAD_DIM), dt),
        jnp.zeros((depth, bp, n_hb, HEAD_DIM), dt),
        jnp.zeros((depth, bp, n_hb), dt),
    )
    sample_states = (state_rwkv_shift, state_rwkv_S, state_mlstm_conv,
                     state_mlstm_C, state_mlstm_n, state_mlstm_m)
    y_p, st_p = _trunk(x_prompt, mods_p, prompt_states, layers, g_final)
    y_s, st_s = _trunk(x_sample, mods_s, sample_states, layers, g_final)
    return (y_p, y_s) + st_p + st_s
```

```python
import functools
import math

import jax
import jax.numpy as jnp
from jax import lax
from jax.experimental import pallas as pl
from jax.experimental.pallas import tpu as pltpu

F32 = jnp.float32
BF16 = jnp.bfloat16

HEAD_DIM = 64
NORM_EPS = 1e-6
RWKV_GN_EPS = 64e-5
MLSTM_GN_EPS = 1e-6
ROWS = 64
MLSTM_ROWS = 128
MIX_ROWS = 256
PROJ_ROWS = 256
SUBLANES = 8
LANES = 128
TRI_BASE = 4
VMEM_LIMIT = 60 * 1024 * 1024

NT_DIMS = (((1,), (1,)), ((), ()))
TN_DIMS = (((0,), (0,)), ((), ()))


def _mm(a, b):
    return jnp.dot(a.astype(BF16), b.astype(BF16), preferred_element_type=F32)


def _mm_nt(a, b):
    return lax.dot_general(a.astype(BF16), b.astype(BF16), NT_DIMS, preferred_element_type=F32)


def _mm_tn(a, b):
    return lax.dot_general(a.astype(BF16), b.astype(BF16), TN_DIMS, preferred_element_type=F32)


def _sigmoid(x):
    return 1.0 / (1.0 + jnp.exp(-x))


def _silu(x):
    return x * _sigmoid(x)


def _softplus(x):
    return jnp.maximum(x, 0.0) + jnp.log(1.0 + jnp.exp(-jnp.abs(x)))


def _cumsum_time(x, tri_bf16):
    hi = x.astype(BF16)
    r1 = x - hi.astype(F32)
    mid = r1.astype(BF16)
    lo = (r1 - mid.astype(F32)).astype(BF16)
    dot = functools.partial(jnp.dot, preferred_element_type=F32)
    return dot(tri_bf16, hi) + dot(tri_bf16, mid) + dot(tri_bf16, lo)


def _shr(x, n):
    return lax.shift_right_logical(x, jnp.full(x.shape, n, x.dtype))


def _rows_of_seq(x, n_seq, L):
    if n_seq == 1:
        return x[0]
    return jnp.broadcast_to(x, (n_seq, L, x.shape[-1])).reshape(n_seq * L, x.shape[-1])


def _last_rows(x, n_seq):
    m, n = x.shape
    L = m // n_seq
    return x.reshape(n_seq, L, n)[:, L - 1:L, :]


def _head_sums(x):
    m, n = x.shape
    parts = [jnp.broadcast_to(jnp.sum(x[:, o:o + HEAD_DIM], axis=-1, keepdims=True), (m, HEAD_DIM))
             for o in range(0, n, HEAD_DIM)]
    return jnp.concatenate(parts, axis=-1)


def _head_sums_mxu(x, ones_ref):
    hi = x.astype(BF16)
    lo = (x - hi.astype(F32)).astype(BF16)
    s = jnp.dot(jnp.concatenate([hi, lo], axis=0), ones_ref[...], preferred_element_type=F32)
    return s[:x.shape[0]] + s[x.shape[0]:]


def _cummax_time(x, L, pos):
    y = x
    sh = 1
    while sh < L:
        y = jnp.where(pos >= sh, jnp.maximum(y, pltpu.roll(y, sh, axis=0)), y)
        sh *= 2
    return y


def _tri_inverse_all(mats, row, col, L, mm=_mm):
    bs = min(TRI_BASE, L)
    sh = int(math.log2(bs))
    same = _shr(row, sh) == _shr(col, sh)
    eye = (row == col).astype(F32)
    rows = mats[0].shape[0]
    ns = [jnp.where(same, -a, 0.0) for a in mats]
    ts = [eye + n for n in ns]
    if bs > 2:
        ps = [mm(n, n) for n in ns]
        yield
        k = 2
        while 2 * k < bs:
            both = [mm(jnp.concatenate([t, p], axis=0), p) for t, p in zip(ts, ps)]
            ts = [t + x[:rows] for t, x in zip(ts, both)]
            ps = [x[rows:] for x in both]
            yield
            k *= 2
        ts = [t + mm(t, p) for t, p in zip(ts, ps)]
        yield
    size = bs
    while size < L:
        sh = int(math.log2(size))
        lower_left = (_shr(row, sh + 1) == _shr(col, sh + 1)) & (_shr(row, sh) != _shr(col, sh))
        offs = [jnp.where(lower_left, a, 0.0) for a in mats]
        tmp = [mm(t, o) for t, o in zip(ts, offs)]
        yield
        ts = [t - mm(x, t) for t, x in zip(ts, tmp)]
        yield
        size *= 2
    return ts


def _interleave(plan):
    live = [[g, n, s, 0] for g, n, s in plan]
    while live:
        item = min(live, key=lambda it: it[2] + (1.0 - it[2]) * it[3] / it[1])
        try:
            next(item[0])
            item[3] += 1
            yield
        except StopIteration:
            live.remove(item)


def _chain(*gens):
    for g in gens:
        yield from g


def _run(gen):
    for _ in gen:
        pass


def _when(pred, fn):
    if pred is None:
        return
    if pred is True:
        fn()
    else:
        pl.when(pred)(fn)


def _ada_kernel(c_ref, w_ref, b_ref, o_ref):
    o_ref[...] = _mm(_silu(c_ref[...]), w_ref[...]) + b_ref[...]


def _ada(c, w_ada, b_ada):
    n, d = c.shape
    n3 = w_ada.shape[1]
    tn = d
    return pl.pallas_call(
        _ada_kernel,
        grid=(n3 // tn,),
        in_specs=[pl.BlockSpec((n, d), lambda j: (0, 0)),
                  pl.BlockSpec((d, tn), lambda j: (0, j)),
                  pl.BlockSpec((1, tn), lambda j: (0, j))],
        out_specs=pl.BlockSpec((n, tn), lambda j: (0, j)),
        out_shape=jax.ShapeDtypeStruct((n, n3), F32),
        name="ada",
    )(c, w_ada, b_ada.reshape(1, n3))


PROJ_COLS = 256


def _inproj_body(x_ref, sc_ref, sh_ref, g_ref, shift0_ref, conv0_ref,
                 w_shift, w_za, w_qk, w_vb, w_ob, w_zb, w_gt,
                 mu_ref, w0_ref, a0_ref, kk_ref, ka_ref, rk_ref, wd2_ref, wi2_ref, cw_ref, cb_ref,
                 kq_o, rq_o, bk_o, kkk_o, bk2_o, kk2_o, va_o, bv_o, sz_o, plast_o, shst_o,
                 q_o, k_o, vb_o, gt_o, gob_o, cst_o,
                 prev_scr, xbuf, *, chunk, lora_w, conv_w, first, live):
    nb, tt, d = x_ref.shape
    M = nb * tt
    L = chunk
    n_chunks = M // L
    da = va_o.shape[-1]
    db = vb_o.shape[-1]
    cc = xbuf.shape[-1]
    pad = SUBLANES

    def _init():
        prev_scr[...] = shift0_ref[...]
        xbuf[:, 0:pad, :] = conv0_ref[...]
    _when(first, _init)

    x = x_ref[...]
    y = x * lax.rsqrt(jnp.mean(x * x, axis=-1, keepdims=True) + NORM_EPS) * g_ref[...]
    h = (y * (1.0 + sc_ref[...]) + sh_ref[...]).reshape(M, d).astype(BF16)
    def proj(w_ref):
        n = w_ref.shape[1]
        parts = []
        for o in range(0, n, PROJ_COLS):
            parts.append(jnp.dot(h, w_ref[:, o:min(o + PROJ_COLS, n)], preferred_element_type=F32))
            yield
        return parts[0] if len(parts) == 1 else jnp.concatenate(parts, axis=-1)

    p = yield from proj(w_shift)
    row_w = lax.broadcasted_iota(jnp.int32, p.shape, 0)
    first = (row_w & (tt - 1)) == 0
    prev = jnp.where(first, _rows_of_seq(prev_scr[...], nb, tt), pltpu.roll(p, 1, axis=0))
    last = _last_rows(p, nb)
    prev_scr[...] = last

    def _shift_state():
        shst_o[...] = last
    _when(live, _shift_state)
    ps = p + mu_ref[...] * (prev - p)
    r = ps[:, 0:da]
    k = ps[:, da:2 * da]
    v = ps[:, 2 * da:3 * da]
    wl = ps[:, 3 * da:3 * da + lora_w]
    al = ps[:, 3 * da + lora_w:]

    qk_pre = yield from proj(w_qk)
    z_a = yield from proj(w_za)
    w = w0_ref[...] + _mm(jnp.tanh(wl), wd2_ref[...])
    lw = -math.exp(-0.5) * _sigmoid(w)
    a = _sigmoid(a0_ref[...] + _mm(al, wi2_ref[...]))
    yield
    v_b = yield from proj(w_vb)
    g_t = yield from proj(w_gt)
    o_b = yield from proj(w_ob)
    z_b = yield from proj(w_zb)
    kkr = k * kk_ref[...]
    kk = kkr * lax.rsqrt(jnp.maximum(_head_sums(kkr * kkr), 1e-24))
    k2 = k * (1.0 + (a - 1.0) * ka_ref[...])
    b = kk * a
    va_o[...] = v
    bv_o[...] = _head_sums(r * k2 * rk_ref[...]) * v
    sz_o[...] = _silu(z_a)

    lg = int(math.log2(L))
    row_m = lax.broadcasted_iota(jnp.int32, (M, M), 0)
    col_m = lax.broadcasted_iota(jnp.int32, (M, M), 1)
    tri = ((row_m >= col_m) & (_shr(row_m, lg) == _shr(col_m, lg))).astype(BF16)
    c = _cumsum_time(lw, tri)
    yield
    c_last = _last_rows(c, n_chunks)
    plast_o[...] = jnp.exp(c_last)
    e_nc = jnp.exp(-c)
    e_cl = jnp.exp(_rows_of_seq(c_last, n_chunks, L) - c)
    kq_o[...] = (kk * jnp.exp(c - lw)).astype(kq_o.dtype)
    rq_o[...] = (r * jnp.exp(c)).astype(rq_o.dtype)
    bk_o[...] = (b * e_nc).astype(bk_o.dtype)
    kkk_o[...] = (k2 * e_nc).astype(kkk_o.dtype)
    bk2_o[...] = (b * e_cl).astype(bk2_o.dtype)
    kk2_o[...] = (k2 * e_cl).astype(kk2_o.dtype)

    xbuf[:, pad:pad + tt, :] = qk_pre.reshape(nb, tt, cc)
    conv = cb_ref[...]
    for tap in range(conv_w):
        conv = conv + xbuf[:, pl.ds(pad - (conv_w - 1) + tap, tt), :].reshape(M, cc) * cw_ref[tap:tap + 1, :]
    tail = xbuf[:, tt:tt + pad, :]
    xbuf[:, 0:pad, :] = tail

    def _conv_state():
        cst_o[...] = tail
    _when(live, _conv_state)
    yield
    qk = _silu(conv)
    q_o[...] = qk[:, 0:db].astype(q_o.dtype)
    k_o[...] = (qk[:, db:] * (1.0 / math.sqrt(HEAD_DIM))).astype(k_o.dtype)
    vb_o[...] = v_b
    gt_o[...] = g_t
    gob_o[...] = _sigmoid(o_b) * _silu(z_b)


def _inproj_kernel(*refs, chunk, lora_w, conv_w):
    _run(_inproj_body(*refs, chunk=chunk, lora_w=lora_w, conv_w=conv_w,
                      first=pl.program_id(1) == 0, live=True))


def _inproj(x, scale, shift, g_norm, shift0, conv0p, lp, nb, tt, chunk, act_dtype):
    b, t, d = x.shape
    n_tok = b * t
    m = nb * tt
    nj = t // tt
    sw = shift0.shape[-1]
    cc = conv0p.shape[-1]
    da, db = lp["w_up_a"].shape[0], lp["w_up_b"].shape[0]
    cw = lp["conv_w"].shape[0]
    row = lambda a: a.reshape(1, -1)
    rows = [row(lp[n]) for n in ("mu_shift", "w0", "a0", "k_k", "k_a", "r_k")]
    smalls = rows + [lp["w_decay2"], lp["w_iclr2"], lp["conv_w"], row(lp["conv_b"])]
    const = lambda i, j: (0, 0)
    tok = lambda i, j: (i * nj + j, 0)
    per_b = lambda i, j: (i, 0, 0)
    in_specs = [pl.BlockSpec((nb, tt, d), lambda i, j: (i, j, 0)),
                pl.BlockSpec((nb, 1, d), per_b),
                pl.BlockSpec((nb, 1, d), per_b),
                pl.BlockSpec((1, d), const),
                pl.BlockSpec((nb, 1, sw), per_b),
                pl.BlockSpec((nb, SUBLANES, cc), per_b)]
    in_specs += [pl.BlockSpec(w.shape, const, pipeline_mode=pl.Buffered(1)) for w in lp["w_segs"]]
    in_specs += [pl.BlockSpec(a.shape, const) for a in smalls]
    tok_out = lambda n, dt: (pl.BlockSpec((m, n), tok), jax.ShapeDtypeStruct((n_tok, n), dt))
    outs = [tok_out(da, act_dtype)] * 6 + [tok_out(da, F32)] * 3
    outs += [(pl.BlockSpec((m // chunk, 1, da), lambda i, j: (i * nj + j, 0, 0)),
              jax.ShapeDtypeStruct((n_tok // chunk, 1, da), F32)),
             (pl.BlockSpec((nb, 1, sw), per_b), jax.ShapeDtypeStruct((b, 1, sw), F32))]
    outs += [tok_out(db, act_dtype)] * 2 + [tok_out(db, F32), tok_out(LANES, F32), tok_out(db, F32)]
    outs += [(pl.BlockSpec((nb, SUBLANES, cc), per_b), jax.ShapeDtypeStruct((b, SUBLANES, cc), F32))]
    return pl.pallas_call(
        functools.partial(_inproj_kernel, chunk=chunk, lora_w=lp["w_decay2"].shape[0], conv_w=cw),
        grid=(b // nb, nj),
        in_specs=in_specs,
        out_specs=[o[0] for o in outs], out_shape=[o[1] for o in outs],
        scratch_shapes=[pltpu.VMEM((nb, 1, sw), F32),
                        pltpu.VMEM((nb, tt + SUBLANES, cc), F32)],
        compiler_params=pltpu.CompilerParams(vmem_limit_bytes=VMEM_LIMIT),
        name="inproj",
    )(x, scale, shift, g_norm.reshape(1, d), shift0, conv0p, *lp["w_segs"], *smalls)


def _pair_blockdiag(y, left):
    return jnp.concatenate([jnp.where(left, y, 0.0), jnp.where(left, 0.0, y)], axis=0)


def _rwkv_body(kq_ref, rq_ref, bk_ref, kkk_ref, bk2_ref, kk2_ref, v_ref, bv_ref, sz_ref, plast_ref,
               s0_ref, lnw_ref, lnb_ref, ones_ref, oa_ref, st_ref, s_scr, *, n_seq, chunk, n_heads,
               first, last):
    M = kq_ref.shape[0]
    L = chunk
    t_step = M // n_seq
    RU = min(M, ROWS)
    PW = 2 * HEAD_DIM
    n_units = M // RU
    n_hp = n_heads // 2
    units = range(n_units)
    hps = range(n_hp)
    pairs = [(u, g) for u in units for g in hps]
    zero_blk = jnp.zeros((n_seq, HEAD_DIM, HEAD_DIM), F32)

    def _load_state():
        for g in hps:
            top = jnp.concatenate([s0_ref[:, 2 * g], zero_blk], axis=-1)
            bot = jnp.concatenate([zero_blk, s0_ref[:, 2 * g + 1]], axis=-1)
            s_scr[:, g] = jnp.concatenate([top, bot], axis=-2)
    _when(first, _load_state)

    lg = int(math.log2(L))
    row = lax.broadcasted_iota(jnp.int32, (RU, 2 * RU), 0)
    lane = lax.broadcasted_iota(jnp.int32, (RU, 2 * RU), 1)
    col = lane & (RU - 1)
    same = _shr(row, lg) == _shr(col, lg)
    incl = (row >= col) & same
    strict = (row > col) & same
    left = lax.broadcasted_iota(jnp.int32, (RU, PW), 1) < HEAD_DIM
    bd_mask = (lax.broadcasted_iota(jnp.int32, (PW, PW), 0) < HEAD_DIM) == (
        lax.broadcasted_iota(jnp.int32, (PW, PW), 1) < HEAD_DIM)
    mmp = lambda x, y: _mm(x, _pair_blockdiag(y, left))

    p_last = plast_ref[...]
    pls = [slice(g * PW, (g + 1) * PW) for g in hps]
    rus = [slice(u * RU, (u + 1) * RU) for u in units]
    blk = lambda ref, q: ref[rus[q[0]], pls[q[1]]]
    kq = {q: blk(kq_ref, q) for q in pairs}
    rq = {q: blk(rq_ref, q) for q in pairs}
    bk2 = {q: blk(bk2_ref, q) for q in pairs}
    kk2 = {q: blk(kk2_ref, q) for q in pairs}
    vs = {q: blk(v_ref, q) for q in pairs}
    qr = {q: jnp.concatenate([kq[q], rq[q]], axis=0).astype(BF16) for q in pairs}
    bkk = {q: jnp.concatenate([_pair_blockdiag(blk(bk_ref, q).astype(F32), left),
                               _pair_blockdiag(blk(kkk_ref, q).astype(F32), left)], axis=0).astype(BF16)
           for q in pairs}
    yield

    gs = {q: _mm_nt(qr[q], bkk[q]) for q in pairs}
    yield
    a_ab = {q: jnp.where(strict, gs[q][:RU, :2 * RU], 0.0) for q in pairs}
    a_ak = {q: jnp.where(strict, gs[q][:RU, 2 * RU:], 0.0) for q in pairs}
    m_rb = {q: jnp.where(incl, gs[q][RU:, :2 * RU], 0.0) for q in pairs}
    m_rk = {q: jnp.where(incl, gs[q][RU:, 2 * RU:], 0.0) for q in pairs}
    yield
    t_inv = dict(zip(pairs, (yield from _tri_inverse_all([a_ab[q] for q in pairs], row, col, L, mmp))))
    akv = {q: mmp(a_ak[q], vs[q]) for q in pairs}
    yield

    state = {}
    ys = {}
    for u in units:
        if L == RU:
            b = (u * RU) // t_step
            s0 = [state[b, g] if (b, g) in state else s_scr[b, g] for g in hps]
            ws = [_mm_nt(qr[u, g], s0[g]) for g in hps]
            w1 = [x[:RU] for x in ws]
            wr = [x[RU:] for x in ws]
        else:
            n_in = RU // L
            s0s = [[s_scr[u * n_in + i, g] for g in hps] for i in range(n_in)]
            w1, wr = [], []
            for g in hps:
                parts = [_mm_nt(jnp.concatenate([kq[u, g][i * L:(i + 1) * L], rq[u, g][i * L:(i + 1) * L]], axis=0),
                                s0s[i][g]) for i in range(n_in)]
                w1.append(jnp.concatenate([x[:L] for x in parts], axis=0))
                wr.append(jnp.concatenate([x[L:] for x in parts], axis=0))
        yield
        us = [-mmp(t_inv[u, g], w1[g] + akv[u, g]) for g in hps]
        yield
        for g in hps:
            rhs = jnp.concatenate([_pair_blockdiag(us[g], left), _pair_blockdiag(vs[u, g].astype(F32), left)], axis=0)
            ys[u, g] = wr[g] + _mm(jnp.concatenate([m_rb[u, g], m_rk[u, g]], axis=1), rhs)
        yield
        if L == RU:
            for g in hps:
                uv = jnp.concatenate([us[g], vs[u, g].astype(F32)], axis=0)
                bkk2 = jnp.concatenate([bk2[u, g], kk2[u, g]], axis=0)
                state[b, g] = s0[g] * p_last[u][:, pls[g]] + jnp.where(bd_mask, _mm_tn(uv, bkk2), 0.0)
        else:
            for g in hps:
                for i in range(n_in):
                    rows = slice(i * L, (i + 1) * L)
                    uv_i = jnp.concatenate([us[g][rows], vs[u, g][rows]], axis=0)
                    bkk2_i = jnp.concatenate([bk2[u, g][rows], kk2[u, g][rows]], axis=0)
                    state[u * n_in + i, g] = (s0s[i][g] * p_last[u * n_in + i][:, pls[g]]
                                              + jnp.where(bd_mask, _mm_tn(uv_i, bkk2_i), 0.0))
        yield
    for (b, g), s_new in state.items():
        s_scr[b, g] = s_new

    rows_out = [jnp.concatenate([ys[u, g] for g in hps], axis=-1) for u in units]
    y = rows_out[0] if n_units == 1 else jnp.concatenate(rows_out, axis=0)
    yc = y - _head_sums_mxu(y, ones_ref) * (1.0 / HEAD_DIM)
    yield
    var = _head_sums_mxu(yc * yc, ones_ref) * (1.0 / HEAD_DIM)
    yn = yc * lax.rsqrt(var + RWKV_GN_EPS)
    oa_ref[...] = ((yn * lnw_ref[...] + lnb_ref[...] + bv_ref[...]) * sz_ref[...]).astype(oa_ref.dtype)

    def _store_state():
        for g in hps:
            st_ref[:, 2 * g] = s_scr[:, g, 0:HEAD_DIM, 0:HEAD_DIM]
            st_ref[:, 2 * g + 1] = s_scr[:, g, HEAD_DIM:, HEAD_DIM:]
    _when(last, _store_state)


def _mlstm_body(q_ref, k_ref, v_ref, g_ref, gob_ref, c0_ref, n0_ref, m0_ref, gb_ref, gnw_ref, ones_ref,
                ob_ref, ct_ref, nt_ref, mt_ref, cn_scr, m_scr, *, n_seq, n_heads, first, last):
    M = q_ref.shape[0]
    L = M // n_seq
    heads = range(n_heads)
    seqs = range(n_seq)

    def _load_state():
        cn_scr[:, :, 0:HEAD_DIM, :] = c0_ref[...]
        cn_scr[:, :, HEAD_DIM:, :] = n0_ref[...]
        m_scr[...] = m0_ref[...]
    _when(first, _load_state)

    row = lax.broadcasted_iota(jnp.int32, (M, M), 0)
    col = lax.broadcasted_iota(jnp.int32, (M, M), 1)
    same_seq = _shr(row, int(math.log2(L))) == _shr(col, int(math.log2(L)))
    incl = (row >= col) & same_seq

    g = g_ref[...] + gb_ref[...]
    bcum = pltpu.roll(_cumsum_time(-_softplus(-g), incl.astype(BF16)), LANES - n_heads, axis=1)
    m_prev = m_scr[...]
    m_prev_rows = _rows_of_seq(m_prev, n_seq, L)
    x_all = g - bcum
    pos = lax.broadcasted_iota(jnp.int32, (M, LANES), 0) & (L - 1)
    m_all = bcum + jnp.maximum(_cummax_time(x_all, L, pos), m_prev_rows)
    bm_all = bcum - m_all
    w_in_all = jnp.exp(bcum + m_prev_rows - m_all)
    e_negm_all = jnp.exp(-m_all)
    m_new = _last_rows(m_all, n_seq)
    b_last = _last_rows(bcum, n_seq)
    ws_all = jnp.exp(_rows_of_seq(b_last - m_new, n_seq, L) + x_all)
    dec_all = jnp.exp(b_last + m_prev - m_new)
    m_scr[...] = m_new
    x_t = jnp.transpose(x_all)
    yield

    ones = jnp.ones((M, SUBLANES), F32)
    sls = [slice(h * HEAD_DIM, (h + 1) * HEAD_DIM) for h in heads]
    qs = [q_ref[:, sl] for sl in sls]
    ks = [k_ref[:, sl] for sl in sls]
    v1 = [jnp.concatenate([v_ref[:, sl], ones], axis=-1) for sl in sls]
    cn = [[cn_scr[b, h] for h in heads] for b in seqs]

    qk_t = [_mm_nt(qs[h], ks[h]) for h in heads]
    if n_seq == 1:
        qc = [_mm_nt(qs[h], cn[0][h]) for h in heads]
    else:
        qc = [jnp.concatenate([_mm_nt(qs[h][b * L:(b + 1) * L], cn[b][h]) for b in seqs], axis=0) for h in heads]
    yield
    w_ts = [jnp.where(incl, jnp.exp(bm_all[:, h:h + 1] + x_t[h:h + 1, :]), 0.0) for h in heads]
    yield
    s = [qk_t[h] * w_ts[h] for h in heads]
    numden = [_mm(s[h], v1[h]) + w_in_all[:, h:h + 1] * qc[h] for h in heads]
    yield
    den = [jnp.maximum(jnp.abs(numden[h][:, HEAD_DIM:HEAD_DIM + 1]), e_negm_all[:, h:h + 1]) for h in heads]
    hh = [numden[h][:, 0:HEAD_DIM] / den[h] for h in heads]
    hh_all = jnp.concatenate(hh, axis=-1)
    ssq = _head_sums_mxu(hh_all * hh_all, ones_ref) * (1.0 / HEAD_DIM)

    yield
    v1w = [v1[h] * ws_all[:, h:h + 1] for h in heads]
    for h in heads:
        for b in seqs:
            rows = slice(b * L, (b + 1) * L)
            cn_scr[b, h] = dec_all[b][:, h:h + 1] * cn[b][h] + _mm_tn(v1w[h][rows], ks[h][rows])
    yield

    hb = hh_all * lax.rsqrt(ssq + MLSTM_GN_EPS)
    ob_ref[...] = (hb * gnw_ref[...] * gob_ref[...]).astype(ob_ref.dtype)

    def _store_state():
        ct_ref[...] = cn_scr[:, :, 0:HEAD_DIM, :]
        nt_ref[...] = cn_scr[:, :, HEAD_DIM:, :]
        mt_ref[...] = m_scr[...]
    _when(last, _store_state)


N_RWKV_IN, N_RWKV_OUT, N_MLSTM_IN, N_MLSTM_OUT = 14, 2, 11, 4
RWKV_STAGES, MLSTM_STAGES = 24, 8
MLSTM_START = 0.0


def _mixers_kernel(*refs, n_seq, chunk, n_heads_a, n_heads_b):
    it = iter(refs)
    take = lambda n: [next(it) for _ in range(n)]
    r_in, m_in = take(N_RWKV_IN), take(N_MLSTM_IN)
    r_out, m_out = take(N_RWKV_OUT), take(N_MLSTM_OUT)
    s_scr, cn_scr, m_scr = take(3)
    first = pl.program_id(1) == 0
    last = pl.program_id(1) == pl.num_programs(1) - 1
    rwkv = _rwkv_body(*r_in, *r_out, s_scr, n_seq=n_seq, chunk=chunk, n_heads=n_heads_a, first=first, last=last)
    M = m_in[0].shape[0]
    n_sub = M // MLSTM_ROWS if n_seq == 1 and M > MLSTM_ROWS else 1
    mlstm = []
    for h in range(n_sub):
        rows = (lambda r: r.at[pl.ds(h * MLSTM_ROWS, MLSTM_ROWS)]) if n_sub > 1 else (lambda r: r)
        mlstm.append(_mlstm_body(
            *[rows(r) for r in m_in[:5]], *m_in[5:], rows(m_out[0]), *m_out[1:], cn_scr, m_scr,
            n_seq=n_seq, n_heads=n_heads_b,
            first=first if h == 0 else None, last=last if h == n_sub - 1 else None))
    _run(_interleave([(rwkv, RWKV_STAGES * max(1, M // MLSTM_ROWS), 0.0),
                      (_chain(*mlstm), MLSTM_STAGES * n_sub, MLSTM_START)]))


def _mixers(acts, v_a, bv, sz, p_last, s0, ln_w, ln_b, q, k, v_b, gates, gob, c0, n0, m0, gate_b, gn_w,
            n_seq, t_step, chunk):
    b = s0.shape[0]
    n_ha, n_hb = s0.shape[1], c0.shape[1]
    n_tok, da = v_a.shape
    db = v_b.shape[1]
    m = n_seq * t_step
    nj = n_tok // (b * t_step)
    tok = lambda i, j: (i * nj + j, 0)
    per_b3 = lambda i, j: (i, 0, 0)
    per_b4 = lambda i, j: (i, 0, 0, 0)
    const2 = lambda i, j: (0, 0)
    s_spec = pl.BlockSpec((n_seq, n_ha, HEAD_DIM, HEAD_DIM), per_b4)
    c_spec = pl.BlockSpec((n_seq, n_hb, HEAD_DIM, HEAD_DIM), per_b4)
    n_spec = pl.BlockSpec((n_seq, n_hb, SUBLANES, HEAD_DIM), per_b4)
    m_spec = pl.BlockSpec((n_seq, 1, LANES), per_b3)
    row_a, row_b = pl.BlockSpec((m, da), tok), pl.BlockSpec((m, db), tok)
    in_specs = [row_a] * 9
    ones_a = jnp.kron(jnp.eye(n_ha, dtype=F32), jnp.ones((HEAD_DIM, HEAD_DIM), F32)).astype(BF16)
    ones_b = jnp.kron(jnp.eye(n_hb, dtype=F32), jnp.ones((HEAD_DIM, HEAD_DIM), F32)).astype(BF16)
    in_specs += [pl.BlockSpec((m // chunk, 1, da), lambda i, j: (i * nj + j, 0, 0)), s_spec,
                 pl.BlockSpec(ln_w.shape, const2), pl.BlockSpec(ln_b.shape, const2),
                 pl.BlockSpec(ones_a.shape, const2)]
    in_specs += [row_b, row_b, row_b, pl.BlockSpec((m, LANES), tok), row_b, c_spec, n_spec, m_spec,
                 pl.BlockSpec(gate_b.shape, const2), pl.BlockSpec(gn_w.shape, const2),
                 pl.BlockSpec(ones_b.shape, const2)]
    return pl.pallas_call(
        functools.partial(_mixers_kernel, n_seq=n_seq, chunk=chunk, n_heads_a=n_ha, n_heads_b=n_hb),
        grid=(b // n_seq, nj),
        in_specs=in_specs,
        out_specs=[row_a, s_spec, row_b, c_spec, n_spec, m_spec],
        out_shape=[jax.ShapeDtypeStruct((n_tok, da), BF16), jax.ShapeDtypeStruct(s0.shape, F32),
                   jax.ShapeDtypeStruct((n_tok, db), BF16), jax.ShapeDtypeStruct(c0.shape, F32),
                   jax.ShapeDtypeStruct(n0.shape, F32), jax.ShapeDtypeStruct(m0.shape, F32)],
        scratch_shapes=[pltpu.VMEM((n_seq, n_ha // 2, 2 * HEAD_DIM, 2 * HEAD_DIM), F32),
                        pltpu.VMEM((n_seq, n_hb, HEAD_DIM + SUBLANES, HEAD_DIM), F32),
                        pltpu.VMEM((n_seq, 1, LANES), F32)],
        compiler_params=pltpu.CompilerParams(vmem_limit_bytes=VMEM_LIMIT),
        name="mixers",
    )(*acts, v_a, bv, sz, p_last, s0, ln_w, ln_b, ones_a, q, k, v_b, gates, gob, c0, n0, m0, gate_b, gn_w, ones_b)


def _out_kernel(oa_ref, ob_ref, x_ref, sc_ref, sh_ref, gate_ref, g_ref, wga_ref, wgb_ref, wua_ref, wub_ref, wo_ref,
                gf_ref, y_ref, *, nb, final_norm):
    m = x_ref.shape[0]
    rows = lambda ref: _rows_of_seq(ref[...], nb, m // nb)
    x = x_ref[...]
    h = x * lax.rsqrt(jnp.mean(x * x, axis=-1, keepdims=True) + NORM_EPS) * g_ref[...]
    h = (h * (1.0 + rows(sc_ref)) + rows(sh_ref)).astype(BF16)
    gl_a = jnp.dot(h, wga_ref[...], preferred_element_type=F32)
    gl_b = jnp.dot(h, wgb_ref[...], preferred_element_type=F32)
    ua = jnp.dot(oa_ref[...], wua_ref[...], preferred_element_type=F32)
    ub = jnp.dot(ob_ref[...], wub_ref[...], preferred_element_type=F32)
    merged = _sigmoid(gl_a) * ua + _sigmoid(gl_b) * ub
    mo = jnp.dot(merged.astype(BF16), wo_ref[...], preferred_element_type=F32)
    xn = x + rows(gate_ref) * mo
    if final_norm:
        xn = xn * lax.rsqrt(jnp.mean(xn * xn, axis=-1, keepdims=True) + NORM_EPS) * gf_ref[...]
    y_ref[...] = xn


def _out(out_a, out_b, x, scale, shift, gate, g_norm, w_gl_a, w_gl_b, w_up_a, w_up_b, w_out, g_final,
         b, nb, tt, final_norm):
    n_tok, d = x.shape
    nj = n_tok // (b * tt)
    m = nb * tt
    tok = lambda i, j: (i * nj + j, 0)
    const2 = lambda i, j: (0, 0)
    per_b = lambda i, j: (i, 0, 0)
    wspec = lambda w: pl.BlockSpec(w.shape, const2, pipeline_mode=pl.Buffered(1))
    in_specs = [pl.BlockSpec((m, out_a.shape[-1]), tok),
                pl.BlockSpec((m, out_b.shape[-1]), tok),
                pl.BlockSpec((m, d), tok),
                pl.BlockSpec((nb, 1, d), per_b),
                pl.BlockSpec((nb, 1, d), per_b),
                pl.BlockSpec((nb, 1, d), per_b),
                pl.BlockSpec((1, d), const2),
                wspec(w_gl_a), wspec(w_gl_b), wspec(w_up_a), wspec(w_up_b), wspec(w_out),
                pl.BlockSpec((1, d), const2)]
    return pl.pallas_call(
        functools.partial(_out_kernel, nb=nb, final_norm=final_norm),
        grid=(b // nb, nj),
        in_specs=in_specs,
        out_specs=pl.BlockSpec((m, d), tok),
        out_shape=jax.ShapeDtypeStruct((n_tok, d), F32),
        compiler_params=pltpu.CompilerParams(vmem_limit_bytes=VMEM_LIMIT),
        name="merge_out",
    )(out_a, out_b, x, scale, shift, gate, g_norm.reshape(1, d), w_gl_a, w_gl_b, w_up_a, w_up_b, w_out,
      g_final.reshape(1, d))


def _token_tiling(b, t, target):
    if t >= target:
        return 1, math.gcd(t, target)
    return math.gcd(b, target // t), t


def _layer(x, mod, states, lp, g_final, final_norm):
    b, t, d = x.shape
    shift0, s0, conv0, c0, n0, m0 = states
    n_hb = c0.shape[1]
    cw = lp["conv_w"].shape[0]
    row = lambda a: a.reshape(1, -1)

    ada_shift, ada_scale, ada_gate = (mod[:, None, i * d:(i + 1) * d] for i in range(3))
    nb, tt = _token_tiling(b, t, PROJ_ROWS)
    L = math.gcd(t, ROWS)
    act_dtype = BF16 if L % 16 == 0 else F32
    conv0p = jnp.pad(conv0, ((0, 0), (SUBLANES - (cw - 1), 0), (0, 0)))
    n0p = jnp.broadcast_to(n0[:, :, None, :], (b, n_hb, SUBLANES, HEAD_DIM))
    m0p = jnp.pad(m0, ((0, 0), (0, LANES - n_hb)))[:, None, :]
    gate_b = jnp.zeros((1, LANES), F32).at[0, 0:n_hb].set(lp["b_i"]).at[0, n_hb:2 * n_hb].set(lp["b_f"])
    n_seq, t_step = _token_tiling(b, t, MIX_ROWS if t >= MIX_ROWS else MLSTM_ROWS)
    (kq, rq, bk, kkk, bk2, kk2, v_a, bv, sz, p_last, shift_t,
     q, k, v_b, gates, gob, conv_tail) = _inproj(
        x, ada_scale, ada_shift, lp["g_norm"], shift0[:, None, :], conv0p, lp, nb, tt, L, act_dtype)
    out_a, s_t, out_b, c_t, n_t, m_t = _mixers(
        (kq, rq, bk, kkk, bk2, kk2), v_a, bv, sz, p_last, s0, row(lp["ln_w"]), row(lp["ln_b"]),
        q, k, v_b, gates, gob, c0, n0p, m0p, gate_b, row(lp["gn_w"]), n_seq, t_step, L)

    y = _out(out_a, out_b, x.reshape(b * t, d), ada_scale, ada_shift, ada_gate, lp["g_norm"],
             lp["w_gl_a"], lp["w_gl_b"], lp["w_up_a"], lp["w_up_b"], lp["w_out"],
             g_final, b, nb, tt, final_norm).reshape(b, t, d)
    new_states = (shift_t[:, 0], s_t, conv_tail[:, SUBLANES - (cw - 1):], c_t, n_t[:, :, 0, :], m_t[:, 0, :n_hb])
    return y, new_states


def _trunk(x, mods, states, layers, g_final):
    depth = len(layers)
    new = [[] for _ in states]
    for l in range(depth):
        st = tuple(s[l] for s in states)
        x, st_new = _layer(x, mods[l], st, layers[l], g_final, final_norm=(l == depth - 1))
        for lst, s in zip(new, st_new):
            lst.append(s.astype(x.dtype))
    return x, tuple(jnp.stack(lst) for lst in new)


def kernel(x_prompt, x_sample, c_prompt, c_sample, state_rwkv_shift, state_rwkv_S, state_mlstm_conv, state_mlstm_C, state_mlstm_n, state_mlstm_m, g_norm, w_ada, b_ada, w_in, mu_shift, w_decay2, w0, w_iclr2, a0, k_k, k_a, r_k, ln_w, ln_b, conv_w, conv_b, b_i, b_f, gn_w, w_up_a, w_up_b, w_out, g_final):
    depth = g_norm.shape[0]
    bp, bs = x_prompt.shape[0], x_sample.shape[0]
    d = x_prompt.shape[-1]
    da, db = w_up_a.shape[1], w_up_b.shape[1]
    n_ha, n_hb = da // HEAD_DIM, db // HEAD_DIM
    sw = mu_shift.shape[-1]
    cc = conv_w.shape[-1]
    cw = conv_w.shape[1]
    dt = x_prompt.dtype

    sizes = (sw, da, cc, db, db, n_hb, n_hb, db, d, d)
    offs = [0]
    for s in sizes:
        offs.append(offs[-1] + s)
    seg = lambda w, i: w[:, offs[i]:offs[i + 1]]

    c_all = jnp.concatenate([c_prompt, c_sample], axis=0)
    n_c = c_all.shape[0]
    c_pad = jnp.pad(c_all, ((0, (-n_c) % SUBLANES), (0, 0)))

    layers, mods_p, mods_s = [], [], []
    for l in range(depth):
        w = w_in[l]
        wg = jnp.concatenate([seg(w, 5), seg(w, 6)], axis=1)
        wg = jnp.pad(wg, ((0, 0), (0, LANES - wg.shape[1])))
        w_segs = tuple(x.astype(BF16) for x in
                       (seg(w, 0), seg(w, 1), seg(w, 2), seg(w, 3), seg(w, 4), seg(w, 7), wg))
        layers.append(dict(
            g_norm=g_norm[l], w_segs=w_segs, w_gl_a=seg(w, 8).astype(BF16), w_gl_b=seg(w, 9).astype(BF16),
            mu_shift=mu_shift[l], w_decay2=w_decay2[l].astype(BF16),
            w0=w0[l], w_iclr2=w_iclr2[l].astype(BF16), a0=a0[l], k_k=k_k[l], k_a=k_a[l], r_k=r_k[l],
            ln_w=ln_w[l], ln_b=ln_b[l], conv_w=conv_w[l], conv_b=conv_b[l], b_i=b_i[l], b_f=b_f[l],
            gn_w=gn_w[l], w_up_a=w_up_a[l].astype(BF16), w_up_b=w_up_b[l].astype(BF16),
            w_out=w_out[l].astype(BF16)))
        mod = _ada(c_pad, w_ada[l], b_ada[l])
        mods_p.append(mod[:bp])
        mods_s.append(mod[bp:bp + bs])

    prompt_states = (
        jnp.zeros((depth, bp, sw), dt),
        jnp.zeros((depth, bp, n_ha, HEAD_DIM, HEAD_DIM), dt),
        jnp.zeros((depth, bp, cw - 1, cc), dt),
        jnp.zeros((depth, bp, n_hb, HEAD_DIM, HEAD_DIM), dt),
        jnp.zeros((depth, bp, n_hb, HEAD_DIM), dt),
        jnp.zeros((depth, bp, n_hb), dt),
    )
    sample_states = (state_rwkv_shift, state_rwkv_S, state_mlstm_conv,
                     state_mlstm_C, state_mlstm_n, state_mlstm_m)
    y_p, st_p = _trunk(x_prompt, mods_p, prompt_states, layers, g_final)
    y_s, st_s = _trunk(x_sample, mods_s, sample_states, layers, g_final)
    return (y_p, y_s) + st_p + st_s
```

```python
import functools
import math

import jax
import jax.numpy as jnp
from jax import lax
from jax.experimental import pallas as pl
from jax.experimental.pallas import tpu as pltpu

F32 = jnp.float32
BF16 = jnp.bfloat16

HEAD_DIM = 64
NORM_EPS = 1e-6
RWKV_GN_EPS = 64e-5
MLSTM_GN_EPS = 1e-6
ROWS = 64
MLSTM_ROWS = 128
MIX_ROWS = 256
PROJ_ROWS = 256
INPROJ_SPLIT = 2
SUBLANES = 8
LANES = 128
TRI_BASE = 4
VMEM_LIMIT = 60 * 1024 * 1024

LOG2_E = 1.4426950408889634
NT_DIMS = (((1,), (1,)), ((), ()))
TN_DIMS = (((0,), (0,)), ((), ()))


def _mm(a, b):
    return jnp.dot(a.astype(BF16), b.astype(BF16), preferred_element_type=F32)


def _mm_nt(a, b):
    return lax.dot_general(a.astype(BF16), b.astype(BF16), NT_DIMS, preferred_element_type=F32)


def _mm_tn(a, b):
    return lax.dot_general(a.astype(BF16), b.astype(BF16), TN_DIMS, preferred_element_type=F32)


def _sigmoid(x):
    return 1.0 / (1.0 + jnp.exp2(x * -LOG2_E))


def _silu(x):
    return x * _sigmoid(x)


def _softplus(x):
    return jnp.maximum(x, 0.0) + jnp.log(1.0 + jnp.exp(-jnp.abs(x)))


def _cumsum_time(x, tri_bf16, terms=3):
    dot = functools.partial(jnp.dot, preferred_element_type=F32)
    rest = x
    total = None
    for i in range(terms):
        part = rest.astype(BF16)
        if i + 1 < terms:
            rest = rest - part.astype(F32)
        total = dot(tri_bf16, part) if total is None else total + dot(tri_bf16, part)
    return total


def _shr(x, n):
    return lax.shift_right_logical(x, jnp.full(x.shape, n, x.dtype))


def _rows_of_seq(x, n_seq, L):
    if n_seq == 1:
        return x[0]
    return jnp.broadcast_to(x, (n_seq, L, x.shape[-1])).reshape(n_seq * L, x.shape[-1])


def _last_rows(x, n_seq):
    m, n = x.shape
    L = m // n_seq
    return x.reshape(n_seq, L, n)[:, L - 1:L, :]


def _head_sums(x):
    m, n = x.shape
    parts = [jnp.broadcast_to(jnp.sum(x[:, o:o + HEAD_DIM], axis=-1, keepdims=True), (m, HEAD_DIM))
             for o in range(0, n, HEAD_DIM)]
    return jnp.concatenate(parts, axis=-1)


def _head_sums_mxu(x, ones_ref):
    hi = x.astype(BF16)
    lo = (x - hi.astype(F32)).astype(BF16)
    s = jnp.dot(jnp.concatenate([hi, lo], axis=0), ones_ref[...], preferred_element_type=F32)
    return s[:x.shape[0]] + s[x.shape[0]:]


def _cummax_time(x, L, pos):
    y = x
    sh = 1
    while sh < L:
        y = jnp.where(pos >= sh, jnp.maximum(y, pltpu.roll(y, sh, axis=0)), y)
        sh *= 2
    return y


def _tri_inverse_all(mats, row, col, L, mm=_mm):
    bs = min(TRI_BASE, L)
    sh = int(math.log2(bs))
    same = _shr(row, sh) == _shr(col, sh)
    eye = (row == col).astype(F32)
    rows = mats[0].shape[0]
    ns = [jnp.where(same, -a, 0.0) for a in mats]
    ts = [eye + n for n in ns]
    if bs > 2:
        ps = [mm(n, n) for n in ns]
        yield
        k = 2
        while 2 * k < bs:
            both = [mm(jnp.concatenate([t, p], axis=0), p) for t, p in zip(ts, ps)]
            ts = [t + x[:rows] for t, x in zip(ts, both)]
            ps = [x[rows:] for x in both]
            yield
            k *= 2
        ts = [t + mm(t, p) for t, p in zip(ts, ps)]
        yield
    size = bs
    while size < L:
        sh = int(math.log2(size))
        lower_left = (_shr(row, sh + 1) == _shr(col, sh + 1)) & (_shr(row, sh) != _shr(col, sh))
        offs = [jnp.where(lower_left, a, 0.0) for a in mats]
        tmp = [mm(t, o) for t, o in zip(ts, offs)]
        yield
        ts = [t - mm(x, t) for t, x in zip(ts, tmp)]
        yield
        size *= 2
    return ts


def _interleave(plan):
    live = [[g, n, s, w, 0] for g, n, s, w in plan]
    while live:
        item = min(live, key=lambda it: it[2] + it[3] * it[4] / it[1])
        try:
            next(item[0])
            item[4] += 1
            yield
        except StopIteration:
            live.remove(item)


def _chain(*gens):
    for g in gens:
        yield from g


def _run(gen):
    for _ in gen:
        pass


def _when(pred, fn):
    if pred is None:
        return
    if pred is True:
        fn()
    else:
        pl.when(pred)(fn)


def _ada_kernel(c_ref, w_ref, b_ref, o_ref):
    o_ref[...] = _mm(_silu(c_ref[...]), w_ref[...]) + b_ref[...]


def _ada(c, w_ada, b_ada):
    n, d = c.shape
    n3 = w_ada.shape[1]
    tn = d
    return pl.pallas_call(
        _ada_kernel,
        grid=(n3 // tn,),
        in_specs=[pl.BlockSpec((n, d), lambda j: (0, 0)),
                  pl.BlockSpec((d, tn), lambda j: (0, j)),
                  pl.BlockSpec((1, tn), lambda j: (0, j))],
        out_specs=pl.BlockSpec((n, tn), lambda j: (0, j)),
        out_shape=jax.ShapeDtypeStruct((n, n3), F32),
        name="ada",
    )(c, w_ada, b_ada.reshape(1, n3))


PROJ_COLS = 256


def _inproj_body(x_ref, sc_ref, sh_ref, g_ref, shift0_ref, conv0_ref,
                 w_shift, w_za, w_qk, w_vb, w_ob, w_zb, w_gt,
                 mu_ref, w0_ref, a0_ref, kk_ref, ka_ref, rk_ref, wd2_ref, wi2_ref, cw_ref, cb_ref,
                 kq_o, rq_o, bk_o, kkk_o, bk2_o, kk2_o, va_o, bv_o, sz_o, plast_o, shst_o,
                 q_o, k_o, vb_o, gt_o, gob_o, cst_o,
                 prev_scr, xbuf, *, chunk, lora_w, conv_w, first, live):
    nb, tt, d = x_ref.shape
    M = nb * tt
    L = chunk
    n_chunks = M // L
    da = va_o.shape[-1]
    db = vb_o.shape[-1]
    cc = xbuf.shape[-1]
    pad = SUBLANES

    def _init():
        prev_scr[...] = shift0_ref[...]
        xbuf[:, 0:pad, :] = conv0_ref[...]
    _when(first, _init)

    x = x_ref[...]
    y = x * lax.rsqrt(jnp.mean(x * x, axis=-1, keepdims=True) + NORM_EPS) * g_ref[...]
    h = (y * (1.0 + sc_ref[...]) + sh_ref[...]).reshape(M, d).astype(BF16)
    def proj(w_ref):
        n = w_ref.shape[1]
        parts = []
        for o in range(0, n, PROJ_COLS):
            parts.append(jnp.dot(h, w_ref[:, o:min(o + PROJ_COLS, n)], preferred_element_type=F32))
            yield
        return parts[0] if len(parts) == 1 else jnp.concatenate(parts, axis=-1)

    p = yield from proj(w_shift)
    row_w = lax.broadcasted_iota(jnp.int32, p.shape, 0)
    first = (row_w & (tt - 1)) == 0
    prev = jnp.where(first, _rows_of_seq(prev_scr[...], nb, tt), pltpu.roll(p, 1, axis=0))
    last = _last_rows(p, nb)
    prev_scr[...] = last

    def _shift_state():
        shst_o[...] = last
    _when(live, _shift_state)
    ps = p + mu_ref[...] * (prev - p)
    r = ps[:, 0:da]
    k = ps[:, da:2 * da]
    v = ps[:, 2 * da:3 * da]
    wl = ps[:, 3 * da:3 * da + lora_w]
    al = ps[:, 3 * da + lora_w:]

    qk_pre = yield from proj(w_qk)
    z_a = yield from proj(w_za)
    w = w0_ref[...] + _mm(jnp.tanh(wl), wd2_ref[...])
    lw = -math.exp(-0.5) * _sigmoid(w)
    a = _sigmoid(a0_ref[...] + _mm(al, wi2_ref[...]))
    yield
    v_b = yield from proj(w_vb)
    g_t = yield from proj(w_gt)
    o_b = yield from proj(w_ob)
    z_b = yield from proj(w_zb)
    kkr = k * kk_ref[...]
    kk = kkr * lax.rsqrt(jnp.maximum(_head_sums(kkr * kkr), 1e-24))
    k2 = k * (1.0 + (a - 1.0) * ka_ref[...])
    b = kk * a
    va_o[...] = v
    bv_o[...] = _head_sums(r * k2 * rk_ref[...]) * v
    sz_o[...] = _silu(z_a)

    lg = int(math.log2(L))
    row_m = lax.broadcasted_iota(jnp.int32, (M, M), 0)
    col_m = lax.broadcasted_iota(jnp.int32, (M, M), 1)
    tri = ((row_m >= col_m) & (_shr(row_m, lg) == _shr(col_m, lg))).astype(BF16)
    c = _cumsum_time(lw, tri, terms=2)
    yield
    c_last = _last_rows(c, n_chunks)
    plast_o[...] = jnp.exp(c_last)
    e_nc = jnp.exp(-c)
    e_cl = jnp.exp(_rows_of_seq(c_last, n_chunks, L) - c)
    kq_o[...] = (kk * jnp.exp(c - lw)).astype(kq_o.dtype)
    rq_o[...] = (r * jnp.exp(c)).astype(rq_o.dtype)
    bk_o[...] = (b * e_nc).astype(bk_o.dtype)
    kkk_o[...] = (k2 * e_nc).astype(kkk_o.dtype)
    bk2_o[...] = (b * e_cl).astype(bk2_o.dtype)
    kk2_o[...] = (k2 * e_cl).astype(kk2_o.dtype)

    xbuf[:, pad:pad + tt, :] = qk_pre.reshape(nb, tt, cc)
    xe = xbuf[...].reshape(nb * (tt + pad), cc)
    conv = cb_ref[...] + qk_pre * cw_ref[conv_w - 1:conv_w, :]
    for back in range(1, conv_w):
        shifted = pltpu.roll(xe, back, axis=0).reshape(nb, tt + pad, cc)[:, pad:, :].reshape(M, cc)
        conv = conv + shifted * cw_ref[conv_w - 1 - back:conv_w - back, :]
    tail = xbuf[:, tt:tt + pad, :]
    xbuf[:, 0:pad, :] = tail

    def _conv_state():
        cst_o[...] = tail
    _when(live, _conv_state)
    yield
    qk = _silu(conv)
    q_o[...] = qk[:, 0:db].astype(q_o.dtype)
    k_o[...] = (qk[:, db:] * (1.0 / math.sqrt(HEAD_DIM))).astype(k_o.dtype)
    vb_o[...] = v_b
    gt_o[...] = g_t
    gob_o[...] = _sigmoid(o_b) * _silu(z_b)


N_INPROJ_IN = 23
INPROJ_ROW_OUTS = (0, 1, 2, 3, 4, 5, 6, 7, 8, 11, 12, 13, 14, 15)
INPROJ_CHUNK_OUT = 9
INPROJ_STAGES = 28


def _inproj_kernel(*refs, chunk, lora_w, conv_w, n_split):
    ins, outs, scratch = refs[:N_INPROJ_IN], refs[N_INPROJ_IN:-2], refs[-2:]
    first = pl.program_id(1) == 0
    if n_split == 1:
        _run(_inproj_body(*refs, chunk=chunk, lora_w=lora_w, conv_w=conv_w, first=first, live=True))
        return
    tp = ins[0].shape[1] // n_split
    plan = []
    for h in range(n_split):
        rows = pl.ds(h * tp, tp)
        t_ins = [ins[0].at[:, rows]] + list(ins[1:])
        t_outs = [o.at[rows] if i in INPROJ_ROW_OUTS else o for i, o in enumerate(outs)]
        t_outs[INPROJ_CHUNK_OUT] = outs[INPROJ_CHUNK_OUT].at[pl.ds(h * (tp // chunk), tp // chunk)]
        body = _inproj_body(*t_ins, *t_outs, *scratch, chunk=chunk, lora_w=lora_w, conv_w=conv_w,
                            first=first if h == 0 else None, live=True)
        plan.append((body, INPROJ_STAGES, 0.5 * h, 1.0))
    _run(_interleave(plan))


def _inproj(x, scale, shift, g_norm, shift0, conv0p, lp, nb, tt, chunk, act_dtype):
    b, t, d = x.shape
    n_tok = b * t
    n_split = INPROJ_SPLIT if nb == 1 and t % (INPROJ_SPLIT * tt) == 0 else 1
    tb = n_split * tt
    m = nb * tb
    nj = t // tb
    sw = shift0.shape[-1]
    cc = conv0p.shape[-1]
    da, db = lp["w_up_a"].shape[0], lp["w_up_b"].shape[0]
    cw = lp["conv_w"].shape[0]
    row = lambda a: a.reshape(1, -1)
    rows = [row(lp[n]) for n in ("mu_shift", "w0", "a0", "k_k", "k_a", "r_k")]
    smalls = rows + [lp["w_decay2"], lp["w_iclr2"], lp["conv_w"], row(lp["conv_b"])]
    const = lambda i, j: (0, 0)
    tok = lambda i, j: (i * nj + j, 0)
    per_b = lambda i, j: (i, 0, 0)
    in_specs = [pl.BlockSpec((nb, tb, d), lambda i, j: (i, j, 0)),
                pl.BlockSpec((nb, 1, d), per_b),
                pl.BlockSpec((nb, 1, d), per_b),
                pl.BlockSpec((1, d), const),
                pl.BlockSpec((nb, 1, sw), per_b),
                pl.BlockSpec((nb, SUBLANES, cc), per_b)]
    in_specs += [pl.BlockSpec(w.shape, const, pipeline_mode=pl.Buffered(1)) for w in lp["w_segs"]]
    in_specs += [pl.BlockSpec(a.shape, const) for a in smalls]
    tok_out = lambda n, dt: (pl.BlockSpec((m, n), tok), jax.ShapeDtypeStruct((n_tok, n), dt))
    outs = [tok_out(da, act_dtype)] * 6 + [tok_out(da, F32)] * 3
    outs += [(pl.BlockSpec((m // chunk, 1, da), lambda i, j: (i * nj + j, 0, 0)),
              jax.ShapeDtypeStruct((n_tok // chunk, 1, da), F32)),
             (pl.BlockSpec((nb, 1, sw), per_b), jax.ShapeDtypeStruct((b, 1, sw), F32))]
    outs += [tok_out(db, act_dtype)] * 2 + [tok_out(db, F32), tok_out(LANES, F32), tok_out(db, F32)]
    outs += [(pl.BlockSpec((nb, SUBLANES, cc), per_b), jax.ShapeDtypeStruct((b, SUBLANES, cc), F32))]
    return pl.pallas_call(
        functools.partial(_inproj_kernel, chunk=chunk, lora_w=lp["w_decay2"].shape[0], conv_w=cw,
                          n_split=n_split),
        grid=(b // nb, nj),
        in_specs=in_specs,
        out_specs=[o[0] for o in outs], out_shape=[o[1] for o in outs],
        scratch_shapes=[pltpu.VMEM((nb, 1, sw), F32),
                        pltpu.VMEM((nb, tt + SUBLANES, cc), F32)],
        compiler_params=pltpu.CompilerParams(vmem_limit_bytes=VMEM_LIMIT),
        name="inproj",
    )(x, scale, shift, g_norm.reshape(1, d), shift0, conv0p, *lp["w_segs"], *smalls)


def _pair_blockdiag(y, left):
    return jnp.concatenate([jnp.where(left, y, 0.0), jnp.where(left, 0.0, y)], axis=0)


def _rwkv_body(kq_ref, rq_ref, bk_ref, kkk_ref, bk2_ref, kk2_ref, v_ref, bv_ref, sz_ref, plast_ref,
               s0_ref, lnw_ref, lnb_ref, ones_ref, oa_ref, st_ref, s_scr, *, n_seq, chunk, n_heads,
               first, last):
    M = kq_ref.shape[0]
    L = chunk
    t_step = M // n_seq
    RU = min(M, ROWS)
    PW = 2 * HEAD_DIM
    n_units = M // RU
    n_hp = n_heads // 2
    units = range(n_units)
    hps = range(n_hp)
    pairs = [(u, g) for u in units for g in hps]
    zero_blk = jnp.zeros((n_seq, HEAD_DIM, HEAD_DIM), F32)

    def _load_state():
        for g in hps:
            top = jnp.concatenate([s0_ref[:, 2 * g], zero_blk], axis=-1)
            bot = jnp.concatenate([zero_blk, s0_ref[:, 2 * g + 1]], axis=-1)
            s_scr[:, g] = jnp.concatenate([top, bot], axis=-2)
    _when(first, _load_state)

    lg = int(math.log2(L))
    row = lax.broadcasted_iota(jnp.int32, (RU, 2 * RU), 0)
    lane = lax.broadcasted_iota(jnp.int32, (RU, 2 * RU), 1)
    col = lane & (RU - 1)
    same = _shr(row, lg) == _shr(col, lg)
    incl = (row >= col) & same
    strict = (row > col) & same
    left = lax.broadcasted_iota(jnp.int32, (RU, PW), 1) < HEAD_DIM
    bd_mask = (lax.broadcasted_iota(jnp.int32, (PW, PW), 0) < HEAD_DIM) == (
        lax.broadcasted_iota(jnp.int32, (PW, PW), 1) < HEAD_DIM)
    mmp = lambda x, y: _mm(x, _pair_blockdiag(y, left))

    p_last = plast_ref[...]
    pls = [slice(g * PW, (g + 1) * PW) for g in hps]
    rus = [slice(u * RU, (u + 1) * RU) for u in units]
    blk = lambda ref, q: ref[rus[q[0]], pls[q[1]]]
    kq = {q: blk(kq_ref, q) for q in pairs}
    rq = {q: blk(rq_ref, q) for q in pairs}
    bk2 = {q: blk(bk2_ref, q) for q in pairs}
    kk2 = {q: blk(kk2_ref, q) for q in pairs}
    vs = {q: blk(v_ref, q) for q in pairs}
    qr = {q: jnp.concatenate([kq[q], rq[q]], axis=0).astype(BF16) for q in pairs}
    bkk = {q: jnp.concatenate([_pair_blockdiag(blk(bk_ref, q).astype(F32), left),
                               _pair_blockdiag(blk(kkk_ref, q).astype(F32), left)], axis=0).astype(BF16)
           for q in pairs}
    yield

    gs = {q: _mm_nt(qr[q], bkk[q]) for q in pairs}
    yield
    a_ab = {q: jnp.where(strict, gs[q][:RU, :2 * RU], 0.0) for q in pairs}
    a_ak = {q: jnp.where(strict, gs[q][:RU, 2 * RU:], 0.0) for q in pairs}
    m_rb = {q: jnp.where(incl, gs[q][RU:, :2 * RU], 0.0) for q in pairs}
    m_rk = {q: jnp.where(incl, gs[q][RU:, 2 * RU:], 0.0) for q in pairs}
    yield
    t_inv = dict(zip(pairs, (yield from _tri_inverse_all([a_ab[q] for q in pairs], row, col, L, mmp))))
    akv = {q: mmp(a_ak[q], vs[q]) for q in pairs}
    yield

    state = {}
    ys = {}
    for u in units:
        if L == RU:
            b = (u * RU) // t_step
            s0 = [state[b, g] if (b, g) in state else s_scr[b, g] for g in hps]
            ws = [_mm_nt(qr[u, g], s0[g]) for g in hps]
            w1 = [x[:RU] for x in ws]
            wr = [x[RU:] for x in ws]
        else:
            n_in = RU // L
            s0s = [[s_scr[u * n_in + i, g] for g in hps] for i in range(n_in)]
            w1, wr = [], []
            for g in hps:
                parts = [_mm_nt(jnp.concatenate([kq[u, g][i * L:(i + 1) * L], rq[u, g][i * L:(i + 1) * L]], axis=0),
                                s0s[i][g]) for i in range(n_in)]
                w1.append(jnp.concatenate([x[:L] for x in parts], axis=0))
                wr.append(jnp.concatenate([x[L:] for x in parts], axis=0))
        yield
        us = [-mmp(t_inv[u, g], w1[g] + akv[u, g]) for g in hps]
        yield
        for g in hps:
            rhs = jnp.concatenate([_pair_blockdiag(us[g], left), _pair_blockdiag(vs[u, g].astype(F32), left)], axis=0)
            ys[u, g] = wr[g] + _mm(jnp.concatenate([m_rb[u, g], m_rk[u, g]], axis=1), rhs)
        yield
        if L == RU:
            for g in hps:
                uv = jnp.concatenate([us[g], vs[u, g].astype(F32)], axis=0)
                bkk2 = jnp.concatenate([bk2[u, g], kk2[u, g]], axis=0)
                state[b, g] = s0[g] * p_last[u][:, pls[g]] + jnp.where(bd_mask, _mm_tn(uv, bkk2), 0.0)
        else:
            for g in hps:
                for i in range(n_in):
                    rows = slice(i * L, (i + 1) * L)
                    uv_i = jnp.concatenate([us[g][rows], vs[u, g][rows]], axis=0)
                    bkk2_i = jnp.concatenate([bk2[u, g][rows], kk2[u, g][rows]], axis=0)
                    state[u * n_in + i, g] = (s0s[i][g] * p_last[u * n_in + i][:, pls[g]]
                                              + jnp.where(bd_mask, _mm_tn(uv_i, bkk2_i), 0.0))
        yield
    for (b, g), s_new in state.items():
        s_scr[b, g] = s_new

    rows_out = [jnp.concatenate([ys[u, g] for g in hps], axis=-1) for u in units]
    y = rows_out[0] if n_units == 1 else jnp.concatenate(rows_out, axis=0)
    yc = y - _head_sums_mxu(y, ones_ref) * (1.0 / HEAD_DIM)
    yield
    var = _head_sums_mxu(yc * yc, ones_ref) * (1.0 / HEAD_DIM)
    yn = yc * lax.rsqrt(var + RWKV_GN_EPS)
    oa_ref[...] = ((yn * lnw_ref[...] + lnb_ref[...] + bv_ref[...]) * sz_ref[...]).astype(oa_ref.dtype)

    def _store_state():
        for g in hps:
            st_ref[:, 2 * g] = s_scr[:, g, 0:HEAD_DIM, 0:HEAD_DIM]
            st_ref[:, 2 * g + 1] = s_scr[:, g, HEAD_DIM:, HEAD_DIM:]
    _when(last, _store_state)


def _mlstm_body(q_ref, k_ref, v_ref, g_ref, gob_ref, c0_ref, n0_ref, m0_ref, gb_ref, gnw_ref, ones_ref,
                ob_ref, ct_ref, nt_ref, mt_ref, cn_scr, m_scr, *, n_seq, n_heads, first, last):
    M = q_ref.shape[0]
    L = M // n_seq
    heads = range(n_heads)
    seqs = range(n_seq)

    def _load_state():
        cn_scr[:, :, 0:HEAD_DIM, :] = c0_ref[...]
        cn_scr[:, :, HEAD_DIM:, :] = n0_ref[...]
        m_scr[...] = m0_ref[...]
    _when(first, _load_state)

    row = lax.broadcasted_iota(jnp.int32, (M, M), 0)
    col = lax.broadcasted_iota(jnp.int32, (M, M), 1)
    same_seq = _shr(row, int(math.log2(L))) == _shr(col, int(math.log2(L)))
    incl = (row >= col) & same_seq

    g = g_ref[...] + gb_ref[...]
    bcum = pltpu.roll(_cumsum_time(-_softplus(-g), incl.astype(BF16)), LANES - n_heads, axis=1)
    m_prev = m_scr[...]
    m_prev_rows = _rows_of_seq(m_prev, n_seq, L)
    x_all = g - bcum
    pos = lax.broadcasted_iota(jnp.int32, (M, LANES), 0) & (L - 1)
    m_all = bcum + jnp.maximum(_cummax_time(x_all, L, pos), m_prev_rows)
    bm_all = bcum - m_all
    w_in_all = jnp.exp(bcum + m_prev_rows - m_all)
    e_negm_all = jnp.exp(-m_all)
    m_new = _last_rows(m_all, n_seq)
    b_last = _last_rows(bcum, n_seq)
    ws_all = jnp.exp(_rows_of_seq(b_last - m_new, n_seq, L) + x_all)
    dec_all = jnp.exp(b_last + m_prev - m_new)
    m_scr[...] = m_new
    x_t = jnp.transpose(x_all)
    yield

    ones = jnp.ones((M, SUBLANES), F32)
    sls = [slice(h * HEAD_DIM, (h + 1) * HEAD_DIM) for h in heads]
    qs = [q_ref[:, sl] for sl in sls]
    ks = [k_ref[:, sl] for sl in sls]
    v1 = [jnp.concatenate([v_ref[:, sl], ones], axis=-1) for sl in sls]
    cn = [[cn_scr[b, h] for h in heads] for b in seqs]

    qk_t = [_mm_nt(qs[h], ks[h]) for h in heads]
    if n_seq == 1:
        qc = [_mm_nt(qs[h], cn[0][h]) for h in heads]
    else:
        qc = [jnp.concatenate([_mm_nt(qs[h][b * L:(b + 1) * L], cn[b][h]) for b in seqs], axis=0) for h in heads]
    yield
    w_ts = [jnp.where(incl, jnp.exp(bm_all[:, h:h + 1] + x_t[h:h + 1, :]), 0.0) for h in heads]
    yield
    s = [qk_t[h] * w_ts[h] for h in heads]
    numden = [_mm(s[h], v1[h]) + w_in_all[:, h:h + 1] * qc[h] for h in heads]
    yield
    den = [jnp.maximum(jnp.abs(numden[h][:, HEAD_DIM:HEAD_DIM + 1]), e_negm_all[:, h:h + 1]) for h in heads]
    hh = [numden[h][:, 0:HEAD_DIM] / den[h] for h in heads]
    hh_all = jnp.concatenate(hh, axis=-1)
    ssq = _head_sums_mxu(hh_all * hh_all, ones_ref) * (1.0 / HEAD_DIM)

    yield
    v1w = [v1[h] * ws_all[:, h:h + 1] for h in heads]
    for h in heads:
        for b in seqs:
            rows = slice(b * L, (b + 1) * L)
            cn_scr[b, h] = dec_all[b][:, h:h + 1] * cn[b][h] + _mm_tn(v1w[h][rows], ks[h][rows])
    yield

    hb = hh_all * lax.rsqrt(ssq + MLSTM_GN_EPS)
    ob_ref[...] = (hb * gnw_ref[...] * gob_ref[...]).astype(ob_ref.dtype)

    def _store_state():
        ct_ref[...] = cn_scr[:, :, 0:HEAD_DIM, :]
        nt_ref[...] = cn_scr[:, :, HEAD_DIM:, :]
        mt_ref[...] = m_scr[...]
    _when(last, _store_state)


N_RWKV_IN, N_RWKV_OUT, N_MLSTM_IN, N_MLSTM_OUT = 14, 2, 11, 4
N_MLSTM_ROW_IN = 5
RWKV_STAGES, MLSTM_STAGES = 24, 8
MLSTM_START = 0.0


def _mixers_kernel(*refs, n_seq, chunk, n_heads_a, n_heads_b):
    it = iter(refs)
    take = lambda n: [next(it) for _ in range(n)]
    r_in, m_in = take(N_RWKV_IN), take(N_MLSTM_IN)
    r_out, m_out = take(N_RWKV_OUT), take(N_MLSTM_OUT)
    s_scr, cn_scr, m_scr = take(3)
    first = pl.program_id(1) == 0
    last = pl.program_id(1) == pl.num_programs(1) - 1
    rwkv = _rwkv_body(*r_in, *r_out, s_scr, n_seq=n_seq, chunk=chunk, n_heads=n_heads_a, first=first, last=last)
    M = m_in[0].shape[0]
    n_sub = M // MLSTM_ROWS if n_seq == 1 and M > MLSTM_ROWS else 1
    mlstm = []
    for h in range(n_sub):
        rows = (lambda r: r.at[pl.ds(h * MLSTM_ROWS, MLSTM_ROWS)]) if n_sub > 1 else (lambda r: r)
        mlstm.append(_mlstm_body(
            *[rows(r) for r in m_in[:N_MLSTM_ROW_IN]], *m_in[N_MLSTM_ROW_IN:], rows(m_out[0]), *m_out[1:],
            cn_scr, m_scr,
            n_seq=n_seq, n_heads=n_heads_b,
            first=first if h == 0 else None, last=last if h == n_sub - 1 else None))
    _run(_interleave([(rwkv, RWKV_STAGES * max(1, M // MLSTM_ROWS), 0.0, 1.0),
                      (_chain(*mlstm), MLSTM_STAGES * n_sub, MLSTM_START, 1.0 - MLSTM_START)]))


def _mixers(acts, v_a, bv, sz, p_last, s0, ln_w, ln_b, q, k, v_b, gates, gob, c0, n0, m0, gate_b, gn_w,
            n_seq, t_step, chunk):
    b = s0.shape[0]
    n_ha, n_hb = s0.shape[1], c0.shape[1]
    n_tok, da = v_a.shape
    db = v_b.shape[1]
    m = n_seq * t_step
    nj = n_tok // (b * t_step)
    tok = lambda i, j: (i * nj + j, 0)
    per_b3 = lambda i, j: (i, 0, 0)
    per_b4 = lambda i, j: (i, 0, 0, 0)
    const2 = lambda i, j: (0, 0)
    s_spec = pl.BlockSpec((n_seq, n_ha, HEAD_DIM, HEAD_DIM), per_b4)
    c_spec = pl.BlockSpec((n_seq, n_hb, HEAD_DIM, HEAD_DIM), per_b4)
    n_spec = pl.BlockSpec((n_seq, n_hb, SUBLANES, HEAD_DIM), per_b4)
    m_spec = pl.BlockSpec((n_seq, 1, LANES), per_b3)
    row_a, row_b = pl.BlockSpec((m, da), tok), pl.BlockSpec((m, db), tok)
    in_specs = [row_a] * 9
    ones_a = jnp.kron(jnp.eye(n_ha, dtype=F32), jnp.ones((HEAD_DIM, HEAD_DIM), F32)).astype(BF16)
    ones_b = jnp.kron(jnp.eye(n_hb, dtype=F32), jnp.ones((HEAD_DIM, HEAD_DIM), F32)).astype(BF16)
    in_specs += [pl.BlockSpec((m // chunk, 1, da), lambda i, j: (i * nj + j, 0, 0)), s_spec,
                 pl.BlockSpec(ln_w.shape, const2), pl.BlockSpec(ln_b.shape, const2),
                 pl.BlockSpec(ones_a.shape, const2)]
    in_specs += [row_b, row_b, row_b, pl.BlockSpec((m, LANES), tok), row_b, c_spec, n_spec, m_spec,
                 pl.BlockSpec(gate_b.shape, const2), pl.BlockSpec(gn_w.shape, const2),
                 pl.BlockSpec(ones_b.shape, const2)]
    return pl.pallas_call(
        functools.partial(_mixers_kernel, n_seq=n_seq, chunk=chunk, n_heads_a=n_ha, n_heads_b=n_hb),
        grid=(b // n_seq, nj),
        in_specs=in_specs,
        out_specs=[row_a, s_spec, row_b, c_spec, n_spec, m_spec],
        out_shape=[jax.ShapeDtypeStruct((n_tok, da), BF16), jax.ShapeDtypeStruct(s0.shape, F32),
                   jax.ShapeDtypeStruct((n_tok, db), BF16), jax.ShapeDtypeStruct(c0.shape, F32),
                   jax.ShapeDtypeStruct(n0.shape, F32), jax.ShapeDtypeStruct(m0.shape, F32)],
        scratch_shapes=[pltpu.VMEM((n_seq, n_ha // 2, 2 * HEAD_DIM, 2 * HEAD_DIM), F32),
                        pltpu.VMEM((n_seq, n_hb, HEAD_DIM + SUBLANES, HEAD_DIM), F32),
                        pltpu.VMEM((n_seq, 1, LANES), F32)],
        compiler_params=pltpu.CompilerParams(vmem_limit_bytes=VMEM_LIMIT),
        name="mixers",
    )(*acts, v_a, bv, sz, p_last, s0, ln_w, ln_b, ones_a, q, k, v_b, gates, gob, c0, n0, m0, gate_b, gn_w, ones_b)


def _out_kernel(oa_ref, ob_ref, x_ref, sc_ref, sh_ref, gate_ref, g_ref, wga_ref, wgb_ref, wua_ref, wub_ref, wo_ref,
                gf_ref, y_ref, *, nb, final_norm):
    m = x_ref.shape[0]
    rows = lambda ref: _rows_of_seq(ref[...], nb, m // nb)
    x = x_ref[...]
    h = x * lax.rsqrt(jnp.mean(x * x, axis=-1, keepdims=True) + NORM_EPS) * g_ref[...]
    h = (h * (1.0 + rows(sc_ref)) + rows(sh_ref)).astype(BF16)
    gl_a = jnp.dot(h, wga_ref[...], preferred_element_type=F32)
    gl_b = jnp.dot(h, wgb_ref[...], preferred_element_type=F32)
    ua = jnp.dot(oa_ref[...], wua_ref[...], preferred_element_type=F32)
    ub = jnp.dot(ob_ref[...], wub_ref[...], preferred_element_type=F32)
    merged = _sigmoid(gl_a) * ua + _sigmoid(gl_b) * ub
    mo = jnp.dot(merged.astype(BF16), wo_ref[...], preferred_element_type=F32)
    xn = x + rows(gate_ref) * mo
    if final_norm:
        xn = xn * lax.rsqrt(jnp.mean(xn * xn, axis=-1, keepdims=True) + NORM_EPS) * gf_ref[...]
    y_ref[...] = xn


def _out(out_a, out_b, x, scale, shift, gate, g_norm, w_gl_a, w_gl_b, w_up_a, w_up_b, w_out, g_final,
         b, nb, tt, final_norm):
    n_tok, d = x.shape
    nj = n_tok // (b * tt)
    m = nb * tt
    tok = lambda i, j: (i * nj + j, 0)
    const2 = lambda i, j: (0, 0)
    per_b = lambda i, j: (i, 0, 0)
    wspec = lambda w: pl.BlockSpec(w.shape, const2, pipeline_mode=pl.Buffered(1))
    in_specs = [pl.BlockSpec((m, out_a.shape[-1]), tok),
                pl.BlockSpec((m, out_b.shape[-1]), tok),
                pl.BlockSpec((m, d), tok),
                pl.BlockSpec((nb, 1, d), per_b),
                pl.BlockSpec((nb, 1, d), per_b),
                pl.BlockSpec((nb, 1, d), per_b),
                pl.BlockSpec((1, d), const2),
                wspec(w_gl_a), wspec(w_gl_b), wspec(w_up_a), wspec(w_up_b), wspec(w_out),
                pl.BlockSpec((1, d), const2)]
    return pl.pallas_call(
        functools.partial(_out_kernel, nb=nb, final_norm=final_norm),
        grid=(b // nb, nj),
        in_specs=in_specs,
        out_specs=pl.BlockSpec((m, d), tok),
        out_shape=jax.ShapeDtypeStruct((n_tok, d), F32),
        compiler_params=pltpu.CompilerParams(vmem_limit_bytes=VMEM_LIMIT),
        name="merge_out",
    )(out_a, out_b, x, scale, shift, gate, g_norm.reshape(1, d), w_gl_a, w_gl_b, w_up_a, w_up_b, w_out,
      g_final.reshape(1, d))


def _token_tiling(b, t, target):
    if t >= target:
        return 1, math.gcd(t, target)
    return math.gcd(b, target // t), t


def _layer(x, mod, states, lp, g_final, final_norm):
    b, t, d = x.shape
    shift0, s0, conv0, c0, n0, m0 = states
    n_hb = c0.shape[1]
    cw = lp["conv_w"].shape[0]
    row = lambda a: a.reshape(1, -1)

    ada_shift, ada_scale, ada_gate = (mod[:, None, i * d:(i + 1) * d] for i in range(3))
    nb, tt = _token_tiling(b, t, PROJ_ROWS)
    L = math.gcd(t, ROWS)
    act_dtype = BF16 if L % 16 == 0 else F32
    conv0p = jnp.pad(conv0, ((0, 0), (SUBLANES - (cw - 1), 0), (0, 0)))
    n0p = jnp.broadcast_to(n0[:, :, None, :], (b, n_hb, SUBLANES, HEAD_DIM))
    m0p = jnp.pad(m0, ((0, 0), (0, LANES - n_hb)))[:, None, :]
    gate_b = jnp.zeros((1, LANES), F32).at[0, 0:n_hb].set(lp["b_i"]).at[0, n_hb:2 * n_hb].set(lp["b_f"])
    n_seq, t_step = _token_tiling(b, t, MIX_ROWS if t >= MIX_ROWS else MLSTM_ROWS)
    (kq, rq, bk, kkk, bk2, kk2, v_a, bv, sz, p_last, shift_t,
     q, k, v_b, gates, gob, conv_tail) = _inproj(
        x, ada_scale, ada_shift, lp["g_norm"], shift0[:, None, :], conv0p, lp, nb, tt, L, act_dtype)
    out_a, s_t, out_b, c_t, n_t, m_t = _mixers(
        (kq, rq, bk, kkk, bk2, kk2), v_a, bv, sz, p_last, s0, row(lp["ln_w"]), row(lp["ln_b"]),
        q, k, v_b, gates, gob, c0, n0p, m0p, gate_b, row(lp["gn_w"]), n_seq, t_step, L)

    y = _out(out_a, out_b, x.reshape(b * t, d), ada_scale, ada_shift, ada_gate, lp["g_norm"],
             lp["w_gl_a"], lp["w_gl_b"], lp["w_up_a"], lp["w_up_b"], lp["w_out"],
             g_final, b, nb, tt, final_norm).reshape(b, t, d)
    new_states = (shift_t[:, 0], s_t, conv_tail[:, SUBLANES - (cw - 1):], c_t, n_t[:, :, 0, :], m_t[:, 0, :n_hb])
    return y, new_states


def _trunk(x, mods, states, layers, g_final):
    depth = len(layers)
    new = [[] for _ in states]
    for l in range(depth):
        st = tuple(s[l] for s in states)
        x, st_new = _layer(x, mods[l], st, layers[l], g_final, final_norm=(l == depth - 1))
        for lst, s in zip(new, st_new):
            lst.append(s.astype(x.dtype))
    return x, tuple(jnp.stack(lst) for lst in new)


def kernel(x_prompt, x_sample, c_prompt, c_sample, state_rwkv_shift, state_rwkv_S, state_mlstm_conv, state_mlstm_C, state_mlstm_n, state_mlstm_m, g_norm, w_ada, b_ada, w_in, mu_shift, w_decay2, w0, w_iclr2, a0, k_k, k_a, r_k, ln_w, ln_b, conv_w, conv_b, b_i, b_f, gn_w, w_up_a, w_up_b, w_out, g_final):
    depth = g_norm.shape[0]
    bp, bs = x_prompt.shape[0], x_sample.shape[0]
    d = x_prompt.shape[-1]
    da, db = w_up_a.shape[1], w_up_b.shape[1]
    n_ha, n_hb = da // HEAD_DIM, db // HEAD_DIM
    sw = mu_shift.shape[-1]
    cc = conv_w.shape[-1]
    cw = conv_w.shape[1]
    dt = x_prompt.dtype

    sizes = (sw, da, cc, db, db, n_hb, n_hb, db, d, d)
    offs = [0]
    for s in sizes:
        offs.append(offs[-1] + s)
    seg = lambda w, i: w[:, offs[i]:offs[i + 1]]

    c_all = jnp.concatenate([c_prompt, c_sample], axis=0)
    n_c = c_all.shape[0]
    c_pad = jnp.pad(c_all, ((0, (-n_c) % SUBLANES), (0, 0)))

    layers, mods_p, mods_s = [], [], []
    for l in range(depth):
        w = w_in[l]
        wg = jnp.concatenate([seg(w, 5), seg(w, 6)], axis=1)
        wg = jnp.pad(wg, ((0, 0), (0, LANES - wg.shape[1])))
        w_segs = tuple(x.astype(BF16) for x in
                       (seg(w, 0), seg(w, 1), seg(w, 2), seg(w, 3), seg(w, 4), seg(w, 7), wg))
        layers.append(dict(
            g_norm=g_norm[l], w_segs=w_segs, w_gl_a=seg(w, 8).astype(BF16), w_gl_b=seg(w, 9).astype(BF16),
            mu_shift=mu_shift[l], w_decay2=w_decay2[l].astype(BF16),
            w0=w0[l], w_iclr2=w_iclr2[l].astype(BF16), a0=a0[l], k_k=k_k[l], k_a=k_a[l], r_k=r_k[l],
            ln_w=ln_w[l], ln_b=ln_b[l], conv_w=conv_w[l], conv_b=conv_b[l], b_i=b_i[l], b_f=b_f[l],
            gn_w=gn_w[l], w_up_a=w_up_a[l].astype(BF16), w_up_b=w_up_b[l].astype(BF16),
            w_out=w_out[l].astype(BF16)))
        mod = _ada(c_pad, w_ada[l], b_ada[l])
        mods_p.append(mod[:bp])
        mods_s.append(mod[bp:bp + bs])

    prompt_states = (
        jnp.zeros((depth, bp, sw), dt),
        jnp.zeros((depth, bp, n_ha, HEAD_DIM, HEAD_DIM), dt),
        jnp.zeros((depth, bp, cw - 1, cc), dt),
        jnp.zeros((depth, bp, n_hb, HEAD_DIM, HEAD_DIM), dt),
        jnp.zeros((depth, bp, n_hb, HEAD_DIM), dt),
        jnp.zeros((depth, bp, n_hb), dt),
    )
    sample_states = (state_rwkv_shift, state_rwkv_S, state_mlstm_conv,
                     state_mlstm_C, state_mlstm_n, state_mlstm_m)
    y_p, st_p = _trunk(x_prompt, mods_p, prompt_states, layers, g_final)
    y_s, st_s = _trunk(x_sample, mods_s, sample_states, layers, g_final)
    return (y_p, y_s) + st_p + st_s
```

```python
import functools
import math

import jax
import jax.numpy as jnp
from jax import lax
from jax.experimental import pallas as pl
from jax.experimental.pallas import tpu as pltpu

F32 = jnp.float32
BF16 = jnp.bfloat16

HEAD_DIM = 64
NORM_EPS = 1e-6
RWKV_GN_EPS = 64e-5
MLSTM_GN_EPS = 1e-6
ROWS = 64
MLSTM_ROWS = 128
MIX_ROWS = 512
PROJ_ROWS = 256
INPROJ_SPLIT = 2
SUBLANES = 8
LANES = 128
TRI_BASE = 4
VMEM_LIMIT = 60 * 1024 * 1024

LOG2_E = 1.4426950408889634
NT_DIMS = (((1,), (1,)), ((), ()))
TN_DIMS = (((0,), (0,)), ((), ()))


def _mm(a, b):
    return jnp.dot(a.astype(BF16), b.astype(BF16), preferred_element_type=F32)


def _mm_nt(a, b):
    return lax.dot_general(a.astype(BF16), b.astype(BF16), NT_DIMS, preferred_element_type=F32)


def _mm_tn(a, b):
    return lax.dot_general(a.astype(BF16), b.astype(BF16), TN_DIMS, preferred_element_type=F32)


def _sigmoid(x):
    return 1.0 / (1.0 + jnp.exp2(x * -LOG2_E))


def _silu(x):
    return x * _sigmoid(x)


def _softplus(x):
    return jnp.maximum(x, 0.0) + jnp.log(1.0 + jnp.exp(-jnp.abs(x)))


def _cumsum_time(x, tri_bf16, terms=3):
    dot = functools.partial(jnp.dot, preferred_element_type=F32)
    rest = x
    total = None
    for i in range(terms):
        part = rest.astype(BF16)
        if i + 1 < terms:
            rest = rest - part.astype(F32)
        total = dot(tri_bf16, part) if total is None else total + dot(tri_bf16, part)
    return total


def _shr(x, n):
    return lax.shift_right_logical(x, jnp.full(x.shape, n, x.dtype))


def _rows_of_seq(x, n_seq, L):
    if n_seq == 1:
        return x[0]
    return jnp.broadcast_to(x, (n_seq, L, x.shape[-1])).reshape(n_seq * L, x.shape[-1])


def _last_rows(x, n_seq):
    m, n = x.shape
    L = m // n_seq
    return x.reshape(n_seq, L, n)[:, L - 1:L, :]


def _head_sums(x):
    m, n = x.shape
    parts = [jnp.broadcast_to(jnp.sum(x[:, o:o + HEAD_DIM], axis=-1, keepdims=True), (m, HEAD_DIM))
             for o in range(0, n, HEAD_DIM)]
    return jnp.concatenate(parts, axis=-1)


def _head_sums_mxu(x, ones_ref):
    hi = x.astype(BF16)
    lo = (x - hi.astype(F32)).astype(BF16)
    s = jnp.dot(jnp.concatenate([hi, lo], axis=0), ones_ref[...], preferred_element_type=F32)
    return s[:x.shape[0]] + s[x.shape[0]:]


def _cummax_time(x, L, pos):
    y = x
    sh = 1
    while sh < L:
        y = jnp.where(pos >= sh, jnp.maximum(y, pltpu.roll(y, sh, axis=0)), y)
        sh *= 2
    return y


def _tri_inverse_all(mats, row, col, L, mm=_mm):
    bs = min(TRI_BASE, L)
    sh = int(math.log2(bs))
    same = _shr(row, sh) == _shr(col, sh)
    eye = (row == col).astype(F32)
    rows = mats[0].shape[0]
    ns = [jnp.where(same, -a, 0.0) for a in mats]
    ts = [eye + n for n in ns]
    if bs > 2:
        ps = [mm(n, n) for n in ns]
        yield
        k = 2
        while 2 * k < bs:
            both = [mm(jnp.concatenate([t, p], axis=0), p) for t, p in zip(ts, ps)]
            ts = [t + x[:rows] for t, x in zip(ts, both)]
            ps = [x[rows:] for x in both]
            yield
            k *= 2
        ts = [t + mm(t, p) for t, p in zip(ts, ps)]
        yield
    size = bs
    while size < L:
        sh = int(math.log2(size))
        lower_left = (_shr(row, sh + 1) == _shr(col, sh + 1)) & (_shr(row, sh) != _shr(col, sh))
        offs = [jnp.where(lower_left, a, 0.0) for a in mats]
        tmp = [mm(t, o) for t, o in zip(ts, offs)]
        yield
        ts = [t - mm(x, t) for t, x in zip(ts, tmp)]
        yield
        size *= 2
    return ts


def _interleave(plan):
    live = [[g, n, s, w, 0] for g, n, s, w in plan]
    while live:
        item = min(live, key=lambda it: it[2] + it[3] * it[4] / it[1])
        try:
            next(item[0])
            item[4] += 1
            yield
        except StopIteration:
            live.remove(item)


def _chain(*gens):
    for g in gens:
        yield from g


def _run(gen):
    for _ in gen:
        pass


def _when(pred, fn):
    if pred is None:
        return
    if pred is True:
        fn()
    else:
        pl.when(pred)(fn)


def _ada_kernel(c_ref, w_ref, b_ref, o_ref):
    o_ref[...] = _mm(_silu(c_ref[...]), w_ref[...]) + b_ref[...]


def _ada(c, w_ada, b_ada):
    n, d = c.shape
    n3 = w_ada.shape[1]
    tn = d
    return pl.pallas_call(
        _ada_kernel,
        grid=(n3 // tn,),
        in_specs=[pl.BlockSpec((n, d), lambda j: (0, 0)),
                  pl.BlockSpec((d, tn), lambda j: (0, j)),
                  pl.BlockSpec((1, tn), lambda j: (0, j))],
        out_specs=pl.BlockSpec((n, tn), lambda j: (0, j)),
        out_shape=jax.ShapeDtypeStruct((n, n3), F32),
        name="ada",
    )(c, w_ada, b_ada.reshape(1, n3))


PROJ_COLS = 256


def _inproj_body(x_ref, sc_ref, sh_ref, g_ref, shift0_ref, conv0_ref,
                 w_shift, w_za, w_qk, w_vb, w_ob, w_zb, w_gt,
                 mu_ref, w0_ref, a0_ref, kk_ref, ka_ref, rk_ref, wd2_ref, wi2_ref, cw_ref, cb_ref,
                 kq_o, rq_o, bk_o, kkk_o, bk2_o, kk2_o, va_o, bv_o, sz_o, plast_o, shst_o,
                 q_o, k_o, vb_o, gt_o, gob_o, cst_o,
                 prev_scr, xbuf, *, chunk, lora_w, conv_w, first, live):
    nb, tt, d = x_ref.shape
    M = nb * tt
    L = chunk
    n_chunks = M // L
    da = va_o.shape[-1]
    db = vb_o.shape[-1]
    cc = xbuf.shape[-1]
    pad = SUBLANES

    def _init():
        prev_scr[...] = shift0_ref[...]
        xbuf[:, 0:pad, :] = conv0_ref[...]
    _when(first, _init)

    x = x_ref[...]
    y = x * lax.rsqrt(jnp.mean(x * x, axis=-1, keepdims=True) + NORM_EPS) * g_ref[...]
    h = (y * (1.0 + sc_ref[...]) + sh_ref[...]).reshape(M, d).astype(BF16)
    def proj(w_ref):
        n = w_ref.shape[1]
        parts = []
        for o in range(0, n, PROJ_COLS):
            parts.append(jnp.dot(h, w_ref[:, o:min(o + PROJ_COLS, n)], preferred_element_type=F32))
            yield
        return parts[0] if len(parts) == 1 else jnp.concatenate(parts, axis=-1)

    p = yield from proj(w_shift)
    row_w = lax.broadcasted_iota(jnp.int32, p.shape, 0)
    first = (row_w & (tt - 1)) == 0
    prev = jnp.where(first, _rows_of_seq(prev_scr[...], nb, tt), pltpu.roll(p, 1, axis=0))
    last = _last_rows(p, nb)
    prev_scr[...] = last

    def _shift_state():
        shst_o[...] = last
    _when(live, _shift_state)
    ps = p + mu_ref[...] * (prev - p)
    r = ps[:, 0:da]
    k = ps[:, da:2 * da]
    v = ps[:, 2 * da:3 * da]
    wl = ps[:, 3 * da:3 * da + lora_w]
    al = ps[:, 3 * da + lora_w:]

    qk_pre = yield from proj(w_qk)
    z_a = yield from proj(w_za)
    w = w0_ref[...] + _mm(jnp.tanh(wl), wd2_ref[...])
    lw = -math.exp(-0.5) * _sigmoid(w)
    a = _sigmoid(a0_ref[...] + _mm(al, wi2_ref[...]))
    yield
    v_b = yield from proj(w_vb)
    g_t = yield from proj(w_gt)
    o_b = yield from proj(w_ob)
    z_b = yield from proj(w_zb)
    kkr = k * kk_ref[...]
    kk = kkr * lax.rsqrt(jnp.maximum(_head_sums(kkr * kkr), 1e-24))
    k2 = k * (1.0 + (a - 1.0) * ka_ref[...])
    b = kk * a
    va_o[...] = v
    bv_o[...] = _head_sums(r * k2 * rk_ref[...]) * v
    sz_o[...] = _silu(z_a)

    lg = int(math.log2(L))
    row_m = lax.broadcasted_iota(jnp.int32, (M, M), 0)
    col_m = lax.broadcasted_iota(jnp.int32, (M, M), 1)
    tri = ((row_m >= col_m) & (_shr(row_m, lg) == _shr(col_m, lg))).astype(BF16)
    c = _cumsum_time(lw, tri, terms=2)
    yield
    c_last = _last_rows(c, n_chunks)
    plast_o[...] = jnp.exp(c_last)
    e_nc = jnp.exp(-c)
    e_cl = jnp.exp(_rows_of_seq(c_last, n_chunks, L) - c)
    kq_o[...] = (kk * jnp.exp(c - lw)).astype(kq_o.dtype)
    rq_o[...] = (r * jnp.exp(c)).astype(rq_o.dtype)
    bk_o[...] = (b * e_nc).astype(bk_o.dtype)
    kkk_o[...] = (k2 * e_nc).astype(kkk_o.dtype)
    bk2_o[...] = (b * e_cl).astype(bk2_o.dtype)
    kk2_o[...] = (k2 * e_cl).astype(kk2_o.dtype)

    xbuf[:, pad:pad + tt, :] = qk_pre.reshape(nb, tt, cc)
    xe = xbuf[...].reshape(nb * (tt + pad), cc)
    conv = cb_ref[...] + qk_pre * cw_ref[conv_w - 1:conv_w, :]
    for back in range(1, conv_w):
        shifted = pltpu.roll(xe, back, axis=0).reshape(nb, tt + pad, cc)[:, pad:, :].reshape(M, cc)
        conv = conv + shifted * cw_ref[conv_w - 1 - back:conv_w - back, :]
    tail = xbuf[:, tt:tt + pad, :]
    xbuf[:, 0:pad, :] = tail

    def _conv_state():
        cst_o[...] = tail
    _when(live, _conv_state)
    yield
    qk = _silu(conv)
    q_o[...] = qk[:, 0:db].astype(q_o.dtype)
    k_o[...] = (qk[:, db:] * (1.0 / math.sqrt(HEAD_DIM))).astype(k_o.dtype)
    vb_o[...] = v_b
    gt_o[...] = g_t
    gob_o[...] = _sigmoid(o_b) * _silu(z_b)


N_INPROJ_IN = 23
INPROJ_ROW_OUTS = (0, 1, 2, 3, 4, 5, 6, 7, 8, 11, 12, 13, 14, 15)
INPROJ_CHUNK_OUT = 9
INPROJ_STAGES = 28


def _inproj_kernel(*refs, chunk, lora_w, conv_w, n_split):
    ins, outs, scratch = refs[:N_INPROJ_IN], refs[N_INPROJ_IN:-2], refs[-2:]
    first = pl.program_id(1) == 0
    if n_split == 1:
        _run(_inproj_body(*refs, chunk=chunk, lora_w=lora_w, conv_w=conv_w, first=first, live=True))
        return
    tp = ins[0].shape[1] // n_split
    plan = []
    for h in range(n_split):
        rows = pl.ds(h * tp, tp)
        t_ins = [ins[0].at[:, rows]] + list(ins[1:])
        t_outs = [o.at[rows] if i in INPROJ_ROW_OUTS else o for i, o in enumerate(outs)]
        t_outs[INPROJ_CHUNK_OUT] = outs[INPROJ_CHUNK_OUT].at[pl.ds(h * (tp // chunk), tp // chunk)]
        body = _inproj_body(*t_ins, *t_outs, *scratch, chunk=chunk, lora_w=lora_w, conv_w=conv_w,
                            first=first if h == 0 else None, live=True)
        plan.append((body, INPROJ_STAGES, 0.5 * h, 1.0))
    _run(_interleave(plan))


def _inproj(x, scale, shift, g_norm, shift0, conv0p, lp, nb, tt, chunk, act_dtype):
    b, t, d = x.shape
    n_tok = b * t
    n_split = INPROJ_SPLIT if nb == 1 and t % (INPROJ_SPLIT * tt) == 0 else 1
    tb = n_split * tt
    m = nb * tb
    nj = t // tb
    sw = shift0.shape[-1]
    cc = conv0p.shape[-1]
    da, db = lp["w_up_a"].shape[0], lp["w_up_b"].shape[0]
    cw = lp["conv_w"].shape[0]
    row = lambda a: a.reshape(1, -1)
    rows = [row(lp[n]) for n in ("mu_shift", "w0", "a0", "k_k", "k_a", "r_k")]
    smalls = rows + [lp["w_decay2"], lp["w_iclr2"], lp["conv_w"], row(lp["conv_b"])]
    const = lambda i, j: (0, 0)
    tok = lambda i, j: (i * nj + j, 0)
    per_b = lambda i, j: (i, 0, 0)
    in_specs = [pl.BlockSpec((nb, tb, d), lambda i, j: (i, j, 0)),
                pl.BlockSpec((nb, 1, d), per_b),
                pl.BlockSpec((nb, 1, d), per_b),
                pl.BlockSpec((1, d), const),
                pl.BlockSpec((nb, 1, sw), per_b),
                pl.BlockSpec((nb, SUBLANES, cc), per_b)]
    in_specs += [pl.BlockSpec(w.shape, const, pipeline_mode=pl.Buffered(1)) for w in lp["w_segs"]]
    in_specs += [pl.BlockSpec(a.shape, const) for a in smalls]
    tok_out = lambda n, dt: (pl.BlockSpec((m, n), tok), jax.ShapeDtypeStruct((n_tok, n), dt))
    outs = [tok_out(da, act_dtype)] * 6 + [tok_out(da, F32)] * 3
    outs += [(pl.BlockSpec((m // chunk, 1, da), lambda i, j: (i * nj + j, 0, 0)),
              jax.ShapeDtypeStruct((n_tok // chunk, 1, da), F32)),
             (pl.BlockSpec((nb, 1, sw), per_b), jax.ShapeDtypeStruct((b, 1, sw), F32))]
    outs += [tok_out(db, act_dtype)] * 2 + [tok_out(db, F32), tok_out(LANES, F32), tok_out(db, F32)]
    outs += [(pl.BlockSpec((nb, SUBLANES, cc), per_b), jax.ShapeDtypeStruct((b, SUBLANES, cc), F32))]
    return pl.pallas_call(
        functools.partial(_inproj_kernel, chunk=chunk, lora_w=lp["w_decay2"].shape[0], conv_w=cw,
                          n_split=n_split),
        grid=(b // nb, nj),
        in_specs=in_specs,
        out_specs=[o[0] for o in outs], out_shape=[o[1] for o in outs],
        scratch_shapes=[pltpu.VMEM((nb, 1, sw), F32),
                        pltpu.VMEM((nb, tt + SUBLANES, cc), F32)],
        compiler_params=pltpu.CompilerParams(vmem_limit_bytes=VMEM_LIMIT),
        name="inproj",
    )(x, scale, shift, g_norm.reshape(1, d), shift0, conv0p, *lp["w_segs"], *smalls)


def _pair_blockdiag(y, left):
    return jnp.concatenate([jnp.where(left, y, 0.0), jnp.where(left, 0.0, y)], axis=0)


def _rwkv_body(kq_ref, rq_ref, bk_ref, kkk_ref, bk2_ref, kk2_ref, v_ref, bv_ref, sz_ref, plast_ref,
               s0_ref, lnw_ref, lnb_ref, ones_ref, oa_ref, st_ref, s_scr, *, n_seq, chunk, n_heads,
               first, last, after, done):
    M = kq_ref.shape[0]
    L = chunk
    t_step = M // n_seq
    RU = min(M, ROWS)
    PW = 2 * HEAD_DIM
    n_units = M // RU
    n_hp = n_heads // 2
    units = range(n_units)
    hps = range(n_hp)
    pairs = [(u, g) for u in units for g in hps]
    zero_blk = jnp.zeros((n_seq, HEAD_DIM, HEAD_DIM), F32)

    def _load_state():
        for g in hps:
            top = jnp.concatenate([s0_ref[:, 2 * g], zero_blk], axis=-1)
            bot = jnp.concatenate([zero_blk, s0_ref[:, 2 * g + 1]], axis=-1)
            s_scr[:, g] = jnp.concatenate([top, bot], axis=-2)
    _when(first, _load_state)

    lg = int(math.log2(L))
    row = lax.broadcasted_iota(jnp.int32, (RU, 2 * RU), 0)
    lane = lax.broadcasted_iota(jnp.int32, (RU, 2 * RU), 1)
    col = lane & (RU - 1)
    same = _shr(row, lg) == _shr(col, lg)
    incl = (row >= col) & same
    strict = (row > col) & same
    left = lax.broadcasted_iota(jnp.int32, (RU, PW), 1) < HEAD_DIM
    bd_mask = (lax.broadcasted_iota(jnp.int32, (PW, PW), 0) < HEAD_DIM) == (
        lax.broadcasted_iota(jnp.int32, (PW, PW), 1) < HEAD_DIM)
    mmp = lambda x, y: _mm(x, _pair_blockdiag(y, left))

    p_last = plast_ref[...]
    pls = [slice(g * PW, (g + 1) * PW) for g in hps]
    rus = [slice(u * RU, (u + 1) * RU) for u in units]
    blk = lambda ref, q: ref[rus[q[0]], pls[q[1]]]
    kq = {q: blk(kq_ref, q) for q in pairs}
    rq = {q: blk(rq_ref, q) for q in pairs}
    bk2 = {q: blk(bk2_ref, q) for q in pairs}
    kk2 = {q: blk(kk2_ref, q) for q in pairs}
    vs = {q: blk(v_ref, q) for q in pairs}
    qr = {q: jnp.concatenate([kq[q], rq[q]], axis=0).astype(BF16) for q in pairs}
    bkk = {q: jnp.concatenate([_pair_blockdiag(blk(bk_ref, q).astype(F32), left),
                               _pair_blockdiag(blk(kkk_ref, q).astype(F32), left)], axis=0).astype(BF16)
           for q in pairs}
    yield

    gs = {q: _mm_nt(qr[q], bkk[q]) for q in pairs}
    yield
    a_ab = {q: jnp.where(strict, gs[q][:RU, :2 * RU], 0.0) for q in pairs}
    a_ak = {q: jnp.where(strict, gs[q][:RU, 2 * RU:], 0.0) for q in pairs}
    m_rb = {q: jnp.where(incl, gs[q][RU:, :2 * RU], 0.0) for q in pairs}
    m_rk = {q: jnp.where(incl, gs[q][RU:, 2 * RU:], 0.0) for q in pairs}
    yield
    t_inv = dict(zip(pairs, (yield from _tri_inverse_all([a_ab[q] for q in pairs], row, col, L, mmp))))
    akv = {q: mmp(a_ak[q], vs[q]) for q in pairs}
    yield

    assert after is None or after, "the body before this one must have stored its state by now"
    state = {}
    ys = {}
    for u in units:
        if L == RU:
            b = (u * RU) // t_step
            s0 = [state[b, g] if (b, g) in state else s_scr[b, g] for g in hps]
            ws = [_mm_nt(qr[u, g], s0[g]) for g in hps]
            w1 = [x[:RU] for x in ws]
            wr = [x[RU:] for x in ws]
        else:
            n_in = RU // L
            s0s = [[s_scr[u * n_in + i, g] for g in hps] for i in range(n_in)]
            w1, wr = [], []
            for g in hps:
                parts = [_mm_nt(jnp.concatenate([kq[u, g][i * L:(i + 1) * L], rq[u, g][i * L:(i + 1) * L]], axis=0),
                                s0s[i][g]) for i in range(n_in)]
                w1.append(jnp.concatenate([x[:L] for x in parts], axis=0))
                wr.append(jnp.concatenate([x[L:] for x in parts], axis=0))
        yield
        us = [-mmp(t_inv[u, g], w1[g] + akv[u, g]) for g in hps]
        yield
        for g in hps:
            rhs = jnp.concatenate([_pair_blockdiag(us[g], left), _pair_blockdiag(vs[u, g].astype(F32), left)], axis=0)
            ys[u, g] = wr[g] + _mm(jnp.concatenate([m_rb[u, g], m_rk[u, g]], axis=1), rhs)
        yield
        if L == RU:
            for g in hps:
                uv = jnp.concatenate([us[g], vs[u, g].astype(F32)], axis=0)
                bkk2 = jnp.concatenate([bk2[u, g], kk2[u, g]], axis=0)
                state[b, g] = s0[g] * p_last[u][:, pls[g]] + jnp.where(bd_mask, _mm_tn(uv, bkk2), 0.0)
        else:
            for g in hps:
                for i in range(n_in):
                    rows = slice(i * L, (i + 1) * L)
                    uv_i = jnp.concatenate([us[g][rows], vs[u, g][rows]], axis=0)
                    bkk2_i = jnp.concatenate([bk2[u, g][rows], kk2[u, g][rows]], axis=0)
                    state[u * n_in + i, g] = (s0s[i][g] * p_last[u * n_in + i][:, pls[g]]
                                              + jnp.where(bd_mask, _mm_tn(uv_i, bkk2_i), 0.0))
        yield
    for (b, g), s_new in state.items():
        s_scr[b, g] = s_new
    done.append(True)

    rows_out = [jnp.concatenate([ys[u, g] for g in hps], axis=-1) for u in units]
    y = rows_out[0] if n_units == 1 else jnp.concatenate(rows_out, axis=0)
    yc = y - _head_sums_mxu(y, ones_ref) * (1.0 / HEAD_DIM)
    yield
    var = _head_sums_mxu(yc * yc, ones_ref) * (1.0 / HEAD_DIM)
    yn = yc * lax.rsqrt(var + RWKV_GN_EPS)
    oa_ref[...] = ((yn * lnw_ref[...] + lnb_ref[...] + bv_ref[...]) * sz_ref[...]).astype(oa_ref.dtype)

    def _store_state():
        for g in hps:
            st_ref[:, 2 * g] = s_scr[:, g, 0:HEAD_DIM, 0:HEAD_DIM]
            st_ref[:, 2 * g + 1] = s_scr[:, g, HEAD_DIM:, HEAD_DIM:]
    _when(last, _store_state)


def _mlstm_body(q_ref, k_ref, v_ref, g_ref, gob_ref, c0_ref, n0_ref, m0_ref, gb_ref, gnw_ref, ones_ref,
                ob_ref, ct_ref, nt_ref, mt_ref, cn_scr, m_scr, *, n_seq, n_heads, first, last):
    M = q_ref.shape[0]
    L = M // n_seq
    heads = range(n_heads)
    seqs = range(n_seq)

    def _load_state():
        cn_scr[:, :, 0:HEAD_DIM, :] = c0_ref[...]
        cn_scr[:, :, HEAD_DIM:, :] = n0_ref[...]
        m_scr[...] = m0_ref[...]
    _when(first, _load_state)

    row = lax.broadcasted_iota(jnp.int32, (M, M), 0)
    col = lax.broadcasted_iota(jnp.int32, (M, M), 1)
    same_seq = _shr(row, int(math.log2(L))) == _shr(col, int(math.log2(L)))
    incl = (row >= col) & same_seq

    g = g_ref[...] + gb_ref[...]
    bcum = pltpu.roll(_cumsum_time(-_softplus(-g), incl.astype(BF16)), LANES - n_heads, axis=1)
    m_prev = m_scr[...]
    m_prev_rows = _rows_of_seq(m_prev, n_seq, L)
    x_all = g - bcum
    pos = lax.broadcasted_iota(jnp.int32, (M, LANES), 0) & (L - 1)
    m_all = bcum + jnp.maximum(_cummax_time(x_all, L, pos), m_prev_rows)
    bm_all = bcum - m_all
    w_in_all = jnp.exp(bcum + m_prev_rows - m_all)
    e_negm_all = jnp.exp(-m_all)
    m_new = _last_rows(m_all, n_seq)
    b_last = _last_rows(bcum, n_seq)
    ws_all = jnp.exp(_rows_of_seq(b_last - m_new, n_seq, L) + x_all)
    dec_all = jnp.exp(b_last + m_prev - m_new)
    m_scr[...] = m_new
    x_t = jnp.transpose(x_all)
    yield

    ones = jnp.ones((M, SUBLANES), F32)
    sls = [slice(h * HEAD_DIM, (h + 1) * HEAD_DIM) for h in heads]
    qs = [q_ref[:, sl] for sl in sls]
    ks = [k_ref[:, sl] for sl in sls]
    v1 = [jnp.concatenate([v_ref[:, sl], ones], axis=-1) for sl in sls]
    cn = [[cn_scr[b, h] for h in heads] for b in seqs]

    qk_t = [_mm_nt(qs[h], ks[h]) for h in heads]
    if n_seq == 1:
        qc = [_mm_nt(qs[h], cn[0][h]) for h in heads]
    else:
        qc = [jnp.concatenate([_mm_nt(qs[h][b * L:(b + 1) * L], cn[b][h]) for b in seqs], axis=0) for h in heads]
    yield
    w_ts = [jnp.where(incl, jnp.exp(bm_all[:, h:h + 1] + x_t[h:h + 1, :]), 0.0) for h in heads]
    yield
    s = [qk_t[h] * w_ts[h] for h in heads]
    numden = [_mm(s[h], v1[h]) + w_in_all[:, h:h + 1] * qc[h] for h in heads]
    yield
    den = [jnp.maximum(jnp.abs(numden[h][:, HEAD_DIM:HEAD_DIM + 1]), e_negm_all[:, h:h + 1]) for h in heads]
    hh = [numden[h][:, 0:HEAD_DIM] / den[h] for h in heads]
    hh_all = jnp.concatenate(hh, axis=-1)
    ssq = _head_sums_mxu(hh_all * hh_all, ones_ref) * (1.0 / HEAD_DIM)

    yield
    v1w = [v1[h] * ws_all[:, h:h + 1] for h in heads]
    for h in heads:
        for b in seqs:
            rows = slice(b * L, (b + 1) * L)
            cn_scr[b, h] = dec_all[b][:, h:h + 1] * cn[b][h] + _mm_tn(v1w[h][rows], ks[h][rows])
    yield

    hb = hh_all * lax.rsqrt(ssq + MLSTM_GN_EPS)
    ob_ref[...] = (hb * gnw_ref[...] * gob_ref[...]).astype(ob_ref.dtype)

    def _store_state():
        ct_ref[...] = cn_scr[:, :, 0:HEAD_DIM, :]
        nt_ref[...] = cn_scr[:, :, HEAD_DIM:, :]
        mt_ref[...] = m_scr[...]
    _when(last, _store_state)


N_RWKV_IN, N_RWKV_OUT, N_MLSTM_IN, N_MLSTM_OUT = 14, 2, 11, 4
N_MLSTM_ROW_IN = 5
N_RWKV_ROW_IN = 9
RWKV_STAGES, MLSTM_STAGES = 29, 8
RWKV_BODY_ROWS = 256
RWKV_STAGGER = 0.6


def _mixers_kernel(*refs, n_seq, chunk, n_heads_a, n_heads_b):
    it = iter(refs)
    take = lambda n: [next(it) for _ in range(n)]
    r_in, m_in = take(N_RWKV_IN), take(N_MLSTM_IN)
    r_out, m_out = take(N_RWKV_OUT), take(N_MLSTM_OUT)
    s_scr, cn_scr, m_scr = take(3)
    first = pl.program_id(1) == 0
    last = pl.program_id(1) == pl.num_programs(1) - 1
    M = m_in[0].shape[0]
    n_r = M // RWKV_BODY_ROWS if n_seq == 1 and M > RWKV_BODY_ROWS else 1
    n_sub = M // MLSTM_ROWS if n_seq == 1 and M > MLSTM_ROWS else 1
    plan = []
    flags = [[] for _ in range(n_r)]
    for h in range(n_r):
        rows = (lambda r: r.at[pl.ds(h * RWKV_BODY_ROWS, RWKV_BODY_ROWS)]) if n_r > 1 else (lambda r: r)
        chunks = (lambda r: r.at[pl.ds(h * (RWKV_BODY_ROWS // chunk), RWKV_BODY_ROWS // chunk)]) if n_r > 1 else (
            lambda r: r)
        body = _rwkv_body(
            *[rows(r) for r in r_in[:N_RWKV_ROW_IN]], chunks(r_in[N_RWKV_ROW_IN]), *r_in[N_RWKV_ROW_IN + 1:],
            rows(r_out[0]), r_out[1], s_scr, n_seq=n_seq, chunk=chunk, n_heads=n_heads_a,
            first=first if h == 0 else None, last=last if h == n_r - 1 else None,
            after=flags[h - 1] if h else None, done=flags[h])
        plan.append((body, RWKV_STAGES, RWKV_STAGGER * h, 1.0))
    mlstm = []
    for h in range(n_sub):
        rows = (lambda r: r.at[pl.ds(h * MLSTM_ROWS, MLSTM_ROWS)]) if n_sub > 1 else (lambda r: r)
        mlstm.append(_mlstm_body(
            *[rows(r) for r in m_in[:N_MLSTM_ROW_IN]], *m_in[N_MLSTM_ROW_IN:], rows(m_out[0]), *m_out[1:],
            cn_scr, m_scr,
            n_seq=n_seq, n_heads=n_heads_b,
            first=first if h == 0 else None, last=last if h == n_sub - 1 else None))
    plan.append((_chain(*mlstm), MLSTM_STAGES * n_sub, 0.0, 1.0 + RWKV_STAGGER * (n_r - 1)))
    _run(_interleave(plan))


def _mixers(acts, v_a, bv, sz, p_last, s0, ln_w, ln_b, q, k, v_b, gates, gob, c0, n0, m0, gate_b, gn_w,
            n_seq, t_step, chunk):
    b = s0.shape[0]
    n_ha, n_hb = s0.shape[1], c0.shape[1]
    n_tok, da = v_a.shape
    db = v_b.shape[1]
    m = n_seq * t_step
    nj = n_tok // (b * t_step)
    tok = lambda i, j: (i * nj + j, 0)
    per_b3 = lambda i, j: (i, 0, 0)
    per_b4 = lambda i, j: (i, 0, 0, 0)
    const2 = lambda i, j: (0, 0)
    s_spec = pl.BlockSpec((n_seq, n_ha, HEAD_DIM, HEAD_DIM), per_b4)
    c_spec = pl.BlockSpec((n_seq, n_hb, HEAD_DIM, HEAD_DIM), per_b4)
    n_spec = pl.BlockSpec((n_seq, n_hb, SUBLANES, HEAD_DIM), per_b4)
    m_spec = pl.BlockSpec((n_seq, 1, LANES), per_b3)
    row_a, row_b = pl.BlockSpec((m, da), tok), pl.BlockSpec((m, db), tok)
    in_specs = [row_a] * 9
    ones_a = jnp.kron(jnp.eye(n_ha, dtype=F32), jnp.ones((HEAD_DIM, HEAD_DIM), F32)).astype(BF16)
    ones_b = jnp.kron(jnp.eye(n_hb, dtype=F32), jnp.ones((HEAD_DIM, HEAD_DIM), F32)).astype(BF16)
    in_specs += [pl.BlockSpec((m // chunk, 1, da), lambda i, j: (i * nj + j, 0, 0)), s_spec,
                 pl.BlockSpec(ln_w.shape, const2), pl.BlockSpec(ln_b.shape, const2),
                 pl.BlockSpec(ones_a.shape, const2)]
    in_specs += [row_b, row_b, row_b, pl.BlockSpec((m, LANES), tok), row_b, c_spec, n_spec, m_spec,
                 pl.BlockSpec(gate_b.shape, const2), pl.BlockSpec(gn_w.shape, const2),
                 pl.BlockSpec(ones_b.shape, const2)]
    return pl.pallas_call(
        functools.partial(_mixers_kernel, n_seq=n_seq, chunk=chunk, n_heads_a=n_ha, n_heads_b=n_hb),
        grid=(b // n_seq, nj),
        in_specs=in_specs,
        out_specs=[row_a, s_spec, row_b, c_spec, n_spec, m_spec],
        out_shape=[jax.ShapeDtypeStruct((n_tok, da), BF16), jax.ShapeDtypeStruct(s0.shape, F32),
                   jax.ShapeDtypeStruct((n_tok, db), BF16), jax.ShapeDtypeStruct(c0.shape, F32),
                   jax.ShapeDtypeStruct(n0.shape, F32), jax.ShapeDtypeStruct(m0.shape, F32)],
        scratch_shapes=[pltpu.VMEM((n_seq, n_ha // 2, 2 * HEAD_DIM, 2 * HEAD_DIM), F32),
                        pltpu.VMEM((n_seq, n_hb, HEAD_DIM + SUBLANES, HEAD_DIM), F32),
                        pltpu.VMEM((n_seq, 1, LANES), F32)],
        compiler_params=pltpu.CompilerParams(vmem_limit_bytes=VMEM_LIMIT),
        name="mixers",
    )(*acts, v_a, bv, sz, p_last, s0, ln_w, ln_b, ones_a, q, k, v_b, gates, gob, c0, n0, m0, gate_b, gn_w, ones_b)


def _out_kernel(oa_ref, ob_ref, x_ref, sc_ref, sh_ref, gate_ref, g_ref, wga_ref, wgb_ref, wua_ref, wub_ref, wo_ref,
                gf_ref, y_ref, *, nb, final_norm):
    m = x_ref.shape[0]
    rows = lambda ref: _rows_of_seq(ref[...], nb, m // nb)
    x = x_ref[...]
    h = x * lax.rsqrt(jnp.mean(x * x, axis=-1, keepdims=True) + NORM_EPS) * g_ref[...]
    h = (h * (1.0 + rows(sc_ref)) + rows(sh_ref)).astype(BF16)
    gl_a = jnp.dot(h, wga_ref[...], preferred_element_type=F32)
    gl_b = jnp.dot(h, wgb_ref[...], preferred_element_type=F32)
    ua = jnp.dot(oa_ref[...], wua_ref[...], preferred_element_type=F32)
    ub = jnp.dot(ob_ref[...], wub_ref[...], preferred_element_type=F32)
    merged = _sigmoid(gl_a) * ua + _sigmoid(gl_b) * ub
    mo = jnp.dot(merged.astype(BF16), wo_ref[...], preferred_element_type=F32)
    xn = x + rows(gate_ref) * mo
    if final_norm:
        xn = xn * lax.rsqrt(jnp.mean(xn * xn, axis=-1, keepdims=True) + NORM_EPS) * gf_ref[...]
    y_ref[...] = xn


def _out(out_a, out_b, x, scale, shift, gate, g_norm, w_gl_a, w_gl_b, w_up_a, w_up_b, w_out, g_final,
         b, nb, tt, final_norm):
    n_tok, d = x.shape
    nj = n_tok // (b * tt)
    m = nb * tt
    tok = lambda i, j: (i * nj + j, 0)
    const2 = lambda i, j: (0, 0)
    per_b = lambda i, j: (i, 0, 0)
    wspec = lambda w: pl.BlockSpec(w.shape, const2, pipeline_mode=pl.Buffered(1))
    in_specs = [pl.BlockSpec((m, out_a.shape[-1]), tok),
                pl.BlockSpec((m, out_b.shape[-1]), tok),
                pl.BlockSpec((m, d), tok),
                pl.BlockSpec((nb, 1, d), per_b),
                pl.BlockSpec((nb, 1, d), per_b),
                pl.BlockSpec((nb, 1, d), per_b),
                pl.BlockSpec((1, d), const2),
                wspec(w_gl_a), wspec(w_gl_b), wspec(w_up_a), wspec(w_up_b), wspec(w_out),
                pl.BlockSpec((1, d), const2)]
    return pl.pallas_call(
        functools.partial(_out_kernel, nb=nb, final_norm=final_norm),
        grid=(b // nb, nj),
        in_specs=in_specs,
        out_specs=pl.BlockSpec((m, d), tok),
        out_shape=jax.ShapeDtypeStruct((n_tok, d), F32),
        compiler_params=pltpu.CompilerParams(vmem_limit_bytes=VMEM_LIMIT),
        name="merge_out",
    )(out_a, out_b, x, scale, shift, gate, g_norm.reshape(1, d), w_gl_a, w_gl_b, w_up_a, w_up_b, w_out,
      g_final.reshape(1, d))


def _token_tiling(b, t, target):
    if t >= target:
        return 1, math.gcd(t, target)
    return math.gcd(b, target // t), t


def _layer(x, mod, states, lp, g_final, final_norm):
    b, t, d = x.shape
    shift0, s0, conv0, c0, n0, m0 = states
    n_hb = c0.shape[1]
    cw = lp["conv_w"].shape[0]
    row = lambda a: a.reshape(1, -1)

    ada_shift, ada_scale, ada_gate = (mod[:, None, i * d:(i + 1) * d] for i in range(3))
    nb, tt = _token_tiling(b, t, PROJ_ROWS)
    L = math.gcd(t, ROWS)
    act_dtype = BF16 if L % 16 == 0 else F32
    conv0p = jnp.pad(conv0, ((0, 0), (SUBLANES - (cw - 1), 0), (0, 0)))
    n0p = jnp.broadcast_to(n0[:, :, None, :], (b, n_hb, SUBLANES, HEAD_DIM))
    m0p = jnp.pad(m0, ((0, 0), (0, LANES - n_hb)))[:, None, :]
    gate_b = jnp.zeros((1, LANES), F32).at[0, 0:n_hb].set(lp["b_i"]).at[0, n_hb:2 * n_hb].set(lp["b_f"])
    n_seq, t_step = _token_tiling(b, t, MIX_ROWS if t >= MIX_ROWS else MLSTM_ROWS)
    (kq, rq, bk, kkk, bk2, kk2, v_a, bv, sz, p_last, shift_t,
     q, k, v_b, gates, gob, conv_tail) = _inproj(
        x, ada_scale, ada_shift, lp["g_norm"], shift0[:, None, :], conv0p, lp, nb, tt, L, act_dtype)
    out_a, s_t, out_b, c_t, n_t, m_t = _mixers(
        (kq, rq, bk, kkk, bk2, kk2), v_a, bv, sz, p_last, s0, row(lp["ln_w"]), row(lp["ln_b"]),
        q, k, v_b, gates, gob, c0, n0p, m0p, gate_b, row(lp["gn_w"]), n_seq, t_step, L)

    y = _out(out_a, out_b, x.reshape(b * t, d), ada_scale, ada_shift, ada_gate, lp["g_norm"],
             lp["w_gl_a"], lp["w_gl_b"], lp["w_up_a"], lp["w_up_b"], lp["w_out"],
             g_final, b, nb, tt, final_norm).reshape(b, t, d)
    new_states = (shift_t[:, 0], s_t, conv_tail[:, SUBLANES - (cw - 1):], c_t, n_t[:, :, 0, :], m_t[:, 0, :n_hb])
    return y, new_states


def _trunk(x, mods, states, layers, g_final):
    depth = len(layers)
    new = [[] for _ in states]
    for l in range(depth):
        st = tuple(s[l] for s in states)
        x, st_new = _layer(x, mods[l], st, layers[l], g_final, final_norm=(l == depth - 1))
        for lst, s in zip(new, st_new):
            lst.append(s.astype(x.dtype))
    return x, tuple(jnp.stack(lst) for lst in new)


def kernel(x_prompt, x_sample, c_prompt, c_sample, state_rwkv_shift, state_rwkv_S, state_mlstm_conv, state_mlstm_C, state_mlstm_n, state_mlstm_m, g_norm, w_ada, b_ada, w_in, mu_shift, w_decay2, w0, w_iclr2, a0, k_k, k_a, r_k, ln_w, ln_b, conv_w, conv_b, b_i, b_f, gn_w, w_up_a, w_up_b, w_out, g_final):
    depth = g_norm.shape[0]
    bp, bs = x_prompt.shape[0], x_sample.shape[0]
    d = x_prompt.shape[-1]
    da, db = w_up_a.shape[1], w_up_b.shape[1]
    n_ha, n_hb = da // HEAD_DIM, db // HEAD_DIM
    sw = mu_shift.shape[-1]
    cc = conv_w.shape[-1]
    cw = conv_w.shape[1]
    dt = x_prompt.dtype

    sizes = (sw, da, cc, db, db, n_hb, n_hb, db, d, d)
    offs = [0]
    for s in sizes:
        offs.append(offs[-1] + s)
    seg = lambda w, i: w[:, offs[i]:offs[i + 1]]

    c_all = jnp.concatenate([c_prompt, c_sample], axis=0)
    n_c = c_all.shape[0]
    c_pad = jnp.pad(c_all, ((0, (-n_c) % SUBLANES), (0, 0)))

    layers, mods_p, mods_s = [], [], []
    for l in range(depth):
        w = w_in[l]
        wg = jnp.concatenate([seg(w, 5), seg(w, 6)], axis=1)
        wg = jnp.pad(wg, ((0, 0), (0, LANES - wg.shape[1])))
        w_segs = tuple(x.astype(BF16) for x in
                       (seg(w, 0), seg(w, 1), seg(w, 2), seg(w, 3), seg(w, 4), seg(w, 7), wg))
        layers.append(dict(
            g_norm=g_norm[l], w_segs=w_segs, w_gl_a=seg(w, 8).astype(BF16), w_gl_b=seg(w, 9).astype(BF16),
            mu_shift=mu_shift[l], w_decay2=w_decay2[l].astype(BF16),
            w0=w0[l], w_iclr2=w_iclr2[l].astype(BF16), a0=a0[l], k_k=k_k[l], k_a=k_a[l], r_k=r_k[l],
            ln_w=ln_w[l], ln_b=ln_b[l], conv_w=conv_w[l], conv_b=conv_b[l], b_i=b_i[l], b_f=b_f[l],
            gn_w=gn_w[l], w_up_a=w_up_a[l].astype(BF16), w_up_b=w_up_b[l].astype(BF16),
            w_out=w_out[l].astype(BF16)))
        mod = _ada(c_pad, w_ada[l], b_ada[l])
        mods_p.append(mod[:bp])
        mods_s.append(mod[bp:bp + bs])

    prompt_states = (
        jnp.zeros((depth, bp, sw), dt),
        jnp.zeros((depth, bp, n_ha, HEAD_DIM, HEAD_DIM), dt),
        jnp.zeros((depth, bp, cw - 1, cc), dt),
        jnp.zeros((depth, bp, n_hb, HEAD_DIM, HEAD_DIM), dt),
        jnp.zeros((depth, bp, n_hb, HEAD_DIM), dt),
        jnp.zeros((depth, bp, n_hb), dt),
    )
    sample_states = (state_rwkv_shift, state_rwkv_S, state_mlstm_conv,
                     state_mlstm_C, state_mlstm_n, state_mlstm_m)
    y_p, st_p = _trunk(x_prompt, mods_p, prompt_states, layers, g_final)
    y_s, st_s = _trunk(x_sample, mods_s, sample_states, layers, g_final)
    return (y_p, y_s) + st_p + st_s
```

```python
import functools
import math

import jax
import jax.numpy as jnp
from jax import lax
from jax.experimental import pallas as pl
from jax.experimental.pallas import tpu as pltpu

F32 = jnp.float32
BF16 = jnp.bfloat16

HEAD_DIM = 64
NORM_EPS = 1e-6
RWKV_GN_EPS = 64e-5
MLSTM_GN_EPS = 1e-6
ROWS = 64
MLSTM_ROWS = 128
MIX_ROWS = 1024
PROJ_ROWS = 256
INPROJ_SPLIT = 2
OUT_SPLIT = 2
SUBLANES = 8
LANES = 128
TRI_BASE = 4
VMEM_LIMIT = 60 * 1024 * 1024

LOG2_E = 1.4426950408889634
NT_DIMS = (((1,), (1,)), ((), ()))
TN_DIMS = (((0,), (0,)), ((), ()))


def _mm(a, b):
    return jnp.dot(a.astype(BF16), b.astype(BF16), preferred_element_type=F32)


def _mm_nt(a, b):
    return lax.dot_general(a.astype(BF16), b.astype(BF16), NT_DIMS, preferred_element_type=F32)


def _mm_tn(a, b):
    return lax.dot_general(a.astype(BF16), b.astype(BF16), TN_DIMS, preferred_element_type=F32)


def _sigmoid(x):
    return 1.0 / (1.0 + jnp.exp2(x * -LOG2_E))


def _silu(x):
    return x * _sigmoid(x)


def _softplus(x):
    return jnp.maximum(x, 0.0) + jnp.log(1.0 + jnp.exp(-jnp.abs(x)))


def _cumsum_time(x, tri_bf16, terms=3):
    dot = functools.partial(jnp.dot, preferred_element_type=F32)
    rest = x
    total = None
    for i in range(terms):
        part = rest.astype(BF16)
        if i + 1 < terms:
            rest = rest - part.astype(F32)
        total = dot(tri_bf16, part) if total is None else total + dot(tri_bf16, part)
    return total


def _shr(x, n):
    return lax.shift_right_logical(x, jnp.full(x.shape, n, x.dtype))


def _rows_of_seq(x, n_seq, L):
    if n_seq == 1:
        return x[0]
    return jnp.broadcast_to(x, (n_seq, L, x.shape[-1])).reshape(n_seq * L, x.shape[-1])


def _last_rows(x, n_seq):
    m, n = x.shape
    L = m // n_seq
    return x.reshape(n_seq, L, n)[:, L - 1:L, :]


def _head_sums(x):
    m, n = x.shape
    parts = [jnp.broadcast_to(jnp.sum(x[:, o:o + HEAD_DIM], axis=-1, keepdims=True), (m, HEAD_DIM))
             for o in range(0, n, HEAD_DIM)]
    return jnp.concatenate(parts, axis=-1)


def _head_sums_mxu(x, ones_ref):
    hi = x.astype(BF16)
    lo = (x - hi.astype(F32)).astype(BF16)
    s = jnp.dot(jnp.concatenate([hi, lo], axis=0), ones_ref[...], preferred_element_type=F32)
    return s[:x.shape[0]] + s[x.shape[0]:]


def _cummax_time(x, L, pos):
    y = x
    sh = 1
    while sh < L:
        y = jnp.where(pos >= sh, jnp.maximum(y, pltpu.roll(y, sh, axis=0)), y)
        sh *= 2
    return y


def _tri_inverse_all(mats, row, col, L, mm=_mm):
    bs = min(TRI_BASE, L)
    sh = int(math.log2(bs))
    same = _shr(row, sh) == _shr(col, sh)
    eye = (row == col).astype(F32)
    rows = mats[0].shape[0]
    ns = [jnp.where(same, -a, 0.0) for a in mats]
    ts = [eye + n for n in ns]
    if bs > 2:
        ps = [mm(n, n) for n in ns]
        yield
        k = 2
        while 2 * k < bs:
            both = [mm(jnp.concatenate([t, p], axis=0), p) for t, p in zip(ts, ps)]
            ts = [t + x[:rows] for t, x in zip(ts, both)]
            ps = [x[rows:] for x in both]
            yield
            k *= 2
        ts = [t + mm(t, p) for t, p in zip(ts, ps)]
        yield
    size = bs
    while size < L:
        sh = int(math.log2(size))
        lower_left = (_shr(row, sh + 1) == _shr(col, sh + 1)) & (_shr(row, sh) != _shr(col, sh))
        offs = [jnp.where(lower_left, a, 0.0) for a in mats]
        tmp = [mm(t, o) for t, o in zip(ts, offs)]
        yield
        ts = [t - mm(x, t) for t, x in zip(ts, tmp)]
        yield
        size *= 2
    return ts


def _interleave(plan):
    live = [[g, n, s, w, 0] for g, n, s, w in plan]
    while live:
        item = min(live, key=lambda it: it[2] + it[3] * it[4] / it[1])
        try:
            next(item[0])
            item[4] += 1
            yield
        except StopIteration:
            live.remove(item)


def _chain(*gens):
    for g in gens:
        yield from g


def _run(gen):
    for _ in gen:
        pass


def _when(pred, fn):
    if pred is None:
        return
    if pred is True:
        fn()
    else:
        pl.when(pred)(fn)


def _ada_kernel(c_ref, w_ref, b_ref, o_ref):
    o_ref[...] = _mm(_silu(c_ref[...]), w_ref[...]) + b_ref[...]


def _ada(c, w_ada, b_ada):
    n, d = c.shape
    n3 = w_ada.shape[1]
    tn = d
    return pl.pallas_call(
        _ada_kernel,
        grid=(n3 // tn,),
        in_specs=[pl.BlockSpec((n, d), lambda j: (0, 0)),
                  pl.BlockSpec((d, tn), lambda j: (0, j)),
                  pl.BlockSpec((1, tn), lambda j: (0, j))],
        out_specs=pl.BlockSpec((n, tn), lambda j: (0, j)),
        out_shape=jax.ShapeDtypeStruct((n, n3), F32),
        name="ada",
    )(c, w_ada, b_ada.reshape(1, n3))


PROJ_COLS = 256


def _inproj_body(x_ref, sc_ref, sh_ref, g_ref, shift0_ref, conv0_ref,
                 w_shift, w_za, w_qk, w_vb, w_ob, w_zb, w_gt,
                 mu_ref, w0_ref, a0_ref, kk_ref, ka_ref, rk_ref, wd2_ref, wi2_ref, cw_ref, cb_ref,
                 kq_o, rq_o, bk_o, kkk_o, bk2_o, kk2_o, va_o, bv_o, sz_o, plast_o, shst_o,
                 q_o, k_o, vb_o, gt_o, gob_o, cst_o,
                 prev_scr, xbuf, *, chunk, lora_w, conv_w, first, live):
    nb, tt, d = x_ref.shape
    M = nb * tt
    L = chunk
    n_chunks = M // L
    da = va_o.shape[-1]
    db = vb_o.shape[-1]
    cc = xbuf.shape[-1]
    pad = SUBLANES

    def _init():
        prev_scr[...] = shift0_ref[...]
        xbuf[:, 0:pad, :] = conv0_ref[...]
    _when(first, _init)

    x = x_ref[...]
    y = x * lax.rsqrt(jnp.mean(x * x, axis=-1, keepdims=True) + NORM_EPS) * g_ref[...]
    h = (y * (1.0 + sc_ref[...]) + sh_ref[...]).reshape(M, d).astype(BF16)
    def proj(w_ref):
        n = w_ref.shape[1]
        parts = []
        for o in range(0, n, PROJ_COLS):
            parts.append(jnp.dot(h, w_ref[:, o:min(o + PROJ_COLS, n)], preferred_element_type=F32))
            yield
        return parts[0] if len(parts) == 1 else jnp.concatenate(parts, axis=-1)

    p = yield from proj(w_shift)
    row_w = lax.broadcasted_iota(jnp.int32, p.shape, 0)
    first = (row_w & (tt - 1)) == 0
    prev = jnp.where(first, _rows_of_seq(prev_scr[...], nb, tt), pltpu.roll(p, 1, axis=0))
    last = _last_rows(p, nb)
    prev_scr[...] = last

    def _shift_state():
        shst_o[...] = last
    _when(live, _shift_state)
    ps = p + mu_ref[...] * (prev - p)
    r = ps[:, 0:da]
    k = ps[:, da:2 * da]
    v = ps[:, 2 * da:3 * da]
    wl = ps[:, 3 * da:3 * da + lora_w]
    al = ps[:, 3 * da + lora_w:]

    qk_pre = yield from proj(w_qk)
    z_a = yield from proj(w_za)
    w = w0_ref[...] + _mm(jnp.tanh(wl), wd2_ref[...])
    lw = -math.exp(-0.5) * _sigmoid(w)
    a = _sigmoid(a0_ref[...] + _mm(al, wi2_ref[...]))
    yield
    v_b = yield from proj(w_vb)
    g_t = yield from proj(w_gt)
    o_b = yield from proj(w_ob)
    z_b = yield from proj(w_zb)
    kkr = k * kk_ref[...]
    kk = kkr * lax.rsqrt(jnp.maximum(_head_sums(kkr * kkr), 1e-24))
    k2 = k * (1.0 + (a - 1.0) * ka_ref[...])
    b = kk * a
    va_o[...] = v
    bv_o[...] = _head_sums(r * k2 * rk_ref[...]) * v
    sz_o[...] = _silu(z_a)

    lg = int(math.log2(L))
    row_m = lax.broadcasted_iota(jnp.int32, (M, M), 0)
    col_m = lax.broadcasted_iota(jnp.int32, (M, M), 1)
    tri = ((row_m >= col_m) & (_shr(row_m, lg) == _shr(col_m, lg))).astype(BF16)
    c = _cumsum_time(lw, tri, terms=2)
    yield
    c_last = _last_rows(c, n_chunks)
    plast_o[...] = jnp.exp(c_last)
    e_nc = jnp.exp(-c)
    e_cl = jnp.exp(_rows_of_seq(c_last, n_chunks, L) - c)
    kq_o[...] = (kk * jnp.exp(c - lw)).astype(kq_o.dtype)
    rq_o[...] = (r * jnp.exp(c)).astype(rq_o.dtype)
    bk_o[...] = (b * e_nc).astype(bk_o.dtype)
    kkk_o[...] = (k2 * e_nc).astype(kkk_o.dtype)
    bk2_o[...] = (b * e_cl).astype(bk2_o.dtype)
    kk2_o[...] = (k2 * e_cl).astype(kk2_o.dtype)

    xbuf[:, pad:pad + tt, :] = qk_pre.reshape(nb, tt, cc)
    xe = xbuf[...].reshape(nb * (tt + pad), cc)
    conv = cb_ref[...] + qk_pre * cw_ref[conv_w - 1:conv_w, :]
    for back in range(1, conv_w):
        shifted = pltpu.roll(xe, back, axis=0).reshape(nb, tt + pad, cc)[:, pad:, :].reshape(M, cc)
        conv = conv + shifted * cw_ref[conv_w - 1 - back:conv_w - back, :]
    tail = xbuf[:, tt:tt + pad, :]
    xbuf[:, 0:pad, :] = tail

    def _conv_state():
        cst_o[...] = tail
    _when(live, _conv_state)
    yield
    qk = _silu(conv)
    q_o[...] = qk[:, 0:db].astype(q_o.dtype)
    k_o[...] = (qk[:, db:] * (1.0 / math.sqrt(HEAD_DIM))).astype(k_o.dtype)
    vb_o[...] = v_b
    gt_o[...] = g_t
    gob_o[...] = _sigmoid(o_b) * _silu(z_b)


N_INPROJ_IN = 23
INPROJ_ROW_OUTS = (0, 1, 2, 3, 4, 5, 6, 7, 8, 11, 12, 13, 14, 15)
INPROJ_CHUNK_OUT = 9
INPROJ_STAGES = 28


def _inproj_kernel(*refs, chunk, lora_w, conv_w, n_split):
    ins, outs, scratch = refs[:N_INPROJ_IN], refs[N_INPROJ_IN:-2], refs[-2:]
    first = pl.program_id(1) == 0
    if n_split == 1:
        _run(_inproj_body(*refs, chunk=chunk, lora_w=lora_w, conv_w=conv_w, first=first, live=True))
        return
    tp = ins[0].shape[1] // n_split
    plan = []
    for h in range(n_split):
        rows = pl.ds(h * tp, tp)
        t_ins = [ins[0].at[:, rows]] + list(ins[1:])
        t_outs = [o.at[rows] if i in INPROJ_ROW_OUTS else o for i, o in enumerate(outs)]
        t_outs[INPROJ_CHUNK_OUT] = outs[INPROJ_CHUNK_OUT].at[pl.ds(h * (tp // chunk), tp // chunk)]
        body = _inproj_body(*t_ins, *t_outs, *scratch, chunk=chunk, lora_w=lora_w, conv_w=conv_w,
                            first=first if h == 0 else None, live=True)
        plan.append((body, INPROJ_STAGES, 0.5 * h, 1.0))
    _run(_interleave(plan))


def _inproj(x, scale, shift, g_norm, shift0, conv0p, lp, nb, tt, chunk, act_dtype):
    b, t, d = x.shape
    n_tok = b * t
    n_split = INPROJ_SPLIT if nb == 1 and t % (INPROJ_SPLIT * tt) == 0 else 1
    tb = n_split * tt
    m = nb * tb
    nj = t // tb
    sw = shift0.shape[-1]
    cc = conv0p.shape[-1]
    da, db = lp["w_up_a"].shape[0], lp["w_up_b"].shape[0]
    cw = lp["conv_w"].shape[0]
    row = lambda a: a.reshape(1, -1)
    rows = [row(lp[n]) for n in ("mu_shift", "w0", "a0", "k_k", "k_a", "r_k")]
    smalls = rows + [lp["w_decay2"], lp["w_iclr2"], lp["conv_w"], row(lp["conv_b"])]
    const = lambda i, j: (0, 0)
    tok = lambda i, j: (i * nj + j, 0)
    per_b = lambda i, j: (i, 0, 0)
    in_specs = [pl.BlockSpec((nb, tb, d), lambda i, j: (i, j, 0)),
                pl.BlockSpec((nb, 1, d), per_b),
                pl.BlockSpec((nb, 1, d), per_b),
                pl.BlockSpec((1, d), const),
                pl.BlockSpec((nb, 1, sw), per_b),
                pl.BlockSpec((nb, SUBLANES, cc), per_b)]
    in_specs += [pl.BlockSpec(w.shape, const, pipeline_mode=pl.Buffered(1)) for w in lp["w_segs"]]
    in_specs += [pl.BlockSpec(a.shape, const) for a in smalls]
    tok_out = lambda n, dt: (pl.BlockSpec((m, n), tok), jax.ShapeDtypeStruct((n_tok, n), dt))
    outs = [tok_out(da, act_dtype)] * 6 + [tok_out(da, F32)] * 3
    outs += [(pl.BlockSpec((m // chunk, 1, da), lambda i, j: (i * nj + j, 0, 0)),
              jax.ShapeDtypeStruct((n_tok // chunk, 1, da), F32)),
             (pl.BlockSpec((nb, 1, sw), per_b), jax.ShapeDtypeStruct((b, 1, sw), F32))]
    outs += [tok_out(db, act_dtype)] * 2 + [tok_out(db, F32), tok_out(LANES, F32), tok_out(db, F32)]
    outs += [(pl.BlockSpec((nb, SUBLANES, cc), per_b), jax.ShapeDtypeStruct((b, SUBLANES, cc), F32))]
    return pl.pallas_call(
        functools.partial(_inproj_kernel, chunk=chunk, lora_w=lp["w_decay2"].shape[0], conv_w=cw,
                          n_split=n_split),
        grid=(b // nb, nj),
        in_specs=in_specs,
        out_specs=[o[0] for o in outs], out_shape=[o[1] for o in outs],
        scratch_shapes=[pltpu.VMEM((nb, 1, sw), F32),
                        pltpu.VMEM((nb, tt + SUBLANES, cc), F32)],
        compiler_params=pltpu.CompilerParams(vmem_limit_bytes=VMEM_LIMIT),
        name="inproj",
    )(x, scale, shift, g_norm.reshape(1, d), shift0, conv0p, *lp["w_segs"], *smalls)


def _pair_blockdiag(y, left):
    return jnp.concatenate([jnp.where(left, y, 0.0), jnp.where(left, 0.0, y)], axis=0)


def _rwkv_body(kq_ref, rq_ref, bk_ref, kkk_ref, bk2_ref, kk2_ref, v_ref, bv_ref, sz_ref, plast_ref,
               s0_ref, lnw_ref, lnb_ref, ones_ref, oa_ref, st_ref, s_scr, *, n_seq, chunk, n_heads,
               first, last, after, done):
    M = kq_ref.shape[0]
    L = chunk
    t_step = M // n_seq
    RU = min(M, ROWS)
    PW = 2 * HEAD_DIM
    n_units = M // RU
    n_hp = n_heads // 2
    units = range(n_units)
    hps = range(n_hp)
    pairs = [(u, g) for u in units for g in hps]
    zero_blk = jnp.zeros((n_seq, HEAD_DIM, HEAD_DIM), F32)

    def _load_state():
        for g in hps:
            top = jnp.concatenate([s0_ref[:, 2 * g], zero_blk], axis=-1)
            bot = jnp.concatenate([zero_blk, s0_ref[:, 2 * g + 1]], axis=-1)
            s_scr[:, g] = jnp.concatenate([top, bot], axis=-2)
    _when(first, _load_state)

    lg = int(math.log2(L))
    row = lax.broadcasted_iota(jnp.int32, (RU, 2 * RU), 0)
    lane = lax.broadcasted_iota(jnp.int32, (RU, 2 * RU), 1)
    col = lane & (RU - 1)
    same = _shr(row, lg) == _shr(col, lg)
    incl = (row >= col) & same
    strict = (row > col) & same
    left = lax.broadcasted_iota(jnp.int32, (RU, PW), 1) < HEAD_DIM
    bd_mask = (lax.broadcasted_iota(jnp.int32, (PW, PW), 0) < HEAD_DIM) == (
        lax.broadcasted_iota(jnp.int32, (PW, PW), 1) < HEAD_DIM)
    mmp = lambda x, y: _mm(x, _pair_blockdiag(y, left))

    p_last = plast_ref[...]
    pls = [slice(g * PW, (g + 1) * PW) for g in hps]
    rus = [slice(u * RU, (u + 1) * RU) for u in units]
    blk = lambda ref, q: ref[rus[q[0]], pls[q[1]]]
    kq = {q: blk(kq_ref, q) for q in pairs}
    rq = {q: blk(rq_ref, q) for q in pairs}
    bk2 = {q: blk(bk2_ref, q) for q in pairs}
    kk2 = {q: blk(kk2_ref, q) for q in pairs}
    vs = {q: blk(v_ref, q) for q in pairs}
    qr = {q: jnp.concatenate([kq[q], rq[q]], axis=0).astype(BF16) for q in pairs}
    bkk = {q: jnp.concatenate([_pair_blockdiag(blk(bk_ref, q).astype(F32), left),
                               _pair_blockdiag(blk(kkk_ref, q).astype(F32), left)], axis=0).astype(BF16)
           for q in pairs}
    yield

    gs = {q: _mm_nt(qr[q], bkk[q]) for q in pairs}
    yield
    a_ab = {q: jnp.where(strict, gs[q][:RU, :2 * RU], 0.0) for q in pairs}
    a_ak = {q: jnp.where(strict, gs[q][:RU, 2 * RU:], 0.0) for q in pairs}
    m_rb = {q: jnp.where(incl, gs[q][RU:, :2 * RU], 0.0) for q in pairs}
    m_rk = {q: jnp.where(incl, gs[q][RU:, 2 * RU:], 0.0) for q in pairs}
    yield
    t_inv = dict(zip(pairs, (yield from _tri_inverse_all([a_ab[q] for q in pairs], row, col, L, mmp))))
    akv = {q: mmp(a_ak[q], vs[q]) for q in pairs}
    yield

    assert after is None or after, "the body before this one must have stored its state by now"
    state = {}
    ys = {}
    for u in units:
        if L == RU:
            b = (u * RU) // t_step
            s0 = [state[b, g] if (b, g) in state else s_scr[b, g] for g in hps]
            ws = [_mm_nt(qr[u, g], s0[g]) for g in hps]
            w1 = [x[:RU] for x in ws]
            wr = [x[RU:] for x in ws]
        else:
            n_in = RU // L
            s0s = [[s_scr[u * n_in + i, g] for g in hps] for i in range(n_in)]
            w1, wr = [], []
            for g in hps:
                parts = [_mm_nt(jnp.concatenate([kq[u, g][i * L:(i + 1) * L], rq[u, g][i * L:(i + 1) * L]], axis=0),
                                s0s[i][g]) for i in range(n_in)]
                w1.append(jnp.concatenate([x[:L] for x in parts], axis=0))
                wr.append(jnp.concatenate([x[L:] for x in parts], axis=0))
        yield
        us = [-mmp(t_inv[u, g], w1[g] + akv[u, g]) for g in hps]
        yield
        for g in hps:
            rhs = jnp.concatenate([_pair_blockdiag(us[g], left), _pair_blockdiag(vs[u, g].astype(F32), left)], axis=0)
            ys[u, g] = wr[g] + _mm(jnp.concatenate([m_rb[u, g], m_rk[u, g]], axis=1), rhs)
        yield
        if L == RU:
            for g in hps:
                uv = jnp.concatenate([us[g], vs[u, g].astype(F32)], axis=0)
                bkk2 = jnp.concatenate([bk2[u, g], kk2[u, g]], axis=0)
                state[b, g] = s0[g] * p_last[u][:, pls[g]] + jnp.where(bd_mask, _mm_tn(uv, bkk2), 0.0)
        else:
            for g in hps:
                for i in range(n_in):
                    rows = slice(i * L, (i + 1) * L)
                    uv_i = jnp.concatenate([us[g][rows], vs[u, g][rows]], axis=0)
                    bkk2_i = jnp.concatenate([bk2[u, g][rows], kk2[u, g][rows]], axis=0)
                    state[u * n_in + i, g] = (s0s[i][g] * p_last[u * n_in + i][:, pls[g]]
                                              + jnp.where(bd_mask, _mm_tn(uv_i, bkk2_i), 0.0))
        yield
    for (b, g), s_new in state.items():
        s_scr[b, g] = s_new
    done.append(True)

    rows_out = [jnp.concatenate([ys[u, g] for g in hps], axis=-1) for u in units]
    y = rows_out[0] if n_units == 1 else jnp.concatenate(rows_out, axis=0)
    yc = y - _head_sums_mxu(y, ones_ref) * (1.0 / HEAD_DIM)
    yield
    var = _head_sums_mxu(yc * yc, ones_ref) * (1.0 / HEAD_DIM)
    yn = yc * lax.rsqrt(var + RWKV_GN_EPS)
    oa_ref[...] = ((yn * lnw_ref[...] + lnb_ref[...] + bv_ref[...]) * sz_ref[...]).astype(oa_ref.dtype)

    def _store_state():
        for g in hps:
            st_ref[:, 2 * g] = s_scr[:, g, 0:HEAD_DIM, 0:HEAD_DIM]
            st_ref[:, 2 * g + 1] = s_scr[:, g, HEAD_DIM:, HEAD_DIM:]
    _when(last, _store_state)


def _mlstm_body(q_ref, k_ref, v_ref, g_ref, gob_ref, c0_ref, n0_ref, m0_ref, gb_ref, gnw_ref, ones_ref,
                ob_ref, ct_ref, nt_ref, mt_ref, cn_scr, m_scr, *, n_seq, n_heads, first, last):
    M = q_ref.shape[0]
    L = M // n_seq
    heads = range(n_heads)
    seqs = range(n_seq)

    def _load_state():
        cn_scr[:, :, 0:HEAD_DIM, :] = c0_ref[...]
        cn_scr[:, :, HEAD_DIM:, :] = n0_ref[...]
        m_scr[...] = m0_ref[...]
    _when(first, _load_state)

    row = lax.broadcasted_iota(jnp.int32, (M, M), 0)
    col = lax.broadcasted_iota(jnp.int32, (M, M), 1)
    same_seq = _shr(row, int(math.log2(L))) == _shr(col, int(math.log2(L)))
    incl = (row >= col) & same_seq

    g = g_ref[...] + gb_ref[...]
    bcum = pltpu.roll(_cumsum_time(-_softplus(-g), incl.astype(BF16)), LANES - n_heads, axis=1)
    m_prev = m_scr[...]
    m_prev_rows = _rows_of_seq(m_prev, n_seq, L)
    x_all = g - bcum
    pos = lax.broadcasted_iota(jnp.int32, (M, LANES), 0) & (L - 1)
    m_all = bcum + jnp.maximum(_cummax_time(x_all, L, pos), m_prev_rows)
    bm_all = bcum - m_all
    w_in_all = jnp.exp(bcum + m_prev_rows - m_all)
    e_negm_all = jnp.exp(-m_all)
    m_new = _last_rows(m_all, n_seq)
    b_last = _last_rows(bcum, n_seq)
    ws_all = jnp.exp(_rows_of_seq(b_last - m_new, n_seq, L) + x_all)
    dec_all = jnp.exp(b_last + m_prev - m_new)
    m_scr[...] = m_new
    x_t = jnp.transpose(x_all)
    yield

    ones = jnp.ones((M, SUBLANES), F32)
    sls = [slice(h * HEAD_DIM, (h + 1) * HEAD_DIM) for h in heads]
    qs = [q_ref[:, sl] for sl in sls]
    ks = [k_ref[:, sl] for sl in sls]
    v1 = [jnp.concatenate([v_ref[:, sl], ones], axis=-1) for sl in sls]
    cn = [[cn_scr[b, h] for h in heads] for b in seqs]

    qk_t = [_mm_nt(qs[h], ks[h]) for h in heads]
    if n_seq == 1:
        qc = [_mm_nt(qs[h], cn[0][h]) for h in heads]
    else:
        qc = [jnp.concatenate([_mm_nt(qs[h][b * L:(b + 1) * L], cn[b][h]) for b in seqs], axis=0) for h in heads]
    yield
    w_ts = [jnp.where(incl, jnp.exp(bm_all[:, h:h + 1] + x_t[h:h + 1, :]), 0.0) for h in heads]
    yield
    s = [qk_t[h] * w_ts[h] for h in heads]
    numden = [_mm(s[h], v1[h]) + w_in_all[:, h:h + 1] * qc[h] for h in heads]
    yield
    den = [jnp.maximum(jnp.abs(numden[h][:, HEAD_DIM:HEAD_DIM + 1]), e_negm_all[:, h:h + 1]) for h in heads]
    hh = [numden[h][:, 0:HEAD_DIM] / den[h] for h in heads]
    hh_all = jnp.concatenate(hh, axis=-1)
    ssq = _head_sums_mxu(hh_all * hh_all, ones_ref) * (1.0 / HEAD_DIM)

    yield
    v1w = [v1[h] * ws_all[:, h:h + 1] for h in heads]
    for h in heads:
        for b in seqs:
            rows = slice(b * L, (b + 1) * L)
            cn_scr[b, h] = dec_all[b][:, h:h + 1] * cn[b][h] + _mm_tn(v1w[h][rows], ks[h][rows])
    yield

    hb = hh_all * lax.rsqrt(ssq + MLSTM_GN_EPS)
    ob_ref[...] = (hb * gnw_ref[...] * gob_ref[...]).astype(ob_ref.dtype)

    def _store_state():
        ct_ref[...] = cn_scr[:, :, 0:HEAD_DIM, :]
        nt_ref[...] = cn_scr[:, :, HEAD_DIM:, :]
        mt_ref[...] = m_scr[...]
    _when(last, _store_state)


N_RWKV_IN, N_RWKV_OUT, N_MLSTM_IN, N_MLSTM_OUT = 14, 2, 11, 4
N_MLSTM_ROW_IN = 5
N_RWKV_ROW_IN = 9
RWKV_STAGES, MLSTM_STAGES = 29, 8
RWKV_BODY_ROWS = 256
RWKV_STAGGER = 0.6


def _mixers_kernel(*refs, n_seq, chunk, n_heads_a, n_heads_b):
    it = iter(refs)
    take = lambda n: [next(it) for _ in range(n)]
    r_in, m_in = take(N_RWKV_IN), take(N_MLSTM_IN)
    r_out, m_out = take(N_RWKV_OUT), take(N_MLSTM_OUT)
    s_scr, cn_scr, m_scr = take(3)
    first = pl.program_id(1) == 0
    last = pl.program_id(1) == pl.num_programs(1) - 1
    M = m_in[0].shape[0]
    n_r = M // RWKV_BODY_ROWS if n_seq == 1 and M > RWKV_BODY_ROWS else 1
    n_sub = M // MLSTM_ROWS if n_seq == 1 and M > MLSTM_ROWS else 1
    plan = []
    flags = [[] for _ in range(n_r)]
    for h in range(n_r):
        rows = (lambda r: r.at[pl.ds(h * RWKV_BODY_ROWS, RWKV_BODY_ROWS)]) if n_r > 1 else (lambda r: r)
        chunks = (lambda r: r.at[pl.ds(h * (RWKV_BODY_ROWS // chunk), RWKV_BODY_ROWS // chunk)]) if n_r > 1 else (
            lambda r: r)
        body = _rwkv_body(
            *[rows(r) for r in r_in[:N_RWKV_ROW_IN]], chunks(r_in[N_RWKV_ROW_IN]), *r_in[N_RWKV_ROW_IN + 1:],
            rows(r_out[0]), r_out[1], s_scr, n_seq=n_seq, chunk=chunk, n_heads=n_heads_a,
            first=first if h == 0 else None, last=last if h == n_r - 1 else None,
            after=flags[h - 1] if h else None, done=flags[h])
        plan.append((body, RWKV_STAGES, RWKV_STAGGER * h, 1.0))
    mlstm = []
    for h in range(n_sub):
        rows = (lambda r: r.at[pl.ds(h * MLSTM_ROWS, MLSTM_ROWS)]) if n_sub > 1 else (lambda r: r)
        mlstm.append(_mlstm_body(
            *[rows(r) for r in m_in[:N_MLSTM_ROW_IN]], *m_in[N_MLSTM_ROW_IN:], rows(m_out[0]), *m_out[1:],
            cn_scr, m_scr,
            n_seq=n_seq, n_heads=n_heads_b,
            first=first if h == 0 else None, last=last if h == n_sub - 1 else None))
    plan.append((_chain(*mlstm), MLSTM_STAGES * n_sub, 0.0, 1.0 + RWKV_STAGGER * (n_r - 1)))
    _run(_interleave(plan))


def _mixers(acts, v_a, bv, sz, p_last, s0, ln_w, ln_b, q, k, v_b, gates, gob, c0, n0, m0, gate_b, gn_w,
            n_seq, t_step, chunk):
    b = s0.shape[0]
    n_ha, n_hb = s0.shape[1], c0.shape[1]
    n_tok, da = v_a.shape
    db = v_b.shape[1]
    m = n_seq * t_step
    nj = n_tok // (b * t_step)
    tok = lambda i, j: (i * nj + j, 0)
    per_b3 = lambda i, j: (i, 0, 0)
    per_b4 = lambda i, j: (i, 0, 0, 0)
    const2 = lambda i, j: (0, 0)
    s_spec = pl.BlockSpec((n_seq, n_ha, HEAD_DIM, HEAD_DIM), per_b4)
    c_spec = pl.BlockSpec((n_seq, n_hb, HEAD_DIM, HEAD_DIM), per_b4)
    n_spec = pl.BlockSpec((n_seq, n_hb, SUBLANES, HEAD_DIM), per_b4)
    m_spec = pl.BlockSpec((n_seq, 1, LANES), per_b3)
    row_a, row_b = pl.BlockSpec((m, da), tok), pl.BlockSpec((m, db), tok)
    in_specs = [row_a] * 9
    ones_a = jnp.kron(jnp.eye(n_ha, dtype=F32), jnp.ones((HEAD_DIM, HEAD_DIM), F32)).astype(BF16)
    ones_b = jnp.kron(jnp.eye(n_hb, dtype=F32), jnp.ones((HEAD_DIM, HEAD_DIM), F32)).astype(BF16)
    in_specs += [pl.BlockSpec((m // chunk, 1, da), lambda i, j: (i * nj + j, 0, 0)), s_spec,
                 pl.BlockSpec(ln_w.shape, const2), pl.BlockSpec(ln_b.shape, const2),
                 pl.BlockSpec(ones_a.shape, const2)]
    in_specs += [row_b, row_b, row_b, pl.BlockSpec((m, LANES), tok), row_b, c_spec, n_spec, m_spec,
                 pl.BlockSpec(gate_b.shape, const2), pl.BlockSpec(gn_w.shape, const2),
                 pl.BlockSpec(ones_b.shape, const2)]
    return pl.pallas_call(
        functools.partial(_mixers_kernel, n_seq=n_seq, chunk=chunk, n_heads_a=n_ha, n_heads_b=n_hb),
        grid=(b // n_seq, nj),
        in_specs=in_specs,
        out_specs=[row_a, s_spec, row_b, c_spec, n_spec, m_spec],
        out_shape=[jax.ShapeDtypeStruct((n_tok, da), BF16), jax.ShapeDtypeStruct(s0.shape, F32),
                   jax.ShapeDtypeStruct((n_tok, db), BF16), jax.ShapeDtypeStruct(c0.shape, F32),
                   jax.ShapeDtypeStruct(n0.shape, F32), jax.ShapeDtypeStruct(m0.shape, F32)],
        scratch_shapes=[pltpu.VMEM((n_seq, n_ha // 2, 2 * HEAD_DIM, 2 * HEAD_DIM), F32),
                        pltpu.VMEM((n_seq, n_hb, HEAD_DIM + SUBLANES, HEAD_DIM), F32),
                        pltpu.VMEM((n_seq, 1, LANES), F32)],
        compiler_params=pltpu.CompilerParams(vmem_limit_bytes=VMEM_LIMIT),
        name="mixers",
    )(*acts, v_a, bv, sz, p_last, s0, ln_w, ln_b, ones_a, q, k, v_b, gates, gob, c0, n0, m0, gate_b, gn_w, ones_b)


def _out_body(oa_ref, ob_ref, x_ref, sc_ref, sh_ref, gate_ref, g_ref, wga_ref, wgb_ref, wua_ref, wub_ref, wo_ref,
              gf_ref, y_ref, *, nb, final_norm):
    m = x_ref.shape[0]
    rows = lambda ref: _rows_of_seq(ref[...], nb, m // nb)
    x = x_ref[...]
    h = x * lax.rsqrt(jnp.mean(x * x, axis=-1, keepdims=True) + NORM_EPS) * g_ref[...]
    h = (h * (1.0 + rows(sc_ref)) + rows(sh_ref)).astype(BF16)
    gl_a = jnp.dot(h, wga_ref[...], preferred_element_type=F32)
    yield
    gl_b = jnp.dot(h, wgb_ref[...], preferred_element_type=F32)
    yield
    ua = jnp.dot(oa_ref[...], wua_ref[...], preferred_element_type=F32)
    ub = jnp.dot(ob_ref[...], wub_ref[...], preferred_element_type=F32)
    yield
    merged = _sigmoid(gl_a) * ua + _sigmoid(gl_b) * ub
    mo = jnp.dot(merged.astype(BF16), wo_ref[...], preferred_element_type=F32)
    yield
    xn = x + rows(gate_ref) * mo
    if final_norm:
        xn = xn * lax.rsqrt(jnp.mean(xn * xn, axis=-1, keepdims=True) + NORM_EPS) * gf_ref[...]
    y_ref[...] = xn


def _out_kernel(*refs, nb, final_norm, n_split):
    if n_split == 1:
        _run(_out_body(*refs, nb=nb, final_norm=final_norm))
        return
    oa_ref, ob_ref, x_ref = refs[:3]
    y_ref = refs[-1]
    tp = x_ref.shape[0] // n_split
    plan = []
    for h in range(n_split):
        rows = pl.ds(h * tp, tp)
        body = _out_body(oa_ref.at[rows], ob_ref.at[rows], x_ref.at[rows], *refs[3:-1], y_ref.at[rows],
                         nb=nb, final_norm=final_norm)
        plan.append((body, 4, 0.5 * h, 1.0))
    _run(_interleave(plan))


def _out(out_a, out_b, x, scale, shift, gate, g_norm, w_gl_a, w_gl_b, w_up_a, w_up_b, w_out, g_final,
         b, nb, tt, final_norm):
    n_tok, d = x.shape
    t = n_tok // b
    n_split = OUT_SPLIT if nb == 1 and t % (OUT_SPLIT * tt) == 0 else 1
    nj = t // (n_split * tt)
    m = nb * n_split * tt
    tok = lambda i, j: (i * nj + j, 0)
    const2 = lambda i, j: (0, 0)
    per_b = lambda i, j: (i, 0, 0)
    wspec = lambda w: pl.BlockSpec(w.shape, const2, pipeline_mode=pl.Buffered(1))
    in_specs = [pl.BlockSpec((m, out_a.shape[-1]), tok),
                pl.BlockSpec((m, out_b.shape[-1]), tok),
                pl.BlockSpec((m, d), tok),
                pl.BlockSpec((nb, 1, d), per_b),
                pl.BlockSpec((nb, 1, d), per_b),
                pl.BlockSpec((nb, 1, d), per_b),
                pl.BlockSpec((1, d), const2),
                wspec(w_gl_a), wspec(w_gl_b), wspec(w_up_a), wspec(w_up_b), wspec(w_out),
                pl.BlockSpec((1, d), const2)]
    return pl.pallas_call(
        functools.partial(_out_kernel, nb=nb, final_norm=final_norm, n_split=n_split),
        grid=(b // nb, nj),
        in_specs=in_specs,
        out_specs=pl.BlockSpec((m, d), tok),
        out_shape=jax.ShapeDtypeStruct((n_tok, d), F32),
        compiler_params=pltpu.CompilerParams(vmem_limit_bytes=VMEM_LIMIT),
        name="merge_out",
    )(out_a, out_b, x, scale, shift, gate, g_norm.reshape(1, d), w_gl_a, w_gl_b, w_up_a, w_up_b, w_out,
      g_final.reshape(1, d))


def _token_tiling(b, t, target):
    if t >= target:
        return 1, math.gcd(t, target)
    return math.gcd(b, target // t), t


def _layer(x, mod, states, lp, g_final, final_norm):
    b, t, d = x.shape
    shift0, s0, conv0, c0, n0, m0 = states
    n_hb = c0.shape[1]
    cw = lp["conv_w"].shape[0]
    row = lambda a: a.reshape(1, -1)

    ada_shift, ada_scale, ada_gate = (mod[:, None, i * d:(i + 1) * d] for i in range(3))
    nb, tt = _token_tiling(b, t, PROJ_ROWS)
    L = math.gcd(t, ROWS)
    act_dtype = BF16 if L % 16 == 0 else F32
    conv0p = jnp.pad(conv0, ((0, 0), (SUBLANES - (cw - 1), 0), (0, 0)))
    n0p = jnp.broadcast_to(n0[:, :, None, :], (b, n_hb, SUBLANES, HEAD_DIM))
    m0p = jnp.pad(m0, ((0, 0), (0, LANES - n_hb)))[:, None, :]
    gate_b = jnp.zeros((1, LANES), F32).at[0, 0:n_hb].set(lp["b_i"]).at[0, n_hb:2 * n_hb].set(lp["b_f"])
    n_seq, t_step = _token_tiling(b, t, MIX_ROWS if t >= MIX_ROWS else MLSTM_ROWS)
    (kq, rq, bk, kkk, bk2, kk2, v_a, bv, sz, p_last, shift_t,
     q, k, v_b, gates, gob, conv_tail) = _inproj(
        x, ada_scale, ada_shift, lp["g_norm"], shift0[:, None, :], conv0p, lp, nb, tt, L, act_dtype)
    out_a, s_t, out_b, c_t, n_t, m_t = _mixers(
        (kq, rq, bk, kkk, bk2, kk2), v_a, bv, sz, p_last, s0, row(lp["ln_w"]), row(lp["ln_b"]),
        q, k, v_b, gates, gob, c0, n0p, m0p, gate_b, row(lp["gn_w"]), n_seq, t_step, L)

    y = _out(out_a, out_b, x.reshape(b * t, d), ada_scale, ada_shift, ada_gate, lp["g_norm"],
             lp["w_gl_a"], lp["w_gl_b"], lp["w_up_a"], lp["w_up_b"], lp["w_out"],
             g_final, b, nb, tt, final_norm).reshape(b, t, d)
    new_states = (shift_t[:, 0], s_t, conv_tail[:, SUBLANES - (cw - 1):], c_t, n_t[:, :, 0, :], m_t[:, 0, :n_hb])
    return y, new_states


def _trunk(x, mods, states, layers, g_final):
    depth = len(layers)
    new = [[] for _ in states]
    for l in range(depth):
        st = tuple(s[l] for s in states)
        x, st_new = _layer(x, mods[l], st, layers[l], g_final, final_norm=(l == depth - 1))
        for lst, s in zip(new, st_new):
            lst.append(s.astype(x.dtype))
    return x, tuple(jnp.stack(lst) for lst in new)


def kernel(x_prompt, x_sample, c_prompt, c_sample, state_rwkv_shift, state_rwkv_S, state_mlstm_conv, state_mlstm_C, state_mlstm_n, state_mlstm_m, g_norm, w_ada, b_ada, w_in, mu_shift, w_decay2, w0, w_iclr2, a0, k_k, k_a, r_k, ln_w, ln_b, conv_w, conv_b, b_i, b_f, gn_w, w_up_a, w_up_b, w_out, g_final):
    depth = g_norm.shape[0]
    bp, bs = x_prompt.shape[0], x_sample.shape[0]
    d = x_prompt.shape[-1]
    da, db = w_up_a.shape[1], w_up_b.shape[1]
    n_ha, n_hb = da // HEAD_DIM, db // HEAD_DIM
    sw = mu_shift.shape[-1]
    cc = conv_w.shape[-1]
    cw = conv_w.shape[1]
    dt = x_prompt.dtype

    sizes = (sw, da, cc, db, db, n_hb, n_hb, db, d, d)
    offs = [0]
    for s in sizes:
        offs.append(offs[-1] + s)
    seg = lambda w, i: w[:, offs[i]:offs[i + 1]]

    c_all = jnp.concatenate([c_prompt, c_sample], axis=0)
    n_c = c_all.shape[0]
    c_pad = jnp.pad(c_all, ((0, (-n_c) % SUBLANES), (0, 0)))

    layers, mods_p, mods_s = [], [], []
    for l in range(depth):
        w = w_in[l]
        wg = jnp.concatenate([seg(w, 5), seg(w, 6)], axis=1)
        wg = jnp.pad(wg, ((0, 0), (0, LANES - wg.shape[1])))
        w_segs = tuple(x.astype(BF16) for x in
                       (seg(w, 0), seg(w, 1), seg(w, 2), seg(w, 3), seg(w, 4), seg(w, 7), wg))
        layers.append(dict(
            g_norm=g_norm[l], w_segs=w_segs, w_gl_a=seg(w, 8).astype(BF16), w_gl_b=seg(w, 9).astype(BF16),
            mu_shift=mu_shift[l], w_decay2=w_decay2[l].astype(BF16),
            w0=w0[l], w_iclr2=w_iclr2[l].astype(BF16), a0=a0[l], k_k=k_k[l], k_a=k_a[l], r_k=r_k[l],
            ln_w=ln_w[l], ln_b=ln_b[l], conv_w=conv_w[l], conv_b=conv_b[l], b_i=b_i[l], b_f=b_f[l],
            gn_w=gn_w[l], w_up_a=w_up_a[l].astype(BF16), w_up_b=w_up_b[l].astype(BF16),
            w_out=w_out[l].astype(BF16)))
        mod = _ada(c_pad, w_ada[l], b_ada[l])
        mods_p.append(mod[:bp])
        mods_s.append(mod[bp:bp + bs])

    prompt_states = (
        jnp.zeros((depth, bp, sw), dt),
        jnp.zeros((depth, bp, n_ha, HEAD_DIM, HEAD_DIM), dt),
        jnp.zeros((depth, bp, cw - 1, cc), dt),
        jnp.zeros((depth, bp, n_hb, HEAD_DIM, HEAD_DIM), dt),
        jnp.zeros((depth, bp, n_hb, HEAD_DIM), dt),
        jnp.zeros((depth, bp, n_hb), dt),
    )
    sample_states = (state_rwkv_shift, state_rwkv_S, state_mlstm_conv,
                     state_mlstm_C, state_mlstm_n, state_mlstm_m)
    y_p, st_p = _trunk(x_prompt, mods_p, prompt_states, layers, g_final)
    y_s, st_s = _trunk(x_sample, mods_s, sample_states, layers, g_final)
    return (y_p, y_s) + st_p + st_s
```

```python
import functools
import math

import jax
import jax.numpy as jnp
from jax import lax
from jax.experimental import pallas as pl
from jax.experimental.pallas import tpu as pltpu

F32 = jnp.float32
BF16 = jnp.bfloat16

HEAD_DIM = 64
NORM_EPS = 1e-6
RWKV_GN_EPS = 64e-5
MLSTM_GN_EPS = 1e-6
ROWS = 64
MLSTM_ROWS = 128
MIX_ROWS = 1024
PROJ_ROWS = 256
INPROJ_SPLIT = 2
OUT_SPLIT = 2
SUBLANES = 8
LANES = 128
TRI_BASE = 4
VMEM_LIMIT = 60 * 1024 * 1024

LOG2_E = 1.4426950408889634
NT_DIMS = (((1,), (1,)), ((), ()))
TN_DIMS = (((0,), (0,)), ((), ()))


def _mm(a, b):
    return jnp.dot(a.astype(BF16), b.astype(BF16), preferred_element_type=F32)


def _mm_nt(a, b):
    return lax.dot_general(a.astype(BF16), b.astype(BF16), NT_DIMS, preferred_element_type=F32)


def _mm_tn(a, b):
    return lax.dot_general(a.astype(BF16), b.astype(BF16), TN_DIMS, preferred_element_type=F32)


def _sigmoid(x):
    return 1.0 / (1.0 + jnp.exp2(x * -LOG2_E))


def _silu(x):
    return x * _sigmoid(x)


def _softplus(x):
    return jnp.maximum(x, 0.0) + jnp.log(1.0 + jnp.exp(-jnp.abs(x)))


def _cumsum_time(x, tri_bf16, terms=3):
    dot = functools.partial(jnp.dot, preferred_element_type=F32)
    rest = x
    total = None
    for i in range(terms):
        part = rest.astype(BF16)
        if i + 1 < terms:
            rest = rest - part.astype(F32)
        total = dot(tri_bf16, part) if total is None else total + dot(tri_bf16, part)
    return total


def _shr(x, n):
    return lax.shift_right_logical(x, jnp.full(x.shape, n, x.dtype))


def _rows_of_seq(x, n_seq, L):
    if n_seq == 1:
        return x[0]
    return jnp.broadcast_to(x, (n_seq, L, x.shape[-1])).reshape(n_seq * L, x.shape[-1])


def _last_rows(x, n_seq):
    m, n = x.shape
    L = m // n_seq
    return x.reshape(n_seq, L, n)[:, L - 1:L, :]


def _head_sums(x):
    m, n = x.shape
    parts = [jnp.broadcast_to(jnp.sum(x[:, o:o + HEAD_DIM], axis=-1, keepdims=True), (m, HEAD_DIM))
             for o in range(0, n, HEAD_DIM)]
    return jnp.concatenate(parts, axis=-1)


def _head_sums_mxu(x, ones_ref):
    hi = x.astype(BF16)
    lo = (x - hi.astype(F32)).astype(BF16)
    s = jnp.dot(jnp.concatenate([hi, lo], axis=0), ones_ref[...], preferred_element_type=F32)
    return s[:x.shape[0]] + s[x.shape[0]:]


def _cummax_time(x, L, pos):
    y = x
    sh = 1
    while sh < L:
        y = jnp.where(pos >= sh, jnp.maximum(y, pltpu.roll(y, sh, axis=0)), y)
        sh *= 2
    return y


def _tri_inverse_all(mats, row, col, L, mm=_mm):
    bs = min(TRI_BASE, L)
    sh = int(math.log2(bs))
    same = _shr(row, sh) == _shr(col, sh)
    eye = (row == col).astype(F32)
    rows = mats[0].shape[0]
    ns = [jnp.where(same, -a, 0.0) for a in mats]
    ts = [eye + n for n in ns]
    if bs > 2:
        ps = [mm(n, n) for n in ns]
        yield
        k = 2
        while 2 * k < bs:
            both = [mm(jnp.concatenate([t, p], axis=0), p) for t, p in zip(ts, ps)]
            ts = [t + x[:rows] for t, x in zip(ts, both)]
            ps = [x[rows:] for x in both]
            yield
            k *= 2
        ts = [t + mm(t, p) for t, p in zip(ts, ps)]
        yield
    size = bs
    while size < L:
        sh = int(math.log2(size))
        lower_left = (_shr(row, sh + 1) == _shr(col, sh + 1)) & (_shr(row, sh) != _shr(col, sh))
        offs = [jnp.where(lower_left, a, 0.0) for a in mats]
        tmp = [mm(t, o) for t, o in zip(ts, offs)]
        yield
        ts = [t - mm(x, t) for t, x in zip(ts, tmp)]
        yield
        size *= 2
    return ts


def _interleave(plan):
    live = [[g, n, s, w, 0] for g, n, s, w in plan]
    while live:
        item = min(live, key=lambda it: it[2] + it[3] * it[4] / it[1])
        try:
            next(item[0])
            item[4] += 1
            yield
        except StopIteration:
            live.remove(item)


def _chain(*gens):
    for g in gens:
        yield from g


def _run(gen):
    for _ in gen:
        pass


def _when(pred, fn):
    if pred is None:
        return
    if pred is True:
        fn()
    else:
        pl.when(pred)(fn)


def _ada_kernel(c_ref, w_ref, b_ref, o_ref):
    o_ref[...] = _mm(_silu(c_ref[...]), w_ref[...]) + b_ref[...]


def _ada(c, w_ada, b_ada):
    n, d = c.shape
    n3 = w_ada.shape[1]
    tn = d
    return pl.pallas_call(
        _ada_kernel,
        grid=(n3 // tn,),
        in_specs=[pl.BlockSpec((n, d), lambda j: (0, 0)),
                  pl.BlockSpec((d, tn), lambda j: (0, j)),
                  pl.BlockSpec((1, tn), lambda j: (0, j))],
        out_specs=pl.BlockSpec((n, tn), lambda j: (0, j)),
        out_shape=jax.ShapeDtypeStruct((n, n3), F32),
        name="ada",
    )(c, w_ada, b_ada.reshape(1, n3))


PROJ_COLS = 256


def _inproj_body(x_ref, sc_ref, sh_ref, g_ref, shift0_ref, conv0_ref,
                 w_ref,
                 mu_ref, w0_ref, a0_ref, kk_ref, ka_ref, rk_ref, wd2_ref, wi2_ref, cw_ref, cb_ref,
                 kq_o, rq_o, bk_o, kkk_o, bk2_o, kk2_o, va_o, bv_o, sz_o, plast_o, shst_o,
                 q_o, k_o, vb_o, gt_o, gob_o, cst_o,
                 prev_scr, xbuf, *, chunk, lora_w, conv_w, cols, first, live):
    nb, tt, d = x_ref.shape
    M = nb * tt
    L = chunk
    n_chunks = M // L
    da = va_o.shape[-1]
    db = vb_o.shape[-1]
    cc = xbuf.shape[-1]
    pad = SUBLANES

    def _init():
        prev_scr[...] = shift0_ref[...]
        xbuf[:, 0:pad, :] = conv0_ref[...]
    _when(first, _init)

    x = x_ref[...]
    y = x * lax.rsqrt(jnp.mean(x * x, axis=-1, keepdims=True) + NORM_EPS) * g_ref[...]
    h = (y * (1.0 + sc_ref[...]) + sh_ref[...]).reshape(M, d).astype(BF16)
    def proj(name):
        lo, hi = cols[name]
        parts = []
        for o in range(lo, hi, PROJ_COLS):
            parts.append(jnp.dot(h, w_ref[:, o:min(o + PROJ_COLS, hi)], preferred_element_type=F32))
            yield
        return parts[0] if len(parts) == 1 else jnp.concatenate(parts, axis=-1)

    p = yield from proj("shift")
    row_w = lax.broadcasted_iota(jnp.int32, p.shape, 0)
    first = (row_w & (tt - 1)) == 0
    prev = jnp.where(first, _rows_of_seq(prev_scr[...], nb, tt), pltpu.roll(p, 1, axis=0))
    last = _last_rows(p, nb)
    prev_scr[...] = last

    def _shift_state():
        shst_o[...] = last
    _when(live, _shift_state)
    ps = p + mu_ref[...] * (prev - p)
    r = ps[:, 0:da]
    k = ps[:, da:2 * da]
    v = ps[:, 2 * da:3 * da]
    wl = ps[:, 3 * da:3 * da + lora_w]
    al = ps[:, 3 * da + lora_w:]

    qk_pre = yield from proj("qk")
    z_a = yield from proj("z_a")
    w = w0_ref[...] + _mm(jnp.tanh(wl), wd2_ref[...])
    lw = -math.exp(-0.5) * _sigmoid(w)
    a = _sigmoid(a0_ref[...] + _mm(al, wi2_ref[...]))
    yield
    v_b = yield from proj("v_b")
    g_t = yield from proj("gates")
    o_b = yield from proj("o_b")
    z_b = yield from proj("z_b")
    kkr = k * kk_ref[...]
    kk = kkr * lax.rsqrt(jnp.maximum(_head_sums(kkr * kkr), 1e-24))
    k2 = k * (1.0 + (a - 1.0) * ka_ref[...])
    b = kk * a
    va_o[...] = v
    bv_o[...] = _head_sums(r * k2 * rk_ref[...]) * v
    sz_o[...] = _silu(z_a)

    lg = int(math.log2(L))
    row_m = lax.broadcasted_iota(jnp.int32, (M, M), 0)
    col_m = lax.broadcasted_iota(jnp.int32, (M, M), 1)
    tri = ((row_m >= col_m) & (_shr(row_m, lg) == _shr(col_m, lg))).astype(BF16)
    c = _cumsum_time(lw, tri, terms=2)
    yield
    c_last = _last_rows(c, n_chunks)
    plast_o[...] = jnp.exp(c_last)
    e_nc = jnp.exp(-c)
    e_cl = jnp.exp(_rows_of_seq(c_last, n_chunks, L) - c)
    kq_o[...] = (kk * jnp.exp(c - lw)).astype(kq_o.dtype)
    rq_o[...] = (r * jnp.exp(c)).astype(rq_o.dtype)
    bk_o[...] = (b * e_nc).astype(bk_o.dtype)
    kkk_o[...] = (k2 * e_nc).astype(kkk_o.dtype)
    bk2_o[...] = (b * e_cl).astype(bk2_o.dtype)
    kk2_o[...] = (k2 * e_cl).astype(kk2_o.dtype)

    xbuf[:, pad:pad + tt, :] = qk_pre.reshape(nb, tt, cc)
    xe = xbuf[...].reshape(nb * (tt + pad), cc)
    conv = cb_ref[...] + qk_pre * cw_ref[conv_w - 1:conv_w, :]
    for back in range(1, conv_w):
        shifted = pltpu.roll(xe, back, axis=0).reshape(nb, tt + pad, cc)[:, pad:, :].reshape(M, cc)
        conv = conv + shifted * cw_ref[conv_w - 1 - back:conv_w - back, :]
    tail = xbuf[:, tt:tt + pad, :]
    xbuf[:, 0:pad, :] = tail

    def _conv_state():
        cst_o[...] = tail
    _when(live, _conv_state)
    yield
    qk = _silu(conv)
    q_o[...] = qk[:, 0:db].astype(q_o.dtype)
    k_o[...] = (qk[:, db:] * (1.0 / math.sqrt(HEAD_DIM))).astype(k_o.dtype)
    vb_o[...] = v_b
    gt_o[...] = g_t
    gob_o[...] = _sigmoid(o_b) * _silu(z_b)


N_INPROJ_IN = 17
INPROJ_ROW_OUTS = (0, 1, 2, 3, 4, 5, 6, 7, 8, 11, 12, 13, 14, 15)
INPROJ_CHUNK_OUT = 9
INPROJ_STAGES = 28


def _inproj_kernel(*refs, n_split, chunk, **static):
    ins, outs, scratch = refs[:N_INPROJ_IN], refs[N_INPROJ_IN:-2], refs[-2:]
    first = pl.program_id(1) == 0
    if n_split == 1:
        _run(_inproj_body(*refs, chunk=chunk, first=first, live=True, **static))
        return
    tp = ins[0].shape[1] // n_split
    plan = []
    for h in range(n_split):
        rows = pl.ds(h * tp, tp)
        t_ins = [ins[0].at[:, rows]] + list(ins[1:])
        t_outs = [o.at[rows] if i in INPROJ_ROW_OUTS else o for i, o in enumerate(outs)]
        t_outs[INPROJ_CHUNK_OUT] = outs[INPROJ_CHUNK_OUT].at[pl.ds(h * (tp // chunk), tp // chunk)]
        body = _inproj_body(*t_ins, *t_outs, *scratch, chunk=chunk,
                            first=first if h == 0 else None, live=True, **static)
        plan.append((body, INPROJ_STAGES, 0.5 * h, 1.0))
    _run(_interleave(plan))


def _inproj(x, scale, shift, g_norm, shift0, conv0p, lp, nb, tt, chunk, act_dtype):
    b, t, d = x.shape
    n_tok = b * t
    n_split = INPROJ_SPLIT if nb == 1 and t % (INPROJ_SPLIT * tt) == 0 else 1
    tb = n_split * tt
    m = nb * tb
    nj = t // tb
    sw = shift0.shape[-1]
    cc = conv0p.shape[-1]
    da, db = lp["w_up_a"].shape[0], lp["w_up_b"].shape[0]
    cw = lp["conv_w"].shape[0]
    row = lambda a: a.reshape(1, -1)
    rows = [row(lp[n]) for n in ("mu_shift", "w0", "a0", "k_k", "k_a", "r_k")]
    smalls = rows + [lp["w_decay2"], lp["w_iclr2"], lp["conv_w"], row(lp["conv_b"])]
    const = lambda i, j: (0, 0)
    tok = lambda i, j: (i * nj + j, 0)
    per_b = lambda i, j: (i, 0, 0)
    in_specs = [pl.BlockSpec((nb, tb, d), lambda i, j: (i, j, 0)),
                pl.BlockSpec((nb, 1, d), per_b),
                pl.BlockSpec((nb, 1, d), per_b),
                pl.BlockSpec((1, d), const),
                pl.BlockSpec((nb, 1, sw), per_b),
                pl.BlockSpec((nb, SUBLANES, cc), per_b)]
    in_specs += [pl.BlockSpec(lp["w_all"].shape, const, pipeline_mode=pl.Buffered(1))]
    in_specs += [pl.BlockSpec(a.shape, const) for a in smalls]
    tok_out = lambda n, dt: (pl.BlockSpec((m, n), tok), jax.ShapeDtypeStruct((n_tok, n), dt))
    outs = [tok_out(da, act_dtype)] * 6 + [tok_out(da, F32)] * 3
    outs += [(pl.BlockSpec((m // chunk, 1, da), lambda i, j: (i * nj + j, 0, 0)),
              jax.ShapeDtypeStruct((n_tok // chunk, 1, da), F32)),
             (pl.BlockSpec((nb, 1, sw), per_b), jax.ShapeDtypeStruct((b, 1, sw), F32))]
    outs += [tok_out(db, act_dtype)] * 2 + [tok_out(db, F32), tok_out(LANES, F32), tok_out(db, F32)]
    outs += [(pl.BlockSpec((nb, SUBLANES, cc), per_b), jax.ShapeDtypeStruct((b, SUBLANES, cc), F32))]
    return pl.pallas_call(
        functools.partial(_inproj_kernel, chunk=chunk, lora_w=lp["w_decay2"].shape[0], conv_w=cw,
                          cols=lp["cols"], n_split=n_split),
        grid=(b // nb, nj),
        in_specs=in_specs,
        out_specs=[o[0] for o in outs], out_shape=[o[1] for o in outs],
        scratch_shapes=[pltpu.VMEM((nb, 1, sw), F32),
                        pltpu.VMEM((nb, tt + SUBLANES, cc), F32)],
        compiler_params=pltpu.CompilerParams(vmem_limit_bytes=VMEM_LIMIT),
        name="inproj",
    )(x, scale, shift, g_norm.reshape(1, d), shift0, conv0p, lp["w_all"], *smalls)


def _pair_blockdiag(y, left):
    return jnp.concatenate([jnp.where(left, y, 0.0), jnp.where(left, 0.0, y)], axis=0)


def _rwkv_body(kq_ref, rq_ref, bk_ref, kkk_ref, bk2_ref, kk2_ref, v_ref, bv_ref, sz_ref, plast_ref,
               s0_ref, lnw_ref, lnb_ref, ones_ref, oa_ref, st_ref, s_scr, *, n_seq, chunk, n_heads,
               first, last, after, done):
    M = kq_ref.shape[0]
    L = chunk
    t_step = M // n_seq
    RU = min(M, ROWS)
    PW = 2 * HEAD_DIM
    n_units = M // RU
    n_hp = n_heads // 2
    units = range(n_units)
    hps = range(n_hp)
    pairs = [(u, g) for u in units for g in hps]
    zero_blk = jnp.zeros((n_seq, HEAD_DIM, HEAD_DIM), F32)

    def _load_state():
        for g in hps:
            top = jnp.concatenate([s0_ref[:, 2 * g], zero_blk], axis=-1)
            bot = jnp.concatenate([zero_blk, s0_ref[:, 2 * g + 1]], axis=-1)
            s_scr[:, g] = jnp.concatenate([top, bot], axis=-2)
    _when(first, _load_state)

    lg = int(math.log2(L))
    row = lax.broadcasted_iota(jnp.int32, (RU, 2 * RU), 0)
    lane = lax.broadcasted_iota(jnp.int32, (RU, 2 * RU), 1)
    col = lane & (RU - 1)
    same = _shr(row, lg) == _shr(col, lg)
    incl = (row >= col) & same
    strict = (row > col) & same
    left = lax.broadcasted_iota(jnp.int32, (RU, PW), 1) < HEAD_DIM
    bd_mask = (lax.broadcasted_iota(jnp.int32, (PW, PW), 0) < HEAD_DIM) == (
        lax.broadcasted_iota(jnp.int32, (PW, PW), 1) < HEAD_DIM)
    mmp = lambda x, y: _mm(x, _pair_blockdiag(y, left))

    p_last = plast_ref[...]
    pls = [slice(g * PW, (g + 1) * PW) for g in hps]
    rus = [slice(u * RU, (u + 1) * RU) for u in units]
    blk = lambda ref, q: ref[rus[q[0]], pls[q[1]]]
    kq = {q: blk(kq_ref, q) for q in pairs}
    rq = {q: blk(rq_ref, q) for q in pairs}
    bk2 = {q: blk(bk2_ref, q) for q in pairs}
    kk2 = {q: blk(kk2_ref, q) for q in pairs}
    vs = {q: blk(v_ref, q) for q in pairs}
    qr = {q: jnp.concatenate([kq[q], rq[q]], axis=0).astype(BF16) for q in pairs}
    bkk = {q: jnp.concatenate([_pair_blockdiag(blk(bk_ref, q).astype(F32), left),
                               _pair_blockdiag(blk(kkk_ref, q).astype(F32), left)], axis=0).astype(BF16)
           for q in pairs}
    yield

    gs = {q: _mm_nt(qr[q], bkk[q]) for q in pairs}
    yield
    a_ab = {q: jnp.where(strict, gs[q][:RU, :2 * RU], 0.0) for q in pairs}
    a_ak = {q: jnp.where(strict, gs[q][:RU, 2 * RU:], 0.0) for q in pairs}
    m_rb = {q: jnp.where(incl, gs[q][RU:, :2 * RU], 0.0) for q in pairs}
    m_rk = {q: jnp.where(incl, gs[q][RU:, 2 * RU:], 0.0) for q in pairs}
    yield
    t_inv = dict(zip(pairs, (yield from _tri_inverse_all([a_ab[q] for q in pairs], row, col, L, mmp))))
    akv = {q: mmp(a_ak[q], vs[q]) for q in pairs}
    yield

    assert after is None or after, "the body before this one must have stored its state by now"
    state = {}
    ys = {}
    for u in units:
        if L == RU:
            b = (u * RU) // t_step
            s0 = [state[b, g] if (b, g) in state else s_scr[b, g] for g in hps]
            ws = [_mm_nt(qr[u, g], s0[g]) for g in hps]
            w1 = [x[:RU] for x in ws]
            wr = [x[RU:] for x in ws]
        else:
            n_in = RU // L
            s0s = [[s_scr[u * n_in + i, g] for g in hps] for i in range(n_in)]
            w1, wr = [], []
            for g in hps:
                parts = [_mm_nt(jnp.concatenate([kq[u, g][i * L:(i + 1) * L], rq[u, g][i * L:(i + 1) * L]], axis=0),
                                s0s[i][g]) for i in range(n_in)]
                w1.append(jnp.concatenate([x[:L] for x in parts], axis=0))
                wr.append(jnp.concatenate([x[L:] for x in parts], axis=0))
        yield
        us = [-mmp(t_inv[u, g], w1[g] + akv[u, g]) for g in hps]
        yield
        for g in hps:
            rhs = jnp.concatenate([_pair_blockdiag(us[g], left), _pair_blockdiag(vs[u, g].astype(F32), left)], axis=0)
            ys[u, g] = wr[g] + _mm(jnp.concatenate([m_rb[u, g], m_rk[u, g]], axis=1), rhs)
        yield
        if L == RU:
            for g in hps:
                uv = jnp.concatenate([us[g], vs[u, g].astype(F32)], axis=0)
                bkk2 = jnp.concatenate([bk2[u, g], kk2[u, g]], axis=0)
                state[b, g] = s0[g] * p_last[u][:, pls[g]] + jnp.where(bd_mask, _mm_tn(uv, bkk2), 0.0)
        else:
            for g in hps:
                for i in range(n_in):
                    rows = slice(i * L, (i + 1) * L)
                    uv_i = jnp.concatenate([us[g][rows], vs[u, g][rows]], axis=0)
                    bkk2_i = jnp.concatenate([bk2[u, g][rows], kk2[u, g][rows]], axis=0)
                    state[u * n_in + i, g] = (s0s[i][g] * p_last[u * n_in + i][:, pls[g]]
                                              + jnp.where(bd_mask, _mm_tn(uv_i, bkk2_i), 0.0))
        yield
    for (b, g), s_new in state.items():
        s_scr[b, g] = s_new
    done.append(True)

    rows_out = [jnp.concatenate([ys[u, g] for g in hps], axis=-1) for u in units]
    y = rows_out[0] if n_units == 1 else jnp.concatenate(rows_out, axis=0)
    yc = y - _head_sums_mxu(y, ones_ref) * (1.0 / HEAD_DIM)
    yield
    var = _head_sums_mxu(yc * yc, ones_ref) * (1.0 / HEAD_DIM)
    yn = yc * lax.rsqrt(var + RWKV_GN_EPS)
    oa_ref[...] = ((yn * lnw_ref[...] + lnb_ref[...] + bv_ref[...]) * sz_ref[...]).astype(oa_ref.dtype)

    def _store_state():
        for g in hps:
            st_ref[:, 2 * g] = s_scr[:, g, 0:HEAD_DIM, 0:HEAD_DIM]
            st_ref[:, 2 * g + 1] = s_scr[:, g, HEAD_DIM:, HEAD_DIM:]
    _when(last, _store_state)


def _mlstm_body(q_ref, k_ref, v_ref, g_ref, gob_ref, c0_ref, n0_ref, m0_ref, gb_ref, gnw_ref, ones_ref,
                ob_ref, ct_ref, nt_ref, mt_ref, cn_scr, m_scr, *, n_seq, n_heads, first, last):
    M = q_ref.shape[0]
    L = M // n_seq
    heads = range(n_heads)
    seqs = range(n_seq)

    def _load_state():
        cn_scr[:, :, 0:HEAD_DIM, :] = c0_ref[...]
        cn_scr[:, :, HEAD_DIM:, :] = n0_ref[...]
        m_scr[...] = m0_ref[...]
    _when(first, _load_state)

    row = lax.broadcasted_iota(jnp.int32, (M, M), 0)
    col = lax.broadcasted_iota(jnp.int32, (M, M), 1)
    same_seq = _shr(row, int(math.log2(L))) == _shr(col, int(math.log2(L)))
    incl = (row >= col) & same_seq

    g = g_ref[...] + gb_ref[...]
    bcum = pltpu.roll(_cumsum_time(-_softplus(-g), incl.astype(BF16)), LANES - n_heads, axis=1)
    m_prev = m_scr[...]
    m_prev_rows = _rows_of_seq(m_prev, n_seq, L)
    x_all = g - bcum
    pos = lax.broadcasted_iota(jnp.int32, (M, LANES), 0) & (L - 1)
    m_all = bcum + jnp.maximum(_cummax_time(x_all, L, pos), m_prev_rows)
    bm_all = bcum - m_all
    w_in_all = jnp.exp(bcum + m_prev_rows - m_all)
    e_negm_all = jnp.exp(-m_all)
    m_new = _last_rows(m_all, n_seq)
    b_last = _last_rows(bcum, n_seq)
    ws_all = jnp.exp(_rows_of_seq(b_last - m_new, n_seq, L) + x_all)
    dec_all = jnp.exp(b_last + m_prev - m_new)
    m_scr[...] = m_new
    x_t = jnp.transpose(x_all)
    yield

    ones = jnp.ones((M, SUBLANES), F32)
    sls = [slice(h * HEAD_DIM, (h + 1) * HEAD_DIM) for h in heads]
    qs = [q_ref[:, sl] for sl in sls]
    ks = [k_ref[:, sl] for sl in sls]
    v1 = [jnp.concatenate([v_ref[:, sl], ones], axis=-1) for sl in sls]
    cn = [[cn_scr[b, h] for h in heads] for b in seqs]

    qk_t = [_mm_nt(qs[h], ks[h]) for h in heads]
    if n_seq == 1:
        qc = [_mm_nt(qs[h], cn[0][h]) for h in heads]
    else:
        qc = [jnp.concatenate([_mm_nt(qs[h][b * L:(b + 1) * L], cn[b][h]) for b in seqs], axis=0) for h in heads]
    yield
    w_ts = [jnp.where(incl, jnp.exp(bm_all[:, h:h + 1] + x_t[h:h + 1, :]), 0.0) for h in heads]
    yield
    s = [qk_t[h] * w_ts[h] for h in heads]
    numden = [_mm(s[h], v1[h]) + w_in_all[:, h:h + 1] * qc[h] for h in heads]
    yield
    den = [jnp.maximum(jnp.abs(numden[h][:, HEAD_DIM:HEAD_DIM + 1]), e_negm_all[:, h:h + 1]) for h in heads]
    hh = [numden[h][:, 0:HEAD_DIM] / den[h] for h in heads]
    hh_all = jnp.concatenate(hh, axis=-1)
    ssq = _head_sums_mxu(hh_all * hh_all, ones_ref) * (1.0 / HEAD_DIM)

    yield
    v1w = [v1[h] * ws_all[:, h:h + 1] for h in heads]
    for h in heads:
        for b in seqs:
            rows = slice(b * L, (b + 1) * L)
            cn_scr[b, h] = dec_all[b][:, h:h + 1] * cn[b][h] + _mm_tn(v1w[h][rows], ks[h][rows])
    yield

    hb = hh_all * lax.rsqrt(ssq + MLSTM_GN_EPS)
    ob_ref[...] = (hb * gnw_ref[...] * gob_ref[...]).astype(ob_ref.dtype)

    def _store_state():
        ct_ref[...] = cn_scr[:, :, 0:HEAD_DIM, :]
        nt_ref[...] = cn_scr[:, :, HEAD_DIM:, :]
        mt_ref[...] = m_scr[...]
    _when(last, _store_state)


N_RWKV_IN, N_RWKV_OUT, N_MLSTM_IN, N_MLSTM_OUT = 14, 2, 11, 4
N_MLSTM_ROW_IN = 5
N_RWKV_ROW_IN = 9
RWKV_STAGES, MLSTM_STAGES = 29, 8
RWKV_BODY_ROWS = 256
RWKV_STAGGER = 0.6


def _mixers_kernel(*refs, n_seq, chunk, n_heads_a, n_heads_b):
    it = iter(refs)
    take = lambda n: [next(it) for _ in range(n)]
    r_in, m_in = take(N_RWKV_IN), take(N_MLSTM_IN)
    r_out, m_out = take(N_RWKV_OUT), take(N_MLSTM_OUT)
    s_scr, cn_scr, m_scr = take(3)
    first = pl.program_id(1) == 0
    last = pl.program_id(1) == pl.num_programs(1) - 1
    M = m_in[0].shape[0]
    n_r = M // RWKV_BODY_ROWS if n_seq == 1 and M > RWKV_BODY_ROWS else 1
    n_sub = M // MLSTM_ROWS if n_seq == 1 and M > MLSTM_ROWS else 1
    plan = []
    flags = [[] for _ in range(n_r)]
    for h in range(n_r):
        rows = (lambda r: r.at[pl.ds(h * RWKV_BODY_ROWS, RWKV_BODY_ROWS)]) if n_r > 1 else (lambda r: r)
        chunks = (lambda r: r.at[pl.ds(h * (RWKV_BODY_ROWS // chunk), RWKV_BODY_ROWS // chunk)]) if n_r > 1 else (
            lambda r: r)
        body = _rwkv_body(
            *[rows(r) for r in r_in[:N_RWKV_ROW_IN]], chunks(r_in[N_RWKV_ROW_IN]), *r_in[N_RWKV_ROW_IN + 1:],
            rows(r_out[0]), r_out[1], s_scr, n_seq=n_seq, chunk=chunk, n_heads=n_heads_a,
            first=first if h == 0 else None, last=last if h == n_r - 1 else None,
            after=flags[h - 1] if h else None, done=flags[h])
        plan.append((body, RWKV_STAGES, RWKV_STAGGER * h, 1.0))
    mlstm = []
    for h in range(n_sub):
        rows = (lambda r: r.at[pl.ds(h * MLSTM_ROWS, MLSTM_ROWS)]) if n_sub > 1 else (lambda r: r)
        mlstm.append(_mlstm_body(
            *[rows(r) for r in m_in[:N_MLSTM_ROW_IN]], *m_in[N_MLSTM_ROW_IN:], rows(m_out[0]), *m_out[1:],
            cn_scr, m_scr,
            n_seq=n_seq, n_heads=n_heads_b,
            first=first if h == 0 else None, last=last if h == n_sub - 1 else None))
    plan.append((_chain(*mlstm), MLSTM_STAGES * n_sub, 0.0, 1.0 + RWKV_STAGGER * (n_r - 1)))
    _run(_interleave(plan))


def _mixers(acts, v_a, bv, sz, p_last, s0, ln_w, ln_b, q, k, v_b, gates, gob, c0, n0, m0, gate_b, gn_w,
            n_seq, t_step, chunk):
    b = s0.shape[0]
    n_ha, n_hb = s0.shape[1], c0.shape[1]
    n_tok, da = v_a.shape
    db = v_b.shape[1]
    m = n_seq * t_step
    nj = n_tok // (b * t_step)
    tok = lambda i, j: (i * nj + j, 0)
    per_b3 = lambda i, j: (i, 0, 0)
    per_b4 = lambda i, j: (i, 0, 0, 0)
    const2 = lambda i, j: (0, 0)
    s_spec = pl.BlockSpec((n_seq, n_ha, HEAD_DIM, HEAD_DIM), per_b4)
    c_spec = pl.BlockSpec((n_seq, n_hb, HEAD_DIM, HEAD_DIM), per_b4)
    n_spec = pl.BlockSpec((n_seq, n_hb, SUBLANES, HEAD_DIM), per_b4)
    m_spec = pl.BlockSpec((n_seq, 1, LANES), per_b3)
    row_a, row_b = pl.BlockSpec((m, da), tok), pl.BlockSpec((m, db), tok)
    in_specs = [row_a] * 9
    ones_a = jnp.kron(jnp.eye(n_ha, dtype=F32), jnp.ones((HEAD_DIM, HEAD_DIM), F32)).astype(BF16)
    ones_b = jnp.kron(jnp.eye(n_hb, dtype=F32), jnp.ones((HEAD_DIM, HEAD_DIM), F32)).astype(BF16)
    in_specs += [pl.BlockSpec((m // chunk, 1, da), lambda i, j: (i * nj + j, 0, 0)), s_spec,
                 pl.BlockSpec(ln_w.shape, const2), pl.BlockSpec(ln_b.shape, const2),
                 pl.BlockSpec(ones_a.shape, const2)]
    in_specs += [row_b, row_b, row_b, pl.BlockSpec((m, LANES), tok), row_b, c_spec, n_spec, m_spec,
                 pl.BlockSpec(gate_b.shape, const2), pl.BlockSpec(gn_w.shape, const2),
                 pl.BlockSpec(ones_b.shape, const2)]
    return pl.pallas_call(
        functools.partial(_mixers_kernel, n_seq=n_seq, chunk=chunk, n_heads_a=n_ha, n_heads_b=n_hb),
        grid=(b // n_seq, nj),
        in_specs=in_specs,
        out_specs=[row_a, s_spec, row_b, c_spec, n_spec, m_spec],
        out_shape=[jax.ShapeDtypeStruct((n_tok, da), BF16), jax.ShapeDtypeStruct(s0.shape, F32),
                   jax.ShapeDtypeStruct((n_tok, db), BF16), jax.ShapeDtypeStruct(c0.shape, F32),
                   jax.ShapeDtypeStruct(n0.shape, F32), jax.ShapeDtypeStruct(m0.shape, F32)],
        scratch_shapes=[pltpu.VMEM((n_seq, n_ha // 2, 2 * HEAD_DIM, 2 * HEAD_DIM), F32),
                        pltpu.VMEM((n_seq, n_hb, HEAD_DIM + SUBLANES, HEAD_DIM), F32),
                        pltpu.VMEM((n_seq, 1, LANES), F32)],
        compiler_params=pltpu.CompilerParams(vmem_limit_bytes=VMEM_LIMIT),
        name="mixers",
    )(*acts, v_a, bv, sz, p_last, s0, ln_w, ln_b, ones_a, q, k, v_b, gates, gob, c0, n0, m0, gate_b, gn_w, ones_b)


def _out_body(oa_ref, ob_ref, x_ref, sc_ref, sh_ref, gate_ref, g_ref, w_ref, wua_ref, wub_ref, wo_ref,
              gf_ref, y_ref, *, nb, final_norm, cols):
    m = x_ref.shape[0]
    rows = lambda ref: _rows_of_seq(ref[...], nb, m // nb)
    x = x_ref[...]
    h = x * lax.rsqrt(jnp.mean(x * x, axis=-1, keepdims=True) + NORM_EPS) * g_ref[...]
    h = (h * (1.0 + rows(sc_ref)) + rows(sh_ref)).astype(BF16)
    gl_a = jnp.dot(h, w_ref[:, cols["gl_a"][0]:cols["gl_a"][1]], preferred_element_type=F32)
    yield
    gl_b = jnp.dot(h, w_ref[:, cols["gl_b"][0]:cols["gl_b"][1]], preferred_element_type=F32)
    yield
    ua = jnp.dot(oa_ref[...], wua_ref[...], preferred_element_type=F32)
    ub = jnp.dot(ob_ref[...], wub_ref[...], preferred_element_type=F32)
    yield
    merged = _sigmoid(gl_a) * ua + _sigmoid(gl_b) * ub
    mo = jnp.dot(merged.astype(BF16), wo_ref[...], preferred_element_type=F32)
    yield
    xn = x + rows(gate_ref) * mo
    if final_norm:
        xn = xn * lax.rsqrt(jnp.mean(xn * xn, axis=-1, keepdims=True) + NORM_EPS) * gf_ref[...]
    y_ref[...] = xn


def _out_kernel(*refs, n_split, **static):
    if n_split == 1:
        _run(_out_body(*refs, **static))
        return
    oa_ref, ob_ref, x_ref = refs[:3]
    y_ref = refs[-1]
    tp = x_ref.shape[0] // n_split
    plan = []
    for h in range(n_split):
        rows = pl.ds(h * tp, tp)
        body = _out_body(oa_ref.at[rows], ob_ref.at[rows], x_ref.at[rows], *refs[3:-1], y_ref.at[rows], **static)
        plan.append((body, 4, 0.5 * h, 1.0))
    _run(_interleave(plan))


def _out(out_a, out_b, x, scale, shift, gate, g_norm, w_all, cols, w_up_a, w_up_b, w_out, g_final,
         b, nb, tt, final_norm):
    n_tok, d = x.shape
    t = n_tok // b
    n_split = OUT_SPLIT if nb == 1 and t % (OUT_SPLIT * tt) == 0 else 1
    nj = t // (n_split * tt)
    m = nb * n_split * tt
    tok = lambda i, j: (i * nj + j, 0)
    const2 = lambda i, j: (0, 0)
    per_b = lambda i, j: (i, 0, 0)
    wspec = lambda w: pl.BlockSpec(w.shape, const2, pipeline_mode=pl.Buffered(1))
    in_specs = [pl.BlockSpec((m, out_a.shape[-1]), tok),
                pl.BlockSpec((m, out_b.shape[-1]), tok),
                pl.BlockSpec((m, d), tok),
                pl.BlockSpec((nb, 1, d), per_b),
                pl.BlockSpec((nb, 1, d), per_b),
                pl.BlockSpec((nb, 1, d), per_b),
                pl.BlockSpec((1, d), const2),
                wspec(w_all), wspec(w_up_a), wspec(w_up_b), wspec(w_out),
                pl.BlockSpec((1, d), const2)]
    return pl.pallas_call(
        functools.partial(_out_kernel, nb=nb, final_norm=final_norm, cols=cols, n_split=n_split),
        grid=(b // nb, nj),
        in_specs=in_specs,
        out_specs=pl.BlockSpec((m, d), tok),
        out_shape=jax.ShapeDtypeStruct((n_tok, d), F32),
        compiler_params=pltpu.CompilerParams(vmem_limit_bytes=VMEM_LIMIT),
        name="merge_out",
    )(out_a, out_b, x, scale, shift, gate, g_norm.reshape(1, d), w_all, w_up_a, w_up_b, w_out,
      g_final.reshape(1, d))


def _token_tiling(b, t, target):
    if t >= target:
        return 1, math.gcd(t, target)
    return math.gcd(b, target // t), t


def _layer(x, mod, states, lp, g_final, final_norm):
    b, t, d = x.shape
    shift0, s0, conv0, c0, n0, m0 = states
    n_hb = c0.shape[1]
    cw = lp["conv_w"].shape[0]
    row = lambda a: a.reshape(1, -1)

    ada_shift, ada_scale, ada_gate = (mod[:, None, i * d:(i + 1) * d] for i in range(3))
    nb, tt = _token_tiling(b, t, PROJ_ROWS)
    L = math.gcd(t, ROWS)
    act_dtype = BF16 if L % 16 == 0 else F32
    conv0p = jnp.pad(conv0, ((0, 0), (SUBLANES - (cw - 1), 0), (0, 0)))
    n0p = jnp.broadcast_to(n0[:, :, None, :], (b, n_hb, SUBLANES, HEAD_DIM))
    m0p = jnp.pad(m0, ((0, 0), (0, LANES - n_hb)))[:, None, :]
    gate_b = jnp.concatenate([lp["b_i"], lp["b_f"], jnp.zeros((LANES - 2 * n_hb,), F32)])[None, :]
    n_seq, t_step = _token_tiling(b, t, MIX_ROWS if t >= MIX_ROWS else MLSTM_ROWS)
    (kq, rq, bk, kkk, bk2, kk2, v_a, bv, sz, p_last, shift_t,
     q, k, v_b, gates, gob, conv_tail) = _inproj(
        x, ada_scale, ada_shift, lp["g_norm"], shift0[:, None, :], conv0p, lp, nb, tt, L, act_dtype)
    out_a, s_t, out_b, c_t, n_t, m_t = _mixers(
        (kq, rq, bk, kkk, bk2, kk2), v_a, bv, sz, p_last, s0, row(lp["ln_w"]), row(lp["ln_b"]),
        q, k, v_b, gates, gob, c0, n0p, m0p, gate_b, row(lp["gn_w"]), n_seq, t_step, L)

    y = _out(out_a, out_b, x.reshape(b * t, d), ada_scale, ada_shift, ada_gate, lp["g_norm"],
             lp["w_all"], lp["cols"], lp["w_up_a"], lp["w_up_b"], lp["w_out"],
             g_final, b, nb, tt, final_norm).reshape(b, t, d)
    new_states = (shift_t[:, 0], s_t, conv_tail[:, SUBLANES - (cw - 1):], c_t, n_t[:, :, 0, :], m_t[:, 0, :n_hb])
    return y, new_states


def _trunk(x, mods, states, layers, g_final):
    depth = len(layers)
    new = [[] for _ in states]
    for l in range(depth):
        st = tuple(s[l] for s in states)
        x, st_new = _layer(x, mods[l], st, layers[l], g_final, final_norm=(l == depth - 1))
        for lst, s in zip(new, st_new):
            lst.append(s.astype(x.dtype))
    return x, tuple(jnp.stack(lst) for lst in new)


def kernel(x_prompt, x_sample, c_prompt, c_sample, state_rwkv_shift, state_rwkv_S, state_mlstm_conv, state_mlstm_C, state_mlstm_n, state_mlstm_m, g_norm, w_ada, b_ada, w_in, mu_shift, w_decay2, w0, w_iclr2, a0, k_k, k_a, r_k, ln_w, ln_b, conv_w, conv_b, b_i, b_f, gn_w, w_up_a, w_up_b, w_out, g_final):
    depth = g_norm.shape[0]
    bp, bs = x_prompt.shape[0], x_sample.shape[0]
    d = x_prompt.shape[-1]
    da, db = w_up_a.shape[1], w_up_b.shape[1]
    n_ha, n_hb = da // HEAD_DIM, db // HEAD_DIM
    sw = mu_shift.shape[-1]
    cc = conv_w.shape[-1]
    cw = conv_w.shape[1]
    dt = x_prompt.dtype

    sizes = (sw, da, cc, db, db, n_hb, n_hb, db, d, d)
    offs = [0]
    for s in sizes:
        offs.append(offs[-1] + s)
    seg = lambda w, i: w[:, offs[i]:offs[i + 1]]

    c_all = jnp.concatenate([c_prompt, c_sample], axis=0)
    n_c = c_all.shape[0]
    c_pad = jnp.pad(c_all, ((0, (-n_c) % SUBLANES), (0, 0)))

    layers, mods_p, mods_s = [], [], []
    for l in range(depth):
        w = w_in[l]
        order = (("shift", 0), ("z_a", 1), ("qk", 2), ("v_b", 3), ("o_b", 4), ("z_b", 7), ("gl_a", 8), ("gl_b", 9))
        cols, at = {}, 0
        for name, i in order:
            cols[name] = (at, at + sizes[i])
            at += sizes[i]
        cols["gates"] = (at, at + LANES)
        w_all = jnp.concatenate([seg(w, i) for _, i in order] + [seg(w, 5), seg(w, 6)], axis=1)
        w_all = jnp.pad(w_all, ((0, 0), (0, at + LANES - w_all.shape[1]))).astype(BF16)
        layers.append(dict(
            g_norm=g_norm[l], w_all=w_all, cols=cols,
            mu_shift=mu_shift[l], w_decay2=w_decay2[l].astype(BF16),
            w0=w0[l], w_iclr2=w_iclr2[l].astype(BF16), a0=a0[l], k_k=k_k[l], k_a=k_a[l], r_k=r_k[l],
            ln_w=ln_w[l], ln_b=ln_b[l], conv_w=conv_w[l], conv_b=conv_b[l], b_i=b_i[l], b_f=b_f[l],
            gn_w=gn_w[l], w_up_a=w_up_a[l].astype(BF16), w_up_b=w_up_b[l].astype(BF16),
            w_out=w_out[l].astype(BF16)))
        mod = _ada(c_pad, w_ada[l], b_ada[l])
        mods_p.append(mod[:bp])
        mods_s.append(mod[bp:bp + bs])

    prompt_states = (
        jnp.zeros((depth, bp, sw), dt),
        jnp.zeros((depth, bp, n_ha, HEAD_DIM, HEAD_DIM), dt),
        jnp.zeros((depth, bp, cw - 1, cc), dt),
        jnp.zeros((depth, bp, n_hb, HEAD_DIM, HEAD_DIM), dt),
        jnp.zeros((depth, bp, n_hb, HEAD_DIM), dt),
        jnp.zeros((depth, bp, n_hb), dt),
    )
    sample_states = (state_rwkv_shift, state_rwkv_S, state_mlstm_conv,
                     state_mlstm_C, state_mlstm_n, state_mlstm_m)
    y_p, st_p = _trunk(x_prompt, mods_p, prompt_states, layers, g_final)
    y_s, st_s = _trunk(x_sample, mods_s, sample_states, layers, g_final)
    return (y_p, y_s) + st_p + st_s
```

```python
import functools
import math

import jax
import jax.numpy as jnp
from jax import lax
from jax.experimental import pallas as pl
from jax.experimental.pallas import tpu as pltpu

F32 = jnp.float32
BF16 = jnp.bfloat16

HEAD_DIM = 64
NORM_EPS = 1e-6
RWKV_GN_EPS = 64e-5
MLSTM_GN_EPS = 1e-6
ROWS = 64
MLSTM_ROWS = 128
MIX_ROWS = 1024
PROJ_ROWS = 256
INPROJ_SPLIT = 2
OUT_SPLIT = 2
SUBLANES = 8
LANES = 128
TRI_BASE = 4
VMEM_LIMIT = 60 * 1024 * 1024

LOG2_E = 1.4426950408889634
NT_DIMS = (((1,), (1,)), ((), ()))
TN_DIMS = (((0,), (0,)), ((), ()))


def _mm(a, b):
    return jnp.dot(a.astype(BF16), b.astype(BF16), preferred_element_type=F32)


def _mm_nt(a, b):
    return lax.dot_general(a.astype(BF16), b.astype(BF16), NT_DIMS, preferred_element_type=F32)


def _mm_tn(a, b):
    return lax.dot_general(a.astype(BF16), b.astype(BF16), TN_DIMS, preferred_element_type=F32)


def _sigmoid(x):
    return 1.0 / (1.0 + jnp.exp2(x * -LOG2_E))


def _silu(x):
    return x * _sigmoid(x)


def _softplus(x):
    return jnp.maximum(x, 0.0) + jnp.log(1.0 + jnp.exp(-jnp.abs(x)))


def _cumsum_time(x, tri_bf16, terms=3):
    dot = functools.partial(jnp.dot, preferred_element_type=F32)
    rest = x
    total = None
    for i in range(terms):
        part = rest.astype(BF16)
        if i + 1 < terms:
            rest = rest - part.astype(F32)
        total = dot(tri_bf16, part) if total is None else total + dot(tri_bf16, part)
    return total


def _shr(x, n):
    return lax.shift_right_logical(x, jnp.full(x.shape, n, x.dtype))


def _rows_of_seq(x, n_seq, L):
    if n_seq == 1:
        return x[0]
    return jnp.broadcast_to(x, (n_seq, L, x.shape[-1])).reshape(n_seq * L, x.shape[-1])


def _last_rows(x, n_seq):
    m, n = x.shape
    L = m // n_seq
    return x.reshape(n_seq, L, n)[:, L - 1:L, :]


def _head_sums(x):
    m, n = x.shape
    parts = [jnp.broadcast_to(jnp.sum(x[:, o:o + HEAD_DIM], axis=-1, keepdims=True), (m, HEAD_DIM))
             for o in range(0, n, HEAD_DIM)]
    return jnp.concatenate(parts, axis=-1)


def _head_sums_mxu(x, ones_ref):
    hi = x.astype(BF16)
    lo = (x - hi.astype(F32)).astype(BF16)
    s = jnp.dot(jnp.concatenate([hi, lo], axis=0), ones_ref[...], preferred_element_type=F32)
    return s[:x.shape[0]] + s[x.shape[0]:]


def _cummax_time(x, L, pos):
    y = x
    sh = 1
    while sh < L:
        y = jnp.where(pos >= sh, jnp.maximum(y, pltpu.roll(y, sh, axis=0)), y)
        sh *= 2
    return y


def _tri_inverse_all(mats, row, col, L, mm=_mm):
    bs = min(TRI_BASE, L)
    sh = int(math.log2(bs))
    same = _shr(row, sh) == _shr(col, sh)
    eye = (row == col).astype(F32)
    rows = mats[0].shape[0]
    ns = [jnp.where(same, -a, 0.0) for a in mats]
    ts = [eye + n for n in ns]
    if bs > 2:
        ps = [mm(n, n) for n in ns]
        yield
        k = 2
        while 2 * k < bs:
            both = [mm(jnp.concatenate([t, p], axis=0), p) for t, p in zip(ts, ps)]
            ts = [t + x[:rows] for t, x in zip(ts, both)]
            ps = [x[rows:] for x in both]
            yield
            k *= 2
        ts = [t + mm(t, p) for t, p in zip(ts, ps)]
        yield
    size = bs
    while size < L:
        sh = int(math.log2(size))
        lower_left = (_shr(row, sh + 1) == _shr(col, sh + 1)) & (_shr(row, sh) != _shr(col, sh))
        offs = [jnp.where(lower_left, a, 0.0) for a in mats]
        tmp = [mm(t, o) for t, o in zip(ts, offs)]
        yield
        ts = [t - mm(x, t) for t, x in zip(ts, tmp)]
        yield
        size *= 2
    return ts


def _interleave(plan):
    live = [[g, n, s, w, 0] for g, n, s, w in plan]
    while live:
        item = min(live, key=lambda it: it[2] + it[3] * it[4] / it[1])
        try:
            next(item[0])
            item[4] += 1
            yield
        except StopIteration:
            live.remove(item)


def _chain(*gens):
    for g in gens:
        yield from g


def _run(gen):
    for _ in gen:
        pass


def _when(pred, fn):
    if pred is not None:
        pl.when(pred)(fn)


def _ada_kernel(c_ref, w_ref, b_ref, o_ref):
    o_ref[...] = _mm(_silu(c_ref[...]), w_ref[...]) + b_ref[...]


def _ada(c, w_ada, b_ada):
    n, d = c.shape
    n3 = w_ada.shape[1]
    tn = d
    return pl.pallas_call(
        _ada_kernel,
        grid=(n3 // tn,),
        in_specs=[pl.BlockSpec((n, d), lambda j: (0, 0)),
                  pl.BlockSpec((d, tn), lambda j: (0, j)),
                  pl.BlockSpec((1, tn), lambda j: (0, j))],
        out_specs=pl.BlockSpec((n, tn), lambda j: (0, j)),
        out_shape=jax.ShapeDtypeStruct((n, n3), F32),
        name="ada",
    )(c, w_ada, b_ada.reshape(1, n3))


PROJ_COLS = 256


def _inproj_body(x_ref, sc_ref, sh_ref, g_ref, shift0_ref, conv0_ref,
                 w_shift, w_za, w_qk, w_vb, w_ob, w_zb, w_gt,
                 mu_ref, w0_ref, a0_ref, kk_ref, ka_ref, rk_ref, wd2_ref, wi2_ref, cw_ref, cb_ref,
                 kq_o, rq_o, bk_o, kkk_o, bk2_o, kk2_o, va_o, bv_o, sz_o, plast_o, shst_o,
                 q_o, k_o, vb_o, gt_o, gob_o, cst_o,
                 prev_scr, xbuf, *, chunk, lora_w, conv_w, first):
    nb, tt, d = x_ref.shape
    M = nb * tt
    L = chunk
    n_chunks = M // L
    da = va_o.shape[-1]
    db = vb_o.shape[-1]
    cc = xbuf.shape[-1]
    pad = SUBLANES

    def _init():
        prev_scr[...] = shift0_ref[...]
        xbuf[:, 0:pad, :] = conv0_ref[...]
    _when(first, _init)

    x = x_ref[...]
    y = x * lax.rsqrt(jnp.mean(x * x, axis=-1, keepdims=True) + NORM_EPS) * g_ref[...]
    h = (y * (1.0 + sc_ref[...]) + sh_ref[...]).reshape(M, d).astype(BF16)
    def proj(w_ref):
        n = w_ref.shape[1]
        parts = []
        for o in range(0, n, PROJ_COLS):
            parts.append(jnp.dot(h, w_ref[:, o:min(o + PROJ_COLS, n)], preferred_element_type=F32))
            yield
        return parts[0] if len(parts) == 1 else jnp.concatenate(parts, axis=-1)

    p = yield from proj(w_shift)
    row_w = lax.broadcasted_iota(jnp.int32, p.shape, 0)
    seq_start = (row_w & (tt - 1)) == 0
    prev = jnp.where(seq_start, _rows_of_seq(prev_scr[...], nb, tt), pltpu.roll(p, 1, axis=0))
    last = _last_rows(p, nb)
    prev_scr[...] = last
    shst_o[...] = last
    ps = p + mu_ref[...] * (prev - p)
    r = ps[:, 0:da]
    k = ps[:, da:2 * da]
    v = ps[:, 2 * da:3 * da]
    wl = ps[:, 3 * da:3 * da + lora_w]
    al = ps[:, 3 * da + lora_w:]

    qk_pre = yield from proj(w_qk)
    z_a = yield from proj(w_za)
    w = w0_ref[...] + _mm(jnp.tanh(wl), wd2_ref[...])
    lw = -math.exp(-0.5) * _sigmoid(w)
    a = _sigmoid(a0_ref[...] + _mm(al, wi2_ref[...]))
    yield
    v_b = yield from proj(w_vb)
    g_t = yield from proj(w_gt)
    o_b = yield from proj(w_ob)
    z_b = yield from proj(w_zb)
    kkr = k * kk_ref[...]
    kk = kkr * lax.rsqrt(jnp.maximum(_head_sums(kkr * kkr), 1e-24))
    k2 = k * (1.0 + (a - 1.0) * ka_ref[...])
    b = kk * a
    va_o[...] = v
    bv_o[...] = _head_sums(r * k2 * rk_ref[...]) * v
    sz_o[...] = _silu(z_a)

    lg = int(math.log2(L))
    row_m = lax.broadcasted_iota(jnp.int32, (M, M), 0)
    col_m = lax.broadcasted_iota(jnp.int32, (M, M), 1)
    tri = ((row_m >= col_m) & (_shr(row_m, lg) == _shr(col_m, lg))).astype(BF16)
    c = _cumsum_time(lw, tri, terms=2)
    yield
    c_last = _last_rows(c, n_chunks)
    plast_o[...] = jnp.exp(c_last)
    e_nc = jnp.exp(-c)
    e_cl = jnp.exp(_rows_of_seq(c_last, n_chunks, L) - c)
    kq_o[...] = (kk * jnp.exp(c - lw)).astype(kq_o.dtype)
    rq_o[...] = (r * jnp.exp(c)).astype(rq_o.dtype)
    bk_o[...] = (b * e_nc).astype(bk_o.dtype)
    kkk_o[...] = (k2 * e_nc).astype(kkk_o.dtype)
    bk2_o[...] = (b * e_cl).astype(bk2_o.dtype)
    kk2_o[...] = (k2 * e_cl).astype(kk2_o.dtype)

    xbuf[:, pad:pad + tt, :] = qk_pre.reshape(nb, tt, cc)
    xe = xbuf[...].reshape(nb * (tt + pad), cc)
    conv = cb_ref[...] + qk_pre * cw_ref[conv_w - 1:conv_w, :]
    for back in range(1, conv_w):
        shifted = pltpu.roll(xe, back, axis=0).reshape(nb, tt + pad, cc)[:, pad:, :].reshape(M, cc)
        conv = conv + shifted * cw_ref[conv_w - 1 - back:conv_w - back, :]
    tail = xbuf[:, tt:tt + pad, :]
    xbuf[:, 0:pad, :] = tail
    cst_o[...] = tail
    yield
    qk = _silu(conv)
    q_o[...] = qk[:, 0:db].astype(q_o.dtype)
    k_o[...] = (qk[:, db:] * (1.0 / math.sqrt(HEAD_DIM))).astype(k_o.dtype)
    vb_o[...] = v_b
    gt_o[...] = g_t
    gob_o[...] = _sigmoid(o_b) * _silu(z_b)


N_INPROJ_IN = 23
INPROJ_ROW_OUTS = (0, 1, 2, 3, 4, 5, 6, 7, 8, 11, 12, 13, 14, 15)
INPROJ_CHUNK_OUT = 9
INPROJ_STAGES = 28


def _inproj_kernel(*refs, n_split, chunk, **static):
    ins, outs, scratch = refs[:N_INPROJ_IN], refs[N_INPROJ_IN:-2], refs[-2:]
    first = pl.program_id(1) == 0
    if n_split == 1:
        _run(_inproj_body(*refs, chunk=chunk, first=first, **static))
        return
    tp = ins[0].shape[1] // n_split
    plan = []
    for h in range(n_split):
        rows = pl.ds(h * tp, tp)
        t_ins = [ins[0].at[:, rows]] + list(ins[1:])
        t_outs = [o.at[rows] if i in INPROJ_ROW_OUTS else o for i, o in enumerate(outs)]
        t_outs[INPROJ_CHUNK_OUT] = outs[INPROJ_CHUNK_OUT].at[pl.ds(h * (tp // chunk), tp // chunk)]
        body = _inproj_body(*t_ins, *t_outs, *scratch, chunk=chunk,
                            first=first if h == 0 else None, **static)
        plan.append((body, INPROJ_STAGES, 0.5 * h, 1.0))
    _run(_interleave(plan))


def _inproj(x, scale, shift, g_norm, shift0, conv0p, lp, nb, tt, chunk, act_dtype):
    b, t, d = x.shape
    n_tok = b * t
    n_split = INPROJ_SPLIT if nb == 1 and t % (INPROJ_SPLIT * tt) == 0 else 1
    tb = n_split * tt
    m = nb * tb
    nj = t // tb
    sw = shift0.shape[-1]
    cc = conv0p.shape[-1]
    da, db = lp["w_up_a"].shape[0], lp["w_up_b"].shape[0]
    cw = lp["conv_w"].shape[0]
    row = lambda a: a.reshape(1, -1)
    rows = [row(lp[n]) for n in ("mu_shift", "w0", "a0", "k_k", "k_a", "r_k")]
    smalls = rows + [lp["w_decay2"], lp["w_iclr2"], lp["conv_w"], row(lp["conv_b"])]
    const = lambda i, j: (0, 0)
    tok = lambda i, j: (i * nj + j, 0)
    per_b = lambda i, j: (i, 0, 0)
    in_specs = [pl.BlockSpec((nb, tb, d), lambda i, j: (i, j, 0)),
                pl.BlockSpec((nb, 1, d), per_b),
                pl.BlockSpec((nb, 1, d), per_b),
                pl.BlockSpec((1, d), const),
                pl.BlockSpec((nb, 1, sw), per_b),
                pl.BlockSpec((nb, SUBLANES, cc), per_b)]
    in_specs += [pl.BlockSpec(w.shape, const, pipeline_mode=pl.Buffered(1)) for w in lp["w_segs"]]
    in_specs += [pl.BlockSpec(a.shape, const) for a in smalls]
    tok_out = lambda n, dt: (pl.BlockSpec((m, n), tok), jax.ShapeDtypeStruct((n_tok, n), dt))
    outs = [tok_out(da, act_dtype)] * 6 + [tok_out(da, F32)] * 3
    outs += [(pl.BlockSpec((m // chunk, 1, da), lambda i, j: (i * nj + j, 0, 0)),
              jax.ShapeDtypeStruct((n_tok // chunk, 1, da), F32)),
             (pl.BlockSpec((nb, 1, sw), per_b), jax.ShapeDtypeStruct((b, 1, sw), F32))]
    outs += [tok_out(db, act_dtype)] * 2 + [tok_out(db, F32), tok_out(LANES, F32), tok_out(db, F32)]
    outs += [(pl.BlockSpec((nb, SUBLANES, cc), per_b), jax.ShapeDtypeStruct((b, SUBLANES, cc), F32))]
    return pl.pallas_call(
        functools.partial(_inproj_kernel, chunk=chunk, lora_w=lp["w_decay2"].shape[0], conv_w=cw,
                          n_split=n_split),
        grid=(b // nb, nj),
        in_specs=in_specs,
        out_specs=[o[0] for o in outs], out_shape=[o[1] for o in outs],
        scratch_shapes=[pltpu.VMEM((nb, 1, sw), F32),
                        pltpu.VMEM((nb, tt + SUBLANES, cc), F32)],
        compiler_params=pltpu.CompilerParams(vmem_limit_bytes=VMEM_LIMIT),
        name="inproj",
    )(x, scale, shift, g_norm.reshape(1, d), shift0, conv0p, *lp["w_segs"], *smalls)


def _pair_blockdiag(y, left):
    return jnp.concatenate([jnp.where(left, y, 0.0), jnp.where(left, 0.0, y)], axis=0)


def _rwkv_body(kq_ref, rq_ref, bk_ref, kkk_ref, bk2_ref, kk2_ref, v_ref, bv_ref, sz_ref, plast_ref,
               s0_ref, lnw_ref, lnb_ref, ones_ref, oa_ref, st_ref, s_scr, *, n_seq, chunk, n_heads,
               first, last, after, done):
    M = kq_ref.shape[0]
    L = chunk
    t_step = M // n_seq
    RU = min(M, ROWS)
    PW = 2 * HEAD_DIM
    n_units = M // RU
    n_hp = n_heads // 2
    units = range(n_units)
    hps = range(n_hp)
    pairs = [(u, g) for u in units for g in hps]
    zero_blk = jnp.zeros((n_seq, HEAD_DIM, HEAD_DIM), F32)

    def _load_state():
        for g in hps:
            top = jnp.concatenate([s0_ref[:, 2 * g], zero_blk], axis=-1)
            bot = jnp.concatenate([zero_blk, s0_ref[:, 2 * g + 1]], axis=-1)
            s_scr[:, g] = jnp.concatenate([top, bot], axis=-2)
    _when(first, _load_state)

    lg = int(math.log2(L))
    row = lax.broadcasted_iota(jnp.int32, (RU, 2 * RU), 0)
    lane = lax.broadcasted_iota(jnp.int32, (RU, 2 * RU), 1)
    col = lane & (RU - 1)
    same = _shr(row, lg) == _shr(col, lg)
    incl = (row >= col) & same
    strict = (row > col) & same
    left = lax.broadcasted_iota(jnp.int32, (RU, PW), 1) < HEAD_DIM
    bd_mask = (lax.broadcasted_iota(jnp.int32, (PW, PW), 0) < HEAD_DIM) == (
        lax.broadcasted_iota(jnp.int32, (PW, PW), 1) < HEAD_DIM)
    mmp = lambda x, y: _mm(x, _pair_blockdiag(y, left))

    p_last = plast_ref[...]
    pls = [slice(g * PW, (g + 1) * PW) for g in hps]
    rus = [slice(u * RU, (u + 1) * RU) for u in units]
    blk = lambda ref, q: ref[rus[q[0]], pls[q[1]]]
    kq = {q: blk(kq_ref, q) for q in pairs}
    rq = {q: blk(rq_ref, q) for q in pairs}
    bk2 = {q: blk(bk2_ref, q) for q in pairs}
    kk2 = {q: blk(kk2_ref, q) for q in pairs}
    vs = {q: blk(v_ref, q) for q in pairs}
    qr = {q: jnp.concatenate([kq[q], rq[q]], axis=0).astype(BF16) for q in pairs}
    bkk = {q: jnp.concatenate([_pair_blockdiag(blk(bk_ref, q).astype(F32), left),
                               _pair_blockdiag(blk(kkk_ref, q).astype(F32), left)], axis=0).astype(BF16)
           for q in pairs}
    yield

    gs = {q: _mm_nt(qr[q], bkk[q]) for q in pairs}
    yield
    a_ab = {q: jnp.where(strict, gs[q][:RU, :2 * RU], 0.0) for q in pairs}
    a_ak = {q: jnp.where(strict, gs[q][:RU, 2 * RU:], 0.0) for q in pairs}
    m_rb = {q: jnp.where(incl, gs[q][RU:, :2 * RU], 0.0) for q in pairs}
    m_rk = {q: jnp.where(incl, gs[q][RU:, 2 * RU:], 0.0) for q in pairs}
    yield
    t_inv = dict(zip(pairs, (yield from _tri_inverse_all([a_ab[q] for q in pairs], row, col, L, mmp))))
    akv = {q: mmp(a_ak[q], vs[q]) for q in pairs}
    yield

    assert after is None or after, "the body before this one must have stored its state by now"
    state = {}
    ys = {}
    for u in units:
        if L == RU:
            b = (u * RU) // t_step
            s0 = [state[b, g] if (b, g) in state else s_scr[b, g] for g in hps]
            ws = [_mm_nt(qr[u, g], s0[g]) for g in hps]
            w1 = [x[:RU] for x in ws]
            wr = [x[RU:] for x in ws]
        else:
            n_in = RU // L
            s0s = [[s_scr[u * n_in + i, g] for g in hps] for i in range(n_in)]
            w1, wr = [], []
            for g in hps:
                parts = [_mm_nt(jnp.concatenate([kq[u, g][i * L:(i + 1) * L], rq[u, g][i * L:(i + 1) * L]], axis=0),
                                s0s[i][g]) for i in range(n_in)]
                w1.append(jnp.concatenate([x[:L] for x in parts], axis=0))
                wr.append(jnp.concatenate([x[L:] for x in parts], axis=0))
        yield
        us = [-mmp(t_inv[u, g], w1[g] + akv[u, g]) for g in hps]
        yield
        for g in hps:
            rhs = jnp.concatenate([_pair_blockdiag(us[g], left), _pair_blockdiag(vs[u, g].astype(F32), left)], axis=0)
            ys[u, g] = wr[g] + _mm(jnp.concatenate([m_rb[u, g], m_rk[u, g]], axis=1), rhs)
        yield
        if L == RU:
            for g in hps:
                uv = jnp.concatenate([us[g], vs[u, g].astype(F32)], axis=0)
                bkk2 = jnp.concatenate([bk2[u, g], kk2[u, g]], axis=0)
                state[b, g] = s0[g] * p_last[u][:, pls[g]] + jnp.where(bd_mask, _mm_tn(uv, bkk2), 0.0)
        else:
            for g in hps:
                for i in range(n_in):
                    rows = slice(i * L, (i + 1) * L)
                    uv_i = jnp.concatenate([us[g][rows], vs[u, g][rows]], axis=0)
                    bkk2_i = jnp.concatenate([bk2[u, g][rows], kk2[u, g][rows]], axis=0)
                    state[u * n_in + i, g] = (s0s[i][g] * p_last[u * n_in + i][:, pls[g]]
                                              + jnp.where(bd_mask, _mm_tn(uv_i, bkk2_i), 0.0))
        yield
    for (b, g), s_new in state.items():
        s_scr[b, g] = s_new
    done.append(True)

    rows_out = [jnp.concatenate([ys[u, g] for g in hps], axis=-1) for u in units]
    y = rows_out[0] if n_units == 1 else jnp.concatenate(rows_out, axis=0)
    yc = y - _head_sums_mxu(y, ones_ref) * (1.0 / HEAD_DIM)
    yield
    var = _head_sums_mxu(yc * yc, ones_ref) * (1.0 / HEAD_DIM)
    yn = yc * lax.rsqrt(var + RWKV_GN_EPS)
    oa_ref[...] = ((yn * lnw_ref[...] + lnb_ref[...] + bv_ref[...]) * sz_ref[...]).astype(oa_ref.dtype)

    def _store_state():
        for g in hps:
            st_ref[:, 2 * g] = s_scr[:, g, 0:HEAD_DIM, 0:HEAD_DIM]
            st_ref[:, 2 * g + 1] = s_scr[:, g, HEAD_DIM:, HEAD_DIM:]
    _when(last, _store_state)


def _mlstm_body(q_ref, k_ref, v_ref, g_ref, gob_ref, c0_ref, n0_ref, m0_ref, gb_ref, gnw_ref, ones_ref,
                ob_ref, ct_ref, nt_ref, mt_ref, cn_scr, m_scr, *, n_seq, n_heads, first, last):
    M = q_ref.shape[0]
    L = M // n_seq
    heads = range(n_heads)
    seqs = range(n_seq)

    def _load_state():
        cn_scr[:, :, 0:HEAD_DIM, :] = c0_ref[...]
        cn_scr[:, :, HEAD_DIM:, :] = n0_ref[...]
        m_scr[...] = m0_ref[...]
    _when(first, _load_state)

    row = lax.broadcasted_iota(jnp.int32, (M, M), 0)
    col = lax.broadcasted_iota(jnp.int32, (M, M), 1)
    same_seq = _shr(row, int(math.log2(L))) == _shr(col, int(math.log2(L)))
    incl = (row >= col) & same_seq

    g = g_ref[...] + gb_ref[...]
    bcum = pltpu.roll(_cumsum_time(-_softplus(-g), incl.astype(BF16)), LANES - n_heads, axis=1)
    m_prev = m_scr[...]
    m_prev_rows = _rows_of_seq(m_prev, n_seq, L)
    x_all = g - bcum
    pos = lax.broadcasted_iota(jnp.int32, (M, LANES), 0) & (L - 1)
    m_all = bcum + jnp.maximum(_cummax_time(x_all, L, pos), m_prev_rows)
    bm_all = bcum - m_all
    w_in_all = jnp.exp(bcum + m_prev_rows - m_all)
    e_negm_all = jnp.exp(-m_all)
    m_new = _last_rows(m_all, n_seq)
    b_last = _last_rows(bcum, n_seq)
    ws_all = jnp.exp(_rows_of_seq(b_last - m_new, n_seq, L) + x_all)
    dec_all = jnp.exp(b_last + m_prev - m_new)
    m_scr[...] = m_new
    x_t = jnp.transpose(x_all)
    yield

    ones = jnp.ones((M, SUBLANES), F32)
    sls = [slice(h * HEAD_DIM, (h + 1) * HEAD_DIM) for h in heads]
    qs = [q_ref[:, sl] for sl in sls]
    ks = [k_ref[:, sl] for sl in sls]
    v1 = [jnp.concatenate([v_ref[:, sl], ones], axis=-1) for sl in sls]
    cn = [[cn_scr[b, h] for h in heads] for b in seqs]

    qk_t = [_mm_nt(qs[h], ks[h]) for h in heads]
    if n_seq == 1:
        qc = [_mm_nt(qs[h], cn[0][h]) for h in heads]
    else:
        qc = [jnp.concatenate([_mm_nt(qs[h][b * L:(b + 1) * L], cn[b][h]) for b in seqs], axis=0) for h in heads]
    yield
    w_ts = [jnp.where(incl, jnp.exp(bm_all[:, h:h + 1] + x_t[h:h + 1, :]), 0.0) for h in heads]
    yield
    s = [qk_t[h] * w_ts[h] for h in heads]
    numden = [_mm(s[h], v1[h]) + w_in_all[:, h:h + 1] * qc[h] for h in heads]
    yield
    den = [jnp.maximum(jnp.abs(numden[h][:, HEAD_DIM:HEAD_DIM + 1]), e_negm_all[:, h:h + 1]) for h in heads]
    hh = [numden[h][:, 0:HEAD_DIM] / den[h] for h in heads]
    hh_all = jnp.concatenate(hh, axis=-1)
    ssq = _head_sums_mxu(hh_all * hh_all, ones_ref) * (1.0 / HEAD_DIM)

    yield
    v1w = [v1[h] * ws_all[:, h:h + 1] for h in heads]
    for h in heads:
        for b in seqs:
            rows = slice(b * L, (b + 1) * L)
            cn_scr[b, h] = dec_all[b][:, h:h + 1] * cn[b][h] + _mm_tn(v1w[h][rows], ks[h][rows])
    yield

    hb = hh_all * lax.rsqrt(ssq + MLSTM_GN_EPS)
    ob_ref[...] = (hb * gnw_ref[...] * gob_ref[...]).astype(ob_ref.dtype)

    def _store_state():
        ct_ref[...] = cn_scr[:, :, 0:HEAD_DIM, :]
        nt_ref[...] = cn_scr[:, :, HEAD_DIM:, :]
        mt_ref[...] = m_scr[...]
    _when(last, _store_state)


N_RWKV_IN, N_RWKV_OUT, N_MLSTM_IN, N_MLSTM_OUT = 14, 2, 11, 4
N_MLSTM_ROW_IN = 5
N_RWKV_ROW_IN = 9
RWKV_STAGES, MLSTM_STAGES = 29, 8
RWKV_BODY_ROWS = 256
RWKV_STAGGER = 0.6


def _mixers_kernel(*refs, n_seq, chunk, n_heads_a, n_heads_b):
    it = iter(refs)
    take = lambda n: [next(it) for _ in range(n)]
    r_in, m_in = take(N_RWKV_IN), take(N_MLSTM_IN)
    r_out, m_out = take(N_RWKV_OUT), take(N_MLSTM_OUT)
    s_scr, cn_scr, m_scr = take(3)
    first = pl.program_id(1) == 0
    last = pl.program_id(1) == pl.num_programs(1) - 1
    M = m_in[0].shape[0]
    n_r = M // RWKV_BODY_ROWS if n_seq == 1 and M > RWKV_BODY_ROWS else 1
    n_sub = M // MLSTM_ROWS if n_seq == 1 and M > MLSTM_ROWS else 1
    plan = []
    flags = [[] for _ in range(n_r)]
    for h in range(n_r):
        rows = (lambda r: r.at[pl.ds(h * RWKV_BODY_ROWS, RWKV_BODY_ROWS)]) if n_r > 1 else (lambda r: r)
        chunks = (lambda r: r.at[pl.ds(h * (RWKV_BODY_ROWS // chunk), RWKV_BODY_ROWS // chunk)]) if n_r > 1 else (
            lambda r: r)
        body = _rwkv_body(
            *[rows(r) for r in r_in[:N_RWKV_ROW_IN]], chunks(r_in[N_RWKV_ROW_IN]), *r_in[N_RWKV_ROW_IN + 1:],
            rows(r_out[0]), r_out[1], s_scr, n_seq=n_seq, chunk=chunk, n_heads=n_heads_a,
            first=first if h == 0 else None, last=last if h == n_r - 1 else None,
            after=flags[h - 1] if h else None, done=flags[h])
        plan.append((body, RWKV_STAGES, RWKV_STAGGER * h, 1.0))
    mlstm = []
    for h in range(n_sub):
        rows = (lambda r: r.at[pl.ds(h * MLSTM_ROWS, MLSTM_ROWS)]) if n_sub > 1 else (lambda r: r)
        mlstm.append(_mlstm_body(
            *[rows(r) for r in m_in[:N_MLSTM_ROW_IN]], *m_in[N_MLSTM_ROW_IN:], rows(m_out[0]), *m_out[1:],
            cn_scr, m_scr,
            n_seq=n_seq, n_heads=n_heads_b,
            first=first if h == 0 else None, last=last if h == n_sub - 1 else None))
    plan.append((_chain(*mlstm), MLSTM_STAGES * n_sub, 0.0, 1.0 + RWKV_STAGGER * (n_r - 1)))
    _run(_interleave(plan))


def _mixers(acts, v_a, bv, sz, p_last, s0, ln_w, ln_b, q, k, v_b, gates, gob, c0, n0, m0, gate_b, gn_w,
            n_seq, t_step, chunk):
    b = s0.shape[0]
    n_ha, n_hb = s0.shape[1], c0.shape[1]
    n_tok, da = v_a.shape
    db = v_b.shape[1]
    m = n_seq * t_step
    nj = n_tok // (b * t_step)
    tok = lambda i, j: (i * nj + j, 0)
    per_b3 = lambda i, j: (i, 0, 0)
    per_b4 = lambda i, j: (i, 0, 0, 0)
    const2 = lambda i, j: (0, 0)
    s_spec = pl.BlockSpec((n_seq, n_ha, HEAD_DIM, HEAD_DIM), per_b4)
    c_spec = pl.BlockSpec((n_seq, n_hb, HEAD_DIM, HEAD_DIM), per_b4)
    n_spec = pl.BlockSpec((n_seq, n_hb, SUBLANES, HEAD_DIM), per_b4)
    m_spec = pl.BlockSpec((n_seq, 1, LANES), per_b3)
    row_a, row_b = pl.BlockSpec((m, da), tok), pl.BlockSpec((m, db), tok)
    in_specs = [row_a] * 9
    ones_a = jnp.kron(jnp.eye(n_ha, dtype=F32), jnp.ones((HEAD_DIM, HEAD_DIM), F32)).astype(BF16)
    ones_b = jnp.kron(jnp.eye(n_hb, dtype=F32), jnp.ones((HEAD_DIM, HEAD_DIM), F32)).astype(BF16)
    in_specs += [pl.BlockSpec((m // chunk, 1, da), lambda i, j: (i * nj + j, 0, 0)), s_spec,
                 pl.BlockSpec(ln_w.shape, const2), pl.BlockSpec(ln_b.shape, const2),
                 pl.BlockSpec(ones_a.shape, const2)]
    in_specs += [row_b, row_b, row_b, pl.BlockSpec((m, LANES), tok), row_b, c_spec, n_spec, m_spec,
                 pl.BlockSpec(gate_b.shape, const2), pl.BlockSpec(gn_w.shape, const2),
                 pl.BlockSpec(ones_b.shape, const2)]
    return pl.pallas_call(
        functools.partial(_mixers_kernel, n_seq=n_seq, chunk=chunk, n_heads_a=n_ha, n_heads_b=n_hb),
        grid=(b // n_seq, nj),
        in_specs=in_specs,
        out_specs=[row_a, s_spec, row_b, c_spec, n_spec, m_spec],
        out_shape=[jax.ShapeDtypeStruct((n_tok, da), BF16), jax.ShapeDtypeStruct(s0.shape, F32),
                   jax.ShapeDtypeStruct((n_tok, db), BF16), jax.ShapeDtypeStruct(c0.shape, F32),
                   jax.ShapeDtypeStruct(n0.shape, F32), jax.ShapeDtypeStruct(m0.shape, F32)],
        scratch_shapes=[pltpu.VMEM((n_seq, n_ha // 2, 2 * HEAD_DIM, 2 * HEAD_DIM), F32),
                        pltpu.VMEM((n_seq, n_hb, HEAD_DIM + SUBLANES, HEAD_DIM), F32),
                        pltpu.VMEM((n_seq, 1, LANES), F32)],
        compiler_params=pltpu.CompilerParams(vmem_limit_bytes=VMEM_LIMIT),
        name="mixers",
    )(*acts, v_a, bv, sz, p_last, s0, ln_w, ln_b, ones_a, q, k, v_b, gates, gob, c0, n0, m0, gate_b, gn_w, ones_b)


def _out_body(oa_ref, ob_ref, x_ref, sc_ref, sh_ref, gate_ref, g_ref, wga_ref, wgb_ref, wua_ref, wub_ref, wo_ref,
              gf_ref, y_ref, *, nb, final_norm):
    m = x_ref.shape[0]
    rows = lambda ref: _rows_of_seq(ref[...], nb, m // nb)
    x = x_ref[...]
    h = x * lax.rsqrt(jnp.mean(x * x, axis=-1, keepdims=True) + NORM_EPS) * g_ref[...]
    h = (h * (1.0 + rows(sc_ref)) + rows(sh_ref)).astype(BF16)
    gl_a = jnp.dot(h, wga_ref[...], preferred_element_type=F32)
    yield
    gl_b = jnp.dot(h, wgb_ref[...], preferred_element_type=F32)
    yield
    ua = jnp.dot(oa_ref[...], wua_ref[...], preferred_element_type=F32)
    ub = jnp.dot(ob_ref[...], wub_ref[...], preferred_element_type=F32)
    yield
    merged = _sigmoid(gl_a) * ua + _sigmoid(gl_b) * ub
    mo = jnp.dot(merged.astype(BF16), wo_ref[...], preferred_element_type=F32)
    yield
    xn = x + rows(gate_ref) * mo
    if final_norm:
        xn = xn * lax.rsqrt(jnp.mean(xn * xn, axis=-1, keepdims=True) + NORM_EPS) * gf_ref[...]
    y_ref[...] = xn


def _out_kernel(*refs, n_split, **static):
    if n_split == 1:
        _run(_out_body(*refs, **static))
        return
    oa_ref, ob_ref, x_ref = refs[:3]
    y_ref = refs[-1]
    tp = x_ref.shape[0] // n_split
    plan = []
    for h in range(n_split):
        rows = pl.ds(h * tp, tp)
        body = _out_body(oa_ref.at[rows], ob_ref.at[rows], x_ref.at[rows], *refs[3:-1], y_ref.at[rows], **static)
        plan.append((body, 4, 0.5 * h, 1.0))
    _run(_interleave(plan))


def _out(out_a, out_b, x, scale, shift, gate, g_norm, w_gl_a, w_gl_b, w_up_a, w_up_b, w_out, g_final,
         b, nb, tt, final_norm):
    n_tok, d = x.shape
    t = n_tok // b
    n_split = OUT_SPLIT if nb == 1 and t % (OUT_SPLIT * tt) == 0 else 1
    nj = t // (n_split * tt)
    m = nb * n_split * tt
    tok = lambda i, j: (i * nj + j, 0)
    const2 = lambda i, j: (0, 0)
    per_b = lambda i, j: (i, 0, 0)
    wspec = lambda w: pl.BlockSpec(w.shape, const2, pipeline_mode=pl.Buffered(1))
    in_specs = [pl.BlockSpec((m, out_a.shape[-1]), tok),
                pl.BlockSpec((m, out_b.shape[-1]), tok),
                pl.BlockSpec((m, d), tok),
                pl.BlockSpec((nb, 1, d), per_b),
                pl.BlockSpec((nb, 1, d), per_b),
                pl.BlockSpec((nb, 1, d), per_b),
                pl.BlockSpec((1, d), const2),
                wspec(w_gl_a), wspec(w_gl_b), wspec(w_up_a), wspec(w_up_b), wspec(w_out),
                pl.BlockSpec((1, d), const2)]
    return pl.pallas_call(
        functools.partial(_out_kernel, nb=nb, final_norm=final_norm, n_split=n_split),
        grid=(b // nb, nj),
        in_specs=in_specs,
        out_specs=pl.BlockSpec((m, d), tok),
        out_shape=jax.ShapeDtypeStruct((n_tok, d), F32),
        compiler_params=pltpu.CompilerParams(vmem_limit_bytes=VMEM_LIMIT),
        name="merge_out",
    )(out_a, out_b, x, scale, shift, gate, g_norm.reshape(1, d), w_gl_a, w_gl_b, w_up_a, w_up_b, w_out,
      g_final.reshape(1, d))


def _token_tiling(b, t, target):
    if t >= target:
        return 1, math.gcd(t, target)
    return math.gcd(b, target // t), t


def _layer(x, mod, states, lp, g_final, final_norm):
    b, t, d = x.shape
    shift0, s0, conv0, c0, n0, m0 = states
    n_hb = c0.shape[1]
    cw = lp["conv_w"].shape[0]
    row = lambda a: a.reshape(1, -1)

    ada_shift, ada_scale, ada_gate = (mod[:, None, i * d:(i + 1) * d] for i in range(3))
    nb, tt = _token_tiling(b, t, PROJ_ROWS)
    L = math.gcd(t, ROWS)
    act_dtype = BF16 if L % 16 == 0 else F32
    conv0p = jnp.pad(conv0, ((0, 0), (SUBLANES - (cw - 1), 0), (0, 0)))
    n0p = jnp.broadcast_to(n0[:, :, None, :], (b, n_hb, SUBLANES, HEAD_DIM))
    m0p = jnp.pad(m0, ((0, 0), (0, LANES - n_hb)))[:, None, :]
    gate_b = jnp.concatenate([lp["b_i"], lp["b_f"], jnp.zeros((LANES - 2 * n_hb,), F32)])[None, :]
    n_seq, t_step = _token_tiling(b, t, MIX_ROWS if t >= MIX_ROWS else MLSTM_ROWS)
    (kq, rq, bk, kkk, bk2, kk2, v_a, bv, sz, p_last, shift_t,
     q, k, v_b, gates, gob, conv_tail) = _inproj(
        x, ada_scale, ada_shift, lp["g_norm"], shift0[:, None, :], conv0p, lp, nb, tt, L, act_dtype)
    out_a, s_t, out_b, c_t, n_t, m_t = _mixers(
        (kq, rq, bk, kkk, bk2, kk2), v_a, bv, sz, p_last, s0, row(lp["ln_w"]), row(lp["ln_b"]),
        q, k, v_b, gates, gob, c0, n0p, m0p, gate_b, row(lp["gn_w"]), n_seq, t_step, L)

    y = _out(out_a, out_b, x.reshape(b * t, d), ada_scale, ada_shift, ada_gate, lp["g_norm"],
             lp["w_gl_a"], lp["w_gl_b"], lp["w_up_a"], lp["w_up_b"], lp["w_out"],
             g_final, b, nb, tt, final_norm).reshape(b, t, d)
    new_states = (shift_t[:, 0], s_t, conv_tail[:, SUBLANES - (cw - 1):], c_t, n_t[:, :, 0, :], m_t[:, 0, :n_hb])
    return y, new_states


def _trunk(x, mods, states, layers, g_final):
    depth = len(layers)
    new = [[] for _ in states]
    for l in range(depth):
        st = tuple(s[l] for s in states)
        x, st_new = _layer(x, mods[l], st, layers[l], g_final, final_norm=(l == depth - 1))
        for lst, s in zip(new, st_new):
            lst.append(s.astype(x.dtype))
    return x, tuple(jnp.stack(lst) for lst in new)


def kernel(x_prompt, x_sample, c_prompt, c_sample, state_rwkv_shift, state_rwkv_S, state_mlstm_conv, state_mlstm_C, state_mlstm_n, state_mlstm_m, g_norm, w_ada, b_ada, w_in, mu_shift, w_decay2, w0, w_iclr2, a0, k_k, k_a, r_k, ln_w, ln_b, conv_w, conv_b, b_i, b_f, gn_w, w_up_a, w_up_b, w_out, g_final):
    depth = g_norm.shape[0]
    bp, bs = x_prompt.shape[0], x_sample.shape[0]
    d = x_prompt.shape[-1]
    da, db = w_up_a.shape[1], w_up_b.shape[1]
    n_ha, n_hb = da // HEAD_DIM, db // HEAD_DIM
    sw = mu_shift.shape[-1]
    cc = conv_w.shape[-1]
    cw = conv_w.shape[1]
    dt = x_prompt.dtype

    sizes = (sw, da, cc, db, db, n_hb, n_hb, db, d, d)
    offs = [0]
    for s in sizes:
        offs.append(offs[-1] + s)
    seg = lambda w, i: w[:, offs[i]:offs[i + 1]]

    c_all = jnp.concatenate([c_prompt, c_sample], axis=0)
    n_c = c_all.shape[0]
    c_pad = jnp.pad(c_all, ((0, (-n_c) % SUBLANES), (0, 0)))

    layers, mods_p, mods_s = [], [], []
    for l in range(depth):
        w = w_in[l]
        wg = jnp.concatenate([seg(w, 5), seg(w, 6)], axis=1)
        wg = jnp.pad(wg, ((0, 0), (0, LANES - wg.shape[1])))
        w_segs = tuple(x.astype(BF16) for x in
                       (seg(w, 0), seg(w, 1), seg(w, 2), seg(w, 3), seg(w, 4), seg(w, 7), wg))
        layers.append(dict(
            g_norm=g_norm[l], w_segs=w_segs, w_gl_a=seg(w, 8).astype(BF16), w_gl_b=seg(w, 9).astype(BF16),
            mu_shift=mu_shift[l], w_decay2=w_decay2[l].astype(BF16),
            w0=w0[l], w_iclr2=w_iclr2[l].astype(BF16), a0=a0[l], k_k=k_k[l], k_a=k_a[l], r_k=r_k[l],
            ln_w=ln_w[l], ln_b=ln_b[l], conv_w=conv_w[l], conv_b=conv_b[l], b_i=b_i[l], b_f=b_f[l],
            gn_w=gn_w[l], w_up_a=w_up_a[l].astype(BF16), w_up_b=w_up_b[l].astype(BF16),
            w_out=w_out[l].astype(BF16)))
        mod = _ada(c_pad, w_ada[l], b_ada[l])
        mods_p.append(mod[:bp])
        mods_s.append(mod[bp:bp + bs])

    prompt_states = (
        jnp.zeros((depth, bp, sw), dt),
        jnp.zeros((depth, bp, n_ha, HEAD_DIM, HEAD_DIM), dt),
        jnp.zeros((depth, bp, cw - 1, cc), dt),
        jnp.zeros((depth, bp, n_hb, HEAD_DIM, HEAD_DIM), dt),
        jnp.zeros((depth, bp, n_hb, HEAD_DIM), dt),
        jnp.zeros((depth, bp, n_hb), dt),
    )
    sample_states = (state_rwkv_shift, state_rwkv_S, state_mlstm_conv,
                     state_mlstm_C, state_mlstm_n, state_mlstm_m)
    y_p, st_p = _trunk(x_prompt, mods_p, prompt_states, layers, g_final)
    y_s, st_s = _trunk(x_sample, mods_s, sample_states, layers, g_final)
    return (y_p, y_s) + st_p + st_s
```

```python
import functools
import math

import jax
import jax.numpy as jnp
from jax import lax
from jax.experimental import pallas as pl
from jax.experimental.pallas import tpu as pltpu

F32 = jnp.float32
BF16 = jnp.bfloat16

HEAD_DIM = 64
NORM_EPS = 1e-6
RWKV_GN_EPS = 64e-5
MLSTM_GN_EPS = 1e-6
ROWS = 64
MLSTM_ROWS = 128
MIX_ROWS = 1024
PROJ_ROWS = 256
INPROJ_SPLIT = 2
OUT_SPLIT = 2
SUBLANES = 8
LANES = 128
TRI_BASE = 4
VMEM_LIMIT = 60 * 1024 * 1024

LOG2_E = 1.4426950408889634
NT_DIMS = (((1,), (1,)), ((), ()))
TN_DIMS = (((0,), (0,)), ((), ()))


def _mm(a, b):
    return jnp.dot(a.astype(BF16), b.astype(BF16), preferred_element_type=F32)


def _mm_nt(a, b):
    return lax.dot_general(a.astype(BF16), b.astype(BF16), NT_DIMS, preferred_element_type=F32)


def _mm_tn(a, b):
    return lax.dot_general(a.astype(BF16), b.astype(BF16), TN_DIMS, preferred_element_type=F32)


def _sigmoid(x):
    return 1.0 / (1.0 + jnp.exp2(x * -LOG2_E))


def _silu(x):
    return x * _sigmoid(x)


def _softplus(x):
    return jnp.maximum(x, 0.0) + jnp.log(1.0 + jnp.exp(-jnp.abs(x)))


def _cumsum_time(x, tri_bf16, terms=3):
    dot = functools.partial(jnp.dot, preferred_element_type=F32)
    rest = x
    total = None
    for i in range(terms):
        part = rest.astype(BF16)
        if i + 1 < terms:
            rest = rest - part.astype(F32)
        total = dot(tri_bf16, part) if total is None else total + dot(tri_bf16, part)
    return total


def _shr(x, n):
    return lax.shift_right_logical(x, jnp.full(x.shape, n, x.dtype))


def _rows_of_seq(x, n_seq, L):
    if n_seq == 1:
        return x[0]
    return jnp.broadcast_to(x, (n_seq, L, x.shape[-1])).reshape(n_seq * L, x.shape[-1])


def _last_rows(x, n_seq):
    m, n = x.shape
    L = m // n_seq
    return x.reshape(n_seq, L, n)[:, L - 1:L, :]


def _head_sums(x):
    m, n = x.shape
    parts = [jnp.broadcast_to(jnp.sum(x[:, o:o + HEAD_DIM], axis=-1, keepdims=True), (m, HEAD_DIM))
             for o in range(0, n, HEAD_DIM)]
    return jnp.concatenate(parts, axis=-1)


def _head_sums_mxu(x, ones_ref):
    hi = x.astype(BF16)
    lo = (x - hi.astype(F32)).astype(BF16)
    s = jnp.dot(jnp.concatenate([hi, lo], axis=0), ones_ref[...], preferred_element_type=F32)
    return s[:x.shape[0]] + s[x.shape[0]:]


def _cummax_time(x, L, pos):
    y = x
    sh = 1
    while sh < L:
        y = jnp.where(pos >= sh, jnp.maximum(y, pltpu.roll(y, sh, axis=0)), y)
        sh *= 2
    return y


def _tri_inverse_all(mats, row, col, L, mm=_mm):
    bs = min(TRI_BASE, L)
    sh = int(math.log2(bs))
    same = _shr(row, sh) == _shr(col, sh)
    eye = (row == col).astype(F32)
    rows = mats[0].shape[0]
    ns = [jnp.where(same, -a, 0.0) for a in mats]
    ts = [eye + n for n in ns]
    if bs > 2:
        ps = [mm(n, n) for n in ns]
        yield
        k = 2
        while 2 * k < bs:
            both = [mm(jnp.concatenate([t, p], axis=0), p) for t, p in zip(ts, ps)]
            ts = [t + x[:rows] for t, x in zip(ts, both)]
            ps = [x[rows:] for x in both]
            yield
            k *= 2
        ts = [t + mm(t, p) for t, p in zip(ts, ps)]
        yield
    size = bs
    while size < L:
        sh = int(math.log2(size))
        lower_left = (_shr(row, sh + 1) == _shr(col, sh + 1)) & (_shr(row, sh) != _shr(col, sh))
        offs = [jnp.where(lower_left, a, 0.0) for a in mats]
        tmp = [mm(t, o) for t, o in zip(ts, offs)]
        yield
        ts = [t - mm(x, t) for t, x in zip(ts, tmp)]
        yield
        size *= 2
    return ts


def _interleave(plan):
    live = [[g, n, s, w, 0] for g, n, s, w in plan]
    while live:
        item = min(live, key=lambda it: it[2] + it[3] * it[4] / it[1])
        try:
            next(item[0])
            item[4] += 1
            yield
        except StopIteration:
            live.remove(item)


def _chain(*gens):
    for g in gens:
        yield from g


def _run(gen):
    for _ in gen:
        pass


def _when(pred, fn):
    if pred is not None:
        pl.when(pred)(fn)


def _ada_kernel(c_ref, w_ref, b_ref, o_ref):
    o_ref[...] = _mm(_silu(c_ref[...]), w_ref[...]) + b_ref[...]


def _ada(c, w_ada, b_ada):
    n, d = c.shape
    n3 = w_ada.shape[1]
    tn = d
    return pl.pallas_call(
        _ada_kernel,
        grid=(n3 // tn,),
        in_specs=[pl.BlockSpec((n, d), lambda j: (0, 0)),
                  pl.BlockSpec((d, tn), lambda j: (0, j)),
                  pl.BlockSpec((1, tn), lambda j: (0, j))],
        out_specs=pl.BlockSpec((n, tn), lambda j: (0, j)),
        out_shape=jax.ShapeDtypeStruct((n, n3), F32),
        name="ada",
    )(c, w_ada, b_ada.reshape(1, n3))


PROJ_COLS = 256


def _inproj_body(x_ref, sc_ref, sh_ref, g_ref, shift0_ref, conv0_ref,
                 w_shift, w_za, w_qk, w_vb, w_ob, w_zb, w_gt,
                 mu_ref, w0_ref, a0_ref, kk_ref, ka_ref, rk_ref, wd2_ref, wi2_ref, cw_ref, cb_ref,
                 kq_o, rq_o, bk_o, kkk_o, bk2_o, kk2_o, va_o, bv_o, sz_o, plast_o, shst_o,
                 q_o, k_o, vb_o, gt_o, gob_o, cst_o,
                 prev_scr, xbuf, *, chunk, lora_w, conv_w, first):
    nb, tt, d = x_ref.shape
    M = nb * tt
    L = chunk
    n_chunks = M // L
    da = va_o.shape[-1]
    db = vb_o.shape[-1]
    cc = xbuf.shape[-1]
    pad = SUBLANES

    def _init():
        prev_scr[...] = shift0_ref[...]
        xbuf[:, 0:pad, :] = conv0_ref[...]
    _when(first, _init)

    x = x_ref[...]
    y = x * lax.rsqrt(jnp.mean(x * x, axis=-1, keepdims=True) + NORM_EPS) * g_ref[...]
    h = (y * (1.0 + sc_ref[...]) + sh_ref[...]).reshape(M, d).astype(BF16)
    def proj(w_ref):
        n = w_ref.shape[1]
        parts = []
        for o in range(0, n, PROJ_COLS):
            parts.append(jnp.dot(h, w_ref[:, o:min(o + PROJ_COLS, n)], preferred_element_type=F32))
            yield
        return parts[0] if len(parts) == 1 else jnp.concatenate(parts, axis=-1)

    p = yield from proj(w_shift)
    row_w = lax.broadcasted_iota(jnp.int32, p.shape, 0)
    seq_start = (row_w & (tt - 1)) == 0
    prev = jnp.where(seq_start, _rows_of_seq(prev_scr[...], nb, tt), pltpu.roll(p, 1, axis=0))
    last = _last_rows(p, nb)
    prev_scr[...] = last
    shst_o[...] = last
    ps = p + mu_ref[...] * (prev - p)
    r = ps[:, 0:da]
    k = ps[:, da:2 * da]
    v = ps[:, 2 * da:3 * da]
    wl = ps[:, 3 * da:3 * da + lora_w]
    al = ps[:, 3 * da + lora_w:]

    qk_pre = yield from proj(w_qk)
    z_a = yield from proj(w_za)
    w = w0_ref[...] + _mm(jnp.tanh(wl), wd2_ref[...])
    lw = (-math.exp(-0.5) * LOG2_E) * _sigmoid(w)
    a = _sigmoid(a0_ref[...] + _mm(al, wi2_ref[...]))
    yield
    v_b = yield from proj(w_vb)
    g_t = yield from proj(w_gt)
    o_b = yield from proj(w_ob)
    z_b = yield from proj(w_zb)
    kkr = k * kk_ref[...]
    kk = kkr * lax.rsqrt(jnp.maximum(_head_sums(kkr * kkr), 1e-24))
    k2 = k * (1.0 + (a - 1.0) * ka_ref[...])
    b = kk * a
    va_o[...] = v
    bv_o[...] = _head_sums(r * k2 * rk_ref[...]) * v
    sz_o[...] = _silu(z_a)

    lg = int(math.log2(L))
    row_m = lax.broadcasted_iota(jnp.int32, (M, M), 0)
    col_m = lax.broadcasted_iota(jnp.int32, (M, M), 1)
    tri = ((row_m >= col_m) & (_shr(row_m, lg) == _shr(col_m, lg))).astype(BF16)
    c = _cumsum_time(lw, tri, terms=2)
    yield
    p_last = jnp.exp2(_last_rows(c, n_chunks))
    plast_o[...] = p_last
    p_last_rows = _rows_of_seq(p_last, n_chunks, L)
    e_nc = jnp.exp2(-c)
    bk = b * e_nc
    kkk = k2 * e_nc
    kq_o[...] = (kk * jnp.exp2(c - lw)).astype(kq_o.dtype)
    rq_o[...] = (r * jnp.exp2(c)).astype(rq_o.dtype)
    bk_o[...] = bk.astype(bk_o.dtype)
    kkk_o[...] = kkk.astype(kkk_o.dtype)
    bk2_o[...] = (bk * p_last_rows).astype(bk2_o.dtype)
    kk2_o[...] = (kkk * p_last_rows).astype(kk2_o.dtype)

    xbuf[:, pad:pad + tt, :] = qk_pre.reshape(nb, tt, cc)
    xe = xbuf[...].reshape(nb * (tt + pad), cc)
    conv = cb_ref[...] + qk_pre * cw_ref[conv_w - 1:conv_w, :]
    for back in range(1, conv_w):
        shifted = pltpu.roll(xe, back, axis=0).reshape(nb, tt + pad, cc)[:, pad:, :].reshape(M, cc)
        conv = conv + shifted * cw_ref[conv_w - 1 - back:conv_w - back, :]
    tail = xbuf[:, tt:tt + pad, :]
    xbuf[:, 0:pad, :] = tail
    cst_o[...] = tail
    yield
    qk = _silu(conv)
    q_o[...] = qk[:, 0:db].astype(q_o.dtype)
    k_o[...] = (qk[:, db:] * (1.0 / math.sqrt(HEAD_DIM))).astype(k_o.dtype)
    vb_o[...] = v_b
    gt_o[...] = g_t
    gob_o[...] = _sigmoid(o_b) * _silu(z_b)


N_INPROJ_IN = 23
INPROJ_ROW_OUTS = (0, 1, 2, 3, 4, 5, 6, 7, 8, 11, 12, 13, 14, 15)
INPROJ_CHUNK_OUT = 9
INPROJ_STAGES = 28


def _inproj_kernel(*refs, n_split, chunk, **static):
    ins, outs, scratch = refs[:N_INPROJ_IN], refs[N_INPROJ_IN:-2], refs[-2:]
    first = pl.program_id(1) == 0
    if n_split == 1:
        _run(_inproj_body(*refs, chunk=chunk, first=first, **static))
        return
    tp = ins[0].shape[1] // n_split
    plan = []
    for h in range(n_split):
        rows = pl.ds(h * tp, tp)
        t_ins = [ins[0].at[:, rows]] + list(ins[1:])
        t_outs = [o.at[rows] if i in INPROJ_ROW_OUTS else o for i, o in enumerate(outs)]
        t_outs[INPROJ_CHUNK_OUT] = outs[INPROJ_CHUNK_OUT].at[pl.ds(h * (tp // chunk), tp // chunk)]
        body = _inproj_body(*t_ins, *t_outs, *scratch, chunk=chunk,
                            first=first if h == 0 else None, **static)
        plan.append((body, INPROJ_STAGES, 0.5 * h, 1.0))
    _run(_interleave(plan))


def _inproj(x, scale, shift, g_norm, shift0, conv0p, lp, nb, tt, chunk, act_dtype):
    b, t, d = x.shape
    n_tok = b * t
    n_split = INPROJ_SPLIT if nb == 1 and t % (INPROJ_SPLIT * tt) == 0 else 1
    tb = n_split * tt
    m = nb * tb
    nj = t // tb
    sw = shift0.shape[-1]
    cc = conv0p.shape[-1]
    da, db = lp["w_up_a"].shape[0], lp["w_up_b"].shape[0]
    cw = lp["conv_w"].shape[0]
    row = lambda a: a.reshape(1, -1)
    rows = [row(lp[n]) for n in ("mu_shift", "w0", "a0", "k_k", "k_a", "r_k")]
    smalls = rows + [lp["w_decay2"], lp["w_iclr2"], lp["conv_w"], row(lp["conv_b"])]
    const = lambda i, j: (0, 0)
    tok = lambda i, j: (i * nj + j, 0)
    per_b = lambda i, j: (i, 0, 0)
    in_specs = [pl.BlockSpec((nb, tb, d), lambda i, j: (i, j, 0)),
                pl.BlockSpec((nb, 1, d), per_b),
                pl.BlockSpec((nb, 1, d), per_b),
                pl.BlockSpec((1, d), const),
                pl.BlockSpec((nb, 1, sw), per_b),
                pl.BlockSpec((nb, SUBLANES, cc), per_b)]
    in_specs += [pl.BlockSpec(w.shape, const, pipeline_mode=pl.Buffered(1)) for w in lp["w_segs"]]
    in_specs += [pl.BlockSpec(a.shape, const) for a in smalls]
    tok_out = lambda n, dt: (pl.BlockSpec((m, n), tok), jax.ShapeDtypeStruct((n_tok, n), dt))
    outs = [tok_out(da, act_dtype)] * 6 + [tok_out(da, F32)] * 3
    outs += [(pl.BlockSpec((m // chunk, 1, da), lambda i, j: (i * nj + j, 0, 0)),
              jax.ShapeDtypeStruct((n_tok // chunk, 1, da), F32)),
             (pl.BlockSpec((nb, 1, sw), per_b), jax.ShapeDtypeStruct((b, 1, sw), F32))]
    outs += [tok_out(db, act_dtype)] * 2 + [tok_out(db, F32), tok_out(LANES, F32), tok_out(db, F32)]
    outs += [(pl.BlockSpec((nb, SUBLANES, cc), per_b), jax.ShapeDtypeStruct((b, SUBLANES, cc), F32))]
    return pl.pallas_call(
        functools.partial(_inproj_kernel, chunk=chunk, lora_w=lp["w_decay2"].shape[0], conv_w=cw,
                          n_split=n_split),
        grid=(b // nb, nj),
        in_specs=in_specs,
        out_specs=[o[0] for o in outs], out_shape=[o[1] for o in outs],
        scratch_shapes=[pltpu.VMEM((nb, 1, sw), F32),
                        pltpu.VMEM((nb, tt + SUBLANES, cc), F32)],
        compiler_params=pltpu.CompilerParams(vmem_limit_bytes=VMEM_LIMIT),
        name="inproj",
    )(x, scale, shift, g_norm.reshape(1, d), shift0, conv0p, *lp["w_segs"], *smalls)


def _pair_blockdiag(y, left):
    return jnp.concatenate([jnp.where(left, y, 0.0), jnp.where(left, 0.0, y)], axis=0)


def _rwkv_body(kq_ref, rq_ref, bk_ref, kkk_ref, bk2_ref, kk2_ref, v_ref, bv_ref, sz_ref, plast_ref,
               s0_ref, lnw_ref, lnb_ref, ones_ref, oa_ref, st_ref, s_scr, *, n_seq, chunk, n_heads,
               first, last, after, done):
    M = kq_ref.shape[0]
    L = chunk
    t_step = M // n_seq
    RU = min(M, ROWS)
    PW = 2 * HEAD_DIM
    n_units = M // RU
    n_hp = n_heads // 2
    units = range(n_units)
    hps = range(n_hp)
    pairs = [(u, g) for u in units for g in hps]
    zero_blk = jnp.zeros((n_seq, HEAD_DIM, HEAD_DIM), F32)

    def _load_state():
        for g in hps:
            top = jnp.concatenate([s0_ref[:, 2 * g], zero_blk], axis=-1)
            bot = jnp.concatenate([zero_blk, s0_ref[:, 2 * g + 1]], axis=-1)
            s_scr[:, g] = jnp.concatenate([top, bot], axis=-2)
    _when(first, _load_state)

    lg = int(math.log2(L))
    row = lax.broadcasted_iota(jnp.int32, (RU, 2 * RU), 0)
    lane = lax.broadcasted_iota(jnp.int32, (RU, 2 * RU), 1)
    col = lane & (RU - 1)
    same = _shr(row, lg) == _shr(col, lg)
    incl = (row >= col) & same
    strict = (row > col) & same
    left = lax.broadcasted_iota(jnp.int32, (RU, PW), 1) < HEAD_DIM
    bd_mask = (lax.broadcasted_iota(jnp.int32, (PW, PW), 0) < HEAD_DIM) == (
        lax.broadcasted_iota(jnp.int32, (PW, PW), 1) < HEAD_DIM)
    mmp = lambda x, y: _mm(x, _pair_blockdiag(y, left))

    p_last = plast_ref[...]
    pls = [slice(g * PW, (g + 1) * PW) for g in hps]
    rus = [slice(u * RU, (u + 1) * RU) for u in units]
    blk = lambda ref, q: ref[rus[q[0]], pls[q[1]]]
    kq = {q: blk(kq_ref, q) for q in pairs}
    rq = {q: blk(rq_ref, q) for q in pairs}
    bk2 = {q: blk(bk2_ref, q) for q in pairs}
    kk2 = {q: blk(kk2_ref, q) for q in pairs}
    vs = {q: blk(v_ref, q) for q in pairs}
    qr = {q: jnp.concatenate([kq[q], rq[q]], axis=0).astype(BF16) for q in pairs}
    bkk = {q: jnp.concatenate([_pair_blockdiag(blk(bk_ref, q).astype(F32), left),
                               _pair_blockdiag(blk(kkk_ref, q).astype(F32), left)], axis=0).astype(BF16)
           for q in pairs}
    yield

    gs = {q: _mm_nt(qr[q], bkk[q]) for q in pairs}
    yield
    a_ab = {q: jnp.where(strict, gs[q][:RU, :2 * RU], 0.0) for q in pairs}
    a_ak = {q: jnp.where(strict, gs[q][:RU, 2 * RU:], 0.0) for q in pairs}
    m_rb = {q: jnp.where(incl, gs[q][RU:, :2 * RU], 0.0) for q in pairs}
    m_rk = {q: jnp.where(incl, gs[q][RU:, 2 * RU:], 0.0) for q in pairs}
    yield
    t_inv = dict(zip(pairs, (yield from _tri_inverse_all([a_ab[q] for q in pairs], row, col, L, mmp))))
    akv = {q: mmp(a_ak[q], vs[q]) for q in pairs}
    yield

    assert after is None or after, "the body before this one must have stored its state by now"
    state = {}
    ys = {}
    for u in units:
        if L == RU:
            b = (u * RU) // t_step
            s0 = [state[b, g] if (b, g) in state else s_scr[b, g] for g in hps]
            ws = [_mm_nt(qr[u, g], s0[g]) for g in hps]
            w1 = [x[:RU] for x in ws]
            wr = [x[RU:] for x in ws]
        else:
            n_in = RU // L
            s0s = [[s_scr[u * n_in + i, g] for g in hps] for i in range(n_in)]
            w1, wr = [], []
            for g in hps:
                parts = [_mm_nt(jnp.concatenate([kq[u, g][i * L:(i + 1) * L], rq[u, g][i * L:(i + 1) * L]], axis=0),
                                s0s[i][g]) for i in range(n_in)]
                w1.append(jnp.concatenate([x[:L] for x in parts], axis=0))
                wr.append(jnp.concatenate([x[L:] for x in parts], axis=0))
        yield
        us = [-mmp(t_inv[u, g], w1[g] + akv[u, g]) for g in hps]
        yield
        for g in hps:
            rhs = jnp.concatenate([_pair_blockdiag(us[g], left), _pair_blockdiag(vs[u, g].astype(F32), left)], axis=0)
            ys[u, g] = wr[g] + _mm(jnp.concatenate([m_rb[u, g], m_rk[u, g]], axis=1), rhs)
        yield
        if L == RU:
            for g in hps:
                uv = jnp.concatenate([us[g], vs[u, g].astype(F32)], axis=0)
                bkk2 = jnp.concatenate([bk2[u, g], kk2[u, g]], axis=0)
                state[b, g] = s0[g] * p_last[u][:, pls[g]] + jnp.where(bd_mask, _mm_tn(uv, bkk2), 0.0)
        else:
            for g in hps:
                for i in range(n_in):
                    rows = slice(i * L, (i + 1) * L)
                    uv_i = jnp.concatenate([us[g][rows], vs[u, g][rows]], axis=0)
                    bkk2_i = jnp.concatenate([bk2[u, g][rows], kk2[u, g][rows]], axis=0)
                    state[u * n_in + i, g] = (s0s[i][g] * p_last[u * n_in + i][:, pls[g]]
                                              + jnp.where(bd_mask, _mm_tn(uv_i, bkk2_i), 0.0))
        yield
    for (b, g), s_new in state.items():
        s_scr[b, g] = s_new
    done.append(True)

    rows_out = [jnp.concatenate([ys[u, g] for g in hps], axis=-1) for u in units]
    y = rows_out[0] if n_units == 1 else jnp.concatenate(rows_out, axis=0)
    yc = y - _head_sums_mxu(y, ones_ref) * (1.0 / HEAD_DIM)
    yield
    var = _head_sums_mxu(yc * yc, ones_ref) * (1.0 / HEAD_DIM)
    yn = yc * lax.rsqrt(var + RWKV_GN_EPS)
    oa_ref[...] = ((yn * lnw_ref[...] + lnb_ref[...] + bv_ref[...]) * sz_ref[...]).astype(oa_ref.dtype)

    def _store_state():
        for g in hps:
            st_ref[:, 2 * g] = s_scr[:, g, 0:HEAD_DIM, 0:HEAD_DIM]
            st_ref[:, 2 * g + 1] = s_scr[:, g, HEAD_DIM:, HEAD_DIM:]
    _when(last, _store_state)


def _mlstm_body(q_ref, k_ref, v_ref, g_ref, gob_ref, c0_ref, n0_ref, m0_ref, gb_ref, gnw_ref, ones_ref,
                ob_ref, ct_ref, nt_ref, mt_ref, cn_scr, m_scr, *, n_seq, n_heads, first, last):
    M = q_ref.shape[0]
    L = M // n_seq
    heads = range(n_heads)
    seqs = range(n_seq)

    def _load_state():
        cn_scr[:, :, 0:HEAD_DIM, :] = c0_ref[...]
        cn_scr[:, :, HEAD_DIM:, :] = n0_ref[...]
        m_scr[...] = m0_ref[...]
    _when(first, _load_state)

    row = lax.broadcasted_iota(jnp.int32, (M, M), 0)
    col = lax.broadcasted_iota(jnp.int32, (M, M), 1)
    same_seq = _shr(row, int(math.log2(L))) == _shr(col, int(math.log2(L)))
    incl = (row >= col) & same_seq

    g = g_ref[...] + gb_ref[...]
    bcum = pltpu.roll(_cumsum_time(-_softplus(-g), incl.astype(BF16)), LANES - n_heads, axis=1)
    m_prev = m_scr[...]
    m_prev_rows = _rows_of_seq(m_prev, n_seq, L)
    x_all = g - bcum
    pos = lax.broadcasted_iota(jnp.int32, (M, LANES), 0) & (L - 1)
    m_all = bcum + jnp.maximum(_cummax_time(x_all, L, pos), m_prev_rows)
    bm_all = bcum - m_all
    w_in_all = jnp.exp(bcum + m_prev_rows - m_all)
    e_negm_all = jnp.exp(-m_all)
    m_new = _last_rows(m_all, n_seq)
    b_last = _last_rows(bcum, n_seq)
    ws_all = jnp.exp(_rows_of_seq(b_last - m_new, n_seq, L) + x_all)
    dec_all = jnp.exp(b_last + m_prev - m_new)
    m_scr[...] = m_new
    x_t = jnp.transpose(x_all)
    yield

    ones = jnp.ones((M, SUBLANES), F32)
    sls = [slice(h * HEAD_DIM, (h + 1) * HEAD_DIM) for h in heads]
    qs = [q_ref[:, sl] for sl in sls]
    ks = [k_ref[:, sl] for sl in sls]
    v1 = [jnp.concatenate([v_ref[:, sl], ones], axis=-1) for sl in sls]
    cn = [[cn_scr[b, h] for h in heads] for b in seqs]

    qk_t = [_mm_nt(qs[h], ks[h]) for h in heads]
    if n_seq == 1:
        qc = [_mm_nt(qs[h], cn[0][h]) for h in heads]
    else:
        qc = [jnp.concatenate([_mm_nt(qs[h][b * L:(b + 1) * L], cn[b][h]) for b in seqs], axis=0) for h in heads]
    yield
    w_ts = [jnp.where(incl, jnp.exp(bm_all[:, h:h + 1] + x_t[h:h + 1, :]), 0.0) for h in heads]
    yield
    s = [qk_t[h] * w_ts[h] for h in heads]
    numden = [_mm(s[h], v1[h]) + w_in_all[:, h:h + 1] * qc[h] for h in heads]
    yield
    den = [jnp.maximum(jnp.abs(numden[h][:, HEAD_DIM:HEAD_DIM + 1]), e_negm_all[:, h:h + 1]) for h in heads]
    hh = [numden[h][:, 0:HEAD_DIM] / den[h] for h in heads]
    hh_all = jnp.concatenate(hh, axis=-1)
    ssq = _head_sums_mxu(hh_all * hh_all, ones_ref) * (1.0 / HEAD_DIM)

    yield
    v1w = [v1[h] * ws_all[:, h:h + 1] for h in heads]
    for h in heads:
        for b in seqs:
            rows = slice(b * L, (b + 1) * L)
            cn_scr[b, h] = dec_all[b][:, h:h + 1] * cn[b][h] + _mm_tn(v1w[h][rows], ks[h][rows])
    yield

    hb = hh_all * lax.rsqrt(ssq + MLSTM_GN_EPS)
    ob_ref[...] = (hb * gnw_ref[...] * gob_ref[...]).astype(ob_ref.dtype)

    def _store_state():
        ct_ref[...] = cn_scr[:, :, 0:HEAD_DIM, :]
        nt_ref[...] = cn_scr[:, :, HEAD_DIM:, :]
        mt_ref[...] = m_scr[...]
    _when(last, _store_state)


N_RWKV_IN, N_RWKV_OUT, N_MLSTM_IN, N_MLSTM_OUT = 14, 2, 11, 4
N_MLSTM_ROW_IN = 5
N_RWKV_ROW_IN = 9
RWKV_STAGES, MLSTM_STAGES = 29, 8
RWKV_BODY_ROWS = 256
RWKV_STAGGER = 0.6


def _mixers_kernel(*refs, n_seq, chunk, n_heads_a, n_heads_b):
    it = iter(refs)
    take = lambda n: [next(it) for _ in range(n)]
    r_in, m_in = take(N_RWKV_IN), take(N_MLSTM_IN)
    r_out, m_out = take(N_RWKV_OUT), take(N_MLSTM_OUT)
    s_scr, cn_scr, m_scr = take(3)
    first = pl.program_id(1) == 0
    last = pl.program_id(1) == pl.num_programs(1) - 1
    M = m_in[0].shape[0]
    n_r = M // RWKV_BODY_ROWS if n_seq == 1 and M > RWKV_BODY_ROWS else 1
    n_sub = M // MLSTM_ROWS if n_seq == 1 and M > MLSTM_ROWS else 1
    plan = []
    flags = [[] for _ in range(n_r)]
    for h in range(n_r):
        rows = (lambda r: r.at[pl.ds(h * RWKV_BODY_ROWS, RWKV_BODY_ROWS)]) if n_r > 1 else (lambda r: r)
        chunks = (lambda r: r.at[pl.ds(h * (RWKV_BODY_ROWS // chunk), RWKV_BODY_ROWS // chunk)]) if n_r > 1 else (
            lambda r: r)
        body = _rwkv_body(
            *[rows(r) for r in r_in[:N_RWKV_ROW_IN]], chunks(r_in[N_RWKV_ROW_IN]), *r_in[N_RWKV_ROW_IN + 1:],
            rows(r_out[0]), r_out[1], s_scr, n_seq=n_seq, chunk=chunk, n_heads=n_heads_a,
            first=first if h == 0 else None, last=last if h == n_r - 1 else None,
            after=flags[h - 1] if h else None, done=flags[h])
        plan.append((body, RWKV_STAGES, RWKV_STAGGER * h, 1.0))
    mlstm = []
    for h in range(n_sub):
        rows = (lambda r: r.at[pl.ds(h * MLSTM_ROWS, MLSTM_ROWS)]) if n_sub > 1 else (lambda r: r)
        mlstm.append(_mlstm_body(
            *[rows(r) for r in m_in[:N_MLSTM_ROW_IN]], *m_in[N_MLSTM_ROW_IN:], rows(m_out[0]), *m_out[1:],
            cn_scr, m_scr,
            n_seq=n_seq, n_heads=n_heads_b,
            first=first if h == 0 else None, last=last if h == n_sub - 1 else None))
    plan.append((_chain(*mlstm), MLSTM_STAGES * n_sub, 0.0, 1.0 + RWKV_STAGGER * (n_r - 1)))
    _run(_interleave(plan))


def _mixers(acts, v_a, bv, sz, p_last, s0, ln_w, ln_b, q, k, v_b, gates, gob, c0, n0, m0, gate_b, gn_w,
            n_seq, t_step, chunk):
    b = s0.shape[0]
    n_ha, n_hb = s0.shape[1], c0.shape[1]
    n_tok, da = v_a.shape
    db = v_b.shape[1]
    m = n_seq * t_step
    nj = n_tok // (b * t_step)
    tok = lambda i, j: (i * nj + j, 0)
    per_b3 = lambda i, j: (i, 0, 0)
    per_b4 = lambda i, j: (i, 0, 0, 0)
    const2 = lambda i, j: (0, 0)
    s_spec = pl.BlockSpec((n_seq, n_ha, HEAD_DIM, HEAD_DIM), per_b4)
    c_spec = pl.BlockSpec((n_seq, n_hb, HEAD_DIM, HEAD_DIM), per_b4)
    n_spec = pl.BlockSpec((n_seq, n_hb, SUBLANES, HEAD_DIM), per_b4)
    m_spec = pl.BlockSpec((n_seq, 1, LANES), per_b3)
    row_a, row_b = pl.BlockSpec((m, da), tok), pl.BlockSpec((m, db), tok)
    in_specs = [row_a] * 9
    ones_a = jnp.kron(jnp.eye(n_ha, dtype=F32), jnp.ones((HEAD_DIM, HEAD_DIM), F32)).astype(BF16)
    ones_b = jnp.kron(jnp.eye(n_hb, dtype=F32), jnp.ones((HEAD_DIM, HEAD_DIM), F32)).astype(BF16)
    in_specs += [pl.BlockSpec((m // chunk, 1, da), lambda i, j: (i * nj + j, 0, 0)), s_spec,
                 pl.BlockSpec(ln_w.shape, const2), pl.BlockSpec(ln_b.shape, const2),
                 pl.BlockSpec(ones_a.shape, const2)]
    in_specs += [row_b, row_b, row_b, pl.BlockSpec((m, LANES), tok), row_b, c_spec, n_spec, m_spec,
                 pl.BlockSpec(gate_b.shape, const2), pl.BlockSpec(gn_w.shape, const2),
                 pl.BlockSpec(ones_b.shape, const2)]
    return pl.pallas_call(
        functools.partial(_mixers_kernel, n_seq=n_seq, chunk=chunk, n_heads_a=n_ha, n_heads_b=n_hb),
        grid=(b // n_seq, nj),
        in_specs=in_specs,
        out_specs=[row_a, s_spec, row_b, c_spec, n_spec, m_spec],
        out_shape=[jax.ShapeDtypeStruct((n_tok, da), BF16), jax.ShapeDtypeStruct(s0.shape, F32),
                   jax.ShapeDtypeStruct((n_tok, db), BF16), jax.ShapeDtypeStruct(c0.shape, F32),
                   jax.ShapeDtypeStruct(n0.shape, F32), jax.ShapeDtypeStruct(m0.shape, F32)],
        scratch_shapes=[pltpu.VMEM((n_seq, n_ha // 2, 2 * HEAD_DIM, 2 * HEAD_DIM), F32),
                        pltpu.VMEM((n_seq, n_hb, HEAD_DIM + SUBLANES, HEAD_DIM), F32),
                        pltpu.VMEM((n_seq, 1, LANES), F32)],
        compiler_params=pltpu.CompilerParams(vmem_limit_bytes=VMEM_LIMIT),
        name="mixers",
    )(*acts, v_a, bv, sz, p_last, s0, ln_w, ln_b, ones_a, q, k, v_b, gates, gob, c0, n0, m0, gate_b, gn_w, ones_b)


def _out_body(oa_ref, ob_ref, x_ref, sc_ref, sh_ref, gate_ref, g_ref, wga_ref, wgb_ref, wua_ref, wub_ref, wo_ref,
              gf_ref, y_ref, *, nb, final_norm):
    m = x_ref.shape[0]
    rows = lambda ref: _rows_of_seq(ref[...], nb, m // nb)
    x = x_ref[...]
    h = x * lax.rsqrt(jnp.mean(x * x, axis=-1, keepdims=True) + NORM_EPS) * g_ref[...]
    h = (h * (1.0 + rows(sc_ref)) + rows(sh_ref)).astype(BF16)
    gl_a = jnp.dot(h, wga_ref[...], preferred_element_type=F32)
    yield
    gl_b = jnp.dot(h, wgb_ref[...], preferred_element_type=F32)
    yield
    ua = jnp.dot(oa_ref[...], wua_ref[...], preferred_element_type=F32)
    ub = jnp.dot(ob_ref[...], wub_ref[...], preferred_element_type=F32)
    yield
    merged = _sigmoid(gl_a) * ua + _sigmoid(gl_b) * ub
    mo = jnp.dot(merged.astype(BF16), wo_ref[...], preferred_element_type=F32)
    yield
    xn = x + rows(gate_ref) * mo
    if final_norm:
        xn = xn * lax.rsqrt(jnp.mean(xn * xn, axis=-1, keepdims=True) + NORM_EPS) * gf_ref[...]
    y_ref[...] = xn


def _out_kernel(*refs, n_split, **static):
    if n_split == 1:
        _run(_out_body(*refs, **static))
        return
    oa_ref, ob_ref, x_ref = refs[:3]
    y_ref = refs[-1]
    tp = x_ref.shape[0] // n_split
    plan = []
    for h in range(n_split):
        rows = pl.ds(h * tp, tp)
        body = _out_body(oa_ref.at[rows], ob_ref.at[rows], x_ref.at[rows], *refs[3:-1], y_ref.at[rows], **static)
        plan.append((body, 4, 0.5 * h, 1.0))
    _run(_interleave(plan))


def _out(out_a, out_b, x, scale, shift, gate, g_norm, w_gl_a, w_gl_b, w_up_a, w_up_b, w_out, g_final,
         b, nb, tt, final_norm):
    n_tok, d = x.shape
    t = n_tok // b
    n_split = OUT_SPLIT if nb == 1 and t % (OUT_SPLIT * tt) == 0 else 1
    nj = t // (n_split * tt)
    m = nb * n_split * tt
    tok = lambda i, j: (i * nj + j, 0)
    const2 = lambda i, j: (0, 0)
    per_b = lambda i, j: (i, 0, 0)
    wspec = lambda w: pl.BlockSpec(w.shape, const2, pipeline_mode=pl.Buffered(1))
    in_specs = [pl.BlockSpec((m, out_a.shape[-1]), tok),
                pl.BlockSpec((m, out_b.shape[-1]), tok),
                pl.BlockSpec((m, d), tok),
                pl.BlockSpec((nb, 1, d), per_b),
                pl.BlockSpec((nb, 1, d), per_b),
                pl.BlockSpec((nb, 1, d), per_b),
                pl.BlockSpec((1, d), const2),
                wspec(w_gl_a), wspec(w_gl_b), wspec(w_up_a), wspec(w_up_b), wspec(w_out),
                pl.BlockSpec((1, d), const2)]
    return pl.pallas_call(
        functools.partial(_out_kernel, nb=nb, final_norm=final_norm, n_split=n_split),
        grid=(b // nb, nj),
        in_specs=in_specs,
        out_specs=pl.BlockSpec((m, d), tok),
        out_shape=jax.ShapeDtypeStruct((n_tok, d), F32),
        compiler_params=pltpu.CompilerParams(vmem_limit_bytes=VMEM_LIMIT),
        name="merge_out",
    )(out_a, out_b, x, scale, shift, gate, g_norm.reshape(1, d), w_gl_a, w_gl_b, w_up_a, w_up_b, w_out,
      g_final.reshape(1, d))


def _token_tiling(b, t, target):
    if t >= target:
        return 1, math.gcd(t, target)
    return math.gcd(b, target // t), t


def _layer(x, mod, states, lp, g_final, final_norm):
    b, t, d = x.shape
    shift0, s0, conv0, c0, n0, m0 = states
    n_hb = c0.shape[1]
    cw = lp["conv_w"].shape[0]
    row = lambda a: a.reshape(1, -1)

    ada_shift, ada_scale, ada_gate = (mod[:, None, i * d:(i + 1) * d] for i in range(3))
    nb, tt = _token_tiling(b, t, PROJ_ROWS)
    L = math.gcd(t, ROWS)
    act_dtype = BF16 if L % 16 == 0 else F32
    conv0p = jnp.pad(conv0, ((0, 0), (SUBLANES - (cw - 1), 0), (0, 0)))
    n0p = jnp.broadcast_to(n0[:, :, None, :], (b, n_hb, SUBLANES, HEAD_DIM))
    m0p = jnp.pad(m0, ((0, 0), (0, LANES - n_hb)))[:, None, :]
    gate_b = jnp.concatenate([lp["b_i"], lp["b_f"], jnp.zeros((LANES - 2 * n_hb,), F32)])[None, :]
    n_seq, t_step = _token_tiling(b, t, MIX_ROWS if t >= MIX_ROWS else MLSTM_ROWS)
    (kq, rq, bk, kkk, bk2, kk2, v_a, bv, sz, p_last, shift_t,
     q, k, v_b, gates, gob, conv_tail) = _inproj(
        x, ada_scale, ada_shift, lp["g_norm"], shift0[:, None, :], conv0p, lp, nb, tt, L, act_dtype)
    out_a, s_t, out_b, c_t, n_t, m_t = _mixers(
        (kq, rq, bk, kkk, bk2, kk2), v_a, bv, sz, p_last, s0, row(lp["ln_w"]), row(lp["ln_b"]),
        q, k, v_b, gates, gob, c0, n0p, m0p, gate_b, row(lp["gn_w"]), n_seq, t_step, L)

    y = _out(out_a, out_b, x.reshape(b * t, d), ada_scale, ada_shift, ada_gate, lp["g_norm"],
             lp["w_gl_a"], lp["w_gl_b"], lp["w_up_a"], lp["w_up_b"], lp["w_out"],
             g_final, b, nb, tt, final_norm).reshape(b, t, d)
    new_states = (shift_t[:, 0], s_t, conv_tail[:, SUBLANES - (cw - 1):], c_t, n_t[:, :, 0, :], m_t[:, 0, :n_hb])
    return y, new_states


def _trunk(x, mods, states, layers, g_final):
    depth = len(layers)
    new = [[] for _ in states]
    for l in range(depth):
        st = tuple(s[l] for s in states)
        x, st_new = _layer(x, mods[l], st, layers[l], g_final, final_norm=(l == depth - 1))
        for lst, s in zip(new, st_new):
            lst.append(s.astype(x.dtype))
    return x, tuple(jnp.stack(lst) for lst in new)


def kernel(x_prompt, x_sample, c_prompt, c_sample, state_rwkv_shift, state_rwkv_S, state_mlstm_conv, state_mlstm_C, state_mlstm_n, state_mlstm_m, g_norm, w_ada, b_ada, w_in, mu_shift, w_decay2, w0, w_iclr2, a0, k_k, k_a, r_k, ln_w, ln_b, conv_w, conv_b, b_i, b_f, gn_w, w_up_a, w_up_b, w_out, g_final):
    depth = g_norm.shape[0]
    bp, bs = x_prompt.shape[0], x_sample.shape[0]
    d = x_prompt.shape[-1]
    da, db = w_up_a.shape[1], w_up_b.shape[1]
    n_ha, n_hb = da // HEAD_DIM, db // HEAD_DIM
    sw = mu_shift.shape[-1]
    cc = conv_w.shape[-1]
    cw = conv_w.shape[1]
    dt = x_prompt.dtype

    sizes = (sw, da, cc, db, db, n_hb, n_hb, db, d, d)
    offs = [0]
    for s in sizes:
        offs.append(offs[-1] + s)
    seg = lambda w, i: w[:, offs[i]:offs[i + 1]]

    c_all = jnp.concatenate([c_prompt, c_sample], axis=0)
    n_c = c_all.shape[0]
    c_pad = jnp.pad(c_all, ((0, (-n_c) % SUBLANES), (0, 0)))

    layers, mods_p, mods_s = [], [], []
    for l in range(depth):
        w = w_in[l]
        wg = jnp.concatenate([seg(w, 5), seg(w, 6)], axis=1)
        wg = jnp.pad(wg, ((0, 0), (0, LANES - wg.shape[1])))
        w_segs = tuple(x.astype(BF16) for x in
                       (seg(w, 0), seg(w, 1), seg(w, 2), seg(w, 3), seg(w, 4), seg(w, 7), wg))
        layers.append(dict(
            g_norm=g_norm[l], w_segs=w_segs, w_gl_a=seg(w, 8).astype(BF16), w_gl_b=seg(w, 9).astype(BF16),
            mu_shift=mu_shift[l], w_decay2=w_decay2[l].astype(BF16),
            w0=w0[l], w_iclr2=w_iclr2[l].astype(BF16), a0=a0[l], k_k=k_k[l], k_a=k_a[l], r_k=r_k[l],
            ln_w=ln_w[l], ln_b=ln_b[l], conv_w=conv_w[l], conv_b=conv_b[l], b_i=b_i[l], b_f=b_f[l],
            gn_w=gn_w[l], w_up_a=w_up_a[l].astype(BF16), w_up_b=w_up_b[l].astype(BF16),
            w_out=w_out[l].astype(BF16)))
        mod = _ada(c_pad, w_ada[l], b_ada[l])
        mods_p.append(mod[:bp])
        mods_s.append(mod[bp:bp + bs])

    prompt_states = (
        jnp.zeros((depth, bp, sw), dt),
        jnp.zeros((depth, bp, n_ha, HEAD_DIM, HEAD_DIM), dt),
        jnp.zeros((depth, bp, cw - 1, cc), dt),
        jnp.zeros((depth, bp, n_hb, HEAD_DIM, HEAD_DIM), dt),
        jnp.zeros((depth, bp, n_hb, HEAD_DIM), dt),
        jnp.zeros((depth, bp, n_hb), dt),
    )
    sample_states = (state_rwkv_shift, state_rwkv_S, state_mlstm_conv,
                     state_mlstm_C, state_mlstm_n, state_mlstm_m)
    y_p, st_p = _trunk(x_prompt, mods_p, prompt_states, layers, g_final)
    y_s, st_s = _trunk(x_sample, mods_s, sample_states, layers, g_final)
    return (y_p, y_s) + st_p + st_s
```

```python
import functools
import math

import jax
import jax.numpy as jnp
from jax import lax
from jax.experimental import pallas as pl
from jax.experimental.pallas import tpu as pltpu

F32 = jnp.float32
BF16 = jnp.bfloat16

HEAD_DIM = 64
NORM_EPS = 1e-6
RWKV_GN_EPS = 64e-5
MLSTM_GN_EPS = 1e-6
ROWS = 64
MLSTM_ROWS = 128
MIX_ROWS = 1024
PROJ_ROWS = 256
INPROJ_SPLIT = 2
OUT_SPLIT = 4
SUBLANES = 8
LANES = 128
TRI_BASE = 4
VMEM_LIMIT = 60 * 1024 * 1024

LOG2_E = 1.4426950408889634
NT_DIMS = (((1,), (1,)), ((), ()))
TN_DIMS = (((0,), (0,)), ((), ()))


def _mm(a, b):
    return jnp.dot(a.astype(BF16), b.astype(BF16), preferred_element_type=F32)


def _mm_nt(a, b):
    return lax.dot_general(a.astype(BF16), b.astype(BF16), NT_DIMS, preferred_element_type=F32)


def _mm_tn(a, b):
    return lax.dot_general(a.astype(BF16), b.astype(BF16), TN_DIMS, preferred_element_type=F32)


def _sigmoid(x):
    return 1.0 / (1.0 + jnp.exp2(x * -LOG2_E))


def _silu(x):
    return x * _sigmoid(x)


def _softplus(x):
    return jnp.maximum(x, 0.0) + jnp.log(1.0 + jnp.exp(-jnp.abs(x)))


def _cumsum_time(x, tri_bf16, terms=3):
    dot = functools.partial(jnp.dot, preferred_element_type=F32)
    rest = x
    total = None
    for i in range(terms):
        part = rest.astype(BF16)
        if i + 1 < terms:
            rest = rest - part.astype(F32)
        total = dot(tri_bf16, part) if total is None else total + dot(tri_bf16, part)
    return total


def _shr(x, n):
    return lax.shift_right_logical(x, jnp.full(x.shape, n, x.dtype))


def _rows_of_seq(x, n_seq, L):
    if n_seq == 1:
        return x[0]
    return jnp.broadcast_to(x, (n_seq, L, x.shape[-1])).reshape(n_seq * L, x.shape[-1])


def _last_rows(x, n_seq):
    m, n = x.shape
    L = m // n_seq
    return x.reshape(n_seq, L, n)[:, L - 1:L, :]


def _head_sums(x):
    m, n = x.shape
    parts = [jnp.broadcast_to(jnp.sum(x[:, o:o + HEAD_DIM], axis=-1, keepdims=True), (m, HEAD_DIM))
             for o in range(0, n, HEAD_DIM)]
    return jnp.concatenate(parts, axis=-1)


def _head_sums_mxu(x, ones_ref):
    hi = x.astype(BF16)
    lo = (x - hi.astype(F32)).astype(BF16)
    s = jnp.dot(jnp.concatenate([hi, lo], axis=0), ones_ref[...], preferred_element_type=F32)
    return s[:x.shape[0]] + s[x.shape[0]:]


def _cummax_time(x, L, pos):
    y = x
    sh = 1
    while sh < L:
        y = jnp.where(pos >= sh, jnp.maximum(y, pltpu.roll(y, sh, axis=0)), y)
        sh *= 2
    return y


def _tri_inverse_all(mats, row, col, L, mm=_mm):
    bs = min(TRI_BASE, L)
    sh = int(math.log2(bs))
    same = _shr(row, sh) == _shr(col, sh)
    eye = (row == col).astype(F32)
    rows = mats[0].shape[0]
    ns = [jnp.where(same, -a, 0.0) for a in mats]
    ts = [eye + n for n in ns]
    if bs > 2:
        ps = [mm(n, n) for n in ns]
        yield
        k = 2
        while 2 * k < bs:
            both = [mm(jnp.concatenate([t, p], axis=0), p) for t, p in zip(ts, ps)]
            ts = [t + x[:rows] for t, x in zip(ts, both)]
            ps = [x[rows:] for x in both]
            yield
            k *= 2
        ts = [t + mm(t, p) for t, p in zip(ts, ps)]
        yield
    size = bs
    while size < L:
        sh = int(math.log2(size))
        lower_left = (_shr(row, sh + 1) == _shr(col, sh + 1)) & (_shr(row, sh) != _shr(col, sh))
        offs = [jnp.where(lower_left, a, 0.0) for a in mats]
        tmp = [mm(t, o) for t, o in zip(ts, offs)]
        yield
        ts = [t - mm(x, t) for t, x in zip(ts, tmp)]
        yield
        size *= 2
    return ts


def _interleave(plan):
    live = [[g, n, s, w, 0] for g, n, s, w in plan]
    while live:
        item = min(live, key=lambda it: it[2] + it[3] * it[4] / it[1])
        try:
            next(item[0])
            item[4] += 1
            yield
        except StopIteration:
            live.remove(item)


def _chain(*gens):
    for g in gens:
        yield from g


def _run(gen):
    for _ in gen:
        pass


def _when(pred, fn):
    if pred is not None:
        pl.when(pred)(fn)


def _ada_kernel(c_ref, w_ref, b_ref, o_ref):
    o_ref[...] = _mm(_silu(c_ref[...]), w_ref[...]) + b_ref[...]


def _ada(c, w_ada, b_ada):
    n, d = c.shape
    n3 = w_ada.shape[1]
    tn = d
    return pl.pallas_call(
        _ada_kernel,
        grid=(n3 // tn,),
        in_specs=[pl.BlockSpec((n, d), lambda j: (0, 0)),
                  pl.BlockSpec((d, tn), lambda j: (0, j)),
                  pl.BlockSpec((1, tn), lambda j: (0, j))],
        out_specs=pl.BlockSpec((n, tn), lambda j: (0, j)),
        out_shape=jax.ShapeDtypeStruct((n, n3), F32),
        name="ada",
    )(c, w_ada, b_ada.reshape(1, n3))


PROJ_COLS = 256


def _inproj_body(x_ref, sc_ref, sh_ref, g_ref, shift0_ref, conv0_ref,
                 w_shift, w_za, w_qk, w_vb, w_ob, w_zb, w_gt,
                 mu_ref, w0_ref, a0_ref, kk_ref, ka_ref, rk_ref, wd2_ref, wi2_ref, cw_ref, cb_ref,
                 kq_o, rq_o, bk_o, kkk_o, bk2_o, kk2_o, va_o, bv_o, sz_o, plast_o, shst_o,
                 q_o, k_o, vb_o, gt_o, gob_o, cst_o,
                 prev_scr, xbuf, *, chunk, lora_w, conv_w, first):
    nb, tt, d = x_ref.shape
    M = nb * tt
    L = chunk
    n_chunks = M // L
    da = va_o.shape[-1]
    db = vb_o.shape[-1]
    cc = xbuf.shape[-1]
    pad = SUBLANES

    def _init():
        prev_scr[...] = shift0_ref[...]
        xbuf[:, 0:pad, :] = conv0_ref[...]
    _when(first, _init)

    x = x_ref[...]
    y = x * lax.rsqrt(jnp.mean(x * x, axis=-1, keepdims=True) + NORM_EPS) * g_ref[...]
    h = (y * (1.0 + sc_ref[...]) + sh_ref[...]).reshape(M, d).astype(BF16)
    def proj(w_ref):
        n = w_ref.shape[1]
        parts = []
        for o in range(0, n, PROJ_COLS):
            parts.append(jnp.dot(h, w_ref[:, o:min(o + PROJ_COLS, n)], preferred_element_type=F32))
            yield
        return parts[0] if len(parts) == 1 else jnp.concatenate(parts, axis=-1)

    p = yield from proj(w_shift)
    row_w = lax.broadcasted_iota(jnp.int32, p.shape, 0)
    seq_start = (row_w & (tt - 1)) == 0
    prev = jnp.where(seq_start, _rows_of_seq(prev_scr[...], nb, tt), pltpu.roll(p, 1, axis=0))
    last = _last_rows(p, nb)
    prev_scr[...] = last
    shst_o[...] = last
    ps = p + mu_ref[...] * (prev - p)
    r = ps[:, 0:da]
    k = ps[:, da:2 * da]
    v = ps[:, 2 * da:3 * da]
    wl = ps[:, 3 * da:3 * da + lora_w]
    al = ps[:, 3 * da + lora_w:]

    qk_pre = yield from proj(w_qk)
    z_a = yield from proj(w_za)
    w = w0_ref[...] + _mm(jnp.tanh(wl), wd2_ref[...])
    lw = (-math.exp(-0.5) * LOG2_E) * _sigmoid(w)
    a = _sigmoid(a0_ref[...] + _mm(al, wi2_ref[...]))
    yield
    v_b = yield from proj(w_vb)
    g_t = yield from proj(w_gt)
    o_b = yield from proj(w_ob)
    z_b = yield from proj(w_zb)
    kkr = k * kk_ref[...]
    kk = kkr * lax.rsqrt(jnp.maximum(_head_sums(kkr * kkr), 1e-24))
    k2 = k * (1.0 + (a - 1.0) * ka_ref[...])
    b = kk * a
    va_o[...] = v
    bv_o[...] = _head_sums(r * k2 * rk_ref[...]) * v
    sz_o[...] = _silu(z_a)

    lg = int(math.log2(L))
    row_m = lax.broadcasted_iota(jnp.int32, (M, M), 0)
    col_m = lax.broadcasted_iota(jnp.int32, (M, M), 1)
    tri = ((row_m >= col_m) & (_shr(row_m, lg) == _shr(col_m, lg))).astype(BF16)
    c = _cumsum_time(lw, tri, terms=2)
    yield
    p_last = jnp.exp2(_last_rows(c, n_chunks))
    plast_o[...] = p_last
    p_last_rows = _rows_of_seq(p_last, n_chunks, L)
    e_nc = jnp.exp2(-c)
    bk = b * e_nc
    kkk = k2 * e_nc
    kq_o[...] = (kk * jnp.exp2(c - lw)).astype(kq_o.dtype)
    rq_o[...] = (r * jnp.exp2(c)).astype(rq_o.dtype)
    bk_o[...] = bk.astype(bk_o.dtype)
    kkk_o[...] = kkk.astype(kkk_o.dtype)
    bk2_o[...] = (bk * p_last_rows).astype(bk2_o.dtype)
    kk2_o[...] = (kkk * p_last_rows).astype(kk2_o.dtype)

    xbuf[:, pad:pad + tt, :] = qk_pre.reshape(nb, tt, cc)
    xe = xbuf[...].reshape(nb * (tt + pad), cc)
    conv = cb_ref[...] + qk_pre * cw_ref[conv_w - 1:conv_w, :]
    for back in range(1, conv_w):
        shifted = pltpu.roll(xe, back, axis=0).reshape(nb, tt + pad, cc)[:, pad:, :].reshape(M, cc)
        conv = conv + shifted * cw_ref[conv_w - 1 - back:conv_w - back, :]
    tail = xbuf[:, tt:tt + pad, :]
    xbuf[:, 0:pad, :] = tail
    cst_o[...] = tail
    yield
    qk = _silu(conv)
    q_o[...] = qk[:, 0:db].astype(q_o.dtype)
    k_o[...] = (qk[:, db:] * (1.0 / math.sqrt(HEAD_DIM))).astype(k_o.dtype)
    vb_o[...] = v_b
    gt_o[...] = g_t
    gob_o[...] = _sigmoid(o_b) * _silu(z_b)


N_INPROJ_IN = 23
INPROJ_ROW_OUTS = (0, 1, 2, 3, 4, 5, 6, 7, 8, 11, 12, 13, 14, 15)
INPROJ_CHUNK_OUT = 9
INPROJ_STAGES = 28


def _inproj_kernel(*refs, n_split, chunk, **static):
    ins, outs, scratch = refs[:N_INPROJ_IN], refs[N_INPROJ_IN:-2], refs[-2:]
    first = pl.program_id(1) == 0
    if n_split == 1:
        _run(_inproj_body(*refs, chunk=chunk, first=first, **static))
        return
    tp = ins[0].shape[1] // n_split
    plan = []
    for h in range(n_split):
        rows = pl.ds(h * tp, tp)
        t_ins = [ins[0].at[:, rows]] + list(ins[1:])
        t_outs = [o.at[rows] if i in INPROJ_ROW_OUTS else o for i, o in enumerate(outs)]
        t_outs[INPROJ_CHUNK_OUT] = outs[INPROJ_CHUNK_OUT].at[pl.ds(h * (tp // chunk), tp // chunk)]
        body = _inproj_body(*t_ins, *t_outs, *scratch, chunk=chunk,
                            first=first if h == 0 else None, **static)
        plan.append((body, INPROJ_STAGES, 0.5 * h, 1.0))
    _run(_interleave(plan))


def _inproj(x, scale, shift, g_norm, shift0, conv0p, lp, nb, tt, chunk, act_dtype):
    b, t, d = x.shape
    n_tok = b * t
    n_split = INPROJ_SPLIT if nb == 1 and t % (INPROJ_SPLIT * tt) == 0 else 1
    tb = n_split * tt
    m = nb * tb
    nj = t // tb
    sw = shift0.shape[-1]
    cc = conv0p.shape[-1]
    da, db = lp["w_up_a"].shape[0], lp["w_up_b"].shape[0]
    cw = lp["conv_w"].shape[0]
    row = lambda a: a.reshape(1, -1)
    rows = [row(lp[n]) for n in ("mu_shift", "w0", "a0", "k_k", "k_a", "r_k")]
    smalls = rows + [lp["w_decay2"], lp["w_iclr2"], lp["conv_w"], row(lp["conv_b"])]
    const = lambda i, j: (0, 0)
    tok = lambda i, j: (i * nj + j, 0)
    per_b = lambda i, j: (i, 0, 0)
    in_specs = [pl.BlockSpec((nb, tb, d), lambda i, j: (i, j, 0)),
                pl.BlockSpec((nb, 1, d), per_b),
                pl.BlockSpec((nb, 1, d), per_b),
                pl.BlockSpec((1, d), const),
                pl.BlockSpec((nb, 1, sw), per_b),
                pl.BlockSpec((nb, SUBLANES, cc), per_b)]
    in_specs += [pl.BlockSpec(w.shape, const, pipeline_mode=pl.Buffered(1)) for w in lp["w_segs"]]
    in_specs += [pl.BlockSpec(a.shape, const) for a in smalls]
    tok_out = lambda n, dt: (pl.BlockSpec((m, n), tok), jax.ShapeDtypeStruct((n_tok, n), dt))
    outs = [tok_out(da, act_dtype)] * 6 + [tok_out(da, F32)] * 3
    outs += [(pl.BlockSpec((m // chunk, 1, da), lambda i, j: (i * nj + j, 0, 0)),
              jax.ShapeDtypeStruct((n_tok // chunk, 1, da), F32)),
             (pl.BlockSpec((nb, 1, sw), per_b), jax.ShapeDtypeStruct((b, 1, sw), F32))]
    outs += [tok_out(db, act_dtype)] * 2 + [tok_out(db, F32), tok_out(LANES, F32), tok_out(db, F32)]
    outs += [(pl.BlockSpec((nb, SUBLANES, cc), per_b), jax.ShapeDtypeStruct((b, SUBLANES, cc), F32))]
    return pl.pallas_call(
        functools.partial(_inproj_kernel, chunk=chunk, lora_w=lp["w_decay2"].shape[0], conv_w=cw,
                          n_split=n_split),
        grid=(b // nb, nj),
        in_specs=in_specs,
        out_specs=[o[0] for o in outs], out_shape=[o[1] for o in outs],
        scratch_shapes=[pltpu.VMEM((nb, 1, sw), F32),
                        pltpu.VMEM((nb, tt + SUBLANES, cc), F32)],
        compiler_params=pltpu.CompilerParams(vmem_limit_bytes=VMEM_LIMIT),
        name="inproj",
    )(x, scale, shift, g_norm.reshape(1, d), shift0, conv0p, *lp["w_segs"], *smalls)


def _pair_blockdiag(y, left):
    return jnp.concatenate([jnp.where(left, y, 0.0), jnp.where(left, 0.0, y)], axis=0)


def _rwkv_body(kq_ref, rq_ref, bk_ref, kkk_ref, bk2_ref, kk2_ref, v_ref, bv_ref, sz_ref, plast_ref,
               s0_ref, lnw_ref, lnb_ref, ones_ref, oa_ref, st_ref, s_scr, *, n_seq, chunk, n_heads,
               first, last, after, done):
    M = kq_ref.shape[0]
    L = chunk
    t_step = M // n_seq
    RU = min(M, ROWS)
    PW = 2 * HEAD_DIM
    n_units = M // RU
    n_hp = n_heads // 2
    units = range(n_units)
    hps = range(n_hp)
    pairs = [(u, g) for u in units for g in hps]
    zero_blk = jnp.zeros((n_seq, HEAD_DIM, HEAD_DIM), F32)

    def _load_state():
        for g in hps:
            top = jnp.concatenate([s0_ref[:, 2 * g], zero_blk], axis=-1)
            bot = jnp.concatenate([zero_blk, s0_ref[:, 2 * g + 1]], axis=-1)
            s_scr[:, g] = jnp.concatenate([top, bot], axis=-2)
    _when(first, _load_state)

    lg = int(math.log2(L))
    row = lax.broadcasted_iota(jnp.int32, (RU, 2 * RU), 0)
    lane = lax.broadcasted_iota(jnp.int32, (RU, 2 * RU), 1)
    col = lane & (RU - 1)
    same = _shr(row, lg) == _shr(col, lg)
    incl = (row >= col) & same
    strict = (row > col) & same
    left = lax.broadcasted_iota(jnp.int32, (RU, PW), 1) < HEAD_DIM
    bd_mask = (lax.broadcasted_iota(jnp.int32, (PW, PW), 0) < HEAD_DIM) == (
        lax.broadcasted_iota(jnp.int32, (PW, PW), 1) < HEAD_DIM)
    mmp = lambda x, y: _mm(x, _pair_blockdiag(y, left))

    p_last = plast_ref[...]
    pls = [slice(g * PW, (g + 1) * PW) for g in hps]
    rus = [slice(u * RU, (u + 1) * RU) for u in units]
    blk = lambda ref, q: ref[rus[q[0]], pls[q[1]]]
    kq = {q: blk(kq_ref, q) for q in pairs}
    rq = {q: blk(rq_ref, q) for q in pairs}
    bk2 = {q: blk(bk2_ref, q) for q in pairs}
    kk2 = {q: blk(kk2_ref, q) for q in pairs}
    vs = {q: blk(v_ref, q) for q in pairs}
    qr = {q: jnp.concatenate([kq[q], rq[q]], axis=0).astype(BF16) for q in pairs}
    bkk = {q: jnp.concatenate([_pair_blockdiag(blk(bk_ref, q).astype(F32), left),
                               _pair_blockdiag(blk(kkk_ref, q).astype(F32), left)], axis=0).astype(BF16)
           for q in pairs}
    yield

    gs = {q: _mm_nt(qr[q], bkk[q]) for q in pairs}
    yield
    a_ab = {q: jnp.where(strict, gs[q][:RU, :2 * RU], 0.0) for q in pairs}
    a_ak = {q: jnp.where(strict, gs[q][:RU, 2 * RU:], 0.0) for q in pairs}
    m_rb = {q: jnp.where(incl, gs[q][RU:, :2 * RU], 0.0) for q in pairs}
    m_rk = {q: jnp.where(incl, gs[q][RU:, 2 * RU:], 0.0) for q in pairs}
    yield
    t_inv = dict(zip(pairs, (yield from _tri_inverse_all([a_ab[q] for q in pairs], row, col, L, mmp))))
    akv = {q: mmp(a_ak[q], vs[q]) for q in pairs}
    yield

    assert after is None or after, "the body before this one must have stored its state by now"
    state = {}
    ys = {}
    for u in units:
        if L == RU:
            b = (u * RU) // t_step
            s0 = [state[b, g] if (b, g) in state else s_scr[b, g] for g in hps]
            ws = [_mm_nt(qr[u, g], s0[g]) for g in hps]
            w1 = [x[:RU] for x in ws]
            wr = [x[RU:] for x in ws]
        else:
            n_in = RU // L
            s0s = [[s_scr[u * n_in + i, g] for g in hps] for i in range(n_in)]
            w1, wr = [], []
            for g in hps:
                parts = [_mm_nt(jnp.concatenate([kq[u, g][i * L:(i + 1) * L], rq[u, g][i * L:(i + 1) * L]], axis=0),
                                s0s[i][g]) for i in range(n_in)]
                w1.append(jnp.concatenate([x[:L] for x in parts], axis=0))
                wr.append(jnp.concatenate([x[L:] for x in parts], axis=0))
        yield
        us = [-mmp(t_inv[u, g], w1[g] + akv[u, g]) for g in hps]
        yield
        for g in hps:
            rhs = jnp.concatenate([_pair_blockdiag(us[g], left), _pair_blockdiag(vs[u, g].astype(F32), left)], axis=0)
            ys[u, g] = wr[g] + _mm(jnp.concatenate([m_rb[u, g], m_rk[u, g]], axis=1), rhs)
        yield
        if L == RU:
            for g in hps:
                uv = jnp.concatenate([us[g], vs[u, g].astype(F32)], axis=0)
                bkk2 = jnp.concatenate([bk2[u, g], kk2[u, g]], axis=0)
                state[b, g] = s0[g] * p_last[u][:, pls[g]] + jnp.where(bd_mask, _mm_tn(uv, bkk2), 0.0)
        else:
            for g in hps:
                for i in range(n_in):
                    rows = slice(i * L, (i + 1) * L)
                    uv_i = jnp.concatenate([us[g][rows], vs[u, g][rows]], axis=0)
                    bkk2_i = jnp.concatenate([bk2[u, g][rows], kk2[u, g][rows]], axis=0)
                    state[u * n_in + i, g] = (s0s[i][g] * p_last[u * n_in + i][:, pls[g]]
                                              + jnp.where(bd_mask, _mm_tn(uv_i, bkk2_i), 0.0))
        yield
    for (b, g), s_new in state.items():
        s_scr[b, g] = s_new
    done.append(True)

    rows_out = [jnp.concatenate([ys[u, g] for g in hps], axis=-1) for u in units]
    y = rows_out[0] if n_units == 1 else jnp.concatenate(rows_out, axis=0)
    yc = y - _head_sums_mxu(y, ones_ref) * (1.0 / HEAD_DIM)
    yield
    var = _head_sums_mxu(yc * yc, ones_ref) * (1.0 / HEAD_DIM)
    yn = yc * lax.rsqrt(var + RWKV_GN_EPS)
    oa_ref[...] = ((yn * lnw_ref[...] + lnb_ref[...] + bv_ref[...]) * sz_ref[...]).astype(oa_ref.dtype)

    def _store_state():
        for g in hps:
            st_ref[:, 2 * g] = s_scr[:, g, 0:HEAD_DIM, 0:HEAD_DIM]
            st_ref[:, 2 * g + 1] = s_scr[:, g, HEAD_DIM:, HEAD_DIM:]
    _when(last, _store_state)


def _mlstm_body(q_ref, k_ref, v_ref, g_ref, gob_ref, c0_ref, n0_ref, m0_ref, gb_ref, gnw_ref, ones_ref,
                ob_ref, ct_ref, nt_ref, mt_ref, cn_scr, m_scr, *, n_seq, n_heads, first, last):
    M = q_ref.shape[0]
    L = M // n_seq
    heads = range(n_heads)
    seqs = range(n_seq)

    def _load_state():
        cn_scr[:, :, 0:HEAD_DIM, :] = c0_ref[...]
        cn_scr[:, :, HEAD_DIM:, :] = n0_ref[...]
        m_scr[...] = m0_ref[...]
    _when(first, _load_state)

    row = lax.broadcasted_iota(jnp.int32, (M, M), 0)
    col = lax.broadcasted_iota(jnp.int32, (M, M), 1)
    same_seq = _shr(row, int(math.log2(L))) == _shr(col, int(math.log2(L)))
    incl = (row >= col) & same_seq

    g = g_ref[...] + gb_ref[...]
    bcum = pltpu.roll(_cumsum_time(-_softplus(-g), incl.astype(BF16)), LANES - n_heads, axis=1)
    m_prev = m_scr[...]
    m_prev_rows = _rows_of_seq(m_prev, n_seq, L)
    x_all = g - bcum
    pos = lax.broadcasted_iota(jnp.int32, (M, LANES), 0) & (L - 1)
    m_all = bcum + jnp.maximum(_cummax_time(x_all, L, pos), m_prev_rows)
    bm_all = bcum - m_all
    w_in_all = jnp.exp(bcum + m_prev_rows - m_all)
    e_negm_all = jnp.exp(-m_all)
    m_new = _last_rows(m_all, n_seq)
    b_last = _last_rows(bcum, n_seq)
    ws_all = jnp.exp(_rows_of_seq(b_last - m_new, n_seq, L) + x_all)
    dec_all = jnp.exp(b_last + m_prev - m_new)
    m_scr[...] = m_new
    x_t = jnp.transpose(x_all)
    yield

    ones = jnp.ones((M, SUBLANES), F32)
    sls = [slice(h * HEAD_DIM, (h + 1) * HEAD_DIM) for h in heads]
    qs = [q_ref[:, sl] for sl in sls]
    ks = [k_ref[:, sl] for sl in sls]
    v1 = [jnp.concatenate([v_ref[:, sl], ones], axis=-1) for sl in sls]
    cn = [[cn_scr[b, h] for h in heads] for b in seqs]

    qk_t = [_mm_nt(qs[h], ks[h]) for h in heads]
    if n_seq == 1:
        qc = [_mm_nt(qs[h], cn[0][h]) for h in heads]
    else:
        qc = [jnp.concatenate([_mm_nt(qs[h][b * L:(b + 1) * L], cn[b][h]) for b in seqs], axis=0) for h in heads]
    yield
    w_ts = [jnp.where(incl, jnp.exp(bm_all[:, h:h + 1] + x_t[h:h + 1, :]), 0.0) for h in heads]
    yield
    s = [qk_t[h] * w_ts[h] for h in heads]
    numden = [_mm(s[h], v1[h]) + w_in_all[:, h:h + 1] * qc[h] for h in heads]
    yield
    den = [jnp.maximum(jnp.abs(numden[h][:, HEAD_DIM:HEAD_DIM + 1]), e_negm_all[:, h:h + 1]) for h in heads]
    hh = [numden[h][:, 0:HEAD_DIM] / den[h] for h in heads]
    hh_all = jnp.concatenate(hh, axis=-1)
    ssq = _head_sums_mxu(hh_all * hh_all, ones_ref) * (1.0 / HEAD_DIM)

    yield
    v1w = [v1[h] * ws_all[:, h:h + 1] for h in heads]
    for h in heads:
        for b in seqs:
            rows = slice(b * L, (b + 1) * L)
            cn_scr[b, h] = dec_all[b][:, h:h + 1] * cn[b][h] + _mm_tn(v1w[h][rows], ks[h][rows])
    yield

    hb = hh_all * lax.rsqrt(ssq + MLSTM_GN_EPS)
    ob_ref[...] = (hb * gnw_ref[...] * gob_ref[...]).astype(ob_ref.dtype)

    def _store_state():
        ct_ref[...] = cn_scr[:, :, 0:HEAD_DIM, :]
        nt_ref[...] = cn_scr[:, :, HEAD_DIM:, :]
        mt_ref[...] = m_scr[...]
    _when(last, _store_state)


N_RWKV_IN, N_RWKV_OUT, N_MLSTM_IN, N_MLSTM_OUT = 14, 2, 11, 4
N_MLSTM_ROW_IN = 5
N_RWKV_ROW_IN = 9
RWKV_STAGES, MLSTM_STAGES = 29, 8
RWKV_BODY_ROWS = 256
RWKV_STAGGER = 0.6


def _mixers_kernel(*refs, n_seq, chunk, n_heads_a, n_heads_b):
    it = iter(refs)
    take = lambda n: [next(it) for _ in range(n)]
    r_in, m_in = take(N_RWKV_IN), take(N_MLSTM_IN)
    r_out, m_out = take(N_RWKV_OUT), take(N_MLSTM_OUT)
    s_scr, cn_scr, m_scr = take(3)
    first = pl.program_id(1) == 0
    last = pl.program_id(1) == pl.num_programs(1) - 1
    M = m_in[0].shape[0]
    n_r = M // RWKV_BODY_ROWS if n_seq == 1 and M > RWKV_BODY_ROWS else 1
    n_sub = M // MLSTM_ROWS if n_seq == 1 and M > MLSTM_ROWS else 1
    plan = []
    flags = [[] for _ in range(n_r)]
    for h in range(n_r):
        rows = (lambda r: r.at[pl.ds(h * RWKV_BODY_ROWS, RWKV_BODY_ROWS)]) if n_r > 1 else (lambda r: r)
        chunks = (lambda r: r.at[pl.ds(h * (RWKV_BODY_ROWS // chunk), RWKV_BODY_ROWS // chunk)]) if n_r > 1 else (
            lambda r: r)
        body = _rwkv_body(
            *[rows(r) for r in r_in[:N_RWKV_ROW_IN]], chunks(r_in[N_RWKV_ROW_IN]), *r_in[N_RWKV_ROW_IN + 1:],
            rows(r_out[0]), r_out[1], s_scr, n_seq=n_seq, chunk=chunk, n_heads=n_heads_a,
            first=first if h == 0 else None, last=last if h == n_r - 1 else None,
            after=flags[h - 1] if h else None, done=flags[h])
        plan.append((body, RWKV_STAGES, RWKV_STAGGER * h, 1.0))
    mlstm = []
    for h in range(n_sub):
        rows = (lambda r: r.at[pl.ds(h * MLSTM_ROWS, MLSTM_ROWS)]) if n_sub > 1 else (lambda r: r)
        mlstm.append(_mlstm_body(
            *[rows(r) for r in m_in[:N_MLSTM_ROW_IN]], *m_in[N_MLSTM_ROW_IN:], rows(m_out[0]), *m_out[1:],
            cn_scr, m_scr,
            n_seq=n_seq, n_heads=n_heads_b,
            first=first if h == 0 else None, last=last if h == n_sub - 1 else None))
    plan.append((_chain(*mlstm), MLSTM_STAGES * n_sub, 0.0, 1.0 + RWKV_STAGGER * (n_r - 1)))
    _run(_interleave(plan))


def _mixers(acts, v_a, bv, sz, p_last, s0, ln_w, ln_b, q, k, v_b, gates, gob, c0, n0, m0, gate_b, gn_w,
            n_seq, t_step, chunk):
    b = s0.shape[0]
    n_ha, n_hb = s0.shape[1], c0.shape[1]
    n_tok, da = v_a.shape
    db = v_b.shape[1]
    m = n_seq * t_step
    nj = n_tok // (b * t_step)
    tok = lambda i, j: (i * nj + j, 0)
    per_b3 = lambda i, j: (i, 0, 0)
    per_b4 = lambda i, j: (i, 0, 0, 0)
    const2 = lambda i, j: (0, 0)
    s_spec = pl.BlockSpec((n_seq, n_ha, HEAD_DIM, HEAD_DIM), per_b4)
    c_spec = pl.BlockSpec((n_seq, n_hb, HEAD_DIM, HEAD_DIM), per_b4)
    n_spec = pl.BlockSpec((n_seq, n_hb, SUBLANES, HEAD_DIM), per_b4)
    m_spec = pl.BlockSpec((n_seq, 1, LANES), per_b3)
    row_a, row_b = pl.BlockSpec((m, da), tok), pl.BlockSpec((m, db), tok)
    in_specs = [row_a] * 9
    ones_a = jnp.kron(jnp.eye(n_ha, dtype=F32), jnp.ones((HEAD_DIM, HEAD_DIM), F32)).astype(BF16)
    ones_b = jnp.kron(jnp.eye(n_hb, dtype=F32), jnp.ones((HEAD_DIM, HEAD_DIM), F32)).astype(BF16)
    in_specs += [pl.BlockSpec((m // chunk, 1, da), lambda i, j: (i * nj + j, 0, 0)), s_spec,
                 pl.BlockSpec(ln_w.shape, const2), pl.BlockSpec(ln_b.shape, const2),
                 pl.BlockSpec(ones_a.shape, const2)]
    in_specs += [row_b, row_b, row_b, pl.BlockSpec((m, LANES), tok), row_b, c_spec, n_spec, m_spec,
                 pl.BlockSpec(gate_b.shape, const2), pl.BlockSpec(gn_w.shape, const2),
                 pl.BlockSpec(ones_b.shape, const2)]
    return pl.pallas_call(
        functools.partial(_mixers_kernel, n_seq=n_seq, chunk=chunk, n_heads_a=n_ha, n_heads_b=n_hb),
        grid=(b // n_seq, nj),
        in_specs=in_specs,
        out_specs=[row_a, s_spec, row_b, c_spec, n_spec, m_spec],
        out_shape=[jax.ShapeDtypeStruct((n_tok, da), BF16), jax.ShapeDtypeStruct(s0.shape, F32),
                   jax.ShapeDtypeStruct((n_tok, db), BF16), jax.ShapeDtypeStruct(c0.shape, F32),
                   jax.ShapeDtypeStruct(n0.shape, F32), jax.ShapeDtypeStruct(m0.shape, F32)],
        scratch_shapes=[pltpu.VMEM((n_seq, n_ha // 2, 2 * HEAD_DIM, 2 * HEAD_DIM), F32),
                        pltpu.VMEM((n_seq, n_hb, HEAD_DIM + SUBLANES, HEAD_DIM), F32),
                        pltpu.VMEM((n_seq, 1, LANES), F32)],
        compiler_params=pltpu.CompilerParams(vmem_limit_bytes=VMEM_LIMIT),
        name="mixers",
    )(*acts, v_a, bv, sz, p_last, s0, ln_w, ln_b, ones_a, q, k, v_b, gates, gob, c0, n0, m0, gate_b, gn_w, ones_b)


def _out_body(oa_ref, ob_ref, x_ref, sc_ref, sh_ref, gate_ref, g_ref, wga_ref, wgb_ref, wua_ref, wub_ref, wo_ref,
              gf_ref, y_ref, *, nb, final_norm):
    m = x_ref.shape[0]
    rows = lambda ref: _rows_of_seq(ref[...], nb, m // nb)
    x = x_ref[...]
    h = x * lax.rsqrt(jnp.mean(x * x, axis=-1, keepdims=True) + NORM_EPS) * g_ref[...]
    h = (h * (1.0 + rows(sc_ref)) + rows(sh_ref)).astype(BF16)
    gl_a = jnp.dot(h, wga_ref[...], preferred_element_type=F32)
    yield
    gl_b = jnp.dot(h, wgb_ref[...], preferred_element_type=F32)
    yield
    ua = jnp.dot(oa_ref[...], wua_ref[...], preferred_element_type=F32)
    ub = jnp.dot(ob_ref[...], wub_ref[...], preferred_element_type=F32)
    yield
    merged = _sigmoid(gl_a) * ua + _sigmoid(gl_b) * ub
    mo = jnp.dot(merged.astype(BF16), wo_ref[...], preferred_element_type=F32)
    yield
    xn = x + rows(gate_ref) * mo
    if final_norm:
        xn = xn * lax.rsqrt(jnp.mean(xn * xn, axis=-1, keepdims=True) + NORM_EPS) * gf_ref[...]
    y_ref[...] = xn


def _out_kernel(*refs, n_split, **static):
    if n_split == 1:
        _run(_out_body(*refs, **static))
        return
    oa_ref, ob_ref, x_ref = refs[:3]
    y_ref = refs[-1]
    tp = x_ref.shape[0] // n_split
    plan = []
    for h in range(n_split):
        rows = pl.ds(h * tp, tp)
        body = _out_body(oa_ref.at[rows], ob_ref.at[rows], x_ref.at[rows], *refs[3:-1], y_ref.at[rows], **static)
        plan.append((body, 4, 0.5 * h, 1.0))
    _run(_interleave(plan))


def _out(out_a, out_b, x, scale, shift, gate, g_norm, w_gl_a, w_gl_b, w_up_a, w_up_b, w_out, g_final,
         b, nb, tt, final_norm):
    n_tok, d = x.shape
    t = n_tok // b
    n_split = OUT_SPLIT if nb == 1 and t % (OUT_SPLIT * tt) == 0 else 1
    nj = t // (n_split * tt)
    m = nb * n_split * tt
    tok = lambda i, j: (i * nj + j, 0)
    const2 = lambda i, j: (0, 0)
    per_b = lambda i, j: (i, 0, 0)
    wspec = lambda w: pl.BlockSpec(w.shape, const2, pipeline_mode=pl.Buffered(1))
    in_specs = [pl.BlockSpec((m, out_a.shape[-1]), tok),
                pl.BlockSpec((m, out_b.shape[-1]), tok),
                pl.BlockSpec((m, d), tok),
                pl.BlockSpec((nb, 1, d), per_b),
                pl.BlockSpec((nb, 1, d), per_b),
                pl.BlockSpec((nb, 1, d), per_b),
                pl.BlockSpec((1, d), const2),
                wspec(w_gl_a), wspec(w_gl_b), wspec(w_up_a), wspec(w_up_b), wspec(w_out),
                pl.BlockSpec((1, d), const2)]
    return pl.pallas_call(
        functools.partial(_out_kernel, nb=nb, final_norm=final_norm, n_split=n_split),
        grid=(b // nb, nj),
        in_specs=in_specs,
        out_specs=pl.BlockSpec((m, d), tok),
        out_shape=jax.ShapeDtypeStruct((n_tok, d), F32),
        compiler_params=pltpu.CompilerParams(vmem_limit_bytes=VMEM_LIMIT),
        name="merge_out",
    )(out_a, out_b, x, scale, shift, gate, g_norm.reshape(1, d), w_gl_a, w_gl_b, w_up_a, w_up_b, w_out,
      g_final.reshape(1, d))


def _token_tiling(b, t, target):
    if t >= target:
        return 1, math.gcd(t, target)
    return math.gcd(b, target // t), t


def _layer(x, mod, states, lp, g_final, final_norm):
    b, t, d = x.shape
    shift0, s0, conv0, c0, n0, m0 = states
    n_hb = c0.shape[1]
    cw = lp["conv_w"].shape[0]
    row = lambda a: a.reshape(1, -1)

    ada_shift, ada_scale, ada_gate = (mod[:, None, i * d:(i + 1) * d] for i in range(3))
    nb, tt = _token_tiling(b, t, PROJ_ROWS)
    L = math.gcd(t, ROWS)
    act_dtype = BF16 if L % 16 == 0 else F32
    conv0p = jnp.pad(conv0, ((0, 0), (SUBLANES - (cw - 1), 0), (0, 0)))
    n0p = jnp.broadcast_to(n0[:, :, None, :], (b, n_hb, SUBLANES, HEAD_DIM))
    m0p = jnp.pad(m0, ((0, 0), (0, LANES - n_hb)))[:, None, :]
    gate_b = jnp.concatenate([lp["b_i"], lp["b_f"], jnp.zeros((LANES - 2 * n_hb,), F32)])[None, :]
    n_seq, t_step = _token_tiling(b, t, MIX_ROWS if t >= MIX_ROWS else MLSTM_ROWS)
    (kq, rq, bk, kkk, bk2, kk2, v_a, bv, sz, p_last, shift_t,
     q, k, v_b, gates, gob, conv_tail) = _inproj(
        x, ada_scale, ada_shift, lp["g_norm"], shift0[:, None, :], conv0p, lp, nb, tt, L, act_dtype)
    out_a, s_t, out_b, c_t, n_t, m_t = _mixers(
        (kq, rq, bk, kkk, bk2, kk2), v_a, bv, sz, p_last, s0, row(lp["ln_w"]), row(lp["ln_b"]),
        q, k, v_b, gates, gob, c0, n0p, m0p, gate_b, row(lp["gn_w"]), n_seq, t_step, L)

    y = _out(out_a, out_b, x.reshape(b * t, d), ada_scale, ada_shift, ada_gate, lp["g_norm"],
             lp["w_gl_a"], lp["w_gl_b"], lp["w_up_a"], lp["w_up_b"], lp["w_out"],
             g_final, b, nb, tt, final_norm).reshape(b, t, d)
    new_states = (shift_t[:, 0], s_t, conv_tail[:, SUBLANES - (cw - 1):], c_t, n_t[:, :, 0, :], m_t[:, 0, :n_hb])
    return y, new_states


def _trunk(x, mods, states, layers, g_final):
    depth = len(layers)
    new = [[] for _ in states]
    for l in range(depth):
        st = tuple(s[l] for s in states)
        x, st_new = _layer(x, mods[l], st, layers[l], g_final, final_norm=(l == depth - 1))
        for lst, s in zip(new, st_new):
            lst.append(s.astype(x.dtype))
    return x, tuple(jnp.stack(lst) for lst in new)


def kernel(x_prompt, x_sample, c_prompt, c_sample, state_rwkv_shift, state_rwkv_S, state_mlstm_conv, state_mlstm_C, state_mlstm_n, state_mlstm_m, g_norm, w_ada, b_ada, w_in, mu_shift, w_decay2, w0, w_iclr2, a0, k_k, k_a, r_k, ln_w, ln_b, conv_w, conv_b, b_i, b_f, gn_w, w_up_a, w_up_b, w_out, g_final):
    depth = g_norm.shape[0]
    bp, bs = x_prompt.shape[0], x_sample.shape[0]
    d = x_prompt.shape[-1]
    da, db = w_up_a.shape[1], w_up_b.shape[1]
    n_ha, n_hb = da // HEAD_DIM, db // HEAD_DIM
    sw = mu_shift.shape[-1]
    cc = conv_w.shape[-1]
    cw = conv_w.shape[1]
    dt = x_prompt.dtype

    sizes = (sw, da, cc, db, db, n_hb, n_hb, db, d, d)
    offs = [0]
    for s in sizes:
        offs.append(offs[-1] + s)
    seg = lambda w, i: w[:, offs[i]:offs[i + 1]]

    c_all = jnp.concatenate([c_prompt, c_sample], axis=0)
    n_c = c_all.shape[0]
    c_pad = jnp.pad(c_all, ((0, (-n_c) % SUBLANES), (0, 0)))

    layers, mods_p, mods_s = [], [], []
    for l in range(depth):
        w = w_in[l]
        wg = jnp.concatenate([seg(w, 5), seg(w, 6)], axis=1)
        wg = jnp.pad(wg, ((0, 0), (0, LANES - wg.shape[1])))
        w_segs = tuple(x.astype(BF16) for x in
                       (seg(w, 0), seg(w, 1), seg(w, 2), seg(w, 3), seg(w, 4), seg(w, 7), wg))
        layers.append(dict(
            g_norm=g_norm[l], w_segs=w_segs, w_gl_a=seg(w, 8).astype(BF16), w_gl_b=seg(w, 9).astype(BF16),
            mu_shift=mu_shift[l], w_decay2=w_decay2[l].astype(BF16),
            w0=w0[l], w_iclr2=w_iclr2[l].astype(BF16), a0=a0[l], k_k=k_k[l], k_a=k_a[l], r_k=r_k[l],
            ln_w=ln_w[l], ln_b=ln_b[l], conv_w=conv_w[l], conv_b=conv_b[l], b_i=b_i[l], b_f=b_f[l],
            gn_w=gn_w[l], w_up_a=w_up_a[l].astype(BF16), w_up_b=w_up_b[l].astype(BF16),
            w_out=w_out[l].astype(BF16)))
        mod = _ada(c_pad, w_ada[l], b_ada[l])
        mods_p.append(mod[:bp])
        mods_s.append(mod[bp:bp + bs])

    prompt_states = (
        jnp.zeros((depth, bp, sw), dt),
        jnp.zeros((depth, bp, n_ha, HEAD_DIM, HEAD_DIM), dt),
        jnp.zeros((depth, bp, cw - 1, cc), dt),
        jnp.zeros((depth, bp, n_hb, HEAD_DIM, HEAD_DIM), dt),
        jnp.zeros((depth, bp, n_hb, HEAD_DIM), dt),
        jnp.zeros((depth, bp, n_hb), dt),
    )
    sample_states = (state_rwkv_shift, state_rwkv_S, state_mlstm_conv,
                     state_mlstm_C, state_mlstm_n, state_mlstm_m)
    y_p, st_p = _trunk(x_prompt, mods_p, prompt_states, layers, g_final)
    y_s, st_s = _trunk(x_sample, mods_s, sample_states, layers, g_final)
    return (y_p, y_s) + st_p + st_s
```

```python
import functools
import math

import jax
import jax.numpy as jnp
from jax import lax
from jax.experimental import pallas as pl
from jax.experimental.pallas import tpu as pltpu

F32 = jnp.float32
BF16 = jnp.bfloat16

HEAD_DIM = 64
NORM_EPS = 1e-6
RWKV_GN_EPS = 64e-5
MLSTM_GN_EPS = 1e-6
ROWS = 64
MLSTM_ROWS = 128
MIX_ROWS = 1024
PROJ_ROWS = 256
INPROJ_SPLIT = 2
OUT_SPLIT = 4
SUBLANES = 8
LANES = 128
TRI_BASE = 4
VMEM_LIMIT = 60 * 1024 * 1024

LOG2_E = 1.4426950408889634
NT_DIMS = (((1,), (1,)), ((), ()))
TN_DIMS = (((0,), (0,)), ((), ()))


def _mm(a, b):
    return jnp.dot(a.astype(BF16), b.astype(BF16), preferred_element_type=F32)


def _mm_nt(a, b):
    return lax.dot_general(a.astype(BF16), b.astype(BF16), NT_DIMS, preferred_element_type=F32)


def _mm_tn(a, b):
    return lax.dot_general(a.astype(BF16), b.astype(BF16), TN_DIMS, preferred_element_type=F32)


def _sigmoid(x):
    return 1.0 / (1.0 + jnp.exp2(x * -LOG2_E))


def _silu(x):
    return x * _sigmoid(x)


def _softplus(x):
    return jnp.maximum(x, 0.0) + jnp.log(1.0 + jnp.exp(-jnp.abs(x)))


def _cumsum_time(x, tri_bf16, terms=3):
    dot = functools.partial(jnp.dot, preferred_element_type=F32)
    rest = x
    total = None
    for i in range(terms):
        part = rest.astype(BF16)
        if i + 1 < terms:
            rest = rest - part.astype(F32)
        total = dot(tri_bf16, part) if total is None else total + dot(tri_bf16, part)
    return total


def _shr(x, n):
    return lax.shift_right_logical(x, jnp.full(x.shape, n, x.dtype))


def _rows_of_seq(x, n_seq, L):
    if n_seq == 1:
        return x[0]
    return jnp.broadcast_to(x, (n_seq, L, x.shape[-1])).reshape(n_seq * L, x.shape[-1])


def _last_rows(x, n_seq):
    m, n = x.shape
    L = m // n_seq
    return x.reshape(n_seq, L, n)[:, L - 1:L, :]


def _head_sums(x):
    m, n = x.shape
    parts = [jnp.broadcast_to(jnp.sum(x[:, o:o + HEAD_DIM], axis=-1, keepdims=True), (m, HEAD_DIM))
             for o in range(0, n, HEAD_DIM)]
    return jnp.concatenate(parts, axis=-1)


def _head_sums_mxu(x, ones_ref):
    hi = x.astype(BF16)
    lo = (x - hi.astype(F32)).astype(BF16)
    s = jnp.dot(jnp.concatenate([hi, lo], axis=0), ones_ref[...], preferred_element_type=F32)
    return s[:x.shape[0]] + s[x.shape[0]:]


def _cummax_time(x, L, pos):
    y = x
    sh = 1
    while sh < L:
        y = jnp.where(pos >= sh, jnp.maximum(y, pltpu.roll(y, sh, axis=0)), y)
        sh *= 2
    return y


def _tri_inverse_all(mats, row, col, L, mm=_mm):
    bs = min(TRI_BASE, L)
    sh = int(math.log2(bs))
    same = _shr(row, sh) == _shr(col, sh)
    eye = (row == col).astype(F32)
    rows = mats[0].shape[0]
    ns = [jnp.where(same, -a, 0.0) for a in mats]
    ts = [eye + n for n in ns]
    if bs > 2:
        ps = [mm(n, n) for n in ns]
        yield
        k = 2
        while 2 * k < bs:
            both = [mm(jnp.concatenate([t, p], axis=0), p) for t, p in zip(ts, ps)]
            ts = [t + x[:rows] for t, x in zip(ts, both)]
            ps = [x[rows:] for x in both]
            yield
            k *= 2
        ts = [t + mm(t, p) for t, p in zip(ts, ps)]
        yield
    size = bs
    while size < L:
        sh = int(math.log2(size))
        lower_left = (_shr(row, sh + 1) == _shr(col, sh + 1)) & (_shr(row, sh) != _shr(col, sh))
        offs = [jnp.where(lower_left, a, 0.0) for a in mats]
        tmp = [mm(t, o) for t, o in zip(ts, offs)]
        yield
        ts = [t - mm(x, t) for t, x in zip(ts, tmp)]
        yield
        size *= 2
    return ts


def _interleave(plan):
    live = [[g, n, s, w, 0] for g, n, s, w in plan]
    while live:
        item = min(live, key=lambda it: it[2] + it[3] * it[4] / it[1])
        try:
            next(item[0])
            item[4] += 1
            yield
        except StopIteration:
            live.remove(item)


def _chain(*gens):
    for g in gens:
        yield from g


def _run(gen):
    for _ in gen:
        pass


def _when(pred, fn):
    if pred is not None:
        pl.when(pred)(fn)


def _ada_kernel(c_ref, w_ref, b_ref, o_ref):
    o_ref[...] = _mm(_silu(c_ref[...]), w_ref[...]) + b_ref[...]


def _ada(c, w_ada, b_ada):
    n, d = c.shape
    n3 = w_ada.shape[1]
    tn = d
    return pl.pallas_call(
        _ada_kernel,
        grid=(n3 // tn,),
        in_specs=[pl.BlockSpec((n, d), lambda j: (0, 0)),
                  pl.BlockSpec((d, tn), lambda j: (0, j)),
                  pl.BlockSpec((1, tn), lambda j: (0, j))],
        out_specs=pl.BlockSpec((n, tn), lambda j: (0, j)),
        out_shape=jax.ShapeDtypeStruct((n, n3), F32),
        name="ada",
    )(c, w_ada, b_ada.reshape(1, n3))


def _batch_major_kernel(x_ref, o_ref, t_scr):
    half = HEAD_DIM // 2
    for v in range(half):
        lo = x_ref[0, v * HEAD_DIM:(v + 1) * HEAD_DIM, :]
        hi = x_ref[0, (v + half) * HEAD_DIM:(v + half + 1) * HEAD_DIM, :]
        t_scr[v * LANES:(v + 1) * LANES, :] = jnp.concatenate([lo, hi], axis=0).T
    for b in range(LANES):
        rows = t_scr[pl.ds(b, half, stride=LANES), :]
        o_ref[b, 0, :half, :] = rows[:, :HEAD_DIM]
        o_ref[b, 0, half:, :] = rows[:, HEAD_DIM:]


def _batch_major(state):
    b, h, dv, dk = state.shape
    if b != LANES or dv != HEAD_DIM or dk != HEAD_DIM or 2 * HEAD_DIM != LANES or state.dtype != F32:
        return state
    view = jnp.transpose(state, (1, 2, 3, 0)).reshape(h, dv * dk, b)
    return pl.pallas_call(
        _batch_major_kernel,
        grid=(h,),
        in_specs=[pl.BlockSpec((1, dv * dk, b), lambda i: (i, 0, 0))],
        out_specs=pl.BlockSpec((b, 1, dv, dk), lambda i: (0, i, 0, 0)),
        out_shape=jax.ShapeDtypeStruct(state.shape, state.dtype),
        scratch_shapes=[pltpu.VMEM((dv // 2 * LANES, LANES), F32)],
        name="batch_major",
    )(view)


PROJ_COLS = 256


def _inproj_body(x_ref, sc_ref, sh_ref, g_ref, shift0_ref, conv0_ref,
                 w_shift, w_za, w_qk, w_vb, w_ob, w_zb, w_gt,
                 mu_ref, w0_ref, a0_ref, kk_ref, ka_ref, rk_ref, wd2_ref, wi2_ref, cw_ref, cb_ref,
                 kq_o, rq_o, bk_o, kkk_o, bk2_o, kk2_o, va_o, bv_o, sz_o, plast_o, shst_o,
                 q_o, k_o, vb_o, gt_o, gob_o, cst_o,
                 prev_scr, xbuf, *, chunk, lora_w, conv_w, first):
    nb, tt, d = x_ref.shape
    M = nb * tt
    L = chunk
    n_chunks = M // L
    da = va_o.shape[-1]
    db = vb_o.shape[-1]
    cc = xbuf.shape[-1]
    pad = SUBLANES

    def _init():
        prev_scr[...] = shift0_ref[...]
        xbuf[:, 0:pad, :] = conv0_ref[...]
    _when(first, _init)

    x = x_ref[...]
    y = x * lax.rsqrt(jnp.mean(x * x, axis=-1, keepdims=True) + NORM_EPS) * g_ref[...]
    h = (y * (1.0 + sc_ref[...]) + sh_ref[...]).reshape(M, d).astype(BF16)
    def proj(w_ref):
        n = w_ref.shape[1]
        parts = []
        for o in range(0, n, PROJ_COLS):
            parts.append(jnp.dot(h, w_ref[:, o:min(o + PROJ_COLS, n)], preferred_element_type=F32))
            yield
        return parts[0] if len(parts) == 1 else jnp.concatenate(parts, axis=-1)

    p = yield from proj(w_shift)
    row_w = lax.broadcasted_iota(jnp.int32, p.shape, 0)
    seq_start = (row_w & (tt - 1)) == 0
    prev = jnp.where(seq_start, _rows_of_seq(prev_scr[...], nb, tt), pltpu.roll(p, 1, axis=0))
    last = _last_rows(p, nb)
    prev_scr[...] = last
    shst_o[...] = last
    ps = p + mu_ref[...] * (prev - p)
    r = ps[:, 0:da]
    k = ps[:, da:2 * da]
    v = ps[:, 2 * da:3 * da]
    wl = ps[:, 3 * da:3 * da + lora_w]
    al = ps[:, 3 * da + lora_w:]

    qk_pre = yield from proj(w_qk)
    z_a = yield from proj(w_za)
    w = w0_ref[...] + _mm(jnp.tanh(wl), wd2_ref[...])
    lw = (-math.exp(-0.5) * LOG2_E) * _sigmoid(w)
    a = _sigmoid(a0_ref[...] + _mm(al, wi2_ref[...]))
    yield
    v_b = yield from proj(w_vb)
    g_t = yield from proj(w_gt)
    o_b = yield from proj(w_ob)
    z_b = yield from proj(w_zb)
    kkr = k * kk_ref[...]
    kk = kkr * lax.rsqrt(jnp.maximum(_head_sums(kkr * kkr), 1e-24))
    k2 = k * (1.0 + (a - 1.0) * ka_ref[...])
    b = kk * a
    va_o[...] = v
    bv_o[...] = _head_sums(r * k2 * rk_ref[...]) * v
    sz_o[...] = _silu(z_a)

    lg = int(math.log2(L))
    row_m = lax.broadcasted_iota(jnp.int32, (M, M), 0)
    col_m = lax.broadcasted_iota(jnp.int32, (M, M), 1)
    tri = ((row_m >= col_m) & (_shr(row_m, lg) == _shr(col_m, lg))).astype(BF16)
    c = _cumsum_time(lw, tri, terms=2)
    yield
    p_last = jnp.exp2(_last_rows(c, n_chunks))
    plast_o[...] = p_last
    p_last_rows = _rows_of_seq(p_last, n_chunks, L)
    e_nc = jnp.exp2(-c)
    bk = b * e_nc
    kkk = k2 * e_nc
    kq_o[...] = (kk * jnp.exp2(c - lw)).astype(kq_o.dtype)
    rq_o[...] = (r * jnp.exp2(c)).astype(rq_o.dtype)
    bk_o[...] = bk.astype(bk_o.dtype)
    kkk_o[...] = kkk.astype(kkk_o.dtype)
    bk2_o[...] = (bk * p_last_rows).astype(bk2_o.dtype)
    kk2_o[...] = (kkk * p_last_rows).astype(kk2_o.dtype)

    xbuf[:, pad:pad + tt, :] = qk_pre.reshape(nb, tt, cc)
    xe = xbuf[...].reshape(nb * (tt + pad), cc)
    conv = cb_ref[...] + qk_pre * cw_ref[conv_w - 1:conv_w, :]
    for back in range(1, conv_w):
        shifted = pltpu.roll(xe, back, axis=0).reshape(nb, tt + pad, cc)[:, pad:, :].reshape(M, cc)
        conv = conv + shifted * cw_ref[conv_w - 1 - back:conv_w - back, :]
    tail = xbuf[:, tt:tt + pad, :]
    xbuf[:, 0:pad, :] = tail
    cst_o[...] = tail
    yield
    qk = _silu(conv)
    q_o[...] = qk[:, 0:db].astype(q_o.dtype)
    k_o[...] = (qk[:, db:] * (1.0 / math.sqrt(HEAD_DIM))).astype(k_o.dtype)
    vb_o[...] = v_b
    gt_o[...] = g_t
    gob_o[...] = _sigmoid(o_b) * _silu(z_b)


N_INPROJ_IN = 23
INPROJ_ROW_OUTS = (0, 1, 2, 3, 4, 5, 6, 7, 8, 11, 12, 13, 14, 15)
INPROJ_CHUNK_OUT = 9
INPROJ_STAGES = 28


def _inproj_kernel(*refs, n_split, chunk, **static):
    ins, outs, scratch = refs[:N_INPROJ_IN], refs[N_INPROJ_IN:-2], refs[-2:]
    first = pl.program_id(1) == 0
    if n_split == 1:
        _run(_inproj_body(*refs, chunk=chunk, first=first, **static))
        return
    tp = ins[0].shape[1] // n_split
    plan = []
    for h in range(n_split):
        rows = pl.ds(h * tp, tp)
        t_ins = [ins[0].at[:, rows]] + list(ins[1:])
        t_outs = [o.at[rows] if i in INPROJ_ROW_OUTS else o for i, o in enumerate(outs)]
        t_outs[INPROJ_CHUNK_OUT] = outs[INPROJ_CHUNK_OUT].at[pl.ds(h * (tp // chunk), tp // chunk)]
        body = _inproj_body(*t_ins, *t_outs, *scratch, chunk=chunk,
                            first=first if h == 0 else None, **static)
        plan.append((body, INPROJ_STAGES, 0.5 * h, 1.0))
    _run(_interleave(plan))


def _inproj(x, scale, shift, g_norm, shift0, conv0p, lp, nb, tt, chunk, act_dtype):
    b, t, d = x.shape
    n_tok = b * t
    n_split = INPROJ_SPLIT if nb == 1 and t % (INPROJ_SPLIT * tt) == 0 else 1
    tb = n_split * tt
    m = nb * tb
    nj = t // tb
    sw = shift0.shape[-1]
    cc = conv0p.shape[-1]
    da, db = lp["w_up_a"].shape[0], lp["w_up_b"].shape[0]
    cw = lp["conv_w"].shape[0]
    row = lambda a: a.reshape(1, -1)
    rows = [row(lp[n]) for n in ("mu_shift", "w0", "a0", "k_k", "k_a", "r_k")]
    smalls = rows + [lp["w_decay2"], lp["w_iclr2"], lp["conv_w"], row(lp["conv_b"])]
    const = lambda i, j: (0, 0)
    tok = lambda i, j: (i * nj + j, 0)
    per_b = lambda i, j: (i, 0, 0)
    in_specs = [pl.BlockSpec((nb, tb, d), lambda i, j: (i, j, 0)),
                pl.BlockSpec((nb, 1, d), per_b),
                pl.BlockSpec((nb, 1, d), per_b),
                pl.BlockSpec((1, d), const),
                pl.BlockSpec((nb, 1, sw), per_b),
                pl.BlockSpec((nb, SUBLANES, cc), per_b)]
    in_specs += [pl.BlockSpec(w.shape, const, pipeline_mode=pl.Buffered(1)) for w in lp["w_segs"]]
    in_specs += [pl.BlockSpec(a.shape, const) for a in smalls]
    tok_out = lambda n, dt: (pl.BlockSpec((m, n), tok), jax.ShapeDtypeStruct((n_tok, n), dt))
    outs = [tok_out(da, act_dtype)] * 6 + [tok_out(da, F32)] * 3
    outs += [(pl.BlockSpec((m // chunk, 1, da), lambda i, j: (i * nj + j, 0, 0)),
              jax.ShapeDtypeStruct((n_tok // chunk, 1, da), F32)),
             (pl.BlockSpec((nb, 1, sw), per_b), jax.ShapeDtypeStruct((b, 1, sw), F32))]
    outs += [tok_out(db, act_dtype)] * 2 + [tok_out(db, F32), tok_out(LANES, F32), tok_out(db, F32)]
    outs += [(pl.BlockSpec((nb, SUBLANES, cc), per_b), jax.ShapeDtypeStruct((b, SUBLANES, cc), F32))]
    return pl.pallas_call(
        functools.partial(_inproj_kernel, chunk=chunk, lora_w=lp["w_decay2"].shape[0], conv_w=cw,
                          n_split=n_split),
        grid=(b // nb, nj),
        in_specs=in_specs,
        out_specs=[o[0] for o in outs], out_shape=[o[1] for o in outs],
        scratch_shapes=[pltpu.VMEM((nb, 1, sw), F32),
                        pltpu.VMEM((nb, tt + SUBLANES, cc), F32)],
        compiler_params=pltpu.CompilerParams(vmem_limit_bytes=VMEM_LIMIT),
        name="inproj",
    )(x, scale, shift, g_norm.reshape(1, d), shift0, conv0p, *lp["w_segs"], *smalls)


def _pair_blockdiag(y, left):
    return jnp.concatenate([jnp.where(left, y, 0.0), jnp.where(left, 0.0, y)], axis=0)


def _rwkv_body(kq_ref, rq_ref, bk_ref, kkk_ref, bk2_ref, kk2_ref, v_ref, bv_ref, sz_ref, plast_ref,
               s0_ref, lnw_ref, lnb_ref, ones_ref, oa_ref, st_ref, s_scr, *, n_seq, chunk, n_heads,
               first, last, after, done):
    M = kq_ref.shape[0]
    L = chunk
    t_step = M // n_seq
    RU = min(M, ROWS)
    PW = 2 * HEAD_DIM
    n_units = M // RU
    n_hp = n_heads // 2
    units = range(n_units)
    hps = range(n_hp)
    pairs = [(u, g) for u in units for g in hps]
    zero_blk = jnp.zeros((n_seq, HEAD_DIM, HEAD_DIM), F32)

    def _load_state():
        for g in hps:
            top = jnp.concatenate([s0_ref[:, 2 * g], zero_blk], axis=-1)
            bot = jnp.concatenate([zero_blk, s0_ref[:, 2 * g + 1]], axis=-1)
            s_scr[:, g] = jnp.concatenate([top, bot], axis=-2)
    _when(first, _load_state)

    lg = int(math.log2(L))
    row = lax.broadcasted_iota(jnp.int32, (RU, 2 * RU), 0)
    lane = lax.broadcasted_iota(jnp.int32, (RU, 2 * RU), 1)
    col = lane & (RU - 1)
    same = _shr(row, lg) == _shr(col, lg)
    incl = (row >= col) & same
    strict = (row > col) & same
    left = lax.broadcasted_iota(jnp.int32, (RU, PW), 1) < HEAD_DIM
    bd_mask = (lax.broadcasted_iota(jnp.int32, (PW, PW), 0) < HEAD_DIM) == (
        lax.broadcasted_iota(jnp.int32, (PW, PW), 1) < HEAD_DIM)
    mmp = lambda x, y: _mm(x, _pair_blockdiag(y, left))

    p_last = plast_ref[...]
    pls = [slice(g * PW, (g + 1) * PW) for g in hps]
    rus = [slice(u * RU, (u + 1) * RU) for u in units]
    blk = lambda ref, q: ref[rus[q[0]], pls[q[1]]]
    kq = {q: blk(kq_ref, q) for q in pairs}
    rq = {q: blk(rq_ref, q) for q in pairs}
    bk2 = {q: blk(bk2_ref, q) for q in pairs}
    kk2 = {q: blk(kk2_ref, q) for q in pairs}
    vs = {q: blk(v_ref, q) for q in pairs}
    qr = {q: jnp.concatenate([kq[q], rq[q]], axis=0).astype(BF16) for q in pairs}
    bkk = {q: jnp.concatenate([_pair_blockdiag(blk(bk_ref, q).astype(F32), left),
                               _pair_blockdiag(blk(kkk_ref, q).astype(F32), left)], axis=0).astype(BF16)
           for q in pairs}
    yield

    gs = {q: _mm_nt(qr[q], bkk[q]) for q in pairs}
    yield
    a_ab = {q: jnp.where(strict, gs[q][:RU, :2 * RU], 0.0) for q in pairs}
    a_ak = {q: jnp.where(strict, gs[q][:RU, 2 * RU:], 0.0) for q in pairs}
    m_rb = {q: jnp.where(incl, gs[q][RU:, :2 * RU], 0.0) for q in pairs}
    m_rk = {q: jnp.where(incl, gs[q][RU:, 2 * RU:], 0.0) for q in pairs}
    yield
    t_inv = dict(zip(pairs, (yield from _tri_inverse_all([a_ab[q] for q in pairs], row, col, L, mmp))))
    akv = {q: mmp(a_ak[q], vs[q]) for q in pairs}
    yield

    assert after is None or after, "the body before this one must have stored its state by now"
    state = {}
    ys = {}
    for u in units:
        if L == RU:
            b = (u * RU) // t_step
            s0 = [state[b, g] if (b, g) in state else s_scr[b, g] for g in hps]
            ws = [_mm_nt(qr[u, g], s0[g]) for g in hps]
            w1 = [x[:RU] for x in ws]
            wr = [x[RU:] for x in ws]
        else:
            n_in = RU // L
            s0s = [[s_scr[u * n_in + i, g] for g in hps] for i in range(n_in)]
            w1, wr = [], []
            for g in hps:
                parts = [_mm_nt(jnp.concatenate([kq[u, g][i * L:(i + 1) * L], rq[u, g][i * L:(i + 1) * L]], axis=0),
                                s0s[i][g]) for i in range(n_in)]
                w1.append(jnp.concatenate([x[:L] for x in parts], axis=0))
                wr.append(jnp.concatenate([x[L:] for x in parts], axis=0))
        yield
        us = [-mmp(t_inv[u, g], w1[g] + akv[u, g]) for g in hps]
        yield
        for g in hps:
            rhs = jnp.concatenate([_pair_blockdiag(us[g], left), _pair_blockdiag(vs[u, g].astype(F32), left)], axis=0)
            ys[u, g] = wr[g] + _mm(jnp.concatenate([m_rb[u, g], m_rk[u, g]], axis=1), rhs)
        yield
        if L == RU:
            for g in hps:
                uv = jnp.concatenate([us[g], vs[u, g].astype(F32)], axis=0)
                bkk2 = jnp.concatenate([bk2[u, g], kk2[u, g]], axis=0)
                state[b, g] = s0[g] * p_last[u][:, pls[g]] + jnp.where(bd_mask, _mm_tn(uv, bkk2), 0.0)
        else:
            for g in hps:
                for i in range(n_in):
                    rows = slice(i * L, (i + 1) * L)
                    uv_i = jnp.concatenate([us[g][rows], vs[u, g][rows]], axis=0)
                    bkk2_i = jnp.concatenate([bk2[u, g][rows], kk2[u, g][rows]], axis=0)
                    state[u * n_in + i, g] = (s0s[i][g] * p_last[u * n_in + i][:, pls[g]]
                                              + jnp.where(bd_mask, _mm_tn(uv_i, bkk2_i), 0.0))
        yield
    for (b, g), s_new in state.items():
        s_scr[b, g] = s_new
    done.append(True)

    rows_out = [jnp.concatenate([ys[u, g] for g in hps], axis=-1) for u in units]
    y = rows_out[0] if n_units == 1 else jnp.concatenate(rows_out, axis=0)
    yc = y - _head_sums_mxu(y, ones_ref) * (1.0 / HEAD_DIM)
    yield
    var = _head_sums_mxu(yc * yc, ones_ref) * (1.0 / HEAD_DIM)
    yn = yc * lax.rsqrt(var + RWKV_GN_EPS)
    oa_ref[...] = ((yn * lnw_ref[...] + lnb_ref[...] + bv_ref[...]) * sz_ref[...]).astype(oa_ref.dtype)

    def _store_state():
        for g in hps:
            st_ref[:, 2 * g] = s_scr[:, g, 0:HEAD_DIM, 0:HEAD_DIM]
            st_ref[:, 2 * g + 1] = s_scr[:, g, HEAD_DIM:, HEAD_DIM:]
    _when(last, _store_state)


def _mlstm_body(q_ref, k_ref, v_ref, g_ref, gob_ref, c0_ref, n0_ref, m0_ref, gb_ref, gnw_ref, ones_ref,
                ob_ref, ct_ref, nt_ref, mt_ref, cn_scr, m_scr, *, n_seq, n_heads, first, last):
    M = q_ref.shape[0]
    L = M // n_seq
    heads = range(n_heads)
    seqs = range(n_seq)

    def _load_state():
        cn_scr[:, :, 0:HEAD_DIM, :] = c0_ref[...]
        cn_scr[:, :, HEAD_DIM:, :] = n0_ref[...]
        m_scr[...] = m0_ref[...]
    _when(first, _load_state)

    row = lax.broadcasted_iota(jnp.int32, (M, M), 0)
    col = lax.broadcasted_iota(jnp.int32, (M, M), 1)
    same_seq = _shr(row, int(math.log2(L))) == _shr(col, int(math.log2(L)))
    incl = (row >= col) & same_seq

    g = g_ref[...] + gb_ref[...]
    bcum = pltpu.roll(_cumsum_time(-_softplus(-g), incl.astype(BF16)), LANES - n_heads, axis=1)
    m_prev = m_scr[...]
    m_prev_rows = _rows_of_seq(m_prev, n_seq, L)
    x_all = g - bcum
    pos = lax.broadcasted_iota(jnp.int32, (M, LANES), 0) & (L - 1)
    m_all = bcum + jnp.maximum(_cummax_time(x_all, L, pos), m_prev_rows)
    bm_all = bcum - m_all
    w_in_all = jnp.exp(bcum + m_prev_rows - m_all)
    e_negm_all = jnp.exp(-m_all)
    m_new = _last_rows(m_all, n_seq)
    b_last = _last_rows(bcum, n_seq)
    ws_all = jnp.exp(_rows_of_seq(b_last - m_new, n_seq, L) + x_all)
    dec_all = jnp.exp(b_last + m_prev - m_new)
    m_scr[...] = m_new
    x_t = jnp.transpose(x_all)
    yield

    ones = jnp.ones((M, SUBLANES), F32)
    sls = [slice(h * HEAD_DIM, (h + 1) * HEAD_DIM) for h in heads]
    qs = [q_ref[:, sl] for sl in sls]
    ks = [k_ref[:, sl] for sl in sls]
    v1 = [jnp.concatenate([v_ref[:, sl], ones], axis=-1) for sl in sls]
    cn = [[cn_scr[b, h] for h in heads] for b in seqs]

    qk_t = [_mm_nt(qs[h], ks[h]) for h in heads]
    if n_seq == 1:
        qc = [_mm_nt(qs[h], cn[0][h]) for h in heads]
    else:
        qc = [jnp.concatenate([_mm_nt(qs[h][b * L:(b + 1) * L], cn[b][h]) for b in seqs], axis=0) for h in heads]
    yield
    w_ts = [jnp.where(incl, jnp.exp(bm_all[:, h:h + 1] + x_t[h:h + 1, :]), 0.0) for h in heads]
    yield
    s = [qk_t[h] * w_ts[h] for h in heads]
    numden = [_mm(s[h], v1[h]) + w_in_all[:, h:h + 1] * qc[h] for h in heads]
    yield
    den = [jnp.maximum(jnp.abs(numden[h][:, HEAD_DIM:HEAD_DIM + 1]), e_negm_all[:, h:h + 1]) for h in heads]
    hh = [numden[h][:, 0:HEAD_DIM] / den[h] for h in heads]
    hh_all = jnp.concatenate(hh, axis=-1)
    ssq = _head_sums_mxu(hh_all * hh_all, ones_ref) * (1.0 / HEAD_DIM)

    yield
    v1w = [v1[h] * ws_all[:, h:h + 1] for h in heads]
    for h in heads:
        for b in seqs:
            rows = slice(b * L, (b + 1) * L)
            cn_scr[b, h] = dec_all[b][:, h:h + 1] * cn[b][h] + _mm_tn(v1w[h][rows], ks[h][rows])
    yield

    hb = hh_all * lax.rsqrt(ssq + MLSTM_GN_EPS)
    ob_ref[...] = (hb * gnw_ref[...] * gob_ref[...]).astype(ob_ref.dtype)

    def _store_state():
        ct_ref[...] = cn_scr[:, :, 0:HEAD_DIM, :]
        nt_ref[...] = cn_scr[:, :, HEAD_DIM:, :]
        mt_ref[...] = m_scr[...]
    _when(last, _store_state)


N_RWKV_IN, N_RWKV_OUT, N_MLSTM_IN, N_MLSTM_OUT = 14, 2, 11, 4
N_MLSTM_ROW_IN = 5
N_RWKV_ROW_IN = 9
RWKV_STAGES, MLSTM_STAGES = 29, 8
RWKV_BODY_ROWS = 256
RWKV_STAGGER = 0.6


def _mixers_kernel(*refs, n_seq, chunk, n_heads_a, n_heads_b):
    it = iter(refs)
    take = lambda n: [next(it) for _ in range(n)]
    r_in, m_in = take(N_RWKV_IN), take(N_MLSTM_IN)
    r_out, m_out = take(N_RWKV_OUT), take(N_MLSTM_OUT)
    s_scr, cn_scr, m_scr = take(3)
    first = pl.program_id(1) == 0
    last = pl.program_id(1) == pl.num_programs(1) - 1
    M = m_in[0].shape[0]
    n_r = M // RWKV_BODY_ROWS if n_seq == 1 and M > RWKV_BODY_ROWS else 1
    n_sub = M // MLSTM_ROWS if n_seq == 1 and M > MLSTM_ROWS else 1
    plan = []
    flags = [[] for _ in range(n_r)]
    for h in range(n_r):
        rows = (lambda r: r.at[pl.ds(h * RWKV_BODY_ROWS, RWKV_BODY_ROWS)]) if n_r > 1 else (lambda r: r)
        chunks = (lambda r: r.at[pl.ds(h * (RWKV_BODY_ROWS // chunk), RWKV_BODY_ROWS // chunk)]) if n_r > 1 else (
            lambda r: r)
        body = _rwkv_body(
            *[rows(r) for r in r_in[:N_RWKV_ROW_IN]], chunks(r_in[N_RWKV_ROW_IN]), *r_in[N_RWKV_ROW_IN + 1:],
            rows(r_out[0]), r_out[1], s_scr, n_seq=n_seq, chunk=chunk, n_heads=n_heads_a,
            first=first if h == 0 else None, last=last if h == n_r - 1 else None,
            after=flags[h - 1] if h else None, done=flags[h])
        plan.append((body, RWKV_STAGES, RWKV_STAGGER * h, 1.0))
    mlstm = []
    for h in range(n_sub):
        rows = (lambda r: r.at[pl.ds(h * MLSTM_ROWS, MLSTM_ROWS)]) if n_sub > 1 else (lambda r: r)
        mlstm.append(_mlstm_body(
            *[rows(r) for r in m_in[:N_MLSTM_ROW_IN]], *m_in[N_MLSTM_ROW_IN:], rows(m_out[0]), *m_out[1:],
            cn_scr, m_scr,
            n_seq=n_seq, n_heads=n_heads_b,
            first=first if h == 0 else None, last=last if h == n_sub - 1 else None))
    plan.append((_chain(*mlstm), MLSTM_STAGES * n_sub, 0.0, 1.0 + RWKV_STAGGER * (n_r - 1)))
    _run(_interleave(plan))


def _mixers(acts, v_a, bv, sz, p_last, s0, ln_w, ln_b, q, k, v_b, gates, gob, c0, n0, m0, gate_b, gn_w,
            n_seq, t_step, chunk):
    b = s0.shape[0]
    n_ha, n_hb = s0.shape[1], c0.shape[1]
    n_tok, da = v_a.shape
    db = v_b.shape[1]
    m = n_seq * t_step
    nj = n_tok // (b * t_step)
    tok = lambda i, j: (i * nj + j, 0)
    per_b3 = lambda i, j: (i, 0, 0)
    per_b4 = lambda i, j: (i, 0, 0, 0)
    const2 = lambda i, j: (0, 0)
    s_spec = pl.BlockSpec((n_seq, n_ha, HEAD_DIM, HEAD_DIM), per_b4)
    c_spec = pl.BlockSpec((n_seq, n_hb, HEAD_DIM, HEAD_DIM), per_b4)
    n_spec = pl.BlockSpec((n_seq, n_hb, SUBLANES, HEAD_DIM), per_b4)
    m_spec = pl.BlockSpec((n_seq, 1, LANES), per_b3)
    row_a, row_b = pl.BlockSpec((m, da), tok), pl.BlockSpec((m, db), tok)
    in_specs = [row_a] * 9
    ones_a = jnp.kron(jnp.eye(n_ha, dtype=F32), jnp.ones((HEAD_DIM, HEAD_DIM), F32)).astype(BF16)
    ones_b = jnp.kron(jnp.eye(n_hb, dtype=F32), jnp.ones((HEAD_DIM, HEAD_DIM), F32)).astype(BF16)
    in_specs += [pl.BlockSpec((m // chunk, 1, da), lambda i, j: (i * nj + j, 0, 0)), s_spec,
                 pl.BlockSpec(ln_w.shape, const2), pl.BlockSpec(ln_b.shape, const2),
                 pl.BlockSpec(ones_a.shape, const2)]
    in_specs += [row_b, row_b, row_b, pl.BlockSpec((m, LANES), tok), row_b, c_spec, n_spec, m_spec,
                 pl.BlockSpec(gate_b.shape, const2), pl.BlockSpec(gn_w.shape, const2),
                 pl.BlockSpec(ones_b.shape, const2)]
    return pl.pallas_call(
        functools.partial(_mixers_kernel, n_seq=n_seq, chunk=chunk, n_heads_a=n_ha, n_heads_b=n_hb),
        grid=(b // n_seq, nj),
        in_specs=in_specs,
        out_specs=[row_a, s_spec, row_b, c_spec, n_spec, m_spec],
        out_shape=[jax.ShapeDtypeStruct((n_tok, da), BF16), jax.ShapeDtypeStruct(s0.shape, F32),
                   jax.ShapeDtypeStruct((n_tok, db), BF16), jax.ShapeDtypeStruct(c0.shape, F32),
                   jax.ShapeDtypeStruct(n0.shape, F32), jax.ShapeDtypeStruct(m0.shape, F32)],
        scratch_shapes=[pltpu.VMEM((n_seq, n_ha // 2, 2 * HEAD_DIM, 2 * HEAD_DIM), F32),
                        pltpu.VMEM((n_seq, n_hb, HEAD_DIM + SUBLANES, HEAD_DIM), F32),
                        pltpu.VMEM((n_seq, 1, LANES), F32)],
        compiler_params=pltpu.CompilerParams(vmem_limit_bytes=VMEM_LIMIT),
        name="mixers",
    )(*acts, v_a, bv, sz, p_last, s0, ln_w, ln_b, ones_a, q, k, v_b, gates, gob, c0, n0, m0, gate_b, gn_w, ones_b)


def _out_body(oa_ref, ob_ref, x_ref, sc_ref, sh_ref, gate_ref, g_ref, wga_ref, wgb_ref, wua_ref, wub_ref, wo_ref,
              gf_ref, y_ref, *, nb, final_norm):
    m = x_ref.shape[0]
    rows = lambda ref: _rows_of_seq(ref[...], nb, m // nb)
    x = x_ref[...]
    h = x * lax.rsqrt(jnp.mean(x * x, axis=-1, keepdims=True) + NORM_EPS) * g_ref[...]
    h = (h * (1.0 + rows(sc_ref)) + rows(sh_ref)).astype(BF16)
    gl_a = jnp.dot(h, wga_ref[...], preferred_element_type=F32)
    yield
    gl_b = jnp.dot(h, wgb_ref[...], preferred_element_type=F32)
    yield
    ua = jnp.dot(oa_ref[...], wua_ref[...], preferred_element_type=F32)
    ub = jnp.dot(ob_ref[...], wub_ref[...], preferred_element_type=F32)
    yield
    merged = _sigmoid(gl_a) * ua + _sigmoid(gl_b) * ub
    mo = jnp.dot(merged.astype(BF16), wo_ref[...], preferred_element_type=F32)
    yield
    xn = x + rows(gate_ref) * mo
    if final_norm:
        xn = xn * lax.rsqrt(jnp.mean(xn * xn, axis=-1, keepdims=True) + NORM_EPS) * gf_ref[...]
    y_ref[...] = xn


def _out_kernel(*refs, n_split, **static):
    if n_split == 1:
        _run(_out_body(*refs, **static))
        return
    oa_ref, ob_ref, x_ref = refs[:3]
    y_ref = refs[-1]
    tp = x_ref.shape[0] // n_split
    plan = []
    for h in range(n_split):
        rows = pl.ds(h * tp, tp)
        body = _out_body(oa_ref.at[rows], ob_ref.at[rows], x_ref.at[rows], *refs[3:-1], y_ref.at[rows], **static)
        plan.append((body, 4, 0.5 * h, 1.0))
    _run(_interleave(plan))


def _out(out_a, out_b, x, scale, shift, gate, g_norm, w_gl_a, w_gl_b, w_up_a, w_up_b, w_out, g_final,
         b, nb, tt, final_norm):
    n_tok, d = x.shape
    t = n_tok // b
    n_split = OUT_SPLIT if nb == 1 and t % (OUT_SPLIT * tt) == 0 else 1
    nj = t // (n_split * tt)
    m = nb * n_split * tt
    tok = lambda i, j: (i * nj + j, 0)
    const2 = lambda i, j: (0, 0)
    per_b = lambda i, j: (i, 0, 0)
    wspec = lambda w: pl.BlockSpec(w.shape, const2, pipeline_mode=pl.Buffered(1))
    in_specs = [pl.BlockSpec((m, out_a.shape[-1]), tok),
                pl.BlockSpec((m, out_b.shape[-1]), tok),
                pl.BlockSpec((m, d), tok),
                pl.BlockSpec((nb, 1, d), per_b),
                pl.BlockSpec((nb, 1, d), per_b),
                pl.BlockSpec((nb, 1, d), per_b),
                pl.BlockSpec((1, d), const2),
                wspec(w_gl_a), wspec(w_gl_b), wspec(w_up_a), wspec(w_up_b), wspec(w_out),
                pl.BlockSpec((1, d), const2)]
    return pl.pallas_call(
        functools.partial(_out_kernel, nb=nb, final_norm=final_norm, n_split=n_split),
        grid=(b // nb, nj),
        in_specs=in_specs,
        out_specs=pl.BlockSpec((m, d), tok),
        out_shape=jax.ShapeDtypeStruct((n_tok, d), F32),
        compiler_params=pltpu.CompilerParams(vmem_limit_bytes=VMEM_LIMIT),
        name="merge_out",
    )(out_a, out_b, x, scale, shift, gate, g_norm.reshape(1, d), w_gl_a, w_gl_b, w_up_a, w_up_b, w_out,
      g_final.reshape(1, d))


def _token_tiling(b, t, target):
    if t >= target:
        return 1, math.gcd(t, target)
    return math.gcd(b, target // t), t


def _layer(x, mod, states, lp, g_final, final_norm):
    b, t, d = x.shape
    shift0, s0, conv0, c0, n0, m0 = states
    n_hb = c0.shape[1]
    cw = lp["conv_w"].shape[0]
    row = lambda a: a.reshape(1, -1)

    ada_shift, ada_scale, ada_gate = (mod[:, None, i * d:(i + 1) * d] for i in range(3))
    nb, tt = _token_tiling(b, t, PROJ_ROWS)
    L = math.gcd(t, ROWS)
    act_dtype = BF16 if L % 16 == 0 else F32
    conv0p = jnp.pad(conv0, ((0, 0), (SUBLANES - (cw - 1), 0), (0, 0)))
    n0p = jnp.broadcast_to(n0[:, :, None, :], (b, n_hb, SUBLANES, HEAD_DIM))
    m0p = jnp.pad(m0, ((0, 0), (0, LANES - n_hb)))[:, None, :]
    gate_b = jnp.concatenate([lp["b_i"], lp["b_f"], jnp.zeros((LANES - 2 * n_hb,), F32)])[None, :]
    n_seq, t_step = _token_tiling(b, t, MIX_ROWS if t >= MIX_ROWS else MLSTM_ROWS)
    (kq, rq, bk, kkk, bk2, kk2, v_a, bv, sz, p_last, shift_t,
     q, k, v_b, gates, gob, conv_tail) = _inproj(
        x, ada_scale, ada_shift, lp["g_norm"], shift0[:, None, :], conv0p, lp, nb, tt, L, act_dtype)
    out_a, s_t, out_b, c_t, n_t, m_t = _mixers(
        (kq, rq, bk, kkk, bk2, kk2), v_a, bv, sz, p_last, s0, row(lp["ln_w"]), row(lp["ln_b"]),
        q, k, v_b, gates, gob, c0, n0p, m0p, gate_b, row(lp["gn_w"]), n_seq, t_step, L)

    y = _out(out_a, out_b, x.reshape(b * t, d), ada_scale, ada_shift, ada_gate, lp["g_norm"],
             lp["w_gl_a"], lp["w_gl_b"], lp["w_up_a"], lp["w_up_b"], lp["w_out"],
             g_final, b, nb, tt, final_norm).reshape(b, t, d)
    new_states = (shift_t[:, 0], s_t, conv_tail[:, SUBLANES - (cw - 1):], c_t, n_t[:, :, 0, :], m_t[:, 0, :n_hb])
    return y, new_states


def _trunk(x, mods, states, layers, g_final):
    depth = len(layers)
    new = [[] for _ in states]
    for l in range(depth):
        st = tuple(s[l] for s in states)
        x, st_new = _layer(x, mods[l], st, layers[l], g_final, final_norm=(l == depth - 1))
        for lst, s in zip(new, st_new):
            lst.append(s.astype(x.dtype))
    return x, tuple(jnp.stack(lst) for lst in new)


def kernel(x_prompt, x_sample, c_prompt, c_sample, state_rwkv_shift, state_rwkv_S, state_mlstm_conv, state_mlstm_C, state_mlstm_n, state_mlstm_m, g_norm, w_ada, b_ada, w_in, mu_shift, w_decay2, w0, w_iclr2, a0, k_k, k_a, r_k, ln_w, ln_b, conv_w, conv_b, b_i, b_f, gn_w, w_up_a, w_up_b, w_out, g_final):
    depth = g_norm.shape[0]
    bp, bs = x_prompt.shape[0], x_sample.shape[0]
    d = x_prompt.shape[-1]
    da, db = w_up_a.shape[1], w_up_b.shape[1]
    n_ha, n_hb = da // HEAD_DIM, db // HEAD_DIM
    sw = mu_shift.shape[-1]
    cc = conv_w.shape[-1]
    cw = conv_w.shape[1]
    dt = x_prompt.dtype

    sizes = (sw, da, cc, db, db, n_hb, n_hb, db, d, d)
    offs = [0]
    for s in sizes:
        offs.append(offs[-1] + s)
    seg = lambda w, i: w[:, offs[i]:offs[i + 1]]

    c_all = jnp.concatenate([c_prompt, c_sample], axis=0)
    n_c = c_all.shape[0]
    c_pad = jnp.pad(c_all, ((0, (-n_c) % SUBLANES), (0, 0)))

    layers, mods_p, mods_s = [], [], []
    for l in range(depth):
        w = w_in[l]
        wg = jnp.concatenate([seg(w, 5), seg(w, 6)], axis=1)
        wg = jnp.pad(wg, ((0, 0), (0, LANES - wg.shape[1])))
        w_segs = tuple(x.astype(BF16) for x in
                       (seg(w, 0), seg(w, 1), seg(w, 2), seg(w, 3), seg(w, 4), seg(w, 7), wg))
        layers.append(dict(
            g_norm=g_norm[l], w_segs=w_segs, w_gl_a=seg(w, 8).astype(BF16), w_gl_b=seg(w, 9).astype(BF16),
            mu_shift=mu_shift[l], w_decay2=w_decay2[l].astype(BF16),
            w0=w0[l], w_iclr2=w_iclr2[l].astype(BF16), a0=a0[l], k_k=k_k[l], k_a=k_a[l], r_k=r_k[l],
            ln_w=ln_w[l], ln_b=ln_b[l], conv_w=conv_w[l], conv_b=conv_b[l], b_i=b_i[l], b_f=b_f[l],
            gn_w=gn_w[l], w_up_a=w_up_a[l].astype(BF16), w_up_b=w_up_b[l].astype(BF16),
            w_out=w_out[l].astype(BF16)))
        mod = _ada(c_pad, w_ada[l], b_ada[l])
        mods_p.append(mod[:bp])
        mods_s.append(mod[bp:bp + bs])

    prompt_states = (
        jnp.zeros((depth, bp, sw), dt),
        jnp.zeros((depth, bp, n_ha, HEAD_DIM, HEAD_DIM), dt),
        jnp.zeros((depth, bp, cw - 1, cc), dt),
        jnp.zeros((depth, bp, n_hb, HEAD_DIM, HEAD_DIM), dt),
        jnp.zeros((depth, bp, n_hb, HEAD_DIM), dt),
        jnp.zeros((depth, bp, n_hb), dt),
    )
    per_layer = lambda s: tuple(_batch_major(s[l]) for l in range(depth))
    sample_states = (state_rwkv_shift, per_layer(state_rwkv_S), state_mlstm_conv,
                     per_layer(state_mlstm_C), state_mlstm_n, state_mlstm_m)
    y_p, st_p = _trunk(x_prompt, mods_p, prompt_states, layers, g_final)
    y_s, st_s = _trunk(x_sample, mods_s, sample_states, layers, g_final)
    return (y_p, y_s) + st_p + st_s
```
